```python
import math
import jax
import jax.numpy as jnp
from jax import lax
import numpy as np

D_MODEL = 1024
BATCH = 8
SEQ = 2048
DEPTH = 1
DEC_BATCH = 128
DEC_SEQ = 8
PAST_LEN = 2048
PAGE_SIZE = 128

NSA_HEADS = 8
NSA_KV_HEADS = 2
NSA_GROUP = NSA_HEADS // NSA_KV_HEADS
HEAD_DIM = 64
CMP_LEN = 32
CMP_STRIDE = 16
CMP_HIDDEN = 4 * HEAD_DIM
SEL_BLOCK = 64
SEL_TOP = 16
WINDOW = 512
WIN_Q_BLOCK = 128
SEL_Q_BLOCK = 64
DN_HEADS = 8
DN_HEAD_DIM = 64
CONV_W = 4
DN_CHUNK = 64
NUM_BUCKETS = 32
MAX_DISTANCE = 128
N_GROUPS = 4
EXPERTS_PER_GROUP = 8
N_EXPERTS = N_GROUPS * EXPERTS_PER_GROUP
TOP_K_IN_GROUP = 2
D_EXPERT = D_MODEL // 4

EPS = 1e-6
NEG_INF = -1e30
F32 = jnp.float32

NSA_Q_DIM = NSA_HEADS * HEAD_DIM
NSA_KV_DIM = NSA_KV_HEADS * HEAD_DIM
DN_DIM = DN_HEADS * DN_HEAD_DIM
DN_CONV_DIM = 3 * DN_DIM
MIX_WIDTH = NSA_Q_DIM + DN_DIM
IN_SPLITS = (NSA_Q_DIM, NSA_KV_DIM, NSA_KV_DIM, NSA_KV_DIM, NSA_KV_DIM, NSA_KV_DIM, NSA_KV_DIM,
             3 * NSA_HEADS, DN_CONV_DIM, DN_HEADS, DN_HEADS, DN_DIM)
IN_DIM = sum(IN_SPLITS)

kernel_name = 'hymba_nsa_gdn_hiermoe_step'


def rmsnorm(x, g):
    xf = x.astype(F32)
    y = xf * lax.rsqrt(jnp.mean(xf * xf, axis=-1, keepdims=True) + EPS)
    return (y * g.astype(F32)).astype(x.dtype)


def l2norm(x):
    xf = x.astype(F32)
    return (xf * lax.rsqrt(jnp.sum(xf * xf, axis=-1, keepdims=True) + EPS)).astype(x.dtype)


def t5_bucket(rel):
    n = jnp.maximum(rel, 0)
    max_exact = NUM_BUCKETS // 2
    nf = jnp.maximum(n, 1).astype(F32)
    large = max_exact + (jnp.log(nf / max_exact) / math.log(MAX_DISTANCE / max_exact)
                         * (NUM_BUCKETS - max_exact)).astype(jnp.int32)
    large = jnp.minimum(large, NUM_BUCKETS - 1)
    return jnp.where(n < max_exact, n, large)


def rel_bias_heads(rel, rel_bias):
    b = rel_bias.astype(F32)[t5_bucket(rel)]
    b = jnp.moveaxis(b, -1, -3)
    return b.reshape(b.shape[:-3] + (NSA_KV_HEADS, NSA_GROUP) + b.shape[-2:])


def attend(q, k, v, bias, mask):
    s = jnp.einsum('bntkgd,bnlkd->bnkgtl', q, k).astype(F32) * HEAD_DIM ** -0.5 + bias
    m = mask[:, None, None]
    p = jax.nn.softmax(jnp.where(m, s, NEG_INF), axis=-1)
    p = jnp.where(m, p, 0.0)
    o = jnp.einsum('bnkgtl,bnlkd->bntkgd', p.astype(v.dtype), v)
    return o, p


def compress_rows(rows, pos_emb, w1, w2):
    B, L = rows.shape[:2]
    n_cmp = (L - CMP_LEN) // CMP_STRIDE + 1
    idx = jnp.arange(n_cmp)[:, None] * CMP_STRIDE + jnp.arange(CMP_LEN)[None, :]
    blk = rows[:, idx] + pos_emb[:, None, :]
    blk = jnp.swapaxes(blk, 2, 3).reshape(B, n_cmp, NSA_KV_HEADS, CMP_LEN * HEAD_DIM)
    return jax.nn.silu(blk @ w1) @ w2


def nsa_cmp_sel(q, q_pos, kc_rows, vc_rows, ks_rows, vs_rows, lp, rel_bias):
    B, Tq = q.shape[:2]
    L = kc_rows.shape[1]
    kc = compress_rows(kc_rows, lp['cmp_pos_k'], lp['cmp_w1_k'], lp['cmp_w2_k'])
    vc = compress_rows(vc_rows, lp['cmp_pos_v'], lp['cmp_w1_v'], lp['cmp_w2_v'])
    n_cmp = kc.shape[1]
    cmp_end = jnp.arange(n_cmp) * CMP_STRIDE + CMP_LEN - 1
    rel_c = q_pos[:, None] - cmp_end[None, :]
    o_cmp, p_cmp = attend(q[:, None], kc[:, None], vc[:, None],
                          rel_bias_heads(rel_c[None], rel_bias), (rel_c >= 0)[None])
    o_cmp = o_cmp[:, 0]
    n_sel = -(-L // SEL_BLOCK)
    c_start = jnp.arange(n_cmp) * CMP_STRIDE
    s_start = jnp.arange(n_sel) * SEL_BLOCK
    overlap = jnp.clip(jnp.minimum(c_start[:, None] + CMP_LEN, s_start[None] + SEL_BLOCK)
                       - jnp.maximum(c_start[:, None], s_start[None]), 0, None).astype(F32) / CMP_LEN
    imp = jnp.einsum('bkgtc,cs->btks', p_cmp[:, 0], overlap)
    q_blk = q_pos // SEL_BLOCK
    sb = jnp.arange(n_sel)
    forced = (sb[None] == 0) | (sb[None] == q_blk[:, None]) | (sb[None] == q_blk[:, None] - 1)
    avail = s_start[None] <= q_pos[:, None]
    score = jnp.where(forced[None, :, None], 1e9, jnp.where(avail[None, :, None], imp, NEG_INF))
    n_top = min(SEL_TOP, n_sel)
    _, sel_idx = lax.top_k(score, n_top)
    pad = n_sel * SEL_BLOCK - L
    def to_blocks(r):
        r = jnp.pad(r, ((0, 0), (0, pad), (0, 0), (0, 0)))
        return r.reshape(B, n_sel, SEL_BLOCK, NSA_KV_HEADS, HEAD_DIM).transpose(0, 3, 1, 2, 4)
    ks_b = to_blocks(ks_rows)
    vs_b = to_blocks(vs_rows)
    bi = jnp.arange(B)[:, None, None, None]
    ki = jnp.arange(NSA_KV_HEADS)[None, None, :, None]
    tbl = rel_bias.astype(F32).reshape(NUM_BUCKETS, NSA_KV_HEADS, NSA_GROUP).transpose(1, 0, 2)
    qb = SEL_Q_BLOCK if Tq % SEL_Q_BLOCK == 0 else Tq
    nqb = Tq // qb

    def sel_block(args):
        q_i, pos_i, idx_i = args
        kg = ks_b[bi, ki, idx_i]
        vg = vs_b[bi, ki, idx_i]
        k_pos = idx_i[..., None] * SEL_BLOCK + jnp.arange(SEL_BLOCK)
        rel = pos_i[None, :, None, None, None] - k_pos
        bias = tbl[ki[..., None], t5_bucket(rel)]
        s = jnp.einsum('bqkgd,bqknsd->bqkgns', q_i, kg).astype(F32) * HEAD_DIM ** -0.5 + jnp.moveaxis(bias, -1, 3)
        mask = (rel >= 0)[:, :, :, None]
        s = jnp.where(mask, s, NEG_INF).reshape(B, qb, NSA_KV_HEADS, NSA_GROUP, n_top * SEL_BLOCK)
        p = jax.nn.softmax(s, axis=-1)
        p = jnp.where(mask.reshape(B, qb, NSA_KV_HEADS, 1, n_top * SEL_BLOCK), p, 0.0)
        return jnp.einsum('bqkgl,bqkld->bqkgd', p.astype(vg.dtype),
                          vg.reshape(B, qb, NSA_KV_HEADS, n_top * SEL_BLOCK, HEAD_DIM))

    q_m = q.reshape(B, nqb, qb, NSA_KV_HEADS, NSA_GROUP, HEAD_DIM).swapaxes(0, 1)
    pos_m = q_pos.reshape(nqb, qb)
    idx_m = sel_idx.reshape(B, nqb, qb, NSA_KV_HEADS, n_top).swapaxes(0, 1)
    o_sel = lax.map(sel_block, (q_m, pos_m, idx_m))
    o_sel = o_sel.swapaxes(0, 1).reshape(B, Tq, NSA_KV_HEADS, NSA_GROUP, HEAD_DIM)
    return o_cmp, o_sel


def window_prompt(q, k, v, rel_bias):
    B, T = q.shape[:2]
    nb = T // WIN_Q_BLOCK
    span = WINDOW + WIN_Q_BLOCK
    kidx = jnp.arange(nb)[:, None] * WIN_Q_BLOCK + jnp.arange(span)[None]
    padw = ((0, 0), (WINDOW, 0), (0, 0), (0, 0))
    kb = jnp.pad(k, padw)[:, kidx]
    vb = jnp.pad(v, padw)[:, kidx]
    k_pos = kidx - WINDOW
    q_pos = jnp.arange(T).reshape(nb, WIN_Q_BLOCK)
    rel = q_pos[:, :, None] - k_pos[:, None, :]
    mask = (rel >= 0) & (rel < WINDOW) & (k_pos[:, None, :] >= 0)
    o, _ = attend(q.reshape(B, nb, WIN_Q_BLOCK, NSA_KV_HEADS, NSA_GROUP, HEAD_DIM), kb, vb,
                  rel_bias_heads(rel, rel_bias), mask)
    return o.reshape(B, T, NSA_KV_HEADS, NSA_GROUP, HEAD_DIM)


def window_sample(q, q_pos, k, v, k_pos, rel_bias):
    rel = q_pos[:, None] - k_pos[None]
    mask = (rel >= 0) & (rel < WINDOW)
    o, _ = attend(q[:, None], k[:, None], v[:, None], rel_bias_heads(rel[None], rel_bias), mask[None])
    return o[:, 0]


def combine_branches(gate_logit, o_cmp, o_sel, o_win):
    B, T = gate_logit.shape[:2]
    g = jax.nn.sigmoid(gate_logit.astype(F32)).reshape(B, T, NSA_KV_HEADS, NSA_GROUP, 3).astype(o_cmp.dtype)
    o = g[..., 0:1] * o_cmp + g[..., 1:2] * o_sel + g[..., 2:3] * o_win
    return o.reshape(B, T, NSA_Q_DIM)


def causal_conv(x_hist, x_new, w):
    xc = jnp.concatenate([x_hist, x_new], axis=1)
    y = lax.conv_general_dilated(xc, w[:, None, :], window_strides=(1,), padding='VALID',
                                 dimension_numbers=('NWC', 'WIO', 'NWC'), feature_group_count=xc.shape[-1])
    return jax.nn.silu(y), xc[:, -(CONV_W - 1):]


def gated_delta_rule(q, k, v, g, beta, s0):
    B, T, H, DK = q.shape
    C = DN_CHUNK if T >= DN_CHUNK else T
    n = -(-T // C)
    pad = n * C - T

    def prep(a):
        a = jnp.pad(a.astype(F32), ((0, 0), (0, pad)) + ((0, 0),) * (a.ndim - 2))
        a = a.reshape((B, n, C) + a.shape[2:])
        return jnp.moveaxis(a, (1, 3), (0, 2))

    qc, kc, vc, gc, bc = prep(q), prep(k), prep(v), prep(g), prep(beta)
    gcum = jnp.cumsum(gc, axis=-1)
    ii = jnp.arange(C)
    incl = ii[:, None] >= ii[None, :]
    strict = ii[:, None] > ii[None, :]
    diff = gcum[..., :, None] - gcum[..., None, :]
    decay = jnp.where(incl, jnp.exp(jnp.where(incl, diff, 0.0)), 0.0)
    kb = kc * bc[..., None]
    a_mat = jnp.where(strict, jnp.einsum('...id,...jd->...ij', kb, kc) * decay, 0.0)
    eye_plus = a_mat + jnp.eye(C, dtype=F32)
    u = lax.linalg.triangular_solve(eye_plus, vc * bc[..., None], left_side=True, lower=True, unit_diagonal=True)
    w = lax.linalg.triangular_solve(eye_plus, kb * jnp.exp(gcum)[..., None], left_side=True, lower=True, unit_diagonal=True)
    qk = jnp.where(incl, jnp.einsum('...id,...jd->...ij', qc, kc) * decay, 0.0)
    qg = qc * jnp.exp(gcum)[..., None]
    kg = kc * jnp.exp(gcum[..., -1:] - gcum)[..., None]
    glast = jnp.exp(gcum[..., -1])

    def step(S, xs):
        u_i, w_i, qk_i, qg_i, kg_i, gl_i = xs
        v_new = u_i - jnp.einsum('bhck,bhkv->bhcv', w_i, S)
        o = jnp.einsum('bhck,bhkv->bhcv', qg_i, S) + jnp.einsum('bhij,bhjv->bhiv', qk_i, v_new)
        S = S * gl_i[..., None, None] + jnp.einsum('bhck,bhcv->bhkv', kg_i, v_new)
        return S, o

    S, o = lax.scan(step, s0.astype(F32), (u, w, qk, qg, kg, glast))
    o = o.transpose(1, 0, 3, 2, 4).reshape(B, n * C, H, v.shape[-1])[:, :T]
    return o, S


def deltanet_mixer(qkv_raw, b_logit, a_logit, gate, conv_hist, s0, lp):
    B, T = qkv_raw.shape[:2]
    qkv, new_hist = causal_conv(conv_hist, qkv_raw, lp['conv_w'])
    q, k, v = jnp.split(qkv, 3, axis=-1)
    q = l2norm(q.reshape(B, T, DN_HEADS, DN_HEAD_DIM)) * DN_HEAD_DIM ** -0.5
    k = l2norm(k.reshape(B, T, DN_HEADS, DN_HEAD_DIM))
    v = v.reshape(B, T, DN_HEADS, DN_HEAD_DIM)
    beta = jax.nn.sigmoid(b_logit.astype(F32))
    g = -jnp.exp(lp['dn_a_log'].astype(F32)) * jax.nn.softplus(a_logit.astype(F32) + lp['dn_dt_bias'].astype(F32))
    o, s_new = gated_delta_rule(q, k, v, g, beta, s0)
    o = rmsnorm(o, lp['dn_norm_g']) * jax.nn.silu(gate.astype(F32).reshape(B, T, DN_HEADS, DN_HEAD_DIM))
    return o.astype(qkv_raw.dtype).reshape(B, T, DN_DIM), s_new.astype(s0.dtype), new_hist


def hier_moe(h, lp):
    B, T, D = h.shape
    x = h.reshape(B * T, D)
    lg = (x @ lp['router_group_w'] + lp['router_group_b']).astype(F32)
    pg = jax.nn.softmax(lg, axis=-1)
    grp = jnp.argmax(lg, axis=-1)
    wg = jnp.take_along_axis(pg, grp[:, None], axis=-1)
    le = (x @ lp['router_expert_w'] + lp['router_expert_b']).astype(F32).reshape(-1, N_GROUPS, EXPERTS_PER_GROUP)
    le = jnp.take_along_axis(le, grp[:, None, None], axis=1)[:, 0]
    pe, ie = lax.top_k(jax.nn.softmax(le, axis=-1), TOP_K_IN_GROUP)
    pe = pe / jnp.sum(pe, axis=-1, keepdims=True)
    eid = grp[:, None] * EXPERTS_PER_GROUP + ie
    gates = jnp.sum(jax.nn.one_hot(eid, N_EXPERTS, dtype=F32) * (wg * pe)[..., None], axis=1).astype(x.dtype)
    y = jnp.zeros_like(x)
    for e in range(N_EXPERTS):
        he = jax.nn.silu(x @ lp['w_gate'][e]) * (x @ lp['w_up'][e])
        y = y + gates[:, e:e + 1] * (he @ lp['w_down'][e])
    return y.reshape(B, T, D)


def modulation(c, lp):
    mod = jax.nn.silu(c) @ lp['w_ada'] + lp['b_ada']
    return jnp.split(mod[:, None, :], 6, axis=-1)


def modulate(x, g, shift, scale):
    return rmsnorm(x, g) * (1 + scale) + shift


def split_projection(h, lp):
    z = h @ lp['w_in']
    return jnp.split(z, np.cumsum(IN_SPLITS)[:-1].tolist(), axis=-1)


def finish_layer(x, o_nsa, o_dn, g1, sh2, sc2, g2, lp):
    x = x + g1 * (jnp.concatenate([o_nsa, o_dn], axis=-1) @ lp['w_out'])
    return x + g2 * hier_moe(modulate(x, lp['norm_ffn_g'], sh2, sc2), lp)


def layer_prompt(x, c, lp, rel_bias):
    B, T, _ = x.shape
    sh1, sc1, g1, sh2, sc2, g2 = modulation(c, lp)
    h = modulate(x, lp['norm_mix_g'], sh1, sc1)
    q, ck, cv, sk, sv, wk, wv, gl, qkv_raw, bl, al, dg = split_projection(h, lp)
    kv = lambda r: r.reshape(B, T, NSA_KV_HEADS, HEAD_DIM)
    ck, cv, sk, sv, wk, wv = kv(ck), kv(cv), kv(sk), kv(sv), kv(wk), kv(wv)
    q = q.reshape(B, T, NSA_KV_HEADS, NSA_GROUP, HEAD_DIM)
    pos = jnp.arange(T)
    o_cmp, o_sel = nsa_cmp_sel(q, pos, ck, cv, sk, sv, lp, rel_bias)
    o_win = window_prompt(q, wk, wv, rel_bias)
    o_nsa = combine_branches(gl, o_cmp, o_sel, o_win)
    conv_hist = jnp.zeros((B, CONV_W - 1, DN_CONV_DIM), x.dtype)
    s0 = jnp.zeros((B, DN_HEADS, DN_HEAD_DIM, DN_HEAD_DIM), x.dtype)
    o_dn, s_new, conv_new = deltanet_mixer(qkv_raw, bl, al, dg, conv_hist, s0, lp)
    x = finish_layer(x, o_nsa, o_dn, g1, sh2, sc2, g2, lp)
    w_keep = min(WINDOW, T)
    return x, (ck, cv, sk, sv, wk[:, -w_keep:], wv[:, -w_keep:], s_new, conv_new)


def layer_sample(x, c, cmp_k_pool, cmp_v_pool, sel_k_pool, sel_v_pool, win_k, win_v, s0, conv_hist,
                 page_table, lp, rel_bias):
    B, T, _ = x.shape
    past = page_table.shape[1] * PAGE_SIZE
    sh1, sc1, g1, sh2, sc2, g2 = modulation(c, lp)
    h = modulate(x, lp['norm_mix_g'], sh1, sc1)
    q, ck, cv, sk, sv, wk, wv, gl, qkv_raw, bl, al, dg = split_projection(h, lp)
    kv = lambda r: r.reshape(B, T, NSA_KV_HEADS, HEAD_DIM)
    ck, cv, sk, sv, wk, wv = kv(ck), kv(cv), kv(sk), kv(sv), kv(wk), kv(wv)
    q = q.reshape(B, T, NSA_KV_HEADS, NSA_GROUP, HEAD_DIM)

    def with_past(pool, new):
        rows = pool[page_table].reshape(B, past, NSA_KV_HEADS, HEAD_DIM)
        return jnp.concatenate([rows, new], axis=1)

    pos = past + jnp.arange(T)
    o_cmp, o_sel = nsa_cmp_sel(q, pos, with_past(cmp_k_pool, ck), with_past(cmp_v_pool, cv),
                               with_past(sel_k_pool, sk), with_past(sel_v_pool, sv), lp, rel_bias)
    W = win_k.shape[1]
    wk_all = jnp.concatenate([win_k, wk], axis=1)
    wv_all = jnp.concatenate([win_v, wv], axis=1)
    o_win = window_sample(q, pos, wk_all, wv_all, past - W + jnp.arange(W + T), rel_bias)
    o_nsa = combine_branches(gl, o_cmp, o_sel, o_win)
    o_dn, s_new, conv_new = deltanet_mixer(qkv_raw, bl, al, dg, conv_hist, s0, lp)
    x = finish_layer(x, o_nsa, o_dn, g1, sh2, sc2, g2, lp)
    w_keep = min(WINDOW, W + T)
    return x, (ck, cv, sk, sv, wk_all[:, -w_keep:], wv_all[:, -w_keep:], s_new, conv_new)


def setup_inputs(seed: int = 0) -> dict:
    key = jax.random.key(seed)
    ks = jax.random.split(key, 48)

    def nrm(i, shape, scale=1.0):
        return jax.random.normal(ks[i], shape, jnp.float32) * scale

    n_pages = PAST_LEN // PAGE_SIZE
    n_pool = (DEC_BATCH * n_pages * 5) // 4
    win_rows = min(WINDOW, PAST_LEN)
    kv_row = (NSA_KV_HEADS, HEAD_DIM)
    page_table = jax.random.permutation(ks[40], n_pool)[:DEC_BATCH * n_pages].reshape(DEC_BATCH, n_pages).astype(jnp.int32)
    dt = jnp.exp(jax.random.uniform(ks[41], (DEPTH, DN_HEADS), jnp.float32, math.log(1e-3), math.log(1e-1)))
    dt_bias = dt + jnp.log(-jnp.expm1(-dt))
    a_log = jnp.log(jax.random.uniform(ks[42], (DEPTH, DN_HEADS), jnp.float32, 1.0, 16.0))
    return {
        'x_prompt': nrm(0, (BATCH, SEQ, D_MODEL)),
        'x_sample': nrm(1, (DEC_BATCH, DEC_SEQ, D_MODEL)),
        'cache_cmp_k': nrm(2, (DEPTH, n_pool, PAGE_SIZE) + kv_row),
        'cache_cmp_v': nrm(3, (DEPTH, n_pool, PAGE_SIZE) + kv_row),
        'cache_sel_k': nrm(4, (DEPTH, n_pool, PAGE_SIZE) + kv_row),
        'cache_sel_v': nrm(5, (DEPTH, n_pool, PAGE_SIZE) + kv_row),
        'cache_win_k': nrm(6, (DEPTH, DEC_BATCH, win_rows) + kv_row),
        'cache_win_v': nrm(7, (DEPTH, DEC_BATCH, win_rows) + kv_row),
        'state_delta': nrm(8, (DEPTH, DEC_BATCH, DN_HEADS, DN_HEAD_DIM, DN_HEAD_DIM), 0.3),
        'state_conv': nrm(9, (DEPTH, DEC_BATCH, CONV_W - 1, DN_CONV_DIM)),
        'page_table': page_table,
        'c_prompt': nrm(10, (BATCH, D_MODEL)),
        'c_sample': nrm(11, (DEC_BATCH, D_MODEL)),
        'rel_bias': nrm(12, (NUM_BUCKETS, NSA_HEADS), 0.5),
        'w_ada': nrm(13, (DEPTH, D_MODEL, 6 * D_MODEL), 0.5 * D_MODEL ** -0.5),
        'b_ada': nrm(14, (DEPTH, 6 * D_MODEL), 0.01),
        'norm_mix_g': 1.0 + nrm(15, (DEPTH, D_MODEL), 0.05),
        'norm_ffn_g': 1.0 + nrm(16, (DEPTH, D_MODEL), 0.05),
        'w_in': nrm(17, (DEPTH, D_MODEL, IN_DIM), D_MODEL ** -0.5),
        'cmp_pos_k': nrm(18, (DEPTH, CMP_LEN, HEAD_DIM), 0.1),
        'cmp_w1_k': nrm(19, (DEPTH, CMP_LEN * HEAD_DIM, CMP_HIDDEN), (CMP_LEN * HEAD_DIM) ** -0.5),
        'cmp_w2_k': nrm(20, (DEPTH, CMP_HIDDEN, HEAD_DIM), CMP_HIDDEN ** -0.5),
        'cmp_pos_v': nrm(21, (DEPTH, CMP_LEN, HEAD_DIM), 0.1),
        'cmp_w1_v': nrm(22, (DEPTH, CMP_LEN * HEAD_DIM, CMP_HIDDEN), (CMP_LEN * HEAD_DIM) ** -0.5),
        'cmp_w2_v': nrm(23, (DEPTH, CMP_HIDDEN, HEAD_DIM), CMP_HIDDEN ** -0.5),
        'conv_w': nrm(24, (DEPTH, CONV_W, DN_CONV_DIM), CONV_W ** -0.5),
        'dn_a_log': a_log,
        'dn_dt_bias': dt_bias,
        'dn_norm_g': 1.0 + nrm(25, (DEPTH, DN_HEAD_DIM), 0.05),
        'w_out': nrm(26, (DEPTH, MIX_WIDTH, D_MODEL), MIX_WIDTH ** -0.5),
        'router_group_w': nrm(27, (DEPTH, D_MODEL, N_GROUPS), D_MODEL ** -0.5),
        'router_group_b': nrm(28, (DEPTH, N_GROUPS), 0.01),
        'router_expert_w': nrm(29, (DEPTH, D_MODEL, N_EXPERTS), D_MODEL ** -0.5),
        'router_expert_b': nrm(30, (DEPTH, N_EXPERTS), 0.01),
        'expert_w_gate': nrm(31, (DEPTH, N_EXPERTS, D_MODEL, D_EXPERT), D_MODEL ** -0.5),
        'expert_w_up': nrm(32, (DEPTH, N_EXPERTS, D_MODEL, D_EXPERT), D_MODEL ** -0.5),
        'expert_w_down': nrm(33, (DEPTH, N_EXPERTS, D_EXPERT, D_MODEL), D_EXPERT ** -0.5),
        'final_norm_g': 1.0 + nrm(34, (D_MODEL,), 0.05),
    }


def reference(x_prompt, x_sample, cache_cmp_k, cache_cmp_v, cache_sel_k, cache_sel_v, cache_win_k, cache_win_v,
              state_delta, state_conv, page_table, c_prompt, c_sample, rel_bias, w_ada, b_ada, norm_mix_g,
              norm_ffn_g, w_in, cmp_pos_k, cmp_w1_k, cmp_w2_k, cmp_pos_v, cmp_w1_v, cmp_w2_v, conv_w, dn_a_log,
              dn_dt_bias, dn_norm_g, w_out, router_group_w, router_group_b, router_expert_w, router_expert_b,
              expert_w_gate, expert_w_up, expert_w_down, final_norm_g):
    xp, xs = x_prompt, x_sample
    new_p, new_s = [], []
    for l in range(DEPTH):
        lp = dict(w_ada=w_ada[l], b_ada=b_ada[l], norm_mix_g=norm_mix_g[l], norm_ffn_g=norm_ffn_g[l],
                  w_in=w_in[l], cmp_pos_k=cmp_pos_k[l], cmp_w1_k=cmp_w1_k[l], cmp_w2_k=cmp_w2_k[l],
                  cmp_pos_v=cmp_pos_v[l], cmp_w1_v=cmp_w1_v[l], cmp_w2_v=cmp_w2_v[l], conv_w=conv_w[l],
                  dn_a_log=dn_a_log[l], dn_dt_bias=dn_dt_bias[l], dn_norm_g=dn_norm_g[l], w_out=w_out[l],
                  router_group_w=router_group_w[l], router_group_b=router_group_b[l],
                  router_expert_w=router_expert_w[l], router_expert_b=router_expert_b[l],
                  w_gate=expert_w_gate[l], w_up=expert_w_up[l], w_down=expert_w_down[l])
        xp, st_p = layer_prompt(xp, c_prompt, lp, rel_bias)
        xs, st_s = layer_sample(xs, c_sample, cache_cmp_k[l], cache_cmp_v[l], cache_sel_k[l], cache_sel_v[l],
                                cache_win_k[l], cache_win_v[l], state_delta[l], state_conv[l], page_table,
                                lp, rel_bias)
        new_p.append(st_p)
        new_s.append(st_s)
    y_prompt = rmsnorm(xp, final_norm_g)
    y_sample = rmsnorm(xs, final_norm_g)

    def stk(states, i):
        return jnp.stack([s[i] for s in states])

    return (y_prompt, y_sample,
            stk(new_p, 0), stk(new_p, 1), stk(new_p, 2), stk(new_p, 3),
            stk(new_p, 4), stk(new_p, 5), stk(new_p, 6), stk(new_p, 7),
            stk(new_s, 0), stk(new_s, 1), stk(new_s, 2), stk(new_s, 3),
            stk(new_s, 4), stk(new_s, 5), stk(new_s, 6), stk(new_s, 7))
```

```python
import functools
import math

import jax
import jax.numpy as jnp
import numpy as np
from jax import lax
from jax.experimental import pallas as pl
from jax.experimental.pallas import tpu as pltpu

D_MODEL = 1024
DEPTH = 1
PAGE_SIZE = 128

NSA_HEADS = 8
NSA_KV_HEADS = 2
NSA_GROUP = NSA_HEADS // NSA_KV_HEADS
HEAD_DIM = 64
CMP_LEN = 32
CMP_STRIDE = 16
CMP_HIDDEN = 4 * HEAD_DIM
SEL_BLOCK = 64
SEL_TOP = 16
WINDOW = 512
WIN_Q_BLOCK = 128
SEL_Q_BLOCK = 64
DN_HEADS = 8
DN_HEAD_DIM = 64
CONV_W = 4
DN_CHUNK = 64
NUM_BUCKETS = 32
MAX_DISTANCE = 128
N_GROUPS = 4
EXPERTS_PER_GROUP = 8
N_EXPERTS = N_GROUPS * EXPERTS_PER_GROUP
TOP_K_IN_GROUP = 2
D_EXPERT = D_MODEL // 4

EPS = 1e-6
NEG_INF = -1e30
F32 = jnp.float32
BF16 = jnp.bfloat16

NSA_Q_DIM = NSA_HEADS * HEAD_DIM
NSA_KV_DIM = NSA_KV_HEADS * HEAD_DIM
DN_DIM = DN_HEADS * DN_HEAD_DIM
DN_CONV_DIM = 3 * DN_DIM
MIX_WIDTH = NSA_Q_DIM + DN_DIM
IN_SPLITS = (NSA_Q_DIM, NSA_KV_DIM, NSA_KV_DIM, NSA_KV_DIM, NSA_KV_DIM, NSA_KV_DIM, NSA_KV_DIM,
             3 * NSA_HEADS, DN_CONV_DIM, DN_HEADS, DN_HEADS, DN_DIM)
IN_DIM = sum(IN_SPLITS)

LANES = 128
VMEM_LIMIT = 56 * 1024 * 1024

ATT_TILE = 128
MASKED = -1e30


def rmsnorm(x, g):
    xf = x.astype(F32)
    y = xf * lax.rsqrt(jnp.mean(xf * xf, axis=-1, keepdims=True) + EPS)
    return (y * g.astype(F32)).astype(x.dtype)


def l2norm(x):
    xf = x.astype(F32)
    return (xf * lax.rsqrt(jnp.sum(xf * xf, axis=-1, keepdims=True) + EPS)).astype(x.dtype)


def t5_bucket(rel):
    n = jnp.maximum(rel, 0)
    max_exact = NUM_BUCKETS // 2
    nf = jnp.maximum(n, 1).astype(F32)
    large = max_exact + (jnp.log(nf / max_exact) / math.log(MAX_DISTANCE / max_exact)
                         * (NUM_BUCKETS - max_exact)).astype(jnp.int32)
    large = jnp.minimum(large, NUM_BUCKETS - 1)
    return jnp.where(n < max_exact, n, large)


def rel_bias_heads(rel, rel_bias):
    b = rel_bias.astype(F32)[t5_bucket(rel)]
    b = jnp.moveaxis(b, -1, -3)
    return b.reshape(b.shape[:-3] + (NSA_KV_HEADS, NSA_GROUP) + b.shape[-2:])


def attend(q, k, v, bias, mask):
    s = jnp.einsum('bntkgd,bnlkd->bnkgtl', q, k).astype(F32) * HEAD_DIM ** -0.5 + bias
    m = mask[:, None, None]
    p = jax.nn.softmax(jnp.where(m, s, NEG_INF), axis=-1)
    p = jnp.where(m, p, 0.0)
    o = jnp.einsum('bnkgtl,bnlkd->bntkgd', p.astype(v.dtype), v)
    return o, p


def compress_rows(rows, pos_emb, w1, w2):
    B, L = rows.shape[:2]
    n_cmp = (L - CMP_LEN) // CMP_STRIDE + 1
    idx = jnp.arange(n_cmp)[:, None] * CMP_STRIDE + jnp.arange(CMP_LEN)[None, :]
    blk = rows[:, idx] + pos_emb[:, None, :]
    blk = jnp.swapaxes(blk, 2, 3).reshape(B, n_cmp, NSA_KV_HEADS, CMP_LEN * HEAD_DIM)
    return jax.nn.silu(blk @ w1) @ w2


def nsa_cmp_sel(q, q_pos, kc_rows, vc_rows, ks_rows, vs_rows, lp, rel_bias):
    B, Tq = q.shape[:2]
    L = kc_rows.shape[1]
    kc = compress_rows(kc_rows, lp['cmp_pos_k'], lp['cmp_w1_k'], lp['cmp_w2_k'])
    vc = compress_rows(vc_rows, lp['cmp_pos_v'], lp['cmp_w1_v'], lp['cmp_w2_v'])
    n_cmp = kc.shape[1]
    cmp_end = jnp.arange(n_cmp) * CMP_STRIDE + CMP_LEN - 1
    rel_c = q_pos[:, None] - cmp_end[None, :]
    o_cmp, p_cmp = attend(q[:, None], kc[:, None], vc[:, None],
                          rel_bias_heads(rel_c[None], rel_bias), (rel_c >= 0)[None])
    o_cmp = o_cmp[:, 0]
    n_sel = -(-L // SEL_BLOCK)
    c_start = jnp.arange(n_cmp) * CMP_STRIDE
    s_start = jnp.arange(n_sel) * SEL_BLOCK
    overlap = jnp.clip(jnp.minimum(c_start[:, None] + CMP_LEN, s_start[None] + SEL_BLOCK)
                       - jnp.maximum(c_start[:, None], s_start[None]), 0, None).astype(F32) / CMP_LEN
    imp = jnp.einsum('bkgtc,cs->btks', p_cmp[:, 0], overlap)
    q_blk = q_pos // SEL_BLOCK
    sb = jnp.arange(n_sel)
    forced = (sb[None] == 0) | (sb[None] == q_blk[:, None]) | (sb[None] == q_blk[:, None] - 1)
    avail = s_start[None] <= q_pos[:, None]
    score = jnp.where(forced[None, :, None], 1e9, jnp.where(avail[None, :, None], imp, NEG_INF))
    n_top = min(SEL_TOP, n_sel)
    _, sel_idx = lax.top_k(score, n_top)
    pad = n_sel * SEL_BLOCK - L

    def to_blocks(r):
        r = jnp.pad(r, ((0, 0), (0, pad), (0, 0), (0, 0)))
        return r.reshape(B, n_sel, SEL_BLOCK, NSA_KV_HEADS, HEAD_DIM).transpose(0, 3, 1, 2, 4)
    ks_b = to_blocks(ks_rows)
    vs_b = to_blocks(vs_rows)
    bi = jnp.arange(B)[:, None, None, None]
    ki = jnp.arange(NSA_KV_HEADS)[None, None, :, None]
    tbl = rel_bias.astype(F32).reshape(NUM_BUCKETS, NSA_KV_HEADS, NSA_GROUP).transpose(1, 0, 2)
    qb = SEL_Q_BLOCK if Tq % SEL_Q_BLOCK == 0 else Tq
    nqb = Tq // qb

    def sel_block(args):
        q_i, pos_i, idx_i = args
        kg = ks_b[bi, ki, idx_i]
        vg = vs_b[bi, ki, idx_i]
        k_pos = idx_i[..., None] * SEL_BLOCK + jnp.arange(SEL_BLOCK)
        rel = pos_i[None, :, None, None, None] - k_pos
        bias = tbl[ki[..., None], t5_bucket(rel)]
        s = jnp.einsum('bqkgd,bqknsd->bqkgns', q_i, kg).astype(F32) * HEAD_DIM ** -0.5 + jnp.moveaxis(bias, -1, 3)
        mask = (rel >= 0)[:, :, :, None]
        s = jnp.where(mask, s, NEG_INF).reshape(B, qb, NSA_KV_HEADS, NSA_GROUP, n_top * SEL_BLOCK)
        p = jax.nn.softmax(s, axis=-1)
        p = jnp.where(mask.reshape(B, qb, NSA_KV_HEADS, 1, n_top * SEL_BLOCK), p, 0.0)
        return jnp.einsum('bqkgl,bqkld->bqkgd', p.astype(vg.dtype),
                          vg.reshape(B, qb, NSA_KV_HEADS, n_top * SEL_BLOCK, HEAD_DIM))

    q_m = q.reshape(B, nqb, qb, NSA_KV_HEADS, NSA_GROUP, HEAD_DIM).swapaxes(0, 1)
    pos_m = q_pos.reshape(nqb, qb)
    idx_m = sel_idx.reshape(B, nqb, qb, NSA_KV_HEADS, n_top).swapaxes(0, 1)
    o_sel = lax.map(sel_block, (q_m, pos_m, idx_m))
    o_sel = o_sel.swapaxes(0, 1).reshape(B, Tq, NSA_KV_HEADS, NSA_GROUP, HEAD_DIM)
    return o_cmp, o_sel


def window_sample(q, q_pos, k, v, k_pos, rel_bias):
    rel = q_pos[:, None] - k_pos[None]
    mask = (rel >= 0) & (rel < WINDOW)
    o, _ = attend(q[:, None], k[:, None], v[:, None], rel_bias_heads(rel[None], rel_bias), mask[None])
    return o[:, 0]


def combine_branches(gate_logit, o_cmp, o_sel, o_win):
    B, T = gate_logit.shape[:2]
    g = jax.nn.sigmoid(gate_logit.astype(F32)).reshape(B, T, NSA_KV_HEADS, NSA_GROUP, 3).astype(o_cmp.dtype)
    o = g[..., 0:1] * o_cmp + g[..., 1:2] * o_sel + g[..., 2:3] * o_win
    return o.reshape(B, T, NSA_Q_DIM)


def causal_conv(x_hist, x_new, w):
    xc = jnp.concatenate([x_hist, x_new], axis=1)
    y = lax.conv_general_dilated(xc, w[:, None, :], window_strides=(1,), padding='VALID',
                                 dimension_numbers=('NWC', 'WIO', 'NWC'), feature_group_count=xc.shape[-1])
    return jax.nn.silu(y), xc[:, -(CONV_W - 1):]


def gated_delta_rule(q, k, v, g, beta, s0):
    B, T, H, DK = q.shape
    C = DN_CHUNK if T >= DN_CHUNK else T
    n = -(-T // C)
    pad = n * C - T

    def prep(a):
        a = jnp.pad(a.astype(F32), ((0, 0), (0, pad)) + ((0, 0),) * (a.ndim - 2))
        a = a.reshape((B, n, C) + a.shape[2:])
        return jnp.moveaxis(a, (1, 3), (0, 2))

    qc, kc, vc, gc, bc = prep(q), prep(k), prep(v), prep(g), prep(beta)
    gcum = jnp.cumsum(gc, axis=-1)
    ii = jnp.arange(C)
    incl = ii[:, None] >= ii[None, :]
    strict = ii[:, None] > ii[None, :]
    diff = gcum[..., :, None] - gcum[..., None, :]
    decay = jnp.where(incl, jnp.exp(jnp.where(incl, diff, 0.0)), 0.0)
    kb = kc * bc[..., None]
    a_mat = jnp.where(strict, jnp.einsum('...id,...jd->...ij', kb, kc) * decay, 0.0)
    eye_plus = a_mat + jnp.eye(C, dtype=F32)
    u = lax.linalg.triangular_solve(eye_plus, vc * bc[..., None], left_side=True, lower=True, unit_diagonal=True)
    w = lax.linalg.triangular_solve(eye_plus, kb * jnp.exp(gcum)[..., None], left_side=True, lower=True, unit_diagonal=True)
    qk = jnp.where(incl, jnp.einsum('...id,...jd->...ij', qc, kc) * decay, 0.0)
    qg = qc * jnp.exp(gcum)[..., None]
    kg = kc * jnp.exp(gcum[..., -1:] - gcum)[..., None]
    glast = jnp.exp(gcum[..., -1])

    def step(S, xs):
        u_i, w_i, qk_i, qg_i, kg_i, gl_i = xs
        v_new = u_i - jnp.einsum('bhck,bhkv->bhcv', w_i, S)
        o = jnp.einsum('bhck,bhkv->bhcv', qg_i, S) + jnp.einsum('bhij,bhjv->bhiv', qk_i, v_new)
        S = S * gl_i[..., None, None] + jnp.einsum('bhck,bhcv->bhkv', kg_i, v_new)
        return S, o

    S, o = lax.scan(step, s0.astype(F32), (u, w, qk, qg, kg, glast))
    o = o.transpose(1, 0, 3, 2, 4).reshape(B, n * C, H, v.shape[-1])[:, :T]
    return o, S


def deltanet_mixer(qkv_raw, b_logit, a_logit, gate, conv_hist, s0, lp):
    B, T = qkv_raw.shape[:2]
    qkv, new_hist = causal_conv(conv_hist, qkv_raw, lp['conv_w'])
    q, k, v = jnp.split(qkv, 3, axis=-1)
    q = l2norm(q.reshape(B, T, DN_HEADS, DN_HEAD_DIM)) * DN_HEAD_DIM ** -0.5
    k = l2norm(k.reshape(B, T, DN_HEADS, DN_HEAD_DIM))
    v = v.reshape(B, T, DN_HEADS, DN_HEAD_DIM)
    beta = jax.nn.sigmoid(b_logit.astype(F32))
    g = -jnp.exp(lp['dn_a_log'].astype(F32)) * jax.nn.softplus(a_logit.astype(F32) + lp['dn_dt_bias'].astype(F32))
    o, s_new = gated_delta_rule(q, k, v, g, beta, s0)
    o = rmsnorm(o, lp['dn_norm_g']) * jax.nn.silu(gate.astype(F32).reshape(B, T, DN_HEADS, DN_HEAD_DIM))
    return o.astype(qkv_raw.dtype).reshape(B, T, DN_DIM), s_new.astype(s0.dtype), new_hist


def hier_moe(h, lp):
    B, T, D = h.shape
    x = h.reshape(B * T, D)
    lg = (x @ lp['router_group_w'] + lp['router_group_b']).astype(F32)
    pg = jax.nn.softmax(lg, axis=-1)
    grp = jnp.argmax(lg, axis=-1)
    wg = jnp.take_along_axis(pg, grp[:, None], axis=-1)
    le = (x @ lp['router_expert_w'] + lp['router_expert_b']).astype(F32).reshape(-1, N_GROUPS, EXPERTS_PER_GROUP)
    le = jnp.take_along_axis(le, grp[:, None, None], axis=1)[:, 0]
    pe, ie = lax.top_k(jax.nn.softmax(le, axis=-1), TOP_K_IN_GROUP)
    pe = pe / jnp.sum(pe, axis=-1, keepdims=True)
    eid = grp[:, None] * EXPERTS_PER_GROUP + ie
    gates = jnp.sum(jax.nn.one_hot(eid, N_EXPERTS, dtype=F32) * (wg * pe)[..., None], axis=1).astype(x.dtype)
    y = jnp.zeros_like(x)
    for e in range(N_EXPERTS):
        he = jax.nn.silu(x @ lp['w_gate'][e]) * (x @ lp['w_up'][e])
        y = y + gates[:, e:e + 1] * (he @ lp['w_down'][e])
    return y.reshape(B, T, D)


def modulation(c, lp):
    mod = jax.nn.silu(c) @ lp['w_ada'] + lp['b_ada']
    return jnp.split(mod[:, None, :], 6, axis=-1)


def modulate(x, g, shift, scale):
    return rmsnorm(x, g) * (1 + scale) + shift


def split_projection(h, lp):
    z = h @ lp['w_in']
    return jnp.split(z, np.cumsum(IN_SPLITS)[:-1].tolist(), axis=-1)


def finish_layer(x, o_nsa, o_dn, g1, sh2, sc2, g2, lp):
    x = x + g1 * (jnp.concatenate([o_nsa, o_dn], axis=-1) @ lp['w_out'])
    return x + g2 * hier_moe(modulate(x, lp['norm_ffn_g'], sh2, sc2), lp)


_PK_Q = (0, NSA_Q_DIM)
_PK_KV = (_PK_Q[1], _PK_Q[1] + 6 * NSA_KV_DIM)
_PK_QKV = (_PK_KV[1], _PK_KV[1] + DN_CONV_DIM)
_PK_DG = (_PK_QKV[1], _PK_QKV[1] + DN_DIM)
_PK_SM = (_PK_DG[1], _PK_DG[1] + LANES)
PK_DIM = _PK_SM[1]
SM_GATE = 0
SM_BETA = 3 * NSA_HEADS
SM_DECAY = SM_BETA + DN_HEADS


def pack_w_in(w_in):
    offs = np.cumsum((0,) + IN_SPLITS)
    part = lambda i: w_in[:, offs[i]:offs[i + 1]]
    d = w_in.shape[0]
    q = part(0).reshape(d, NSA_KV_HEADS, NSA_GROUP, HEAD_DIM).transpose(0, 2, 1, 3).reshape(d, NSA_Q_DIM)
    q = q * HEAD_DIM ** -0.5
    small = jnp.concatenate([part(7), part(9), part(10)], axis=1)
    small = jnp.pad(small, ((0, 0), (0, LANES - small.shape[1])))
    cols = [q] + [part(i) for i in range(1, 7)] + [part(8), part(11), small]
    return jnp.concatenate(cols, axis=1).astype(BF16)


def _in_proj_body(x_ref, g_ref, sc_ref, sh_ref, w_ref, q_ref, ck_ref, cv_ref, sk_ref, sv_ref, wk_ref, wv_ref,
                  skb_ref, svb_ref, wkb_ref, wvb_ref, qkv_ref, dg_ref, sm_ref):
    x = x_ref[...]
    tm, d = x.shape
    nb = sc_ref.shape[0]
    xn = x * lax.rsqrt(jnp.mean(x * x, axis=-1, keepdims=True) + EPS) * g_ref[...]
    h = xn.reshape(nb, tm // nb, d) * (1 + sc_ref[...]) + sh_ref[...]
    h = h.reshape(tm, d).astype(BF16)

    def mm(span):
        return jnp.dot(h, w_ref[:, span[0]:span[1]], preferred_element_type=F32)

    q_ref[...] = mm(_PK_Q).astype(BF16)
    kv = mm(_PK_KV)
    for i, r in enumerate((ck_ref, cv_ref, sk_ref, sv_ref, wk_ref, wv_ref)):
        r[...] = kv[:, i * NSA_KV_DIM:(i + 1) * NSA_KV_DIM]
    for i, r in enumerate((skb_ref, svb_ref, wkb_ref, wvb_ref)):
        r[...] = kv[:, (i + 2) * NSA_KV_DIM:(i + 3) * NSA_KV_DIM].astype(BF16)
    qkv_ref[...] = mm(_PK_QKV)
    dg_ref[...] = mm(_PK_DG)
    sm_ref[...] = mm(_PK_SM)


def in_projection(x, norm_g, scale, shift, w_packed, *, tm=256):
    B, T, D = x.shape
    n = B * T
    tm = min(tm, n)
    rows = max(tm // T, 1)
    per = max(T // tm, 1)
    row = lambda i: (i, 0)
    outs = [((n, NSA_Q_DIM), BF16)] + [((n, NSA_KV_DIM), F32)] * 6 + [((n, NSA_KV_DIM), BF16)] * 4 + \
           [((n, DN_CONV_DIM), F32), ((n, DN_DIM), F32), ((n, LANES), F32)]
    return pl.pallas_call(
        _in_proj_body,
        grid=(n // tm,),
        in_specs=[pl.BlockSpec((tm, D), row),
                  pl.BlockSpec((1, D), lambda i: (0, 0)),
                  pl.BlockSpec((rows, 1, D), lambda i: (i // per, 0, 0)),
                  pl.BlockSpec((rows, 1, D), lambda i: (i // per, 0, 0)),
                  pl.BlockSpec((D, PK_DIM), lambda i: (0, 0))],
        out_specs=[pl.BlockSpec((tm, s[1]), row) for s, _ in outs],
        out_shape=[jax.ShapeDtypeStruct(s, dt) for s, dt in outs],
        compiler_params=pltpu.CompilerParams(dimension_semantics=("parallel",), vmem_limit_bytes=VMEM_LIMIT),
        name="in_projection",
    )(x.reshape(n, D), norm_g.reshape(1, D), scale, shift, w_packed)


CHUNK_W = CMP_STRIDE * NSA_KV_DIM
CMP_HALVES = CMP_LEN // CMP_STRIDE


def pack_compress(pos, w1, w2):
    eye = jnp.eye(NSA_KV_HEADS, dtype=F32)
    w1r = w1.reshape(CMP_HALVES, CMP_STRIDE, HEAD_DIM, CMP_HIDDEN)
    w1p = jnp.einsum('ijdc,kl->ijkdlc', w1r, eye).reshape(CMP_HALVES, CHUNK_W, NSA_KV_HEADS * CMP_HIDDEN)
    w2p = jnp.einsum('cd,kl->kcld', w2, eye).reshape(NSA_KV_HEADS * CMP_HIDDEN, NSA_KV_DIM)
    posp = jnp.broadcast_to(pos.reshape(CMP_HALVES, CMP_STRIDE, 1, HEAD_DIM),
                            (CMP_HALVES, CMP_STRIDE, NSA_KV_HEADS, HEAD_DIM)).reshape(CMP_HALVES, CHUNK_W)
    return posp, w1p.astype(BF16), w2p.astype(BF16)


def _compress_one(x, pos_ref, w1_ref, w2_ref):
    n_chunk = x.shape[0]
    hid = None
    for i in range(CMP_HALVES):
        part = jnp.dot((x + pos_ref[i:i + 1]).astype(BF16), w1_ref[i], preferred_element_type=F32)
        if i:
            part = pltpu.roll(part, shift=n_chunk - i, axis=0)
        hid = part if hid is None else hid + part
    hid = hid * jax.nn.sigmoid(hid)
    return jnp.dot(hid.astype(BF16), w2_ref[...], preferred_element_type=F32)


def _compress_body(xk_ref, xv_ref, pk_ref, pv_ref, w1k_ref, w1v_ref, w2k_ref, w2v_ref, kc_ref, vc_ref):
    kc_ref[0] = _compress_one(xk_ref[0], pk_ref, w1k_ref, w2k_ref).astype(BF16)
    vc_ref[0] = _compress_one(xv_ref[0], pv_ref, w1v_ref, w2v_ref).astype(BF16)


def compress_tokens(ck, cv, pk, pv, B):
    n_chunk = ck.shape[0] // B // CMP_STRIDE
    xk = ck.reshape(B, n_chunk, CHUNK_W)
    xv = cv.reshape(B, n_chunk, CHUNK_W)
    full = lambda a: pl.BlockSpec(a.shape, lambda b: (0,) * a.ndim)
    blk = lambda w: pl.BlockSpec((1, n_chunk, w), lambda b: (b, 0, 0))
    return pl.pallas_call(
        _compress_body,
        grid=(B,),
        in_specs=[blk(CHUNK_W), blk(CHUNK_W), full(pk[0]), full(pv[0]), full(pk[1]), full(pv[1]),
                  full(pk[2]), full(pv[2])],
        out_specs=[blk(NSA_KV_DIM), blk(NSA_KV_DIM)],
        out_shape=[jax.ShapeDtypeStruct((B, n_chunk, NSA_KV_DIM), BF16)] * 2,
        compiler_params=pltpu.CompilerParams(dimension_semantics=("parallel",), vmem_limit_bytes=VMEM_LIMIT),
        name="compress_tokens",
    )(xk, xv, pk[0], pv[0], pk[1], pv[1], pk[2], pv[2])


def _t5_bucket_np(rel):
    n = np.maximum(rel, 0)
    max_exact = NUM_BUCKETS // 2
    nf = np.maximum(n, 1).astype(np.float32)
    large = max_exact + (np.log(nf / max_exact) / math.log(MAX_DISTANCE / max_exact)
                         * (NUM_BUCKETS - max_exact)).astype(np.int32)
    return np.where(n < max_exact, n, np.minimum(large, NUM_BUCKETS - 1))


assert (_t5_bucket_np(np.arange(ATT_TILE + 1, 1 << 16)) == NUM_BUCKETS - 1).all()


def near_bias_tables(rel_bias):
    t = ATT_TILE
    i = jnp.arange(t)[:, None]
    j = jnp.arange(t)[None, :]
    tbl = rel_bias.astype(F32).T
    diag = jnp.where(i >= j, tbl[:, t5_bucket(i - j)], MASKED)
    prev = tbl[:, t5_bucket(t + i - j)]
    far = jnp.broadcast_to(tbl[:, NUM_BUCKETS - 1][:, None, None], prev.shape)
    edge = jnp.where(j > i, far, MASKED)
    return jnp.stack([diag, prev, far, edge])


def cmp_bias_table(rel_bias, q_pos, n_chunk):
    cmp_end = jnp.arange(n_chunk) * CMP_STRIDE + CMP_LEN - 1
    rel = q_pos[:, None] - cmp_end[None, :]
    tbl = rel_bias.astype(F32).T
    return jnp.where(rel >= 0, tbl[:, t5_bucket(rel)], MASKED)


def overlap_matrix(n_chunk, n_sel):
    c0 = np.arange(n_chunk)[:, None] * CMP_STRIDE
    s0 = np.arange(LANES)[None, :] * SEL_BLOCK
    ov = np.clip(np.minimum(c0 + CMP_LEN, s0 + SEL_BLOCK) - np.maximum(c0, s0), 0, None) / CMP_LEN
    ov = np.where(np.arange(LANES)[None, :] < n_sel, ov, 0.0)
    return jnp.asarray(ov, F32)


def expand_matrix(n_keys):
    e = (np.arange(n_keys)[None, :] // SEL_BLOCK) == np.arange(LANES)[:, None]
    return jnp.asarray(e, BF16)


def _stack_queries(q_ref, tq):
    low = lax.broadcasted_iota(jnp.int32, (tq, LANES), 1) < HEAD_DIM
    zero = jnp.zeros((tq, LANES), BF16)
    parts = []
    for kv in range(NSA_KV_HEADS):
        for g in range(NSA_GROUP):
            blk = q_ref[:, g * LANES:(g + 1) * LANES]
            parts.append(jnp.where(low if kv == 0 else ~low, blk, zero))
    return jnp.concatenate(parts, axis=0)


def _dot_nt(a, b):
    return lax.dot_general(a, b, (((1,), (1,)), ((), ())), preferred_element_type=F32)


def _select_blocks(imp, q_pos, n_sel, n_top):
    lane = lax.broadcasted_iota(jnp.int32, imp.shape, 1)
    q_blk = q_pos // SEL_BLOCK
    forced = jnp.where(lane == 0, 1.0, jnp.where(lane == q_blk, 1.0, jnp.where(lane == q_blk - 1, 1.0, 0.0)))
    score = jnp.where(lane * SEL_BLOCK <= q_pos, imp, NEG_INF)
    score = jnp.where(forced > 0.5, 1e9, score)
    score = jnp.where(lane < n_sel, score, -3e38)
    rank = jnp.zeros(imp.shape, F32)
    for s in range(n_sel):
        col = score[:, s:s + 1]
        tie = jnp.where(lane > s, 1.0, 0.0)
        rank = rank + jnp.where(col > score, 1.0, jnp.where(col == score, tie, 0.0))
    return jnp.where(rank < n_top, 1.0, 0.0)


def _softmax_tile(q, k_ref, v_ref, kt, bias, mask_ref, m_ref, l_ref, acc_ref):
    tk = ATT_TILE
    h, tq = acc_ref.shape[:2]
    rows = pl.ds(pl.multiple_of(kt * tk, tk), tk)
    s = _dot_nt(q, k_ref[rows, :]).reshape(h, tq, tk) + bias
    if mask_ref is not None:
        mb = mask_ref[:, :, pl.ds(pl.multiple_of(kt * tk, tk), tk)]
        s = (s.reshape(NSA_KV_HEADS, NSA_GROUP, tq, tk) + mb[:, None]).reshape(h, tq, tk)
    m_old = m_ref[...]
    m_new = jnp.maximum(m_old, jnp.max(s, axis=-1, keepdims=True))
    alpha = jnp.exp(m_old - m_new)
    p = jnp.exp(s - m_new)
    l_ref[...] = alpha * l_ref[...] + jnp.sum(p, axis=-1, keepdims=True)
    pv = jnp.dot(p.reshape(h * tq, tk).astype(BF16), v_ref[rows, :], preferred_element_type=F32)
    acc_ref[...] = alpha * acc_ref[...] + pv.reshape(h, tq, LANES)
    m_ref[...] = m_new


def _reset_softmax(m_ref, l_ref, acc_ref):
    m_ref[...] = jnp.full(m_ref.shape, MASKED, F32)
    l_ref[...] = jnp.zeros(l_ref.shape, F32)
    acc_ref[...] = jnp.zeros(acc_ref.shape, F32)


def _gate_columns(gl_ref, j):
    g = jax.nn.sigmoid(gl_ref[...])
    cols = [g[:, SM_GATE + 3 * h + j:SM_GATE + 3 * h + j + 1] for h in range(NSA_HEADS)]
    return jnp.stack(cols, axis=0)


def _nsa_prompt_body(q_ref, kc_ref, vc_ref, sk_ref, sv_ref, wk_ref, wv_ref, bc_ref, nb_ref, gl_ref, ov_ref, ex_ref,
                     o_ref, mb_ref, m_ref, l_ref, acc_ref, out_ref, *, n_sel, n_top):
    i = pl.program_id(1)
    tq = q_ref.shape[0]
    h = NSA_HEADS
    q = _stack_queries(q_ref, tq)

    s = _dot_nt(q, kc_ref[0]).reshape(h, tq, -1) + bc_ref[...]
    m = jnp.max(s, axis=-1, keepdims=True)
    p = jnp.where(s > 0.5 * MASKED, jnp.exp(s - m), 0.0)
    l = jnp.sum(p, axis=-1, keepdims=True)
    p = p * (1.0 / jnp.where(l > 0.0, l, 1.0))
    o_cmp = jnp.dot(p.reshape(h * tq, -1).astype(BF16), vc_ref[0], preferred_element_type=F32)
    out_ref[...] = _gate_columns(gl_ref, 0) * o_cmp.reshape(h, tq, LANES)

    q_pos = i * tq + lax.broadcasted_iota(jnp.int32, (tq, LANES), 0)
    for kv in range(NSA_KV_HEADS):
        pk = p[kv * NSA_GROUP]
        for g in range(1, NSA_GROUP):
            pk = pk + p[kv * NSA_GROUP + g]
        imp = jnp.dot(pk, ov_ref[...], precision=lax.Precision.HIGHEST, preferred_element_type=F32)
        sel = _select_blocks(imp, q_pos, n_sel, n_top)
        keys = jnp.dot(sel.astype(BF16), ex_ref[...], preferred_element_type=F32)
        mb_ref[kv] = (keys - 1.0) * (-MASKED)

    _reset_softmax(m_ref, l_ref, acc_ref)

    def sel_step(kt, carry):
        _softmax_tile(q, sk_ref, sv_ref, kt, nb_ref[jnp.minimum(i - kt, 2)], mb_ref, m_ref, l_ref, acc_ref)
        return carry
    lax.fori_loop(0, i + 1, sel_step, 0)
    out_ref[...] += _gate_columns(gl_ref, 1) * (acc_ref[...] * (1.0 / l_ref[...]))

    _reset_softmax(m_ref, l_ref, acc_ref)
    n_back = WINDOW // ATT_TILE
    for r in range(n_back + 1):
        table = 3 if r == n_back else min(r, 2)

        @pl.when(i >= r)
        def _():
            _softmax_tile(q, wk_ref, wv_ref, i - r, nb_ref[table], None, m_ref, l_ref, acc_ref)
    out_ref[...] += _gate_columns(gl_ref, 2) * (acc_ref[...] * (1.0 / l_ref[...]))

    low = lax.broadcasted_iota(jnp.int32, (tq, LANES), 1) < HEAD_DIM
    for g in range(NSA_GROUP):
        o_ref[:, g * LANES:(g + 1) * LANES] = jnp.where(low, out_ref[g], out_ref[NSA_GROUP + g]).astype(BF16)


def nsa_prompt(q, kc, vc, skb, svb, wkb, wvb, small, rel_bias, B, T):
    tq = ATT_TILE
    nq = T // tq
    n_chunk = kc.shape[1]
    n_sel = -(-T // SEL_BLOCK)
    n_top = min(SEL_TOP, n_sel)
    bias_c = cmp_bias_table(rel_bias, jnp.arange(T), n_chunk)
    nb = near_bias_tables(rel_bias)
    ov = overlap_matrix(n_chunk, n_sel)
    ex = expand_matrix(T)
    h = NSA_HEADS
    full = lambda a: pl.BlockSpec(a.shape, lambda b, i: (0,) * a.ndim)
    seq = pl.BlockSpec((T, NSA_KV_DIM), lambda b, i: (b, 0))
    tok = pl.BlockSpec((1, n_chunk, NSA_KV_DIM), lambda b, i: (b, 0, 0))
    return pl.pallas_call(
        functools.partial(_nsa_prompt_body, n_sel=n_sel, n_top=n_top),
        grid=(B, nq),
        in_specs=[pl.BlockSpec((tq, NSA_Q_DIM), lambda b, i: (b * nq + i, 0)),
                  tok, tok, seq, seq, seq, seq,
                  pl.BlockSpec((h, tq, n_chunk), lambda b, i: (0, i, 0)),
                  full(nb),
                  pl.BlockSpec((tq, LANES), lambda b, i: (b * nq + i, 0)),
                  full(ov), full(ex)],
        out_specs=pl.BlockSpec((tq, NSA_Q_DIM), lambda b, i: (b * nq + i, 0)),
        out_shape=jax.ShapeDtypeStruct((B * T, NSA_Q_DIM), BF16),
        scratch_shapes=[pltpu.VMEM((NSA_KV_HEADS, tq, T), F32),
                        pltpu.VMEM((h, tq, 1), F32), pltpu.VMEM((h, tq, 1), F32),
                        pltpu.VMEM((h, tq, LANES), F32), pltpu.VMEM((h, tq, LANES), F32)],
        compiler_params=pltpu.CompilerParams(dimension_semantics=("parallel", "arbitrary"),
                                             vmem_limit_bytes=VMEM_LIMIT),
        name="nsa_prompt",
    )(q, kc, vc, skb, svb, wkb, wvb, bias_c, nb, small, ov, ex)


def _final_norm_body(x_ref, g_ref, o_ref):
    x = x_ref[...]
    o_ref[...] = x * lax.rsqrt(jnp.mean(x * x, axis=-1, keepdims=True) + EPS) * g_ref[...]


def final_rmsnorm(x, g, *, tm=512):
    shp = x.shape
    x2 = x.reshape(-1, shp[-1])
    n, d = x2.shape
    return pl.pallas_call(
        _final_norm_body,
        grid=(n // tm,),
        in_specs=[pl.BlockSpec((tm, d), lambda i: (i, 0)), pl.BlockSpec((1, d), lambda i: (0, 0))],
        out_specs=pl.BlockSpec((tm, d), lambda i: (i, 0)),
        out_shape=jax.ShapeDtypeStruct(x2.shape, x2.dtype),
        compiler_params=pltpu.CompilerParams(dimension_semantics=("parallel",)),
        name="final_rmsnorm",
    )(x2, g.reshape(1, d)).reshape(shp)


def permute_w_out_nsa(w_out):
    w = w_out[:NSA_Q_DIM].reshape(NSA_KV_HEADS, NSA_GROUP, HEAD_DIM, -1)
    return w.transpose(1, 0, 2, 3).reshape(NSA_Q_DIM, -1).astype(BF16)


def layer_prompt(x, c, lp, rel_bias, packed):
    B, T, _ = x.shape
    sh1, sc1, g1, sh2, sc2, g2 = modulation(c, lp)
    (q, ck, cv, sk, sv, wk, wv, skb, svb, wkb, wvb, qkv_raw, dg, small) = in_projection(
        x, lp['norm_mix_g'], sc1, sh1, packed['w_in'])
    kc, vc = compress_tokens(ck, cv, packed['cmp_k'], packed['cmp_v'], B)
    o_nsa = nsa_prompt(q, kc, vc, skb, svb, wkb, wvb, small, rel_bias, B, T)
    kv = lambda r: r.reshape(B, T, NSA_KV_HEADS, HEAD_DIM)
    ck, cv, sk, sv, wk, wv = kv(ck), kv(cv), kv(sk), kv(sv), kv(wk), kv(wv)
    conv_hist = jnp.zeros((B, CONV_W - 1, DN_CONV_DIM), x.dtype)
    s0 = jnp.zeros((B, DN_HEADS, DN_HEAD_DIM, DN_HEAD_DIM), x.dtype)
    bl = small[:, SM_BETA:SM_BETA + DN_HEADS].reshape(B, T, DN_HEADS)
    al = small[:, SM_DECAY:SM_DECAY + DN_HEADS].reshape(B, T, DN_HEADS)
    o_dn, s_new, conv_new = deltanet_mixer(qkv_raw.reshape(B, T, DN_CONV_DIM), bl, al, dg.reshape(B, T, DN_DIM),
                                           conv_hist, s0, lp)
    mix = jnp.dot(o_nsa, packed['w_out_nsa'], preferred_element_type=F32).reshape(B, T, -1) \
        + o_dn @ lp['w_out'][NSA_Q_DIM:]
    x = x + g1 * mix
    x = x + g2 * hier_moe(modulate(x, lp['norm_ffn_g'], sh2, sc2), lp)
    w_keep = min(WINDOW, T)
    return x, (ck, cv, sk, sv, wk[:, -w_keep:], wv[:, -w_keep:], s_new, conv_new)


def layer_sample(x, c, cmp_k_pool, cmp_v_pool, sel_k_pool, sel_v_pool, win_k, win_v, s0, conv_hist,
                 page_table, lp, rel_bias):
    B, T, _ = x.shape
    past = page_table.shape[1] * PAGE_SIZE
    sh1, sc1, g1, sh2, sc2, g2 = modulation(c, lp)
    h = modulate(x, lp['norm_mix_g'], sh1, sc1)
    q, ck, cv, sk, sv, wk, wv, gl, qkv_raw, bl, al, dg = split_projection(h, lp)
    kv = lambda r: r.reshape(B, T, NSA_KV_HEADS, HEAD_DIM)
    ck, cv, sk, sv, wk, wv = kv(ck), kv(cv), kv(sk), kv(sv), kv(wk), kv(wv)
    q = q.reshape(B, T, NSA_KV_HEADS, NSA_GROUP, HEAD_DIM)

    def with_past(pool, new):
        rows = pool[page_table].reshape(B, past, NSA_KV_HEADS, HEAD_DIM)
        return jnp.concatenate([rows, new], axis=1)

    pos = past + jnp.arange(T)
    o_cmp, o_sel = nsa_cmp_sel(q, pos, with_past(cmp_k_pool, ck), with_past(cmp_v_pool, cv),
                               with_past(sel_k_pool, sk), with_past(sel_v_pool, sv), lp, rel_bias)
    W = win_k.shape[1]
    wk_all = jnp.concatenate([win_k, wk], axis=1)
    wv_all = jnp.concatenate([win_v, wv], axis=1)
    o_win = window_sample(q, pos, wk_all, wv_all, past - W + jnp.arange(W + T), rel_bias)
    o_nsa = combine_branches(gl, o_cmp, o_sel, o_win)
    o_dn, s_new, conv_new = deltanet_mixer(qkv_raw, bl, al, dg, conv_hist, s0, lp)
    x = finish_layer(x, o_nsa, o_dn, g1, sh2, sc2, g2, lp)
    w_keep = min(WINDOW, W + T)
    return x, (ck, cv, sk, sv, wk_all[:, -w_keep:], wv_all[:, -w_keep:], s_new, conv_new)


def kernel(x_prompt, x_sample, cache_cmp_k, cache_cmp_v, cache_sel_k, cache_sel_v, cache_win_k, cache_win_v,
           state_delta, state_conv, page_table, c_prompt, c_sample, rel_bias, w_ada, b_ada, norm_mix_g,
           norm_ffn_g, w_in, cmp_pos_k, cmp_w1_k, cmp_w2_k, cmp_pos_v, cmp_w1_v, cmp_w2_v, conv_w, dn_a_log,
           dn_dt_bias, dn_norm_g, w_out, router_group_w, router_group_b, router_expert_w, router_expert_b,
           expert_w_gate, expert_w_up, expert_w_down, final_norm_g):
    xp, xs = x_prompt, x_sample
    new_p, new_s = [], []
    for l in range(DEPTH):
        lp = dict(w_ada=w_ada[l], b_ada=b_ada[l], norm_mix_g=norm_mix_g[l], norm_ffn_g=norm_ffn_g[l],
                  w_in=w_in[l], cmp_pos_k=cmp_pos_k[l], cmp_w1_k=cmp_w1_k[l], cmp_w2_k=cmp_w2_k[l],
                  cmp_pos_v=cmp_pos_v[l], cmp_w1_v=cmp_w1_v[l], cmp_w2_v=cmp_w2_v[l], conv_w=conv_w[l],
                  dn_a_log=dn_a_log[l], dn_dt_bias=dn_dt_bias[l], dn_norm_g=dn_norm_g[l], w_out=w_out[l],
                  router_group_w=router_group_w[l], router_group_b=router_group_b[l],
                  router_expert_w=router_expert_w[l], router_expert_b=router_expert_b[l],
                  w_gate=expert_w_gate[l], w_up=expert_w_up[l], w_down=expert_w_down[l])
        packed = dict(w_in=pack_w_in(w_in[l]),
                      cmp_k=pack_compress(cmp_pos_k[l], cmp_w1_k[l], cmp_w2_k[l]),
                      cmp_v=pack_compress(cmp_pos_v[l], cmp_w1_v[l], cmp_w2_v[l]),
                      w_out_nsa=permute_w_out_nsa(w_out[l]))
        xp, st_p = layer_prompt(xp, c_prompt, lp, rel_bias, packed)
        xs, st_s = layer_sample(xs, c_sample, cache_cmp_k[l], cache_cmp_v[l], cache_sel_k[l], cache_sel_v[l],
                                cache_win_k[l], cache_win_v[l], state_delta[l], state_conv[l], page_table,
                                lp, rel_bias)
        new_p.append(st_p)
        new_s.append(st_s)
    y_prompt = final_rmsnorm(xp, final_norm_g)
    y_sample = final_rmsnorm(xs, final_norm_g)

    def stk(states, i):
        return jnp.stack([s[i] for s in states])

    return (y_prompt, y_sample,
            stk(new_p, 0), stk(new_p, 1), stk(new_p, 2), stk(new_p, 3),
            stk(new_p, 4), stk(new_p, 5), stk(new_p, 6), stk(new_p, 7),
            stk(new_s, 0), stk(new_s, 1), stk(new_s, 2), stk(new_s, 3),
            stk(new_s, 4), stk(new_s, 5), stk(new_s, 6), stk(new_s, 7))
```

```python
import functools
import math

import jax
import jax.numpy as jnp
import numpy as np
from jax import lax
from jax.experimental import pallas as pl
from jax.experimental.pallas import tpu as pltpu

D_MODEL = 1024
DEPTH = 1
PAGE_SIZE = 128

NSA_HEADS = 8
NSA_KV_HEADS = 2
NSA_GROUP = NSA_HEADS // NSA_KV_HEADS
HEAD_DIM = 64
CMP_LEN = 32
CMP_STRIDE = 16
CMP_HIDDEN = 4 * HEAD_DIM
SEL_BLOCK = 64
SEL_TOP = 16
WINDOW = 512
WIN_Q_BLOCK = 128
SEL_Q_BLOCK = 64
DN_HEADS = 8
DN_HEAD_DIM = 64
CONV_W = 4
DN_CHUNK = 64
NUM_BUCKETS = 32
MAX_DISTANCE = 128
N_GROUPS = 4
EXPERTS_PER_GROUP = 8
N_EXPERTS = N_GROUPS * EXPERTS_PER_GROUP
TOP_K_IN_GROUP = 2
D_EXPERT = D_MODEL // 4

EPS = 1e-6
NEG_INF = -1e30
F32 = jnp.float32
BF16 = jnp.bfloat16

NSA_Q_DIM = NSA_HEADS * HEAD_DIM
NSA_KV_DIM = NSA_KV_HEADS * HEAD_DIM
DN_DIM = DN_HEADS * DN_HEAD_DIM
DN_CONV_DIM = 3 * DN_DIM
MIX_WIDTH = NSA_Q_DIM + DN_DIM
IN_SPLITS = (NSA_Q_DIM, NSA_KV_DIM, NSA_KV_DIM, NSA_KV_DIM, NSA_KV_DIM, NSA_KV_DIM, NSA_KV_DIM,
             3 * NSA_HEADS, DN_CONV_DIM, DN_HEADS, DN_HEADS, DN_DIM)
IN_DIM = sum(IN_SPLITS)

LANES = 128
VMEM_LIMIT = 56 * 1024 * 1024

ATT_TILE = 128
MASKED = -1e30


def rmsnorm(x, g):
    xf = x.astype(F32)
    y = xf * lax.rsqrt(jnp.mean(xf * xf, axis=-1, keepdims=True) + EPS)
    return (y * g.astype(F32)).astype(x.dtype)


def l2norm(x):
    xf = x.astype(F32)
    return (xf * lax.rsqrt(jnp.sum(xf * xf, axis=-1, keepdims=True) + EPS)).astype(x.dtype)


def t5_bucket(rel):
    n = jnp.maximum(rel, 0)
    max_exact = NUM_BUCKETS // 2
    nf = jnp.maximum(n, 1).astype(F32)
    large = max_exact + (jnp.log(nf / max_exact) / math.log(MAX_DISTANCE / max_exact)
                         * (NUM_BUCKETS - max_exact)).astype(jnp.int32)
    large = jnp.minimum(large, NUM_BUCKETS - 1)
    return jnp.where(n < max_exact, n, large)


def rel_bias_heads(rel, rel_bias):
    b = rel_bias.astype(F32)[t5_bucket(rel)]
    b = jnp.moveaxis(b, -1, -3)
    return b.reshape(b.shape[:-3] + (NSA_KV_HEADS, NSA_GROUP) + b.shape[-2:])


def attend(q, k, v, bias, mask):
    s = jnp.einsum('bntkgd,bnlkd->bnkgtl', q, k).astype(F32) * HEAD_DIM ** -0.5 + bias
    m = mask[:, None, None]
    p = jax.nn.softmax(jnp.where(m, s, NEG_INF), axis=-1)
    p = jnp.where(m, p, 0.0)
    o = jnp.einsum('bnkgtl,bnlkd->bntkgd', p.astype(v.dtype), v)
    return o, p


def compress_rows(rows, pos_emb, w1, w2):
    B, L = rows.shape[:2]
    n_cmp = (L - CMP_LEN) // CMP_STRIDE + 1
    idx = jnp.arange(n_cmp)[:, None] * CMP_STRIDE + jnp.arange(CMP_LEN)[None, :]
    blk = rows[:, idx] + pos_emb[:, None, :]
    blk = jnp.swapaxes(blk, 2, 3).reshape(B, n_cmp, NSA_KV_HEADS, CMP_LEN * HEAD_DIM)
    return jax.nn.silu(blk @ w1) @ w2


def nsa_cmp_sel(q, q_pos, kc_rows, vc_rows, ks_rows, vs_rows, lp, rel_bias):
    B, Tq = q.shape[:2]
    L = kc_rows.shape[1]
    kc = compress_rows(kc_rows, lp['cmp_pos_k'], lp['cmp_w1_k'], lp['cmp_w2_k'])
    vc = compress_rows(vc_rows, lp['cmp_pos_v'], lp['cmp_w1_v'], lp['cmp_w2_v'])
    n_cmp = kc.shape[1]
    cmp_end = jnp.arange(n_cmp) * CMP_STRIDE + CMP_LEN - 1
    rel_c = q_pos[:, None] - cmp_end[None, :]
    o_cmp, p_cmp = attend(q[:, None], kc[:, None], vc[:, None],
                          rel_bias_heads(rel_c[None], rel_bias), (rel_c >= 0)[None])
    o_cmp = o_cmp[:, 0]
    n_sel = -(-L // SEL_BLOCK)
    c_start = jnp.arange(n_cmp) * CMP_STRIDE
    s_start = jnp.arange(n_sel) * SEL_BLOCK
    overlap = jnp.clip(jnp.minimum(c_start[:, None] + CMP_LEN, s_start[None] + SEL_BLOCK)
                       - jnp.maximum(c_start[:, None], s_start[None]), 0, None).astype(F32) / CMP_LEN
    imp = jnp.einsum('bkgtc,cs->btks', p_cmp[:, 0], overlap)
    q_blk = q_pos // SEL_BLOCK
    sb = jnp.arange(n_sel)
    forced = (sb[None] == 0) | (sb[None] == q_blk[:, None]) | (sb[None] == q_blk[:, None] - 1)
    avail = s_start[None] <= q_pos[:, None]
    score = jnp.where(forced[None, :, None], 1e9, jnp.where(avail[None, :, None], imp, NEG_INF))
    n_top = min(SEL_TOP, n_sel)
    _, sel_idx = lax.top_k(score, n_top)
    pad = n_sel * SEL_BLOCK - L

    def to_blocks(r):
        r = jnp.pad(r, ((0, 0), (0, pad), (0, 0), (0, 0)))
        return r.reshape(B, n_sel, SEL_BLOCK, NSA_KV_HEADS, HEAD_DIM).transpose(0, 3, 1, 2, 4)
    ks_b = to_blocks(ks_rows)
    vs_b = to_blocks(vs_rows)
    bi = jnp.arange(B)[:, None, None, None]
    ki = jnp.arange(NSA_KV_HEADS)[None, None, :, None]
    tbl = rel_bias.astype(F32).reshape(NUM_BUCKETS, NSA_KV_HEADS, NSA_GROUP).transpose(1, 0, 2)
    qb = SEL_Q_BLOCK if Tq % SEL_Q_BLOCK == 0 else Tq
    nqb = Tq // qb

    def sel_block(args):
        q_i, pos_i, idx_i = args
        kg = ks_b[bi, ki, idx_i]
        vg = vs_b[bi, ki, idx_i]
        k_pos = idx_i[..., None] * SEL_BLOCK + jnp.arange(SEL_BLOCK)
        rel = pos_i[None, :, None, None, None] - k_pos
        bias = tbl[ki[..., None], t5_bucket(rel)]
        s = jnp.einsum('bqkgd,bqknsd->bqkgns', q_i, kg).astype(F32) * HEAD_DIM ** -0.5 + jnp.moveaxis(bias, -1, 3)
        mask = (rel >= 0)[:, :, :, None]
        s = jnp.where(mask, s, NEG_INF).reshape(B, qb, NSA_KV_HEADS, NSA_GROUP, n_top * SEL_BLOCK)
        p = jax.nn.softmax(s, axis=-1)
        p = jnp.where(mask.reshape(B, qb, NSA_KV_HEADS, 1, n_top * SEL_BLOCK), p, 0.0)
        return jnp.einsum('bqkgl,bqkld->bqkgd', p.astype(vg.dtype),
                          vg.reshape(B, qb, NSA_KV_HEADS, n_top * SEL_BLOCK, HEAD_DIM))

    q_m = q.reshape(B, nqb, qb, NSA_KV_HEADS, NSA_GROUP, HEAD_DIM).swapaxes(0, 1)
    pos_m = q_pos.reshape(nqb, qb)
    idx_m = sel_idx.reshape(B, nqb, qb, NSA_KV_HEADS, n_top).swapaxes(0, 1)
    o_sel = lax.map(sel_block, (q_m, pos_m, idx_m))
    o_sel = o_sel.swapaxes(0, 1).reshape(B, Tq, NSA_KV_HEADS, NSA_GROUP, HEAD_DIM)
    return o_cmp, o_sel


def window_sample(q, q_pos, k, v, k_pos, rel_bias):
    rel = q_pos[:, None] - k_pos[None]
    mask = (rel >= 0) & (rel < WINDOW)
    o, _ = attend(q[:, None], k[:, None], v[:, None], rel_bias_heads(rel[None], rel_bias), mask[None])
    return o[:, 0]


def combine_branches(gate_logit, o_cmp, o_sel, o_win):
    B, T = gate_logit.shape[:2]
    g = jax.nn.sigmoid(gate_logit.astype(F32)).reshape(B, T, NSA_KV_HEADS, NSA_GROUP, 3).astype(o_cmp.dtype)
    o = g[..., 0:1] * o_cmp + g[..., 1:2] * o_sel + g[..., 2:3] * o_win
    return o.reshape(B, T, NSA_Q_DIM)


def causal_conv(x_hist, x_new, w):
    xc = jnp.concatenate([x_hist, x_new], axis=1)
    y = lax.conv_general_dilated(xc, w[:, None, :], window_strides=(1,), padding='VALID',
                                 dimension_numbers=('NWC', 'WIO', 'NWC'), feature_group_count=xc.shape[-1])
    return jax.nn.silu(y), xc[:, -(CONV_W - 1):]


def gated_delta_rule(q, k, v, g, beta, s0):
    B, T, H, DK = q.shape
    C = DN_CHUNK if T >= DN_CHUNK else T
    n = -(-T // C)
    pad = n * C - T

    def prep(a):
        a = jnp.pad(a.astype(F32), ((0, 0), (0, pad)) + ((0, 0),) * (a.ndim - 2))
        a = a.reshape((B, n, C) + a.shape[2:])
        return jnp.moveaxis(a, (1, 3), (0, 2))

    qc, kc, vc, gc, bc = prep(q), prep(k), prep(v), prep(g), prep(beta)
    gcum = jnp.cumsum(gc, axis=-1)
    ii = jnp.arange(C)
    incl = ii[:, None] >= ii[None, :]
    strict = ii[:, None] > ii[None, :]
    diff = gcum[..., :, None] - gcum[..., None, :]
    decay = jnp.where(incl, jnp.exp(jnp.where(incl, diff, 0.0)), 0.0)
    kb = kc * bc[..., None]
    a_mat = jnp.where(strict, jnp.einsum('...id,...jd->...ij', kb, kc) * decay, 0.0)
    eye_plus = a_mat + jnp.eye(C, dtype=F32)
    u = lax.linalg.triangular_solve(eye_plus, vc * bc[..., None], left_side=True, lower=True, unit_diagonal=True)
    w = lax.linalg.triangular_solve(eye_plus, kb * jnp.exp(gcum)[..., None], left_side=True, lower=True, unit_diagonal=True)
    qk = jnp.where(incl, jnp.einsum('...id,...jd->...ij', qc, kc) * decay, 0.0)
    qg = qc * jnp.exp(gcum)[..., None]
    kg = kc * jnp.exp(gcum[..., -1:] - gcum)[..., None]
    glast = jnp.exp(gcum[..., -1])

    def step(S, xs):
        u_i, w_i, qk_i, qg_i, kg_i, gl_i = xs
        v_new = u_i - jnp.einsum('bhck,bhkv->bhcv', w_i, S)
        o = jnp.einsum('bhck,bhkv->bhcv', qg_i, S) + jnp.einsum('bhij,bhjv->bhiv', qk_i, v_new)
        S = S * gl_i[..., None, None] + jnp.einsum('bhck,bhcv->bhkv', kg_i, v_new)
        return S, o

    S, o = lax.scan(step, s0.astype(F32), (u, w, qk, qg, kg, glast))
    o = o.transpose(1, 0, 3, 2, 4).reshape(B, n * C, H, v.shape[-1])[:, :T]
    return o, S


def deltanet_mixer(qkv_raw, b_logit, a_logit, gate, conv_hist, s0, lp):
    B, T = qkv_raw.shape[:2]
    qkv, new_hist = causal_conv(conv_hist, qkv_raw, lp['conv_w'])
    q, k, v = jnp.split(qkv, 3, axis=-1)
    q = l2norm(q.reshape(B, T, DN_HEADS, DN_HEAD_DIM)) * DN_HEAD_DIM ** -0.5
    k = l2norm(k.reshape(B, T, DN_HEADS, DN_HEAD_DIM))
    v = v.reshape(B, T, DN_HEADS, DN_HEAD_DIM)
    beta = jax.nn.sigmoid(b_logit.astype(F32))
    g = -jnp.exp(lp['dn_a_log'].astype(F32)) * jax.nn.softplus(a_logit.astype(F32) + lp['dn_dt_bias'].astype(F32))
    o, s_new = gated_delta_rule(q, k, v, g, beta, s0)
    o = rmsnorm(o, lp['dn_norm_g']) * jax.nn.silu(gate.astype(F32).reshape(B, T, DN_HEADS, DN_HEAD_DIM))
    return o.astype(qkv_raw.dtype).reshape(B, T, DN_DIM), s_new.astype(s0.dtype), new_hist


def hier_moe(h, lp):
    B, T, D = h.shape
    x = h.reshape(B * T, D)
    lg = (x @ lp['router_group_w'] + lp['router_group_b']).astype(F32)
    pg = jax.nn.softmax(lg, axis=-1)
    grp = jnp.argmax(lg, axis=-1)
    wg = jnp.take_along_axis(pg, grp[:, None], axis=-1)
    le = (x @ lp['router_expert_w'] + lp['router_expert_b']).astype(F32).reshape(-1, N_GROUPS, EXPERTS_PER_GROUP)
    le = jnp.take_along_axis(le, grp[:, None, None], axis=1)[:, 0]
    pe, ie = lax.top_k(jax.nn.softmax(le, axis=-1), TOP_K_IN_GROUP)
    pe = pe / jnp.sum(pe, axis=-1, keepdims=True)
    eid = grp[:, None] * EXPERTS_PER_GROUP + ie
    gates = jnp.sum(jax.nn.one_hot(eid, N_EXPERTS, dtype=F32) * (wg * pe)[..., None], axis=1).astype(x.dtype)
    y = jnp.zeros_like(x)
    for e in range(N_EXPERTS):
        he = jax.nn.silu(x @ lp['w_gate'][e]) * (x @ lp['w_up'][e])
        y = y + gates[:, e:e + 1] * (he @ lp['w_down'][e])
    return y.reshape(B, T, D)


def modulation(c, lp):
    mod = jax.nn.silu(c) @ lp['w_ada'] + lp['b_ada']
    return jnp.split(mod[:, None, :], 6, axis=-1)


def modulate(x, g, shift, scale):
    return rmsnorm(x, g) * (1 + scale) + shift


def split_projection(h, lp):
    z = h @ lp['w_in']
    return jnp.split(z, np.cumsum(IN_SPLITS)[:-1].tolist(), axis=-1)


def finish_layer(x, o_nsa, o_dn, g1, sh2, sc2, g2, lp):
    x = x + g1 * (jnp.concatenate([o_nsa, o_dn], axis=-1) @ lp['w_out'])
    return x + g2 * hier_moe(modulate(x, lp['norm_ffn_g'], sh2, sc2), lp)


_PK_Q = (0, NSA_Q_DIM)
_PK_KV = (_PK_Q[1], _PK_Q[1] + 6 * NSA_KV_DIM)
_PK_QKV = (_PK_KV[1], _PK_KV[1] + DN_CONV_DIM)
_PK_DG = (_PK_QKV[1], _PK_QKV[1] + DN_DIM)
_PK_SM = (_PK_DG[1], _PK_DG[1] + LANES)
PK_DIM = _PK_SM[1]
SM_GATE = 0
SM_BETA = 3 * NSA_HEADS
SM_DECAY = SM_BETA + DN_HEADS


def pack_w_in(w_in):
    offs = np.cumsum((0,) + IN_SPLITS)
    part = lambda i: w_in[:, offs[i]:offs[i + 1]]
    d = w_in.shape[0]
    q = part(0).reshape(d, NSA_KV_HEADS, NSA_GROUP, HEAD_DIM).transpose(0, 2, 1, 3).reshape(d, NSA_Q_DIM)
    q = q * HEAD_DIM ** -0.5
    small = jnp.concatenate([part(7), part(9), part(10)], axis=1)
    small = jnp.pad(small, ((0, 0), (0, LANES - small.shape[1])))
    cols = [q] + [part(i) for i in range(1, 7)] + [part(8), part(11), small]
    return jnp.concatenate(cols, axis=1).astype(BF16)


def _in_proj_body(x_ref, g_ref, sc_ref, sh_ref, w_ref, q_ref, ck_ref, cv_ref, sk_ref, sv_ref, wk_ref, wv_ref,
                  skb_ref, svb_ref, wkb_ref, wvb_ref, qkv_ref, dg_ref, sm_ref):
    x = x_ref[...]
    tm, d = x.shape
    nb = sc_ref.shape[0]
    xn = x * lax.rsqrt(jnp.mean(x * x, axis=-1, keepdims=True) + EPS) * g_ref[...]
    h = xn.reshape(nb, tm // nb, d) * (1 + sc_ref[...]) + sh_ref[...]
    h = h.reshape(tm, d).astype(BF16)

    def mm(span):
        return jnp.dot(h, w_ref[:, span[0]:span[1]], preferred_element_type=F32)

    q_ref[...] = mm(_PK_Q).astype(BF16)
    kv = mm(_PK_KV)
    for i, r in enumerate((ck_ref, cv_ref, sk_ref, sv_ref, wk_ref, wv_ref)):
        r[...] = kv[:, i * NSA_KV_DIM:(i + 1) * NSA_KV_DIM]
    for i, r in enumerate((skb_ref, svb_ref, wkb_ref, wvb_ref)):
        r[...] = kv[:, (i + 2) * NSA_KV_DIM:(i + 3) * NSA_KV_DIM].astype(BF16)
    qkv_ref[...] = mm(_PK_QKV)
    dg_ref[...] = mm(_PK_DG)
    sm_ref[...] = mm(_PK_SM)


def in_projection(x, norm_g, scale, shift, w_packed, *, tm=256):
    B, T, D = x.shape
    n = B * T
    tm = min(tm, n)
    rows = max(tm // T, 1)
    per = max(T // tm, 1)
    row = lambda i: (i, 0)
    outs = [((n, NSA_Q_DIM), BF16)] + [((n, NSA_KV_DIM), F32)] * 6 + [((n, NSA_KV_DIM), BF16)] * 4 + \
           [((n, DN_CONV_DIM), F32), ((n, DN_DIM), F32), ((n, LANES), F32)]
    return pl.pallas_call(
        _in_proj_body,
        grid=(n // tm,),
        in_specs=[pl.BlockSpec((tm, D), row),
                  pl.BlockSpec((1, D), lambda i: (0, 0)),
                  pl.BlockSpec((rows, 1, D), lambda i: (i // per, 0, 0)),
                  pl.BlockSpec((rows, 1, D), lambda i: (i // per, 0, 0)),
                  pl.BlockSpec((D, PK_DIM), lambda i: (0, 0))],
        out_specs=[pl.BlockSpec((tm, s[1]), row) for s, _ in outs],
        out_shape=[jax.ShapeDtypeStruct(s, dt) for s, dt in outs],
        compiler_params=pltpu.CompilerParams(dimension_semantics=("parallel",), vmem_limit_bytes=VMEM_LIMIT),
        name="in_projection",
    )(x.reshape(n, D), norm_g.reshape(1, D), scale, shift, w_packed)


CHUNK_W = CMP_STRIDE * NSA_KV_DIM
CMP_HALVES = CMP_LEN // CMP_STRIDE


def pack_compress(pos, w1, w2):
    eye = jnp.eye(NSA_KV_HEADS, dtype=F32)
    w1r = w1.reshape(CMP_HALVES, CMP_STRIDE, HEAD_DIM, CMP_HIDDEN)
    w1p = jnp.einsum('ijdc,kl->ijkdlc', w1r, eye).reshape(CMP_HALVES, CHUNK_W, NSA_KV_HEADS * CMP_HIDDEN)
    w2p = jnp.einsum('cd,kl->kcld', w2, eye).reshape(NSA_KV_HEADS * CMP_HIDDEN, NSA_KV_DIM)
    posp = jnp.broadcast_to(pos.reshape(CMP_HALVES, CMP_STRIDE, 1, HEAD_DIM),
                            (CMP_HALVES, CMP_STRIDE, NSA_KV_HEADS, HEAD_DIM)).reshape(CMP_HALVES, CHUNK_W)
    return posp, w1p.astype(BF16), w2p.astype(BF16)


def _compress_one(x, pos_ref, w1_ref, w2_ref):
    n_chunk = x.shape[0]
    hid = None
    for i in range(CMP_HALVES):
        part = jnp.dot((x + pos_ref[i:i + 1]).astype(BF16), w1_ref[i], preferred_element_type=F32)
        if i:
            part = pltpu.roll(part, shift=n_chunk - i, axis=0)
        hid = part if hid is None else hid + part
    hid = hid * jax.nn.sigmoid(hid)
    return jnp.dot(hid.astype(BF16), w2_ref[...], preferred_element_type=F32)


def _compress_body(xk_ref, xv_ref, pk_ref, pv_ref, w1k_ref, w1v_ref, w2k_ref, w2v_ref, kc_ref, vc_ref):
    kc_ref[0] = _compress_one(xk_ref[0], pk_ref, w1k_ref, w2k_ref).astype(BF16)
    vc_ref[0] = _compress_one(xv_ref[0], pv_ref, w1v_ref, w2v_ref).astype(BF16)


def compress_tokens(ck, cv, pk, pv, B):
    n_chunk = ck.shape[0] // B // CMP_STRIDE
    xk = ck.reshape(B, n_chunk, CHUNK_W)
    xv = cv.reshape(B, n_chunk, CHUNK_W)
    full = lambda a: pl.BlockSpec(a.shape, lambda b: (0,) * a.ndim)
    blk = lambda w: pl.BlockSpec((1, n_chunk, w), lambda b: (b, 0, 0))
    return pl.pallas_call(
        _compress_body,
        grid=(B,),
        in_specs=[blk(CHUNK_W), blk(CHUNK_W), full(pk[0]), full(pv[0]), full(pk[1]), full(pv[1]),
                  full(pk[2]), full(pv[2])],
        out_specs=[blk(NSA_KV_DIM), blk(NSA_KV_DIM)],
        out_shape=[jax.ShapeDtypeStruct((B, n_chunk, NSA_KV_DIM), BF16)] * 2,
        compiler_params=pltpu.CompilerParams(dimension_semantics=("parallel",), vmem_limit_bytes=VMEM_LIMIT),
        name="compress_tokens",
    )(xk, xv, pk[0], pv[0], pk[1], pv[1], pk[2], pv[2])


def _t5_bucket_np(rel):
    n = np.maximum(rel, 0)
    max_exact = NUM_BUCKETS // 2
    nf = np.maximum(n, 1).astype(np.float32)
    large = max_exact + (np.log(nf / max_exact) / math.log(MAX_DISTANCE / max_exact)
                         * (NUM_BUCKETS - max_exact)).astype(np.int32)
    return np.where(n < max_exact, n, np.minimum(large, NUM_BUCKETS - 1))


assert (_t5_bucket_np(np.arange(ATT_TILE + 1, 1 << 16)) == NUM_BUCKETS - 1).all()


def near_bias_tables(rel_bias):
    t = ATT_TILE
    i = jnp.arange(t)[:, None]
    j = jnp.arange(t)[None, :]
    tbl = rel_bias.astype(F32).T
    diag = jnp.where(i >= j, tbl[:, t5_bucket(i - j)], MASKED)
    prev = tbl[:, t5_bucket(t + i - j)]
    far = jnp.broadcast_to(tbl[:, NUM_BUCKETS - 1][:, None, None], prev.shape)
    edge = jnp.where(j > i, far, MASKED)
    return jnp.stack([diag, prev, far, edge])


def cmp_bias_table(rel_bias, q_pos, n_chunk):
    cmp_end = jnp.arange(n_chunk) * CMP_STRIDE + CMP_LEN - 1
    rel = q_pos[:, None] - cmp_end[None, :]
    tbl = rel_bias.astype(F32).T
    return jnp.where(rel >= 0, tbl[:, t5_bucket(rel)], MASKED)


def overlap_matrix(n_chunk, n_sel):
    c0 = np.arange(n_chunk)[:, None] * CMP_STRIDE
    s0 = np.arange(LANES)[None, :] * SEL_BLOCK
    ov = np.clip(np.minimum(c0 + CMP_LEN, s0 + SEL_BLOCK) - np.maximum(c0, s0), 0, None) / CMP_LEN
    ov = np.where(np.arange(LANES)[None, :] < n_sel, ov, 0.0)
    return jnp.asarray(ov, F32)


def expand_matrix(n_keys):
    e = (np.arange(n_keys)[None, :] // SEL_BLOCK) == np.arange(LANES)[:, None]
    return jnp.asarray(e, BF16)


def _stack_queries(q_ref, tq):
    low = lax.broadcasted_iota(jnp.int32, (tq, LANES), 1) < HEAD_DIM
    zero = jnp.zeros((tq, LANES), BF16)
    parts = []
    for kv in range(NSA_KV_HEADS):
        for g in range(NSA_GROUP):
            blk = q_ref[:, g * LANES:(g + 1) * LANES]
            parts.append(jnp.where(low if kv == 0 else ~low, blk, zero))
    return jnp.concatenate(parts, axis=0)


def _dot_nt(a, b):
    return lax.dot_general(a, b, (((1,), (1,)), ((), ())), preferred_element_type=F32)


def _select_blocks(imp, q_pos, n_sel, n_top):
    lane = lax.broadcasted_iota(jnp.int32, imp.shape, 1)
    q_blk = q_pos // SEL_BLOCK
    forced = jnp.where(lane == 0, 1.0, jnp.where(lane == q_blk, 1.0, jnp.where(lane == q_blk - 1, 1.0, 0.0)))
    score = jnp.where(lane * SEL_BLOCK <= q_pos, imp, NEG_INF)
    score = jnp.where(forced > 0.5, 1e9, score)
    score = jnp.where(lane < n_sel, score, -3e38)
    rank = jnp.zeros(imp.shape, F32)
    for s in range(n_sel):
        col = score[:, s:s + 1]
        tie = jnp.where(lane > s, 1.0, 0.0)
        rank = rank + jnp.where(col > score, 1.0, jnp.where(col == score, tie, 0.0))
    return jnp.where(rank < n_top, 1.0, 0.0)


def _softmax_tile(q, k_ref, v_ref, kt, bias, mask_ref, m_ref, l_ref, acc_ref):
    tk = ATT_TILE
    h, tq = acc_ref.shape[:2]
    rows = pl.ds(pl.multiple_of(kt * tk, tk), tk)
    s = _dot_nt(q, k_ref[rows, :]).reshape(h, tq, tk) + bias
    if mask_ref is not None:
        mb = mask_ref[:, :, pl.ds(pl.multiple_of(kt * tk, tk), tk)]
        s = (s.reshape(NSA_KV_HEADS, NSA_GROUP, tq, tk) + mb[:, None]).reshape(h, tq, tk)
    m_old = m_ref[...]
    m_new = jnp.maximum(m_old, jnp.max(s, axis=-1, keepdims=True))
    alpha = jnp.exp(m_old - m_new)
    p = jnp.exp(s - m_new)
    l_ref[...] = alpha * l_ref[...] + jnp.sum(p, axis=-1, keepdims=True)
    pv = jnp.dot(p.reshape(h * tq, tk).astype(BF16), v_ref[rows, :], preferred_element_type=F32)
    acc_ref[...] = alpha * acc_ref[...] + pv.reshape(h, tq, LANES)
    m_ref[...] = m_new


def _reset_softmax(m_ref, l_ref, acc_ref):
    m_ref[...] = jnp.full(m_ref.shape, MASKED, F32)
    l_ref[...] = jnp.zeros(l_ref.shape, F32)
    acc_ref[...] = jnp.zeros(acc_ref.shape, F32)


def _gate_columns(gl_ref, j):
    g = jax.nn.sigmoid(gl_ref[...])
    cols = [g[:, SM_GATE + 3 * h + j:SM_GATE + 3 * h + j + 1] for h in range(NSA_HEADS)]
    return jnp.stack(cols, axis=0)


def _nsa_prompt_body(q_ref, kc_ref, vc_ref, sk_ref, sv_ref, wk_ref, wv_ref, bc_ref, nb_ref, gl_ref, ov_ref, ex_ref,
                     o_ref, mb_ref, m_ref, l_ref, acc_ref, out_ref, *, n_sel, n_top):
    i = pl.program_id(1)
    tq = q_ref.shape[0]
    h = NSA_HEADS
    q = _stack_queries(q_ref, tq)

    s = _dot_nt(q, kc_ref[0]).reshape(h, tq, -1) + bc_ref[...]
    m = jnp.max(s, axis=-1, keepdims=True)
    p = jnp.where(s > 0.5 * MASKED, jnp.exp(s - m), 0.0)
    l = jnp.sum(p, axis=-1, keepdims=True)
    p = p * (1.0 / jnp.where(l > 0.0, l, 1.0))
    o_cmp = jnp.dot(p.reshape(h * tq, -1).astype(BF16), vc_ref[0], preferred_element_type=F32)
    out_ref[...] = _gate_columns(gl_ref, 0) * o_cmp.reshape(h, tq, LANES)

    q_pos = i * tq + lax.broadcasted_iota(jnp.int32, (tq, LANES), 0)
    for kv in range(NSA_KV_HEADS):
        pk = p[kv * NSA_GROUP]
        for g in range(1, NSA_GROUP):
            pk = pk + p[kv * NSA_GROUP + g]
        imp = jnp.dot(pk, ov_ref[...], precision=lax.Precision.HIGHEST, preferred_element_type=F32)
        sel = _select_blocks(imp, q_pos, n_sel, n_top)
        keys = jnp.dot(sel.astype(BF16), ex_ref[...], preferred_element_type=F32)
        mb_ref[kv] = (keys - 1.0) * (-MASKED)

    _reset_softmax(m_ref, l_ref, acc_ref)

    def sel_step(kt, carry):
        _softmax_tile(q, sk_ref, sv_ref, kt, nb_ref[jnp.minimum(i - kt, 2)], mb_ref, m_ref, l_ref, acc_ref)
        return carry
    lax.fori_loop(0, i + 1, sel_step, 0)
    out_ref[...] += _gate_columns(gl_ref, 1) * (acc_ref[...] * (1.0 / l_ref[...]))

    _reset_softmax(m_ref, l_ref, acc_ref)
    n_back = WINDOW // ATT_TILE
    for r in range(n_back + 1):
        table = 3 if r == n_back else min(r, 2)

        @pl.when(i >= r)
        def _():
            _softmax_tile(q, wk_ref, wv_ref, i - r, nb_ref[table], None, m_ref, l_ref, acc_ref)
    out_ref[...] += _gate_columns(gl_ref, 2) * (acc_ref[...] * (1.0 / l_ref[...]))

    low = lax.broadcasted_iota(jnp.int32, (tq, LANES), 1) < HEAD_DIM
    for g in range(NSA_GROUP):
        o_ref[:, g * LANES:(g + 1) * LANES] = jnp.where(low, out_ref[g], out_ref[NSA_GROUP + g]).astype(BF16)


def nsa_prompt(q, kc, vc, skb, svb, wkb, wvb, small, rel_bias, B, T):
    tq = ATT_TILE
    nq = T // tq
    n_chunk = kc.shape[1]
    n_sel = -(-T // SEL_BLOCK)
    n_top = min(SEL_TOP, n_sel)
    bias_c = cmp_bias_table(rel_bias, jnp.arange(T), n_chunk)
    nb = near_bias_tables(rel_bias)
    ov = overlap_matrix(n_chunk, n_sel)
    ex = expand_matrix(T)
    h = NSA_HEADS
    full = lambda a: pl.BlockSpec(a.shape, lambda b, i: (0,) * a.ndim)
    seq = pl.BlockSpec((T, NSA_KV_DIM), lambda b, i: (b, 0))
    tok = pl.BlockSpec((1, n_chunk, NSA_KV_DIM), lambda b, i: (b, 0, 0))
    return pl.pallas_call(
        functools.partial(_nsa_prompt_body, n_sel=n_sel, n_top=n_top),
        grid=(B, nq),
        in_specs=[pl.BlockSpec((tq, NSA_Q_DIM), lambda b, i: (b * nq + i, 0)),
                  tok, tok, seq, seq, seq, seq,
                  pl.BlockSpec((h, tq, n_chunk), lambda b, i: (0, i, 0)),
                  full(nb),
                  pl.BlockSpec((tq, LANES), lambda b, i: (b * nq + i, 0)),
                  full(ov), full(ex)],
        out_specs=pl.BlockSpec((tq, NSA_Q_DIM), lambda b, i: (b * nq + i, 0)),
        out_shape=jax.ShapeDtypeStruct((B * T, NSA_Q_DIM), BF16),
        scratch_shapes=[pltpu.VMEM((NSA_KV_HEADS, tq, T), F32),
                        pltpu.VMEM((h, tq, 1), F32), pltpu.VMEM((h, tq, 1), F32),
                        pltpu.VMEM((h, tq, LANES), F32), pltpu.VMEM((h, tq, LANES), F32)],
        compiler_params=pltpu.CompilerParams(dimension_semantics=("parallel", "arbitrary"),
                                             vmem_limit_bytes=VMEM_LIMIT),
        name="nsa_prompt",
    )(q, kc, vc, skb, svb, wkb, wvb, bias_c, nb, small, ov, ex)


def sample_bias_tables(rel_bias, past, T, W):
    tbl = rel_bias.astype(F32).T
    t = jnp.arange(T)[:, None]
    pad = jnp.arange(ATT_TILE)[None, :]
    new = jnp.where((pad < T) & (t >= pad), tbl[:, t5_bucket(t - pad)], MASKED)
    sel = tbl[:, t5_bucket(past + t - jnp.arange(past)[None, :])]
    rel_w = W + t - jnp.arange(W)[None, :]
    win = jnp.where(rel_w < WINDOW, tbl[:, t5_bucket(rel_w)], MASKED)
    cmp = cmp_bias_table(rel_bias, past + jnp.arange(T), past // CMP_STRIDE)
    return cmp, jnp.concatenate([sel, new], axis=-1), jnp.concatenate([win, new], axis=-1)


def _softmax_rows(s):
    m = jnp.max(s, axis=-1, keepdims=True)
    p = jnp.exp(s - m)
    return p * (1.0 / jnp.sum(p, axis=-1, keepdims=True))


def _nsa_sample_body(pt_ref, q_ref, skn_ref, svn_ref, wkn_ref, wvn_ref, gl_ref, wink_ref, winv_ref,
                     bc_ref, bs_ref, bw_ref, ov_ref, ex_ref, pk_ref, pv_ref, w1k_ref, w1v_ref, w2k_ref, w2v_ref,
                     *rest, n_pages, n_sel, n_top, past):
    pages = [rest[i * n_pages:(i + 1) * n_pages] for i in range(4)]
    o_ref, wko_ref, wvo_ref, s_ref, new_ref = rest[4 * n_pages:]
    h = NSA_HEADS
    tq = q_ref.shape[0]
    tk = ATT_TILE
    q = _stack_queries(q_ref, tq)

    new_ref[...] = jnp.zeros(new_ref.shape, F32)
    for n, r in enumerate((skn_ref, svn_ref, wkn_ref, wvn_ref)):
        new_ref[n, 0:tq, :] = r[...]

    xk = jnp.concatenate([r[0] for r in pages[0]], axis=0)
    xv = jnp.concatenate([r[0] for r in pages[1]], axis=0)
    kc = _compress_one(xk, pk_ref, w1k_ref, w2k_ref).astype(BF16)
    vc = _compress_one(xv, pv_ref, w1v_ref, w2v_ref).astype(BF16)
    s = _dot_nt(q, kc).reshape(h, tq, -1) + bc_ref[...]
    p = _softmax_rows(s)
    o_cmp = jnp.dot(p.reshape(h * tq, -1).astype(BF16), vc, preferred_element_type=F32)
    out = _gate_columns(gl_ref, 0) * o_cmp.reshape(h, tq, LANES)

    q_pos = past + lax.broadcasted_iota(jnp.int32, (tq, LANES), 0)
    masks = []
    for kv in range(NSA_KV_HEADS):
        pkv = p[kv * NSA_GROUP]
        for g in range(1, NSA_GROUP):
            pkv = pkv + p[kv * NSA_GROUP + g]
        imp = jnp.dot(pkv, ov_ref[...], precision=lax.Precision.HIGHEST, preferred_element_type=F32)
        sel = _select_blocks(imp, q_pos, n_sel, n_top)
        keys = jnp.dot(sel.astype(BF16), ex_ref[...], preferred_element_type=F32)
        masks.append((keys - 1.0) * (-MASKED))
    mask = jnp.stack(masks, axis=0)
    for pg in range(n_pages):
        s_ref[:, pg * tk:(pg + 1) * tk] = _dot_nt(q, pages[2][pg][0].astype(BF16))
    s_ref[:, n_pages * tk:(n_pages + 1) * tk] = _dot_nt(q, new_ref[0].astype(BF16))
    width = (n_pages + 1) * tk
    s = s_ref[...].reshape(h, tq, width) + bs_ref[...]
    s = (s.reshape(NSA_KV_HEADS, NSA_GROUP, tq, width) + mask[:, None]).reshape(h, tq, width)
    p = _softmax_rows(s).reshape(h * tq, width).astype(BF16)
    o_sel = jnp.dot(p[:, n_pages * tk:], new_ref[1].astype(BF16), preferred_element_type=F32)
    for pg in range(n_pages):
        o_sel = o_sel + jnp.dot(p[:, pg * tk:(pg + 1) * tk], pages[3][pg][0].astype(BF16),
                                preferred_element_type=F32)
    out = out + _gate_columns(gl_ref, 1) * o_sel.reshape(h, tq, LANES)

    w = wink_ref.shape[1]
    n_wt = w // tk
    for wt in range(n_wt):
        s_ref[:, wt * tk:(wt + 1) * tk] = _dot_nt(q, wink_ref[0, wt * tk:(wt + 1) * tk, :].astype(BF16))
    s_ref[:, w:w + tk] = _dot_nt(q, new_ref[2].astype(BF16))
    s = s_ref[:, 0:w + tk].reshape(h, tq, w + tk) + bw_ref[...]
    p = _softmax_rows(s).reshape(h * tq, w + tk).astype(BF16)
    o_win = jnp.dot(p[:, w:], new_ref[3].astype(BF16), preferred_element_type=F32)
    for wt in range(n_wt):
        o_win = o_win + jnp.dot(p[:, wt * tk:(wt + 1) * tk], winv_ref[0, wt * tk:(wt + 1) * tk, :].astype(BF16),
                                preferred_element_type=F32)
    out = out + _gate_columns(gl_ref, 2) * o_win.reshape(h, tq, LANES)

    low = lax.broadcasted_iota(jnp.int32, (tq, LANES), 1) < HEAD_DIM
    for g in range(NSA_GROUP):
        o_ref[:, g * LANES:(g + 1) * LANES] = jnp.where(low, out[g], out[NSA_GROUP + g])

    keep = wko_ref.shape[1] - tq
    wko_ref[0, 0:keep, :] = wink_ref[0, w - keep:w, :]
    wko_ref[0, keep:keep + tq, :] = wkn_ref[...]
    wvo_ref[0, 0:keep, :] = winv_ref[0, w - keep:w, :]
    wvo_ref[0, keep:keep + tq, :] = wvn_ref[...]


def nsa_sample(q, skn, svn, wkn, wvn, small, pools, win_k, win_v, page_table, rel_bias, pk, pv, B, T):
    n_pages = page_table.shape[1]
    past = n_pages * PAGE_SIZE
    W = win_k.shape[1]
    L = past + T
    n_cmp = (L - CMP_LEN) // CMP_STRIDE + 1
    assert PAGE_SIZE == ATT_TILE and T % 8 == 0 and T <= SEL_BLOCK and past % SEL_BLOCK == 0
    assert (n_cmp - 1) * CMP_STRIDE + CMP_LEN <= past and n_cmp >= past // CMP_STRIDE - CMP_HALVES + 1
    assert W % ATT_TILE == 0 and W + T >= WINDOW
    n_sel = -(-L // SEL_BLOCK)
    n_top = min(SEL_TOP, n_sel)
    w_keep = min(WINDOW, W + T)
    bc, bs, bw = sample_bias_tables(rel_bias, past, T, W)
    ov = overlap_matrix(past // CMP_STRIDE, n_sel)
    ex = expand_matrix(past + ATT_TILE)
    n_pool = pools[0].shape[0]
    chunks = PAGE_SIZE // CMP_STRIDE
    cmp_pools = [p.reshape(n_pool, chunks, CHUNK_W) for p in pools[:2]]
    sel_pools = [p.reshape(n_pool, PAGE_SIZE, NSA_KV_DIM) for p in pools[2:]]
    wk3 = win_k.reshape(B, W, NSA_KV_DIM)
    wv3 = win_v.reshape(B, W, NSA_KV_DIM)

    full = lambda a: pl.BlockSpec(a.shape, lambda b, pt: (0,) * a.ndim)
    row = lambda w: pl.BlockSpec((T, w), lambda b, pt: (b, 0))
    seq = lambda n: pl.BlockSpec((1, n, NSA_KV_DIM), lambda b, pt: (b, 0, 0))
    page = lambda shape, pg: pl.BlockSpec((1,) + shape, lambda b, pt: (pt[b, pg], 0, 0))
    page_specs, page_args = [], []
    for pool, shape in ((cmp_pools[0], (chunks, CHUNK_W)), (cmp_pools[1], (chunks, CHUNK_W)),
                        (sel_pools[0], (PAGE_SIZE, NSA_KV_DIM)), (sel_pools[1], (PAGE_SIZE, NSA_KV_DIM))):
        for pg in range(n_pages):
            page_specs.append(page(shape, pg))
            page_args.append(pool)
    tables = (bc, bs, bw, ov, ex, pk[0], pv[0], pk[1], pv[1], pk[2], pv[2])
    grid_spec = pltpu.PrefetchScalarGridSpec(
        num_scalar_prefetch=1,
        grid=(B,),
        in_specs=[row(NSA_Q_DIM), row(NSA_KV_DIM), row(NSA_KV_DIM), row(NSA_KV_DIM), row(NSA_KV_DIM), row(LANES),
                  seq(W), seq(W)] + [full(a) for a in tables] + page_specs,
        out_specs=[row(NSA_Q_DIM), seq(w_keep), seq(w_keep)],
        scratch_shapes=[pltpu.VMEM((NSA_HEADS * T, past + ATT_TILE), F32),
                        pltpu.VMEM((4, ATT_TILE, NSA_KV_DIM), F32)],
    )
    return pl.pallas_call(
        functools.partial(_nsa_sample_body, n_pages=n_pages, n_sel=n_sel, n_top=n_top, past=past),
        grid_spec=grid_spec,
        out_shape=[jax.ShapeDtypeStruct((B * T, NSA_Q_DIM), F32),
                   jax.ShapeDtypeStruct((B, w_keep, NSA_KV_DIM), F32),
                   jax.ShapeDtypeStruct((B, w_keep, NSA_KV_DIM), F32)],
        compiler_params=pltpu.CompilerParams(dimension_semantics=("arbitrary",), vmem_limit_bytes=VMEM_LIMIT),
        name="nsa_sample",
    )(page_table, q, skn, svn, wkn, wvn, small, wk3, wv3, *tables, *page_args)


HIST_ROWS = 8


def _split_bf16(x):
    hi = x.astype(BF16)
    lo = (x - hi.astype(F32)).astype(BF16)
    return hi, lo


def _mm_hi(a, b):
    ah, al = _split_bf16(a)
    bh, bl = _split_bf16(b)
    d = lambda x, y: jnp.dot(x, y, preferred_element_type=F32)
    return d(ah, bh) + (d(ah, bl) + d(al, bh))


def _cumsum_rows(tri, x):
    hi, lo = _split_bf16(x)
    lo2 = (x - hi.astype(F32) - lo.astype(F32)).astype(BF16)
    d = lambda y: jnp.dot(tri, y, preferred_element_type=F32)
    return d(hi) + (d(lo) + d(lo2))


def _softplus(x):
    return jnp.maximum(x, 0.0) + jnp.log1p(jnp.exp(-jnp.abs(x)))


def _silu(x):
    return x * jax.nn.sigmoid(x)


def _deltanet_body(x_ref, sm_ref, dg_ref, hist_ref, s0_ref, cw_ref, apar_ref, dt_ref, ng_ref,
                   o_ref, s_out_ref, hist_out_ref, xbuf_ref, s_ref):
    j = pl.program_id(1)
    c = x_ref.shape[0]
    hd = DN_HEAD_DIM
    n_hist = CONV_W - 1

    @pl.when(j == 0)
    def _():
        xbuf_ref[HIST_ROWS - n_hist:HIST_ROWS, :] = hist_ref[0]
        s_ref[...] = s0_ref[0]

    xbuf_ref[HIST_ROWS:HIST_ROWS + c, :] = x_ref[...]
    y = None
    for t in range(CONV_W):
        lo = HIST_ROWS - n_hist + t
        term = xbuf_ref[lo:lo + c, :] * cw_ref[t:t + 1, :]
        y = term if y is None else y + term
    y = _silu(y)
    tail = xbuf_ref[HIST_ROWS + c - n_hist:HIST_ROWS + c, :]
    xbuf_ref[HIST_ROWS - n_hist:HIST_ROWS, :] = tail

    sm = sm_ref[...]
    beta_all = jax.nn.sigmoid(sm)
    g_all = apar_ref[...] * _softplus(sm + dt_ref[...])
    row = lax.broadcasted_iota(jnp.int32, (c, c), 0)
    col = lax.broadcasted_iota(jnp.int32, (c, c), 1)
    incl = row >= col
    strict = row > col
    tri = jnp.where(incl, 1.0, 0.0).astype(BF16)
    gcum_all = _cumsum_rows(tri, g_all)
    gcum_t = gcum_all.T
    eye = jnp.where(row == col, 1.0, 0.0)

    outs = []
    for h in range(DN_HEADS):
        q = y[:, h * hd:(h + 1) * hd]
        k = y[:, DN_DIM + h * hd:DN_DIM + (h + 1) * hd]
        v = y[:, 2 * DN_DIM + h * hd:2 * DN_DIM + (h + 1) * hd]
        q = q * lax.rsqrt(jnp.sum(q * q, axis=-1, keepdims=True) + EPS) * hd ** -0.5
        k = k * lax.rsqrt(jnp.sum(k * k, axis=-1, keepdims=True) + EPS)
        beta = beta_all[:, SM_BETA + h:SM_BETA + h + 1]
        gc = gcum_all[:, SM_DECAY + h:SM_DECAY + h + 1]
        gr = gcum_t[SM_DECAY + h:SM_DECAY + h + 1, :]
        g_last = gr[:, c - 1:c]
        decay = jnp.where(incl, jnp.exp(jnp.where(incl, gc - gr, 0.0)), 0.0)
        eg = jnp.exp(gc)
        kb = k * beta
        kbf = k.astype(BF16)
        a = jnp.where(strict, _dot_nt(kb.astype(BF16), kbf) * decay, 0.0)
        t_inv = eye - a
        a_pow = a
        for _ in range(int(math.log2(c)) - 1):
            a_pow = _mm_hi(a_pow, a_pow)
            t_inv = t_inv + _mm_hi(t_inv, a_pow)
        uw = _mm_hi(t_inv, jnp.concatenate([v * beta, kb * eg], axis=1))
        u, w = uw[:, :hd], uw[:, hd:]
        qk = jnp.where(incl, _dot_nt(q.astype(BF16), kbf) * decay, 0.0)
        qg = q * eg
        kg = k * jnp.exp(g_last - gc)
        s_old = s_ref[h]
        ws = jnp.dot(jnp.concatenate([w, qg], axis=0).astype(BF16), s_old.astype(BF16),
                     preferred_element_type=F32)
        v_new = u - ws[:c]
        vb = v_new.astype(BF16)
        o = ws[c:] + jnp.dot(qk.astype(BF16), vb, preferred_element_type=F32)
        s_ref[h] = s_old * jnp.exp(g_last) + lax.dot_general(
            kg.astype(BF16), vb, (((0,), (0,)), ((), ())), preferred_element_type=F32)
        o = o * lax.rsqrt(jnp.mean(o * o, axis=-1, keepdims=True) + EPS) * ng_ref[...]
        outs.append(o * _silu(dg_ref[:, h * hd:(h + 1) * hd]))
    o_ref[...] = jnp.concatenate(outs, axis=1).astype(o_ref.dtype)

    @pl.when(j == pl.num_programs(1) - 1)
    def _():
        s_out_ref[0] = s_ref[...]
        hist_out_ref[0] = tail


def deltanet(qkv_raw, small, dg, conv_hist, s0, lp, B, T):
    c = DN_CHUNK if T >= DN_CHUNK else T
    assert T % c == 0 and c % 8 == 0 and c >= CONV_W - 1 and (c & (c - 1)) == 0
    n = T // c
    lanes = jnp.arange(LANES)
    put = lambda vals, base: jnp.zeros((LANES,), F32).at[base + jnp.arange(DN_HEADS)].set(vals).reshape(1, LANES)
    apar = put(-jnp.exp(lp['dn_a_log'].astype(F32)), SM_DECAY)
    dtb = put(lp['dn_dt_bias'].astype(F32), SM_DECAY)
    del lanes
    rows = lambda w: pl.BlockSpec((c, w), lambda b, j: (b * n + j, 0))
    per_b = lambda a: pl.BlockSpec((1,) + a.shape[1:], lambda b, j: (b,) + (0,) * (a.ndim - 1))
    full = lambda a: pl.BlockSpec(a.shape, lambda b, j: (0,) * a.ndim)
    ng = lp['dn_norm_g'].reshape(1, DN_HEAD_DIM).astype(F32)
    return pl.pallas_call(
        _deltanet_body,
        grid=(B, n),
        in_specs=[rows(DN_CONV_DIM), rows(LANES), rows(DN_DIM), per_b(conv_hist), per_b(s0),
                  full(lp['conv_w']), full(apar), full(dtb), full(ng)],
        out_specs=[rows(DN_DIM), per_b(s0), per_b(conv_hist)],
        out_shape=[jax.ShapeDtypeStruct((B * T, DN_DIM), BF16 if c % 16 == 0 else F32),
                   jax.ShapeDtypeStruct(s0.shape, F32),
                   jax.ShapeDtypeStruct(conv_hist.shape, F32)],
        scratch_shapes=[pltpu.VMEM((HIST_ROWS + c, DN_CONV_DIM), F32),
                        pltpu.VMEM((DN_HEADS, DN_HEAD_DIM, DN_HEAD_DIM), F32)],
        compiler_params=pltpu.CompilerParams(dimension_semantics=("parallel", "arbitrary"),
                                             vmem_limit_bytes=VMEM_LIMIT),
        name="deltanet",
    )(qkv_raw, small, dg, conv_hist, s0, lp['conv_w'], apar, dtb, ng)


RT_GROUP = 0
RT_EXPERT = N_GROUPS


def pack_router(wg, bg, we, be):
    w = jnp.concatenate([wg, we], axis=1)
    b = jnp.concatenate([bg, be], axis=0)
    pad = LANES - w.shape[1]
    return jnp.pad(w, ((0, 0), (0, pad))).astype(BF16), jnp.pad(b, (0, pad)).reshape(1, LANES).astype(F32)


def _route(r):
    lane = lax.broadcasted_iota(jnp.int32, r.shape, 1).astype(F32)
    big = float(LANES)
    is_grp = lane < N_GROUPS
    lg = jnp.where(is_grp, r, -jnp.inf)
    mg = jnp.max(lg, axis=-1, keepdims=True)
    eg = jnp.where(is_grp, jnp.exp(lg - mg), 0.0)
    grp = jnp.min(jnp.where(lg == mg, lane, big), axis=-1, keepdims=True)
    wg = 1.0 / jnp.sum(eg, axis=-1, keepdims=True)
    first = RT_EXPERT + grp * EXPERTS_PER_GROUP
    in_grp = (lane >= first) & (lane < first + EXPERTS_PER_GROUP)
    le = jnp.where(in_grp, r, -jnp.inf)
    me = jnp.max(le, axis=-1, keepdims=True)
    ee = jnp.where(in_grp, jnp.exp(le - me), 0.0)
    pe = jnp.where(in_grp, ee / jnp.sum(ee, axis=-1, keepdims=True), -1.0)
    v1 = jnp.max(pe, axis=-1, keepdims=True)
    i1 = jnp.min(jnp.where(pe == v1, lane, big), axis=-1, keepdims=True)
    pe2 = jnp.where(lane == i1, -1.0, pe)
    v2 = jnp.max(pe2, axis=-1, keepdims=True)
    i2 = jnp.min(jnp.where(pe2 == v2, lane, big), axis=-1, keepdims=True)
    tot = v1 + v2
    return jnp.where(lane == i1, wg * (v1 / tot), jnp.where(lane == i2, wg * (v2 / tot), 0.0))


def _finish_body(x_ref, on_ref, od_ref, g1_ref, sc_ref, sh_ref, g2_ref, won_ref, wod_ref, ng_ref, wr_ref, br_ref,
                 wgu_ref, wd_ref, fg_ref, o_ref, x1_ref, h_ref, gate_ref, y_ref):
    e = pl.program_id(1)
    tm, d = x_ref.shape
    nb = g1_ref.shape[0]
    per_batch = lambda a: a.reshape(nb, tm // nb, d)

    @pl.when(e == 0)
    def _():
        mix = jnp.dot(on_ref[...].astype(BF16), won_ref[...], preferred_element_type=F32) \
            + jnp.dot(od_ref[...].astype(BF16), wod_ref[...], preferred_element_type=F32)
        x1 = per_batch(x_ref[...]) + g1_ref[...] * per_batch(mix)
        x1 = x1.reshape(tm, d)
        x1_ref[...] = x1
        xn = x1 * lax.rsqrt(jnp.mean(x1 * x1, axis=-1, keepdims=True) + EPS) * ng_ref[...]
        h = (per_batch(xn) * (1 + sc_ref[...]) + sh_ref[...]).reshape(tm, d).astype(BF16)
        h_ref[...] = h
        gate_ref[...] = _route(jnp.dot(h, wr_ref[...], preferred_element_type=F32) + br_ref[...])
        y_ref[...] = jnp.zeros(y_ref.shape, F32)

    h = h_ref[...]
    gu = jnp.dot(h, wgu_ref[0], preferred_element_type=F32)
    he = _silu(gu[:, :D_EXPERT]) * gu[:, D_EXPERT:]
    out = jnp.dot(he.astype(BF16), wd_ref[0], preferred_element_type=F32)
    lane = lax.broadcasted_iota(jnp.int32, gate_ref.shape, 1)
    g_col = jnp.sum(jnp.where(lane == RT_EXPERT + e, gate_ref[...], 0.0), axis=-1, keepdims=True)
    y_ref[...] += g_col * out

    @pl.when(e == pl.num_programs(1) - 1)
    def _():
        x2 = per_batch(x1_ref[...]) + g2_ref[...] * per_batch(y_ref[...])
        x2 = x2.reshape(tm, d)
        o_ref[...] = x2 * lax.rsqrt(jnp.mean(x2 * x2, axis=-1, keepdims=True) + EPS) * fg_ref[...]


def finish_layer(x, o_nsa, o_dn, g1, sc2, sh2, g2, packed, norm_ffn_g, final_g, *, tm=1024):
    B, T, D = x.shape
    n = B * T
    tm = min(tm, n)
    rows = max(tm // T, 1)
    per = max(T // tm, 1)
    row = lambda w: pl.BlockSpec((tm, w), lambda i, e: (i, 0))
    mod = pl.BlockSpec((rows, 1, D), lambda i, e: (i // per, 0, 0))
    full = lambda a: pl.BlockSpec(a.shape, lambda i, e: (0,) * a.ndim)
    expert = lambda a: pl.BlockSpec((1,) + a.shape[1:], lambda i, e: (e, 0, 0))
    vec = lambda v: v.reshape(1, D).astype(F32)
    args = (x.reshape(n, D), o_nsa, o_dn, g1, sc2, sh2, g2, packed['w_out_nsa'], packed['w_out_dn'],
            vec(norm_ffn_g), packed['router_w'], packed['router_b'], packed['w_gate_up'], packed['w_down'],
            vec(final_g))
    specs = [row(D), row(o_nsa.shape[1]), row(o_dn.shape[1]), mod, mod, mod, mod] + \
            [full(a) for a in args[7:12]] + [expert(args[12]), expert(args[13]), full(args[14])]
    return pl.pallas_call(
        _finish_body,
        grid=(n // tm, N_EXPERTS),
        in_specs=specs,
        out_specs=row(D),
        out_shape=jax.ShapeDtypeStruct((n, D), F32),
        scratch_shapes=[pltpu.VMEM((tm, D), F32), pltpu.VMEM((tm, D), BF16), pltpu.VMEM((tm, LANES), F32),
                        pltpu.VMEM((tm, D), F32)],
        compiler_params=pltpu.CompilerParams(dimension_semantics=("parallel", "arbitrary"),
                                             vmem_limit_bytes=VMEM_LIMIT),
        name="finish_layer",
    )(*args).reshape(B, T, D)


def _final_norm_body(x_ref, g_ref, o_ref):
    x = x_ref[...]
    o_ref[...] = x * lax.rsqrt(jnp.mean(x * x, axis=-1, keepdims=True) + EPS) * g_ref[...]


def final_rmsnorm(x, g, *, tm=512):
    shp = x.shape
    x2 = x.reshape(-1, shp[-1])
    n, d = x2.shape
    return pl.pallas_call(
        _final_norm_body,
        grid=(n // tm,),
        in_specs=[pl.BlockSpec((tm, d), lambda i: (i, 0)), pl.BlockSpec((1, d), lambda i: (0, 0))],
        out_specs=pl.BlockSpec((tm, d), lambda i: (i, 0)),
        out_shape=jax.ShapeDtypeStruct(x2.shape, x2.dtype),
        compiler_params=pltpu.CompilerParams(dimension_semantics=("parallel",)),
        name="final_rmsnorm",
    )(x2, g.reshape(1, d)).reshape(shp)


def permute_w_out_nsa(w_out):
    w = w_out[:NSA_Q_DIM].reshape(NSA_KV_HEADS, NSA_GROUP, HEAD_DIM, -1)
    return w.transpose(1, 0, 2, 3).reshape(NSA_Q_DIM, -1).astype(BF16)


def layer_prompt(x, c, lp, rel_bias, packed):
    B, T, _ = x.shape
    sh1, sc1, g1, sh2, sc2, g2 = modulation(c, lp)
    (q, ck, cv, sk, sv, wk, wv, skb, svb, wkb, wvb, qkv_raw, dg, small) = in_projection(
        x, lp['norm_mix_g'], sc1, sh1, packed['w_in'])
    kc, vc = compress_tokens(ck, cv, packed['cmp_k'], packed['cmp_v'], B)
    o_nsa = nsa_prompt(q, kc, vc, skb, svb, wkb, wvb, small, rel_bias, B, T)
    kv = lambda r: r.reshape(B, T, NSA_KV_HEADS, HEAD_DIM)
    ck, cv, sk, sv, wk, wv = kv(ck), kv(cv), kv(sk), kv(sv), kv(wk), kv(wv)
    conv_hist = jnp.zeros((B, CONV_W - 1, DN_CONV_DIM), x.dtype)
    s0 = jnp.zeros((B, DN_HEADS, DN_HEAD_DIM, DN_HEAD_DIM), x.dtype)
    o_dn, s_new, conv_new = deltanet(qkv_raw, small, dg, conv_hist, s0, lp, B, T)
    y = finish_layer(x, o_nsa, o_dn, g1, sc2, sh2, g2, packed, lp['norm_ffn_g'], packed['final_norm_g'])
    w_keep = min(WINDOW, T)
    return y, (ck, cv, sk, sv, wk[:, -w_keep:], wv[:, -w_keep:], s_new, conv_new)


def layer_sample(x, c, cmp_k_pool, cmp_v_pool, sel_k_pool, sel_v_pool, win_k, win_v, s0, conv_hist,
                 page_table, lp, rel_bias, packed):
    B, T, _ = x.shape
    sh1, sc1, g1, sh2, sc2, g2 = modulation(c, lp)
    (q, ck, cv, sk, sv, wk, wv, _, _, _, _, qkv_raw, dg, small) = in_projection(
        x, lp['norm_mix_g'], sc1, sh1, packed['w_in'])
    o_nsa, wk_next, wv_next = nsa_sample(q, sk, sv, wk, wv, small, (cmp_k_pool, cmp_v_pool, sel_k_pool, sel_v_pool),
                                         win_k, win_v, page_table, rel_bias, packed['cmp_k'], packed['cmp_v'], B, T)
    kv = lambda r: r.reshape(B, -1, NSA_KV_HEADS, HEAD_DIM)
    o_dn, s_new, conv_new = deltanet(qkv_raw, small, dg, conv_hist, s0, lp, B, T)
    y = finish_layer(x, o_nsa, o_dn, g1, sc2, sh2, g2, packed, lp['norm_ffn_g'], packed['final_norm_g'])
    return y, (kv(ck), kv(cv), kv(sk), kv(sv), kv(wk_next), kv(wv_next), s_new, conv_new)


def kernel(x_prompt, x_sample, cache_cmp_k, cache_cmp_v, cache_sel_k, cache_sel_v, cache_win_k, cache_win_v,
           state_delta, state_conv, page_table, c_prompt, c_sample, rel_bias, w_ada, b_ada, norm_mix_g,
           norm_ffn_g, w_in, cmp_pos_k, cmp_w1_k, cmp_w2_k, cmp_pos_v, cmp_w1_v, cmp_w2_v, conv_w, dn_a_log,
           dn_dt_bias, dn_norm_g, w_out, router_group_w, router_group_b, router_expert_w, router_expert_b,
           expert_w_gate, expert_w_up, expert_w_down, final_norm_g):
    xp, xs = x_prompt, x_sample
    new_p, new_s = [], []
    for l in range(DEPTH):
        lp = dict(w_ada=w_ada[l], b_ada=b_ada[l], norm_mix_g=norm_mix_g[l], norm_ffn_g=norm_ffn_g[l],
                  w_in=w_in[l], cmp_pos_k=cmp_pos_k[l], cmp_w1_k=cmp_w1_k[l], cmp_w2_k=cmp_w2_k[l],
                  cmp_pos_v=cmp_pos_v[l], cmp_w1_v=cmp_w1_v[l], cmp_w2_v=cmp_w2_v[l], conv_w=conv_w[l],
                  dn_a_log=dn_a_log[l], dn_dt_bias=dn_dt_bias[l], dn_norm_g=dn_norm_g[l], w_out=w_out[l],
                  router_group_w=router_group_w[l], router_group_b=router_group_b[l],
                  router_expert_w=router_expert_w[l], router_expert_b=router_expert_b[l],
                  w_gate=expert_w_gate[l], w_up=expert_w_up[l], w_down=expert_w_down[l])
        packed = dict(w_in=pack_w_in(w_in[l]),
                      cmp_k=pack_compress(cmp_pos_k[l], cmp_w1_k[l], cmp_w2_k[l]),
                      cmp_v=pack_compress(cmp_pos_v[l], cmp_w1_v[l], cmp_w2_v[l]),
                      w_out_nsa=permute_w_out_nsa(w_out[l]),
                      w_out_dn=w_out[l][NSA_Q_DIM:].astype(BF16),
                      w_gate_up=jnp.concatenate([expert_w_gate[l], expert_w_up[l]], axis=-1).astype(BF16),
                      w_down=expert_w_down[l].astype(BF16),
                      final_norm_g=final_norm_g)
        packed['router_w'], packed['router_b'] = pack_router(router_group_w[l], router_group_b[l],
                                                             router_expert_w[l], router_expert_b[l])
        xp, st_p = layer_prompt(xp, c_prompt, lp, rel_bias, packed)
        xs, st_s = layer_sample(xs, c_sample, cache_cmp_k[l], cache_cmp_v[l], cache_sel_k[l], cache_sel_v[l],
                                cache_win_k[l], cache_win_v[l], state_delta[l], state_conv[l], page_table,
                                lp, rel_bias, packed)
        new_p.append(st_p)
        new_s.append(st_s)
    assert DEPTH == 1
    y_prompt, y_sample = xp, xs

    def stk(states, i):
        return jnp.stack([s[i] for s in states])

    return (y_prompt, y_sample,
            stk(new_p, 0), stk(new_p, 1), stk(new_p, 2), stk(new_p, 3),
            stk(new_p, 4), stk(new_p, 5), stk(new_p, 6), stk(new_p, 7),
            stk(new_s, 0), stk(new_s, 1), stk(new_s, 2), stk(new_s, 3),
            stk(new_s, 4), stk(new_s, 5), stk(new_s, 6), stk(new_s, 7))
```

```python
import functools
import math

import jax
import jax.numpy as jnp
import numpy as np
from jax import lax
from jax.experimental import pallas as pl
from jax.experimental.pallas import tpu as pltpu

D_MODEL = 1024
DEPTH = 1
PAGE_SIZE = 128

NSA_HEADS = 8
NSA_KV_HEADS = 2
NSA_GROUP = NSA_HEADS // NSA_KV_HEADS
HEAD_DIM = 64
CMP_LEN = 32
CMP_STRIDE = 16
CMP_HIDDEN = 4 * HEAD_DIM
SEL_BLOCK = 64
SEL_TOP = 16
WINDOW = 512
WIN_Q_BLOCK = 128
SEL_Q_BLOCK = 64
DN_HEADS = 8
DN_HEAD_DIM = 64
CONV_W = 4
DN_CHUNK = 64
NUM_BUCKETS = 32
MAX_DISTANCE = 128
N_GROUPS = 4
EXPERTS_PER_GROUP = 8
N_EXPERTS = N_GROUPS * EXPERTS_PER_GROUP
TOP_K_IN_GROUP = 2
D_EXPERT = D_MODEL // 4

EPS = 1e-6
NEG_INF = -1e30
F32 = jnp.float32
BF16 = jnp.bfloat16

NSA_Q_DIM = NSA_HEADS * HEAD_DIM
NSA_KV_DIM = NSA_KV_HEADS * HEAD_DIM
DN_DIM = DN_HEADS * DN_HEAD_DIM
DN_CONV_DIM = 3 * DN_DIM
MIX_WIDTH = NSA_Q_DIM + DN_DIM
IN_SPLITS = (NSA_Q_DIM, NSA_KV_DIM, NSA_KV_DIM, NSA_KV_DIM, NSA_KV_DIM, NSA_KV_DIM, NSA_KV_DIM,
             3 * NSA_HEADS, DN_CONV_DIM, DN_HEADS, DN_HEADS, DN_DIM)
IN_DIM = sum(IN_SPLITS)

LANES = 128
VMEM_LIMIT = 56 * 1024 * 1024

ATT_TILE = 128
MASKED = -1e30


def rmsnorm(x, g):
    xf = x.astype(F32)
    y = xf * lax.rsqrt(jnp.mean(xf * xf, axis=-1, keepdims=True) + EPS)
    return (y * g.astype(F32)).astype(x.dtype)


def l2norm(x):
    xf = x.astype(F32)
    return (xf * lax.rsqrt(jnp.sum(xf * xf, axis=-1, keepdims=True) + EPS)).astype(x.dtype)


def t5_bucket(rel):
    n = jnp.maximum(rel, 0)
    max_exact = NUM_BUCKETS // 2
    nf = jnp.maximum(n, 1).astype(F32)
    large = max_exact + (jnp.log(nf / max_exact) / math.log(MAX_DISTANCE / max_exact)
                         * (NUM_BUCKETS - max_exact)).astype(jnp.int32)
    large = jnp.minimum(large, NUM_BUCKETS - 1)
    return jnp.where(n < max_exact, n, large)


def rel_bias_heads(rel, rel_bias):
    b = rel_bias.astype(F32)[t5_bucket(rel)]
    b = jnp.moveaxis(b, -1, -3)
    return b.reshape(b.shape[:-3] + (NSA_KV_HEADS, NSA_GROUP) + b.shape[-2:])


def attend(q, k, v, bias, mask):
    s = jnp.einsum('bntkgd,bnlkd->bnkgtl', q, k).astype(F32) * HEAD_DIM ** -0.5 + bias
    m = mask[:, None, None]
    p = jax.nn.softmax(jnp.where(m, s, NEG_INF), axis=-1)
    p = jnp.where(m, p, 0.0)
    o = jnp.einsum('bnkgtl,bnlkd->bntkgd', p.astype(v.dtype), v)
    return o, p


def compress_rows(rows, pos_emb, w1, w2):
    B, L = rows.shape[:2]
    n_cmp = (L - CMP_LEN) // CMP_STRIDE + 1
    idx = jnp.arange(n_cmp)[:, None] * CMP_STRIDE + jnp.arange(CMP_LEN)[None, :]
    blk = rows[:, idx] + pos_emb[:, None, :]
    blk = jnp.swapaxes(blk, 2, 3).reshape(B, n_cmp, NSA_KV_HEADS, CMP_LEN * HEAD_DIM)
    return jax.nn.silu(blk @ w1) @ w2


def nsa_cmp_sel(q, q_pos, kc_rows, vc_rows, ks_rows, vs_rows, lp, rel_bias):
    B, Tq = q.shape[:2]
    L = kc_rows.shape[1]
    kc = compress_rows(kc_rows, lp['cmp_pos_k'], lp['cmp_w1_k'], lp['cmp_w2_k'])
    vc = compress_rows(vc_rows, lp['cmp_pos_v'], lp['cmp_w1_v'], lp['cmp_w2_v'])
    n_cmp = kc.shape[1]
    cmp_end = jnp.arange(n_cmp) * CMP_STRIDE + CMP_LEN - 1
    rel_c = q_pos[:, None] - cmp_end[None, :]
    o_cmp, p_cmp = attend(q[:, None], kc[:, None], vc[:, None],
                          rel_bias_heads(rel_c[None], rel_bias), (rel_c >= 0)[None])
    o_cmp = o_cmp[:, 0]
    n_sel = -(-L // SEL_BLOCK)
    c_start = jnp.arange(n_cmp) * CMP_STRIDE
    s_start = jnp.arange(n_sel) * SEL_BLOCK
    overlap = jnp.clip(jnp.minimum(c_start[:, None] + CMP_LEN, s_start[None] + SEL_BLOCK)
                       - jnp.maximum(c_start[:, None], s_start[None]), 0, None).astype(F32) / CMP_LEN
    imp = jnp.einsum('bkgtc,cs->btks', p_cmp[:, 0], overlap)
    q_blk = q_pos // SEL_BLOCK
    sb = jnp.arange(n_sel)
    forced = (sb[None] == 0) | (sb[None] == q_blk[:, None]) | (sb[None] == q_blk[:, None] - 1)
    avail = s_start[None] <= q_pos[:, None]
    score = jnp.where(forced[None, :, None], 1e9, jnp.where(avail[None, :, None], imp, NEG_INF))
    n_top = min(SEL_TOP, n_sel)
    _, sel_idx = lax.top_k(score, n_top)
    pad = n_sel * SEL_BLOCK - L

    def to_blocks(r):
        r = jnp.pad(r, ((0, 0), (0, pad), (0, 0), (0, 0)))
        return r.reshape(B, n_sel, SEL_BLOCK, NSA_KV_HEADS, HEAD_DIM).transpose(0, 3, 1, 2, 4)
    ks_b = to_blocks(ks_rows)
    vs_b = to_blocks(vs_rows)
    bi = jnp.arange(B)[:, None, None, None]
    ki = jnp.arange(NSA_KV_HEADS)[None, None, :, None]
    tbl = rel_bias.astype(F32).reshape(NUM_BUCKETS, NSA_KV_HEADS, NSA_GROUP).transpose(1, 0, 2)
    qb = SEL_Q_BLOCK if Tq % SEL_Q_BLOCK == 0 else Tq
    nqb = Tq // qb

    def sel_block(args):
        q_i, pos_i, idx_i = args
        kg = ks_b[bi, ki, idx_i]
        vg = vs_b[bi, ki, idx_i]
        k_pos = idx_i[..., None] * SEL_BLOCK + jnp.arange(SEL_BLOCK)
        rel = pos_i[None, :, None, None, None] - k_pos
        bias = tbl[ki[..., None], t5_bucket(rel)]
        s = jnp.einsum('bqkgd,bqknsd->bqkgns', q_i, kg).astype(F32) * HEAD_DIM ** -0.5 + jnp.moveaxis(bias, -1, 3)
        mask = (rel >= 0)[:, :, :, None]
        s = jnp.where(mask, s, NEG_INF).reshape(B, qb, NSA_KV_HEADS, NSA_GROUP, n_top * SEL_BLOCK)
        p = jax.nn.softmax(s, axis=-1)
        p = jnp.where(mask.reshape(B, qb, NSA_KV_HEADS, 1, n_top * SEL_BLOCK), p, 0.0)
        return jnp.einsum('bqkgl,bqkld->bqkgd', p.astype(vg.dtype),
                          vg.reshape(B, qb, NSA_KV_HEADS, n_top * SEL_BLOCK, HEAD_DIM))

    q_m = q.reshape(B, nqb, qb, NSA_KV_HEADS, NSA_GROUP, HEAD_DIM).swapaxes(0, 1)
    pos_m = q_pos.reshape(nqb, qb)
    idx_m = sel_idx.reshape(B, nqb, qb, NSA_KV_HEADS, n_top).swapaxes(0, 1)
    o_sel = lax.map(sel_block, (q_m, pos_m, idx_m))
    o_sel = o_sel.swapaxes(0, 1).reshape(B, Tq, NSA_KV_HEADS, NSA_GROUP, HEAD_DIM)
    return o_cmp, o_sel


def window_sample(q, q_pos, k, v, k_pos, rel_bias):
    rel = q_pos[:, None] - k_pos[None]
    mask = (rel >= 0) & (rel < WINDOW)
    o, _ = attend(q[:, None], k[:, None], v[:, None], rel_bias_heads(rel[None], rel_bias), mask[None])
    return o[:, 0]


def combine_branches(gate_logit, o_cmp, o_sel, o_win):
    B, T = gate_logit.shape[:2]
    g = jax.nn.sigmoid(gate_logit.astype(F32)).reshape(B, T, NSA_KV_HEADS, NSA_GROUP, 3).astype(o_cmp.dtype)
    o = g[..., 0:1] * o_cmp + g[..., 1:2] * o_sel + g[..., 2:3] * o_win
    return o.reshape(B, T, NSA_Q_DIM)


def causal_conv(x_hist, x_new, w):
    xc = jnp.concatenate([x_hist, x_new], axis=1)
    y = lax.conv_general_dilated(xc, w[:, None, :], window_strides=(1,), padding='VALID',
                                 dimension_numbers=('NWC', 'WIO', 'NWC'), feature_group_count=xc.shape[-1])
    return jax.nn.silu(y), xc[:, -(CONV_W - 1):]


def gated_delta_rule(q, k, v, g, beta, s0):
    B, T, H, DK = q.shape
    C = DN_CHUNK if T >= DN_CHUNK else T
    n = -(-T // C)
    pad = n * C - T

    def prep(a):
        a = jnp.pad(a.astype(F32), ((0, 0), (0, pad)) + ((0, 0),) * (a.ndim - 2))
        a = a.reshape((B, n, C) + a.shape[2:])
        return jnp.moveaxis(a, (1, 3), (0, 2))

    qc, kc, vc, gc, bc = prep(q), prep(k), prep(v), prep(g), prep(beta)
    gcum = jnp.cumsum(gc, axis=-1)
    ii = jnp.arange(C)
    incl = ii[:, None] >= ii[None, :]
    strict = ii[:, None] > ii[None, :]
    diff = gcum[..., :, None] - gcum[..., None, :]
    decay = jnp.where(incl, jnp.exp(jnp.where(incl, diff, 0.0)), 0.0)
    kb = kc * bc[..., None]
    a_mat = jnp.where(strict, jnp.einsum('...id,...jd->...ij', kb, kc) * decay, 0.0)
    eye_plus = a_mat + jnp.eye(C, dtype=F32)
    u = lax.linalg.triangular_solve(eye_plus, vc * bc[..., None], left_side=True, lower=True, unit_diagonal=True)
    w = lax.linalg.triangular_solve(eye_plus, kb * jnp.exp(gcum)[..., None], left_side=True, lower=True, unit_diagonal=True)
    qk = jnp.where(incl, jnp.einsum('...id,...jd->...ij', qc, kc) * decay, 0.0)
    qg = qc * jnp.exp(gcum)[..., None]
    kg = kc * jnp.exp(gcum[..., -1:] - gcum)[..., None]
    glast = jnp.exp(gcum[..., -1])

    def step(S, xs):
        u_i, w_i, qk_i, qg_i, kg_i, gl_i = xs
        v_new = u_i - jnp.einsum('bhck,bhkv->bhcv', w_i, S)
        o = jnp.einsum('bhck,bhkv->bhcv', qg_i, S) + jnp.einsum('bhij,bhjv->bhiv', qk_i, v_new)
        S = S * gl_i[..., None, None] + jnp.einsum('bhck,bhcv->bhkv', kg_i, v_new)
        return S, o

    S, o = lax.scan(step, s0.astype(F32), (u, w, qk, qg, kg, glast))
    o = o.transpose(1, 0, 3, 2, 4).reshape(B, n * C, H, v.shape[-1])[:, :T]
    return o, S


def deltanet_mixer(qkv_raw, b_logit, a_logit, gate, conv_hist, s0, lp):
    B, T = qkv_raw.shape[:2]
    qkv, new_hist = causal_conv(conv_hist, qkv_raw, lp['conv_w'])
    q, k, v = jnp.split(qkv, 3, axis=-1)
    q = l2norm(q.reshape(B, T, DN_HEADS, DN_HEAD_DIM)) * DN_HEAD_DIM ** -0.5
    k = l2norm(k.reshape(B, T, DN_HEADS, DN_HEAD_DIM))
    v = v.reshape(B, T, DN_HEADS, DN_HEAD_DIM)
    beta = jax.nn.sigmoid(b_logit.astype(F32))
    g = -jnp.exp(lp['dn_a_log'].astype(F32)) * jax.nn.softplus(a_logit.astype(F32) + lp['dn_dt_bias'].astype(F32))
    o, s_new = gated_delta_rule(q, k, v, g, beta, s0)
    o = rmsnorm(o, lp['dn_norm_g']) * jax.nn.silu(gate.astype(F32).reshape(B, T, DN_HEADS, DN_HEAD_DIM))
    return o.astype(qkv_raw.dtype).reshape(B, T, DN_DIM), s_new.astype(s0.dtype), new_hist


def hier_moe(h, lp):
    B, T, D = h.shape
    x = h.reshape(B * T, D)
    lg = (x @ lp['router_group_w'] + lp['router_group_b']).astype(F32)
    pg = jax.nn.softmax(lg, axis=-1)
    grp = jnp.argmax(lg, axis=-1)
    wg = jnp.take_along_axis(pg, grp[:, None], axis=-1)
    le = (x @ lp['router_expert_w'] + lp['router_expert_b']).astype(F32).reshape(-1, N_GROUPS, EXPERTS_PER_GROUP)
    le = jnp.take_along_axis(le, grp[:, None, None], axis=1)[:, 0]
    pe, ie = lax.top_k(jax.nn.softmax(le, axis=-1), TOP_K_IN_GROUP)
    pe = pe / jnp.sum(pe, axis=-1, keepdims=True)
    eid = grp[:, None] * EXPERTS_PER_GROUP + ie
    gates = jnp.sum(jax.nn.one_hot(eid, N_EXPERTS, dtype=F32) * (wg * pe)[..., None], axis=1).astype(x.dtype)
    y = jnp.zeros_like(x)
    for e in range(N_EXPERTS):
        he = jax.nn.silu(x @ lp['w_gate'][e]) * (x @ lp['w_up'][e])
        y = y + gates[:, e:e + 1] * (he @ lp['w_down'][e])
    return y.reshape(B, T, D)


def modulation(c, lp):
    mod = jax.nn.silu(c) @ lp['w_ada'] + lp['b_ada']
    return jnp.split(mod[:, None, :], 6, axis=-1)


def modulate(x, g, shift, scale):
    return rmsnorm(x, g) * (1 + scale) + shift


def split_projection(h, lp):
    z = h @ lp['w_in']
    return jnp.split(z, np.cumsum(IN_SPLITS)[:-1].tolist(), axis=-1)


def finish_layer(x, o_nsa, o_dn, g1, sh2, sc2, g2, lp):
    x = x + g1 * (jnp.concatenate([o_nsa, o_dn], axis=-1) @ lp['w_out'])
    return x + g2 * hier_moe(modulate(x, lp['norm_ffn_g'], sh2, sc2), lp)


_PK_Q = (0, NSA_Q_DIM)
_PK_KV = (_PK_Q[1], _PK_Q[1] + 6 * NSA_KV_DIM)
_PK_QKV = (_PK_KV[1], _PK_KV[1] + DN_CONV_DIM)
_PK_DG = (_PK_QKV[1], _PK_QKV[1] + DN_DIM)
_PK_SM = (_PK_DG[1], _PK_DG[1] + LANES)
PK_DIM = _PK_SM[1]
SM_GATE = 0
SM_BETA = 3 * NSA_HEADS
SM_DECAY = SM_BETA + DN_HEADS


def pack_w_in(w_in):
    offs = np.cumsum((0,) + IN_SPLITS)
    part = lambda i: w_in[:, offs[i]:offs[i + 1]]
    d = w_in.shape[0]
    q = part(0).reshape(d, NSA_KV_HEADS, NSA_GROUP, HEAD_DIM).transpose(0, 2, 1, 3).reshape(d, NSA_Q_DIM)
    q = q * HEAD_DIM ** -0.5
    small = jnp.concatenate([part(7), part(9), part(10)], axis=1)
    small = jnp.pad(small, ((0, 0), (0, LANES - small.shape[1])))
    cols = [q] + [part(i) for i in range(1, 7)] + [part(8), part(11), small]
    return jnp.concatenate(cols, axis=1).astype(BF16)


def _in_proj_body(x_ref, g_ref, sc_ref, sh_ref, w_ref, q_ref, ck_ref, cv_ref, sk_ref, sv_ref, wk_ref, wv_ref,
                  skb_ref, svb_ref, wkb_ref, wvb_ref, qkv_ref, dg_ref, sm_ref):
    x = x_ref[...]
    tm, d = x.shape
    nb = sc_ref.shape[0]
    xn = x * lax.rsqrt(jnp.mean(x * x, axis=-1, keepdims=True) + EPS) * g_ref[...]
    h = xn.reshape(nb, tm // nb, d) * (1 + sc_ref[...]) + sh_ref[...]
    h = h.reshape(tm, d).astype(BF16)

    def mm(span):
        return jnp.dot(h, w_ref[:, span[0]:span[1]], preferred_element_type=F32)

    q_ref[...] = mm(_PK_Q).astype(BF16)
    kv = mm(_PK_KV)
    for i, r in enumerate((ck_ref, cv_ref, sk_ref, sv_ref, wk_ref, wv_ref)):
        r[...] = kv[:, i * NSA_KV_DIM:(i + 1) * NSA_KV_DIM]
    for i, r in enumerate((skb_ref, svb_ref, wkb_ref, wvb_ref)):
        r[...] = kv[:, (i + 2) * NSA_KV_DIM:(i + 3) * NSA_KV_DIM].astype(BF16)
    qkv_ref[...] = mm(_PK_QKV)
    dg_ref[...] = mm(_PK_DG)
    sm_ref[...] = mm(_PK_SM)


def in_projection(x, norm_g, scale, shift, w_packed, *, tm=256):
    B, T, D = x.shape
    n = B * T
    tm = min(tm, n)
    rows = max(tm // T, 1)
    per = max(T // tm, 1)
    row = lambda i: (i, 0)
    outs = [((n, NSA_Q_DIM), BF16)] + [((n, NSA_KV_DIM), F32)] * 6 + [((n, NSA_KV_DIM), BF16)] * 4 + \
           [((n, DN_CONV_DIM), F32), ((n, DN_DIM), F32), ((n, LANES), F32)]
    return pl.pallas_call(
        _in_proj_body,
        grid=(n // tm,),
        in_specs=[pl.BlockSpec((tm, D), row),
                  pl.BlockSpec((1, D), lambda i: (0, 0)),
                  pl.BlockSpec((rows, 1, D), lambda i: (i // per, 0, 0)),
                  pl.BlockSpec((rows, 1, D), lambda i: (i // per, 0, 0)),
                  pl.BlockSpec((D, PK_DIM), lambda i: (0, 0))],
        out_specs=[pl.BlockSpec((tm, s[1]), row) for s, _ in outs],
        out_shape=[jax.ShapeDtypeStruct(s, dt) for s, dt in outs],
        compiler_params=pltpu.CompilerParams(dimension_semantics=("parallel",), vmem_limit_bytes=VMEM_LIMIT),
        name="in_projection",
    )(x.reshape(n, D), norm_g.reshape(1, D), scale, shift, w_packed)


CHUNK_W = CMP_STRIDE * NSA_KV_DIM
CMP_HALVES = CMP_LEN // CMP_STRIDE


def pack_compress(pos, w1, w2):
    eye = jnp.eye(NSA_KV_HEADS, dtype=F32)
    w1r = w1.reshape(CMP_HALVES, CMP_STRIDE, HEAD_DIM, CMP_HIDDEN)
    w1p = jnp.einsum('ijdc,kl->ijkdlc', w1r, eye).reshape(CMP_HALVES, CHUNK_W, NSA_KV_HEADS * CMP_HIDDEN)
    w2p = jnp.einsum('cd,kl->kcld', w2, eye).reshape(NSA_KV_HEADS * CMP_HIDDEN, NSA_KV_DIM)
    posp = jnp.broadcast_to(pos.reshape(CMP_HALVES, CMP_STRIDE, 1, HEAD_DIM),
                            (CMP_HALVES, CMP_STRIDE, NSA_KV_HEADS, HEAD_DIM)).reshape(CMP_HALVES, CHUNK_W)
    return posp, w1p.astype(BF16), w2p.astype(BF16)


def _compress_one(x, pos_ref, w1_ref, w2_ref):
    n_chunk = x.shape[0]
    hid = None
    for i in range(CMP_HALVES):
        part = jnp.dot((x + pos_ref[i:i + 1]).astype(BF16), w1_ref[i], preferred_element_type=F32)
        if i:
            part = pltpu.roll(part, shift=n_chunk - i, axis=0)
        hid = part if hid is None else hid + part
    hid = hid * jax.nn.sigmoid(hid)
    return jnp.dot(hid.astype(BF16), w2_ref[...], preferred_element_type=F32)


def _compress_body(xk_ref, xv_ref, pk_ref, pv_ref, w1k_ref, w1v_ref, w2k_ref, w2v_ref, kc_ref, vc_ref):
    kc_ref[0] = _compress_one(xk_ref[0], pk_ref, w1k_ref, w2k_ref).astype(BF16)
    vc_ref[0] = _compress_one(xv_ref[0], pv_ref, w1v_ref, w2v_ref).astype(BF16)


def compress_tokens(ck, cv, pk, pv, B):
    n_chunk = ck.shape[0] // B // CMP_STRIDE
    xk = ck.reshape(B, n_chunk, CHUNK_W)
    xv = cv.reshape(B, n_chunk, CHUNK_W)
    full = lambda a: pl.BlockSpec(a.shape, lambda b: (0,) * a.ndim)
    blk = lambda w: pl.BlockSpec((1, n_chunk, w), lambda b: (b, 0, 0))
    return pl.pallas_call(
        _compress_body,
        grid=(B,),
        in_specs=[blk(CHUNK_W), blk(CHUNK_W), full(pk[0]), full(pv[0]), full(pk[1]), full(pv[1]),
                  full(pk[2]), full(pv[2])],
        out_specs=[blk(NSA_KV_DIM), blk(NSA_KV_DIM)],
        out_shape=[jax.ShapeDtypeStruct((B, n_chunk, NSA_KV_DIM), BF16)] * 2,
        compiler_params=pltpu.CompilerParams(dimension_semantics=("parallel",), vmem_limit_bytes=VMEM_LIMIT),
        name="compress_tokens",
    )(xk, xv, pk[0], pv[0], pk[1], pv[1], pk[2], pv[2])


def _t5_bucket_np(rel):
    n = np.maximum(rel, 0)
    max_exact = NUM_BUCKETS // 2
    nf = np.maximum(n, 1).astype(np.float32)
    large = max_exact + (np.log(nf / max_exact) / math.log(MAX_DISTANCE / max_exact)
                         * (NUM_BUCKETS - max_exact)).astype(np.int32)
    return np.where(n < max_exact, n, np.minimum(large, NUM_BUCKETS - 1))


assert (_t5_bucket_np(np.arange(ATT_TILE + 1, 1 << 16)) == NUM_BUCKETS - 1).all()


NB_SAME, NB_PREV, NB_FAR, NB_EDGE, NB_NONE = range(5)
FAR_TILE = 512


def near_bias_tables(rel_bias):
    t = ATT_TILE
    i = jnp.arange(t)[:, None]
    j = jnp.arange(t)[None, :]
    tbl = rel_bias.astype(F32).T
    diag = jnp.where(i >= j, tbl[:, t5_bucket(i - j)], MASKED)
    prev = tbl[:, t5_bucket(t + i - j)]
    far = jnp.broadcast_to(tbl[:, NUM_BUCKETS - 1][:, None, None], prev.shape)
    edge = jnp.where(j > i, far, MASKED)
    return jnp.stack([diag, prev, far, edge, jnp.full_like(far, MASKED)])


def cmp_bias_table(rel_bias, q_pos, n_chunk):
    cmp_end = jnp.arange(n_chunk) * CMP_STRIDE + CMP_LEN - 1
    rel = q_pos[:, None] - cmp_end[None, :]
    tbl = rel_bias.astype(F32).T
    return jnp.where(rel >= 0, tbl[:, t5_bucket(rel)], MASKED)


def overlap_matrix(n_chunk, n_sel):
    c0 = np.arange(n_chunk)[:, None] * CMP_STRIDE
    s0 = np.arange(LANES)[None, :] * SEL_BLOCK
    ov = np.clip(np.minimum(c0 + CMP_LEN, s0 + SEL_BLOCK) - np.maximum(c0, s0), 0, None) / CMP_LEN
    ov = np.where(np.arange(LANES)[None, :] < n_sel, ov, 0.0)
    return jnp.asarray(ov, F32)


def expand_matrix(n_keys):
    e = (np.arange(n_keys)[None, :] // SEL_BLOCK) == np.arange(LANES)[:, None]
    return jnp.asarray(e, BF16)


def _stack_queries(q_ref, tq):
    low = lax.broadcasted_iota(jnp.int32, (tq, LANES), 1) < HEAD_DIM
    zero = jnp.zeros((tq, LANES), BF16)
    parts = []
    for kv in range(NSA_KV_HEADS):
        for g in range(NSA_GROUP):
            blk = q_ref[:, g * LANES:(g + 1) * LANES]
            parts.append(jnp.where(low if kv == 0 else ~low, blk, zero))
    return jnp.concatenate(parts, axis=0)


def _dot_nt(a, b):
    return lax.dot_general(a, b, (((1,), (1,)), ((), ())), preferred_element_type=F32)


def _select_blocks(imp, q_pos, n_sel, n_top):
    lane = lax.broadcasted_iota(jnp.int32, imp.shape, 1)
    q_blk = q_pos // SEL_BLOCK
    forced = jnp.where(lane == 0, 1.0, jnp.where(lane == q_blk, 1.0, jnp.where(lane == q_blk - 1, 1.0, 0.0)))
    score = jnp.where(lane * SEL_BLOCK <= q_pos, imp, NEG_INF)
    score = jnp.where(forced > 0.5, 1e9, score)
    score = jnp.where(lane < n_sel, score, -3e38)
    rank = jnp.zeros(imp.shape, F32)
    for s in range(n_sel):
        col = score[:, s:s + 1]
        tie = jnp.where(lane > s, 1.0, 0.0)
        rank = rank + jnp.where(col > score, 1.0, jnp.where(col == score, tie, 0.0))
    return jnp.where(rank < n_top, 1.0, 0.0)


def _gate_columns(gl_ref, j):
    g = jax.nn.sigmoid(gl_ref[...])
    cols = [g[:, SM_GATE + 3 * h + j:SM_GATE + 3 * h + j + 1] for h in range(NSA_HEADS)]
    return jnp.stack(cols, axis=0)


def _nsa_prompt_body(q_ref, kc_ref, vc_ref, sk_ref, sv_ref, wk_ref, wv_ref, bc_ref, nb_ref, gl_ref, ov_ref, ex_ref,
                     o_ref, mb_ref, m_ref, l_ref, acc_ref, out_ref, *, n_sel, n_top):
    i = pl.program_id(1)
    tq = q_ref.shape[0]
    h = NSA_HEADS
    q = _stack_queries(q_ref, tq)

    s = _dot_nt(q, kc_ref[0]).reshape(h, tq, -1) + bc_ref[...]
    m = jnp.max(s, axis=-1, keepdims=True)
    p = jnp.where(s > 0.5 * MASKED, jnp.exp(s - m), 0.0)
    l = jnp.sum(p, axis=-1, keepdims=True)
    p = p * (1.0 / jnp.where(l > 0.0, l, 1.0))
    o_cmp = jnp.dot(p.reshape(h * tq, -1).astype(BF16), vc_ref[0], preferred_element_type=F32)
    out_ref[...] = _gate_columns(gl_ref, 0) * o_cmp.reshape(h, tq, LANES)

    q_pos = i * tq + lax.broadcasted_iota(jnp.int32, (tq, LANES), 0)
    for kv in range(NSA_KV_HEADS):
        pk = p[kv * NSA_GROUP]
        for g in range(1, NSA_GROUP):
            pk = pk + p[kv * NSA_GROUP + g]
        imp = jnp.dot(pk, ov_ref[...], precision=lax.Precision.HIGHEST, preferred_element_type=F32)
        sel = _select_blocks(imp, q_pos, n_sel, n_top)
        keys = jnp.dot(sel.astype(BF16), ex_ref[...], preferred_element_type=F32)
        mb_ref[kv] = (keys - 1.0) * (-MASKED)

    tk = ATT_TILE
    kvg = lambda x: x.reshape(NSA_KV_HEADS, NSA_GROUP, tq, x.shape[-1])

    near0 = jnp.maximum(i - 1, 0)
    near_rows = pl.ds(pl.multiple_of(near0 * tk, tk), 2 * tk)
    first = i == 0
    near_bias = jnp.concatenate([nb_ref[jnp.where(first, NB_SAME, NB_PREV)],
                                 nb_ref[jnp.where(first, NB_NONE, NB_SAME)]], axis=-1)
    s = _dot_nt(q, sk_ref[near_rows, :]).reshape(h, tq, 2 * tk) + near_bias
    s = (kvg(s) + mb_ref[:, :, near_rows][:, None]).reshape(h, tq, 2 * tk)
    m = jnp.max(s, axis=-1, keepdims=True)
    p = jnp.exp(s - m)
    m_ref[...] = m
    l_ref[...] = jnp.sum(p, axis=-1, keepdims=True)
    acc_ref[...] = jnp.dot(p.reshape(h * tq, 2 * tk).astype(BF16), sv_ref[near_rows, :],
                           preferred_element_type=F32).reshape(h, tq, LANES)

    far_end = (i - 1) * tk
    far_bias = nb_ref[NB_FAR][:, 0:1, 0:1]
    col = lax.broadcasted_iota(jnp.int32, (1, 1, FAR_TILE), 2)

    def far_step(f, carry):
        rows = pl.ds(pl.multiple_of(f * FAR_TILE, FAR_TILE), FAR_TILE)
        edge = jnp.where(f * FAR_TILE + col < far_end, 0.0, MASKED)
        s = _dot_nt(q, sk_ref[rows, :]).reshape(h, tq, FAR_TILE) + (far_bias + edge)
        s = (kvg(s) + mb_ref[:, :, rows][:, None]).reshape(h, tq, FAR_TILE)
        m_old = m_ref[...]
        m_new = jnp.maximum(m_old, jnp.max(s, axis=-1, keepdims=True))
        alpha = jnp.exp(m_old - m_new)
        p = jnp.exp(s - m_new)
        l_ref[...] = alpha * l_ref[...] + jnp.sum(p, axis=-1, keepdims=True)
        pv = jnp.dot(p.reshape(h * tq, FAR_TILE).astype(BF16), sv_ref[rows, :], preferred_element_type=F32)
        acc_ref[...] = alpha * acc_ref[...] + pv.reshape(h, tq, LANES)
        m_ref[...] = m_new
        return carry
    lax.fori_loop(0, (jnp.maximum(far_end, 0) + FAR_TILE - 1) // FAR_TILE, far_step, 0)
    out_ref[...] += _gate_columns(gl_ref, 1) * (acc_ref[...] * (1.0 / l_ref[...]))

    n_back = WINDOW // tk
    win0 = jnp.maximum(i - n_back, 0)
    win_rows = pl.ds(pl.multiple_of(win0 * tk, tk), (n_back + 1) * tk)
    tiles = []
    for c in range(n_back + 1):
        r = i - (win0 + c)
        tiles.append(nb_ref[jnp.where(r < 0, NB_NONE, jnp.where(r == n_back, NB_EDGE, jnp.minimum(r, NB_FAR)))])
    s = _dot_nt(q, wk_ref[win_rows, :]).reshape(h, tq, (n_back + 1) * tk) + jnp.concatenate(tiles, axis=-1)
    p = jnp.exp(s - jnp.max(s, axis=-1, keepdims=True))
    l = jnp.sum(p, axis=-1, keepdims=True)
    o_win = jnp.dot(p.reshape(h * tq, (n_back + 1) * tk).astype(BF16), wv_ref[win_rows, :],
                    preferred_element_type=F32).reshape(h, tq, LANES)
    out_ref[...] += _gate_columns(gl_ref, 2) * (o_win * (1.0 / l))

    low = lax.broadcasted_iota(jnp.int32, (tq, LANES), 1) < HEAD_DIM
    for g in range(NSA_GROUP):
        o_ref[:, g * LANES:(g + 1) * LANES] = jnp.where(low, out_ref[g], out_ref[NSA_GROUP + g]).astype(BF16)


def nsa_prompt(q, kc, vc, skb, svb, wkb, wvb, small, rel_bias, B, T):
    tq = ATT_TILE
    nq = T // tq
    assert T % FAR_TILE == 0 and T >= WINDOW + ATT_TILE and WINDOW % ATT_TILE == 0
    n_chunk = kc.shape[1]
    n_sel = -(-T // SEL_BLOCK)
    n_top = min(SEL_TOP, n_sel)
    bias_c = cmp_bias_table(rel_bias, jnp.arange(T), n_chunk)
    nb = near_bias_tables(rel_bias)
    ov = overlap_matrix(n_chunk, n_sel)
    ex = expand_matrix(T)
    h = NSA_HEADS
    full = lambda a: pl.BlockSpec(a.shape, lambda b, i: (0,) * a.ndim)
    seq = pl.BlockSpec((T, NSA_KV_DIM), lambda b, i: (b, 0))
    tok = pl.BlockSpec((1, n_chunk, NSA_KV_DIM), lambda b, i: (b, 0, 0))
    return pl.pallas_call(
        functools.partial(_nsa_prompt_body, n_sel=n_sel, n_top=n_top),
        grid=(B, nq),
        in_specs=[pl.BlockSpec((tq, NSA_Q_DIM), lambda b, i: (b * nq + i, 0)),
                  tok, tok, seq, seq, seq, seq,
                  pl.BlockSpec((h, tq, n_chunk), lambda b, i: (0, i, 0)),
                  full(nb),
                  pl.BlockSpec((tq, LANES), lambda b, i: (b * nq + i, 0)),
                  full(ov), full(ex)],
        out_specs=pl.BlockSpec((tq, NSA_Q_DIM), lambda b, i: (b * nq + i, 0)),
        out_shape=jax.ShapeDtypeStruct((B * T, NSA_Q_DIM), BF16),
        scratch_shapes=[pltpu.VMEM((NSA_KV_HEADS, tq, T), F32),
                        pltpu.VMEM((h, tq, 1), F32), pltpu.VMEM((h, tq, 1), F32),
                        pltpu.VMEM((h, tq, LANES), F32), pltpu.VMEM((h, tq, LANES), F32)],
        compiler_params=pltpu.CompilerParams(dimension_semantics=("parallel", "arbitrary"),
                                             vmem_limit_bytes=VMEM_LIMIT),
        name="nsa_prompt",
    )(q, kc, vc, skb, svb, wkb, wvb, bias_c, nb, small, ov, ex)


def sample_bias_tables(rel_bias, past, T, W):
    tbl = rel_bias.astype(F32).T
    t = jnp.arange(T)[:, None]
    pad = jnp.arange(ATT_TILE)[None, :]
    new = jnp.where((pad < T) & (t >= pad), tbl[:, t5_bucket(t - pad)], MASKED)
    sel = tbl[:, t5_bucket(past + t - jnp.arange(past)[None, :])]
    rel_w = W + t - jnp.arange(W)[None, :]
    win = jnp.where(rel_w < WINDOW, tbl[:, t5_bucket(rel_w)], MASKED)
    cmp = cmp_bias_table(rel_bias, past + jnp.arange(T), past // CMP_STRIDE)
    return cmp, jnp.concatenate([sel, new], axis=-1), jnp.concatenate([win, new], axis=-1)


def _softmax_rows(s):
    m = jnp.max(s, axis=-1, keepdims=True)
    p = jnp.exp(s - m)
    return p * (1.0 / jnp.sum(p, axis=-1, keepdims=True))


def _nsa_sample_body(pt_ref, q_ref, skn_ref, svn_ref, wkn_ref, wvn_ref, gl_ref, wink_ref, winv_ref,
                     bc_ref, bs_ref, bw_ref, ov_ref, ex_ref, pk_ref, pv_ref, w1k_ref, w1v_ref, w2k_ref, w2v_ref,
                     *rest, n_pages, n_sel, n_top, past):
    pages = [rest[i * n_pages:(i + 1) * n_pages] for i in range(4)]
    o_ref, wko_ref, wvo_ref, s_ref, new_ref = rest[4 * n_pages:]
    h = NSA_HEADS
    tq = q_ref.shape[0]
    tk = ATT_TILE
    q = _stack_queries(q_ref, tq)

    new_ref[...] = jnp.zeros(new_ref.shape, F32)
    for n, r in enumerate((skn_ref, svn_ref, wkn_ref, wvn_ref)):
        new_ref[n, 0:tq, :] = r[...]

    xk = jnp.concatenate([r[0] for r in pages[0]], axis=0)
    xv = jnp.concatenate([r[0] for r in pages[1]], axis=0)
    kc = _compress_one(xk, pk_ref, w1k_ref, w2k_ref).astype(BF16)
    vc = _compress_one(xv, pv_ref, w1v_ref, w2v_ref).astype(BF16)
    s = _dot_nt(q, kc).reshape(h, tq, -1) + bc_ref[...]
    p = _softmax_rows(s)
    o_cmp = jnp.dot(p.reshape(h * tq, -1).astype(BF16), vc, preferred_element_type=F32)
    out = _gate_columns(gl_ref, 0) * o_cmp.reshape(h, tq, LANES)

    q_pos = past + lax.broadcasted_iota(jnp.int32, (tq, LANES), 0)
    masks = []
    for kv in range(NSA_KV_HEADS):
        pkv = p[kv * NSA_GROUP]
        for g in range(1, NSA_GROUP):
            pkv = pkv + p[kv * NSA_GROUP + g]
        imp = jnp.dot(pkv, ov_ref[...], precision=lax.Precision.HIGHEST, preferred_element_type=F32)
        sel = _select_blocks(imp, q_pos, n_sel, n_top)
        keys = jnp.dot(sel.astype(BF16), ex_ref[...], preferred_element_type=F32)
        masks.append((keys - 1.0) * (-MASKED))
    mask = jnp.stack(masks, axis=0)
    for pg in range(n_pages):
        s_ref[:, pg * tk:(pg + 1) * tk] = _dot_nt(q, pages[2][pg][0].astype(BF16))
    s_ref[:, n_pages * tk:(n_pages + 1) * tk] = _dot_nt(q, new_ref[0].astype(BF16))
    width = (n_pages + 1) * tk
    s = s_ref[...].reshape(h, tq, width) + bs_ref[...]
    s = (s.reshape(NSA_KV_HEADS, NSA_GROUP, tq, width) + mask[:, None]).reshape(h, tq, width)
    p = _softmax_rows(s).reshape(h * tq, width).astype(BF16)
    o_sel = jnp.dot(p[:, n_pages * tk:], new_ref[1].astype(BF16), preferred_element_type=F32)
    for pg in range(n_pages):
        o_sel = o_sel + jnp.dot(p[:, pg * tk:(pg + 1) * tk], pages[3][pg][0].astype(BF16),
                                preferred_element_type=F32)
    out = out + _gate_columns(gl_ref, 1) * o_sel.reshape(h, tq, LANES)

    w = wink_ref.shape[1]
    n_wt = w // tk
    for wt in range(n_wt):
        s_ref[:, wt * tk:(wt + 1) * tk] = _dot_nt(q, wink_ref[0, wt * tk:(wt + 1) * tk, :].astype(BF16))
    s_ref[:, w:w + tk] = _dot_nt(q, new_ref[2].astype(BF16))
    s = s_ref[:, 0:w + tk].reshape(h, tq, w + tk) + bw_ref[...]
    p = _softmax_rows(s).reshape(h * tq, w + tk).astype(BF16)
    o_win = jnp.dot(p[:, w:], new_ref[3].astype(BF16), preferred_element_type=F32)
    for wt in range(n_wt):
        o_win = o_win + jnp.dot(p[:, wt * tk:(wt + 1) * tk], winv_ref[0, wt * tk:(wt + 1) * tk, :].astype(BF16),
                                preferred_element_type=F32)
    out = out + _gate_columns(gl_ref, 2) * o_win.reshape(h, tq, LANES)

    low = lax.broadcasted_iota(jnp.int32, (tq, LANES), 1) < HEAD_DIM
    for g in range(NSA_GROUP):
        o_ref[:, g * LANES:(g + 1) * LANES] = jnp.where(low, out[g], out[NSA_GROUP + g])

    keep = wko_ref.shape[1] - tq
    wko_ref[0, 0:keep, :] = wink_ref[0, w - keep:w, :]
    wko_ref[0, keep:keep + tq, :] = wkn_ref[...]
    wvo_ref[0, 0:keep, :] = winv_ref[0, w - keep:w, :]
    wvo_ref[0, keep:keep + tq, :] = wvn_ref[...]


def nsa_sample(q, skn, svn, wkn, wvn, small, pools, win_k, win_v, page_table, rel_bias, pk, pv, B, T):
    n_pages = page_table.shape[1]
    past = n_pages * PAGE_SIZE
    W = win_k.shape[1]
    L = past + T
    n_cmp = (L - CMP_LEN) // CMP_STRIDE + 1
    assert PAGE_SIZE == ATT_TILE and T % 8 == 0 and T <= SEL_BLOCK and past % SEL_BLOCK == 0
    assert (n_cmp - 1) * CMP_STRIDE + CMP_LEN <= past and n_cmp >= past // CMP_STRIDE - CMP_HALVES + 1
    assert W % ATT_TILE == 0 and W + T >= WINDOW
    n_sel = -(-L // SEL_BLOCK)
    n_top = min(SEL_TOP, n_sel)
    w_keep = min(WINDOW, W + T)
    bc, bs, bw = sample_bias_tables(rel_bias, past, T, W)
    ov = overlap_matrix(past // CMP_STRIDE, n_sel)
    ex = expand_matrix(past + ATT_TILE)
    n_pool = pools[0].shape[0]
    chunks = PAGE_SIZE // CMP_STRIDE
    cmp_pools = [p.reshape(n_pool, chunks, CHUNK_W) for p in pools[:2]]
    sel_pools = [p.reshape(n_pool, PAGE_SIZE, NSA_KV_DIM) for p in pools[2:]]
    wk3 = win_k.reshape(B, W, NSA_KV_DIM)
    wv3 = win_v.reshape(B, W, NSA_KV_DIM)

    full = lambda a: pl.BlockSpec(a.shape, lambda b, pt: (0,) * a.ndim)
    row = lambda w: pl.BlockSpec((T, w), lambda b, pt: (b, 0))
    seq = lambda n: pl.BlockSpec((1, n, NSA_KV_DIM), lambda b, pt: (b, 0, 0))
    page = lambda shape, pg: pl.BlockSpec((1,) + shape, lambda b, pt: (pt[b, pg], 0, 0))
    page_specs, page_args = [], []
    for pool, shape in ((cmp_pools[0], (chunks, CHUNK_W)), (cmp_pools[1], (chunks, CHUNK_W)),
                        (sel_pools[0], (PAGE_SIZE, NSA_KV_DIM)), (sel_pools[1], (PAGE_SIZE, NSA_KV_DIM))):
        for pg in range(n_pages):
            page_specs.append(page(shape, pg))
            page_args.append(pool)
    tables = (bc, bs, bw, ov, ex, pk[0], pv[0], pk[1], pv[1], pk[2], pv[2])
    grid_spec = pltpu.PrefetchScalarGridSpec(
        num_scalar_prefetch=1,
        grid=(B,),
        in_specs=[row(NSA_Q_DIM), row(NSA_KV_DIM), row(NSA_KV_DIM), row(NSA_KV_DIM), row(NSA_KV_DIM), row(LANES),
                  seq(W), seq(W)] + [full(a) for a in tables] + page_specs,
        out_specs=[row(NSA_Q_DIM), seq(w_keep), seq(w_keep)],
        scratch_shapes=[pltpu.VMEM((NSA_HEADS * T, past + ATT_TILE), F32),
                        pltpu.VMEM((4, ATT_TILE, NSA_KV_DIM), F32)],
    )
    return pl.pallas_call(
        functools.partial(_nsa_sample_body, n_pages=n_pages, n_sel=n_sel, n_top=n_top, past=past),
        grid_spec=grid_spec,
        out_shape=[jax.ShapeDtypeStruct((B * T, NSA_Q_DIM), F32),
                   jax.ShapeDtypeStruct((B, w_keep, NSA_KV_DIM), F32),
                   jax.ShapeDtypeStruct((B, w_keep, NSA_KV_DIM), F32)],
        compiler_params=pltpu.CompilerParams(dimension_semantics=("arbitrary",), vmem_limit_bytes=VMEM_LIMIT),
        name="nsa_sample",
    )(page_table, q, skn, svn, wkn, wvn, small, wk3, wv3, *tables, *page_args)


HIST_ROWS = 8


def _split_bf16(x):
    hi = x.astype(BF16)
    lo = (x - hi.astype(F32)).astype(BF16)
    return hi, lo


def _mm_hi(a, b):
    ah, al = _split_bf16(a)
    bh, bl = _split_bf16(b)
    d = lambda x, y: jnp.dot(x, y, preferred_element_type=F32)
    return d(ah, bh) + (d(ah, bl) + d(al, bh))


def _bmm(a, b):
    return jnp.einsum('hij,hjk->hik', a, b, preferred_element_type=F32)


def _bmm_nt(a, b):
    return jnp.einsum('hid,hjd->hij', a, b, preferred_element_type=F32)


def _bmm_hi(a, b):
    ah, al = _split_bf16(a)
    bh, bl = _split_bf16(b)
    return _bmm(ah, bh) + (_bmm(ah, bl) + _bmm(al, bh))


def _cumsum_rows(tri, x):
    hi, lo = _split_bf16(x)
    lo2 = (x - hi.astype(F32) - lo.astype(F32)).astype(BF16)
    d = lambda y: jnp.dot(tri, y, preferred_element_type=F32)
    return d(hi) + (d(lo) + d(lo2))


def _softplus(x):
    return jnp.maximum(x, 0.0) + jnp.log1p(jnp.exp(-jnp.abs(x)))


def _silu(x):
    return x * jax.nn.sigmoid(x)


def _deltanet_body(x_ref, sm_ref, dg_ref, hist_ref, s0_ref, cw_ref, apar_ref, dt_ref, ng_ref,
                   o_ref, s_out_ref, hist_out_ref, xbuf_ref, s_ref):
    j = pl.program_id(1)
    c = x_ref.shape[0]
    hd = DN_HEAD_DIM
    n_hist = CONV_W - 1

    @pl.when(j == 0)
    def _():
        xbuf_ref[HIST_ROWS - n_hist:HIST_ROWS, :] = hist_ref[0]
        s_ref[...] = s0_ref[0]

    xbuf_ref[HIST_ROWS:HIST_ROWS + c, :] = x_ref[...]
    y = None
    for t in range(CONV_W):
        lo = HIST_ROWS - n_hist + t
        term = xbuf_ref[lo:lo + c, :] * cw_ref[t:t + 1, :]
        y = term if y is None else y + term
    y = _silu(y)
    tail = xbuf_ref[HIST_ROWS + c - n_hist:HIST_ROWS + c, :]
    xbuf_ref[HIST_ROWS - n_hist:HIST_ROWS, :] = tail

    sm = sm_ref[...]
    beta_all = jax.nn.sigmoid(sm)
    g_all = apar_ref[...] * _softplus(sm + dt_ref[...])
    row = lax.broadcasted_iota(jnp.int32, (c, c), 0)
    col = lax.broadcasted_iota(jnp.int32, (c, c), 1)
    incl = row >= col
    strict = row > col
    tri = jnp.where(incl, 1.0, 0.0).astype(BF16)
    gcum_all = _cumsum_rows(tri, g_all)
    gcum_t = gcum_all.T
    eye = jnp.where(row == col, 1.0, 0.0)

    heads = lambda x: jnp.stack([x[:, h * hd:(h + 1) * hd] for h in range(DN_HEADS)], axis=0)
    q = heads(y[:, 0:DN_DIM])
    k = heads(y[:, DN_DIM:2 * DN_DIM])
    v = heads(y[:, 2 * DN_DIM:3 * DN_DIM])
    q = q * lax.rsqrt(jnp.sum(q * q, axis=-1, keepdims=True) + EPS) * hd ** -0.5
    k = k * lax.rsqrt(jnp.sum(k * k, axis=-1, keepdims=True) + EPS)
    beta = jnp.stack([beta_all[:, SM_BETA + h:SM_BETA + h + 1] for h in range(DN_HEADS)], axis=0)
    gc = jnp.stack([gcum_all[:, SM_DECAY + h:SM_DECAY + h + 1] for h in range(DN_HEADS)], axis=0)
    gr = jnp.stack([gcum_t[SM_DECAY + h:SM_DECAY + h + 1, :] for h in range(DN_HEADS)], axis=0)
    g_last = gr[:, :, c - 1:c]
    decay = jnp.where(incl, jnp.exp(jnp.where(incl, gc - gr, 0.0)), 0.0)
    eg = jnp.exp(gc)
    kb = k * beta
    kbf = k.astype(BF16)
    a = jnp.where(strict, _bmm_nt(kb.astype(BF16), kbf) * decay, 0.0)
    t_inv = eye - a
    a_pow = a
    for _ in range(int(math.log2(c)) - 1):
        a_pow = _bmm_hi(a_pow, a_pow)
        t_inv = t_inv + _bmm_hi(t_inv, a_pow)
    uw = _bmm_hi(t_inv, jnp.concatenate([v * beta, kb * eg], axis=2))
    u, w = uw[:, :, :hd], uw[:, :, hd:]
    qk = jnp.where(incl, _bmm_nt(q.astype(BF16), kbf) * decay, 0.0)
    qg = q * eg
    kg = k * jnp.exp(g_last - gc)
    s_old = s_ref[...]
    ws = _bmm(jnp.concatenate([w, qg], axis=1).astype(BF16), s_old.astype(BF16))
    v_new = u - ws[:, :c]
    vb = v_new.astype(BF16)
    o = ws[:, c:] + _bmm(qk.astype(BF16), vb)
    s_ref[...] = s_old * jnp.exp(g_last) + jnp.einsum('hck,hcv->hkv', kg.astype(BF16), vb,
                                                      preferred_element_type=F32)
    o = o * lax.rsqrt(jnp.mean(o * o, axis=-1, keepdims=True) + EPS) * ng_ref[...]
    o = o * _silu(heads(dg_ref[...]))
    o_ref[...] = jnp.concatenate([o[h] for h in range(DN_HEADS)], axis=1).astype(o_ref.dtype)

    @pl.when(j == pl.num_programs(1) - 1)
    def _():
        s_out_ref[0] = s_ref[...]
        hist_out_ref[0] = tail


def deltanet(qkv_raw, small, dg, conv_hist, s0, lp, B, T):
    c = DN_CHUNK if T >= DN_CHUNK else T
    assert T % c == 0 and c % 8 == 0 and c >= CONV_W - 1 and (c & (c - 1)) == 0
    n = T // c
    lanes = jnp.arange(LANES)
    put = lambda vals, base: jnp.zeros((LANES,), F32).at[base + jnp.arange(DN_HEADS)].set(vals).reshape(1, LANES)
    apar = put(-jnp.exp(lp['dn_a_log'].astype(F32)), SM_DECAY)
    dtb = put(lp['dn_dt_bias'].astype(F32), SM_DECAY)
    del lanes
    rows = lambda w: pl.BlockSpec((c, w), lambda b, j: (b * n + j, 0))
    per_b = lambda a: pl.BlockSpec((1,) + a.shape[1:], lambda b, j: (b,) + (0,) * (a.ndim - 1))
    full = lambda a: pl.BlockSpec(a.shape, lambda b, j: (0,) * a.ndim)
    ng = lp['dn_norm_g'].reshape(1, DN_HEAD_DIM).astype(F32)
    return pl.pallas_call(
        _deltanet_body,
        grid=(B, n),
        in_specs=[rows(DN_CONV_DIM), rows(LANES), rows(DN_DIM), per_b(conv_hist), per_b(s0),
                  full(lp['conv_w']), full(apar), full(dtb), full(ng)],
        out_specs=[rows(DN_DIM), per_b(s0), per_b(conv_hist)],
        out_shape=[jax.ShapeDtypeStruct((B * T, DN_DIM), BF16 if c % 16 == 0 else F32),
                   jax.ShapeDtypeStruct(s0.shape, F32),
                   jax.ShapeDtypeStruct(conv_hist.shape, F32)],
        scratch_shapes=[pltpu.VMEM((HIST_ROWS + c, DN_CONV_DIM), F32),
                        pltpu.VMEM((DN_HEADS, DN_HEAD_DIM, DN_HEAD_DIM), F32)],
        compiler_params=pltpu.CompilerParams(dimension_semantics=("parallel", "arbitrary"),
                                             vmem_limit_bytes=VMEM_LIMIT),
        name="deltanet",
    )(qkv_raw, small, dg, conv_hist, s0, lp['conv_w'], apar, dtb, ng)


RT_GROUP = 0
RT_EXPERT = N_GROUPS


def pack_router(wg, bg, we, be):
    w = jnp.concatenate([wg, we], axis=1)
    b = jnp.concatenate([bg, be], axis=0)
    pad = LANES - w.shape[1]
    return jnp.pad(w, ((0, 0), (0, pad))).astype(BF16), jnp.pad(b, (0, pad)).reshape(1, LANES).astype(F32)


def _route(r):
    lane = lax.broadcasted_iota(jnp.int32, r.shape, 1).astype(F32)
    big = float(LANES)
    is_grp = lane < N_GROUPS
    lg = jnp.where(is_grp, r, -jnp.inf)
    mg = jnp.max(lg, axis=-1, keepdims=True)
    eg = jnp.where(is_grp, jnp.exp(lg - mg), 0.0)
    grp = jnp.min(jnp.where(lg == mg, lane, big), axis=-1, keepdims=True)
    wg = 1.0 / jnp.sum(eg, axis=-1, keepdims=True)
    first = RT_EXPERT + grp * EXPERTS_PER_GROUP
    in_grp = (lane >= first) & (lane < first + EXPERTS_PER_GROUP)
    le = jnp.where(in_grp, r, -jnp.inf)
    me = jnp.max(le, axis=-1, keepdims=True)
    ee = jnp.where(in_grp, jnp.exp(le - me), 0.0)
    pe = jnp.where(in_grp, ee / jnp.sum(ee, axis=-1, keepdims=True), -1.0)
    v1 = jnp.max(pe, axis=-1, keepdims=True)
    i1 = jnp.min(jnp.where(pe == v1, lane, big), axis=-1, keepdims=True)
    pe2 = jnp.where(lane == i1, -1.0, pe)
    v2 = jnp.max(pe2, axis=-1, keepdims=True)
    i2 = jnp.min(jnp.where(pe2 == v2, lane, big), axis=-1, keepdims=True)
    tot = v1 + v2
    return jnp.where(lane == i1, wg * (v1 / tot), jnp.where(lane == i2, wg * (v2 / tot), 0.0))


def _finish_body(x_ref, on_ref, od_ref, g1_ref, sc_ref, sh_ref, g2_ref, won_ref, wod_ref, ng_ref, wr_ref, br_ref,
                 wgu_ref, wd_ref, fg_ref, o_ref, x1_ref, h_ref, gate_ref, y_ref):
    e = pl.program_id(1)
    tm, d = x_ref.shape
    nb = g1_ref.shape[0]
    per_batch = lambda a: a.reshape(nb, tm // nb, d)

    @pl.when(e == 0)
    def _():
        mix = jnp.dot(on_ref[...].astype(BF16), won_ref[...], preferred_element_type=F32) \
            + jnp.dot(od_ref[...].astype(BF16), wod_ref[...], preferred_element_type=F32)
        x1 = per_batch(x_ref[...]) + g1_ref[...] * per_batch(mix)
        x1 = x1.reshape(tm, d)
        x1_ref[...] = x1
        xn = x1 * lax.rsqrt(jnp.mean(x1 * x1, axis=-1, keepdims=True) + EPS) * ng_ref[...]
        h = (per_batch(xn) * (1 + sc_ref[...]) + sh_ref[...]).reshape(tm, d).astype(BF16)
        h_ref[...] = h
        gate_ref[...] = _route(jnp.dot(h, wr_ref[...], preferred_element_type=F32) + br_ref[...])
        y_ref[...] = jnp.zeros(y_ref.shape, F32)

    h = h_ref[...]
    gu = jnp.dot(h, wgu_ref[0], preferred_element_type=F32)
    he = _silu(gu[:, :D_EXPERT]) * gu[:, D_EXPERT:]
    out = jnp.dot(he.astype(BF16), wd_ref[0], preferred_element_type=F32)
    lane = lax.broadcasted_iota(jnp.int32, gate_ref.shape, 1)
    g_col = jnp.sum(jnp.where(lane == RT_EXPERT + e, gate_ref[...], 0.0), axis=-1, keepdims=True)
    y_ref[...] += g_col * out

    @pl.when(e == pl.num_programs(1) - 1)
    def _():
        x2 = per_batch(x1_ref[...]) + g2_ref[...] * per_batch(y_ref[...])
        x2 = x2.reshape(tm, d)
        o_ref[...] = x2 * lax.rsqrt(jnp.mean(x2 * x2, axis=-1, keepdims=True) + EPS) * fg_ref[...]


def finish_layer(x, o_nsa, o_dn, g1, sc2, sh2, g2, packed, norm_ffn_g, final_g, *, tm=1024):
    B, T, D = x.shape
    n = B * T
    tm = min(tm, n)
    rows = max(tm // T, 1)
    per = max(T // tm, 1)
    row = lambda w: pl.BlockSpec((tm, w), lambda i, e: (i, 0))
    mod = pl.BlockSpec((rows, 1, D), lambda i, e: (i // per, 0, 0))
    full = lambda a: pl.BlockSpec(a.shape, lambda i, e: (0,) * a.ndim)
    expert = lambda a: pl.BlockSpec((1,) + a.shape[1:], lambda i, e: (e, 0, 0))
    vec = lambda v: v.reshape(1, D).astype(F32)
    args = (x.reshape(n, D), o_nsa, o_dn, g1, sc2, sh2, g2, packed['w_out_nsa'], packed['w_out_dn'],
            vec(norm_ffn_g), packed['router_w'], packed['router_b'], packed['w_gate_up'], packed['w_down'],
            vec(final_g))
    specs = [row(D), row(o_nsa.shape[1]), row(o_dn.shape[1]), mod, mod, mod, mod] + \
            [full(a) for a in args[7:12]] + [expert(args[12]), expert(args[13]), full(args[14])]
    return pl.pallas_call(
        _finish_body,
        grid=(n // tm, N_EXPERTS),
        in_specs=specs,
        out_specs=row(D),
        out_shape=jax.ShapeDtypeStruct((n, D), F32),
        scratch_shapes=[pltpu.VMEM((tm, D), F32), pltpu.VMEM((tm, D), BF16), pltpu.VMEM((tm, LANES), F32),
                        pltpu.VMEM((tm, D), F32)],
        compiler_params=pltpu.CompilerParams(dimension_semantics=("parallel", "arbitrary"),
                                             vmem_limit_bytes=VMEM_LIMIT),
        name="finish_layer",
    )(*args).reshape(B, T, D)


def _final_norm_body(x_ref, g_ref, o_ref):
    x = x_ref[...]
    o_ref[...] = x * lax.rsqrt(jnp.mean(x * x, axis=-1, keepdims=True) + EPS) * g_ref[...]


def final_rmsnorm(x, g, *, tm=512):
    shp = x.shape
    x2 = x.reshape(-1, shp[-1])
    n, d = x2.shape
    return pl.pallas_call(
        _final_norm_body,
        grid=(n // tm,),
        in_specs=[pl.BlockSpec((tm, d), lambda i: (i, 0)), pl.BlockSpec((1, d), lambda i: (0, 0))],
        out_specs=pl.BlockSpec((tm, d), lambda i: (i, 0)),
        out_shape=jax.ShapeDtypeStruct(x2.shape, x2.dtype),
        compiler_params=pltpu.CompilerParams(dimension_semantics=("parallel",)),
        name="final_rmsnorm",
    )(x2, g.reshape(1, d)).reshape(shp)


def permute_w_out_nsa(w_out):
    w = w_out[:NSA_Q_DIM].reshape(NSA_KV_HEADS, NSA_GROUP, HEAD_DIM, -1)
    return w.transpose(1, 0, 2, 3).reshape(NSA_Q_DIM, -1).astype(BF16)


def layer_prompt(x, c, lp, rel_bias, packed):
    B, T, _ = x.shape
    sh1, sc1, g1, sh2, sc2, g2 = modulation(c, lp)
    (q, ck, cv, sk, sv, wk, wv, skb, svb, wkb, wvb, qkv_raw, dg, small) = in_projection(
        x, lp['norm_mix_g'], sc1, sh1, packed['w_in'])
    kc, vc = compress_tokens(ck, cv, packed['cmp_k'], packed['cmp_v'], B)
    o_nsa = nsa_prompt(q, kc, vc, skb, svb, wkb, wvb, small, rel_bias, B, T)
    kv = lambda r: r.reshape(B, T, NSA_KV_HEADS, HEAD_DIM)
    ck, cv, sk, sv, wk, wv = kv(ck), kv(cv), kv(sk), kv(sv), kv(wk), kv(wv)
    conv_hist = jnp.zeros((B, CONV_W - 1, DN_CONV_DIM), x.dtype)
    s0 = jnp.zeros((B, DN_HEADS, DN_HEAD_DIM, DN_HEAD_DIM), x.dtype)
    o_dn, s_new, conv_new = deltanet(qkv_raw, small, dg, conv_hist, s0, lp, B, T)
    y = finish_layer(x, o_nsa, o_dn, g1, sc2, sh2, g2, packed, lp['norm_ffn_g'], packed['final_norm_g'])
    w_keep = min(WINDOW, T)
    return y, (ck, cv, sk, sv, wk[:, -w_keep:], wv[:, -w_keep:], s_new, conv_new)


def layer_sample(x, c, cmp_k_pool, cmp_v_pool, sel_k_pool, sel_v_pool, win_k, win_v, s0, conv_hist,
                 page_table, lp, rel_bias, packed):
    B, T, _ = x.shape
    sh1, sc1, g1, sh2, sc2, g2 = modulation(c, lp)
    (q, ck, cv, sk, sv, wk, wv, _, _, _, _, qkv_raw, dg, small) = in_projection(
        x, lp['norm_mix_g'], sc1, sh1, packed['w_in'])
    o_nsa, wk_next, wv_next = nsa_sample(q, sk, sv, wk, wv, small, (cmp_k_pool, cmp_v_pool, sel_k_pool, sel_v_pool),
                                         win_k, win_v, page_table, rel_bias, packed['cmp_k'], packed['cmp_v'], B, T)
    kv = lambda r: r.reshape(B, -1, NSA_KV_HEADS, HEAD_DIM)
    o_dn, s_new, conv_new = deltanet(qkv_raw, small, dg, conv_hist, s0, lp, B, T)
    y = finish_layer(x, o_nsa, o_dn, g1, sc2, sh2, g2, packed, lp['norm_ffn_g'], packed['final_norm_g'])
    return y, (kv(ck), kv(cv), kv(sk), kv(sv), kv(wk_next), kv(wv_next), s_new, conv_new)


def kernel(x_prompt, x_sample, cache_cmp_k, cache_cmp_v, cache_sel_k, cache_sel_v, cache_win_k, cache_win_v,
           state_delta, state_conv, page_table, c_prompt, c_sample, rel_bias, w_ada, b_ada, norm_mix_g,
           norm_ffn_g, w_in, cmp_pos_k, cmp_w1_k, cmp_w2_k, cmp_pos_v, cmp_w1_v, cmp_w2_v, conv_w, dn_a_log,
           dn_dt_bias, dn_norm_g, w_out, router_group_w, router_group_b, router_expert_w, router_expert_b,
           expert_w_gate, expert_w_up, expert_w_down, final_norm_g):
    xp, xs = x_prompt, x_sample
    new_p, new_s = [], []
    for l in range(DEPTH):
        lp = dict(w_ada=w_ada[l], b_ada=b_ada[l], norm_mix_g=norm_mix_g[l], norm_ffn_g=norm_ffn_g[l],
                  w_in=w_in[l], cmp_pos_k=cmp_pos_k[l], cmp_w1_k=cmp_w1_k[l], cmp_w2_k=cmp_w2_k[l],
                  cmp_pos_v=cmp_pos_v[l], cmp_w1_v=cmp_w1_v[l], cmp_w2_v=cmp_w2_v[l], conv_w=conv_w[l],
                  dn_a_log=dn_a_log[l], dn_dt_bias=dn_dt_bias[l], dn_norm_g=dn_norm_g[l], w_out=w_out[l],
                  router_group_w=router_group_w[l], router_group_b=router_group_b[l],
                  router_expert_w=router_expert_w[l], router_expert_b=router_expert_b[l],
                  w_gate=expert_w_gate[l], w_up=expert_w_up[l], w_down=expert_w_down[l])
        packed = dict(w_in=pack_w_in(w_in[l]),
                      cmp_k=pack_compress(cmp_pos_k[l], cmp_w1_k[l], cmp_w2_k[l]),
                      cmp_v=pack_compress(cmp_pos_v[l], cmp_w1_v[l], cmp_w2_v[l]),
                      w_out_nsa=permute_w_out_nsa(w_out[l]),
                      w_out_dn=w_out[l][NSA_Q_DIM:].astype(BF16),
                      w_gate_up=jnp.concatenate([expert_w_gate[l], expert_w_up[l]], axis=-1).astype(BF16),
                      w_down=expert_w_down[l].astype(BF16),
                      final_norm_g=final_norm_g)
        packed['router_w'], packed['router_b'] = pack_router(router_group_w[l], router_group_b[l],
                                                             router_expert_w[l], router_expert_b[l])
        xp, st_p = layer_prompt(xp, c_prompt, lp, rel_bias, packed)
        xs, st_s = layer_sample(xs, c_sample, cache_cmp_k[l], cache_cmp_v[l], cache_sel_k[l], cache_sel_v[l],
                                cache_win_k[l], cache_win_v[l], state_delta[l], state_conv[l], page_table,
                                lp, rel_bias, packed)
        new_p.append(st_p)
        new_s.append(st_s)
    assert DEPTH == 1
    y_prompt, y_sample = xp, xs

    def stk(states, i):
        return jnp.stack([s[i] for s in states])

    return (y_prompt, y_sample,
            stk(new_p, 0), stk(new_p, 1), stk(new_p, 2), stk(new_p, 3),
            stk(new_p, 4), stk(new_p, 5), stk(new_p, 6), stk(new_p, 7),
            stk(new_s, 0), stk(new_s, 1), stk(new_s, 2), stk(new_s, 3),
            stk(new_s, 4), stk(new_s, 5), stk(new_s, 6), stk(new_s, 7))
```

```python
import functools
import math

import jax
import jax.numpy as jnp
import numpy as np
from jax import lax
from jax.experimental import pallas as pl
from jax.experimental.pallas import tpu as pltpu

D_MODEL = 1024
DEPTH = 1
PAGE_SIZE = 128

NSA_HEADS = 8
NSA_KV_HEADS = 2
NSA_GROUP = NSA_HEADS // NSA_KV_HEADS
HEAD_DIM = 64
CMP_LEN = 32
CMP_STRIDE = 16
CMP_HIDDEN = 4 * HEAD_DIM
SEL_BLOCK = 64
SEL_TOP = 16
WINDOW = 512
WIN_Q_BLOCK = 128
SEL_Q_BLOCK = 64
DN_HEADS = 8
DN_HEAD_DIM = 64
CONV_W = 4
DN_CHUNK = 64
NUM_BUCKETS = 32
MAX_DISTANCE = 128
N_GROUPS = 4
EXPERTS_PER_GROUP = 8
N_EXPERTS = N_GROUPS * EXPERTS_PER_GROUP
TOP_K_IN_GROUP = 2
D_EXPERT = D_MODEL // 4

EPS = 1e-6
NEG_INF = -1e30
F32 = jnp.float32
BF16 = jnp.bfloat16

NSA_Q_DIM = NSA_HEADS * HEAD_DIM
NSA_KV_DIM = NSA_KV_HEADS * HEAD_DIM
DN_DIM = DN_HEADS * DN_HEAD_DIM
DN_CONV_DIM = 3 * DN_DIM
MIX_WIDTH = NSA_Q_DIM + DN_DIM
IN_SPLITS = (NSA_Q_DIM, NSA_KV_DIM, NSA_KV_DIM, NSA_KV_DIM, NSA_KV_DIM, NSA_KV_DIM, NSA_KV_DIM,
             3 * NSA_HEADS, DN_CONV_DIM, DN_HEADS, DN_HEADS, DN_DIM)
IN_DIM = sum(IN_SPLITS)

LANES = 128
VMEM_LIMIT = 56 * 1024 * 1024

ATT_TILE = 128
MASKED = -1e30


def rmsnorm(x, g):
    xf = x.astype(F32)
    y = xf * lax.rsqrt(jnp.mean(xf * xf, axis=-1, keepdims=True) + EPS)
    return (y * g.astype(F32)).astype(x.dtype)


def l2norm(x):
    xf = x.astype(F32)
    return (xf * lax.rsqrt(jnp.sum(xf * xf, axis=-1, keepdims=True) + EPS)).astype(x.dtype)


def t5_bucket(rel):
    n = jnp.maximum(rel, 0)
    max_exact = NUM_BUCKETS // 2
    nf = jnp.maximum(n, 1).astype(F32)
    large = max_exact + (jnp.log(nf / max_exact) / math.log(MAX_DISTANCE / max_exact)
                         * (NUM_BUCKETS - max_exact)).astype(jnp.int32)
    large = jnp.minimum(large, NUM_BUCKETS - 1)
    return jnp.where(n < max_exact, n, large)


def rel_bias_heads(rel, rel_bias):
    b = rel_bias.astype(F32)[t5_bucket(rel)]
    b = jnp.moveaxis(b, -1, -3)
    return b.reshape(b.shape[:-3] + (NSA_KV_HEADS, NSA_GROUP) + b.shape[-2:])


def attend(q, k, v, bias, mask):
    s = jnp.einsum('bntkgd,bnlkd->bnkgtl', q, k).astype(F32) * HEAD_DIM ** -0.5 + bias
    m = mask[:, None, None]
    p = jax.nn.softmax(jnp.where(m, s, NEG_INF), axis=-1)
    p = jnp.where(m, p, 0.0)
    o = jnp.einsum('bnkgtl,bnlkd->bntkgd', p.astype(v.dtype), v)
    return o, p


def compress_rows(rows, pos_emb, w1, w2):
    B, L = rows.shape[:2]
    n_cmp = (L - CMP_LEN) // CMP_STRIDE + 1
    idx = jnp.arange(n_cmp)[:, None] * CMP_STRIDE + jnp.arange(CMP_LEN)[None, :]
    blk = rows[:, idx] + pos_emb[:, None, :]
    blk = jnp.swapaxes(blk, 2, 3).reshape(B, n_cmp, NSA_KV_HEADS, CMP_LEN * HEAD_DIM)
    return jax.nn.silu(blk @ w1) @ w2


def nsa_cmp_sel(q, q_pos, kc_rows, vc_rows, ks_rows, vs_rows, lp, rel_bias):
    B, Tq = q.shape[:2]
    L = kc_rows.shape[1]
    kc = compress_rows(kc_rows, lp['cmp_pos_k'], lp['cmp_w1_k'], lp['cmp_w2_k'])
    vc = compress_rows(vc_rows, lp['cmp_pos_v'], lp['cmp_w1_v'], lp['cmp_w2_v'])
    n_cmp = kc.shape[1]
    cmp_end = jnp.arange(n_cmp) * CMP_STRIDE + CMP_LEN - 1
    rel_c = q_pos[:, None] - cmp_end[None, :]
    o_cmp, p_cmp = attend(q[:, None], kc[:, None], vc[:, None],
                          rel_bias_heads(rel_c[None], rel_bias), (rel_c >= 0)[None])
    o_cmp = o_cmp[:, 0]
    n_sel = -(-L // SEL_BLOCK)
    c_start = jnp.arange(n_cmp) * CMP_STRIDE
    s_start = jnp.arange(n_sel) * SEL_BLOCK
    overlap = jnp.clip(jnp.minimum(c_start[:, None] + CMP_LEN, s_start[None] + SEL_BLOCK)
                       - jnp.maximum(c_start[:, None], s_start[None]), 0, None).astype(F32) / CMP_LEN
    imp = jnp.einsum('bkgtc,cs->btks', p_cmp[:, 0], overlap)
    q_blk = q_pos // SEL_BLOCK
    sb = jnp.arange(n_sel)
    forced = (sb[None] == 0) | (sb[None] == q_blk[:, None]) | (sb[None] == q_blk[:, None] - 1)
    avail = s_start[None] <= q_pos[:, None]
    score = jnp.where(forced[None, :, None], 1e9, jnp.where(avail[None, :, None], imp, NEG_INF))
    n_top = min(SEL_TOP, n_sel)
    _, sel_idx = lax.top_k(score, n_top)
    pad = n_sel * SEL_BLOCK - L

    def to_blocks(r):
        r = jnp.pad(r, ((0, 0), (0, pad), (0, 0), (0, 0)))
        return r.reshape(B, n_sel, SEL_BLOCK, NSA_KV_HEADS, HEAD_DIM).transpose(0, 3, 1, 2, 4)
    ks_b = to_blocks(ks_rows)
    vs_b = to_blocks(vs_rows)
    bi = jnp.arange(B)[:, None, None, None]
    ki = jnp.arange(NSA_KV_HEADS)[None, None, :, None]
    tbl = rel_bias.astype(F32).reshape(NUM_BUCKETS, NSA_KV_HEADS, NSA_GROUP).transpose(1, 0, 2)
    qb = SEL_Q_BLOCK if Tq % SEL_Q_BLOCK == 0 else Tq
    nqb = Tq // qb

    def sel_block(args):
        q_i, pos_i, idx_i = args
        kg = ks_b[bi, ki, idx_i]
        vg = vs_b[bi, ki, idx_i]
        k_pos = idx_i[..., None] * SEL_BLOCK + jnp.arange(SEL_BLOCK)
        rel = pos_i[None, :, None, None, None] - k_pos
        bias = tbl[ki[..., None], t5_bucket(rel)]
        s = jnp.einsum('bqkgd,bqknsd->bqkgns', q_i, kg).astype(F32) * HEAD_DIM ** -0.5 + jnp.moveaxis(bias, -1, 3)
        mask = (rel >= 0)[:, :, :, None]
        s = jnp.where(mask, s, NEG_INF).reshape(B, qb, NSA_KV_HEADS, NSA_GROUP, n_top * SEL_BLOCK)
        p = jax.nn.softmax(s, axis=-1)
        p = jnp.where(mask.reshape(B, qb, NSA_KV_HEADS, 1, n_top * SEL_BLOCK), p, 0.0)
        return jnp.einsum('bqkgl,bqkld->bqkgd', p.astype(vg.dtype),
                          vg.reshape(B, qb, NSA_KV_HEADS, n_top * SEL_BLOCK, HEAD_DIM))

    q_m = q.reshape(B, nqb, qb, NSA_KV_HEADS, NSA_GROUP, HEAD_DIM).swapaxes(0, 1)
    pos_m = q_pos.reshape(nqb, qb)
    idx_m = sel_idx.reshape(B, nqb, qb, NSA_KV_HEADS, n_top).swapaxes(0, 1)
    o_sel = lax.map(sel_block, (q_m, pos_m, idx_m))
    o_sel = o_sel.swapaxes(0, 1).reshape(B, Tq, NSA_KV_HEADS, NSA_GROUP, HEAD_DIM)
    return o_cmp, o_sel


def window_sample(q, q_pos, k, v, k_pos, rel_bias):
    rel = q_pos[:, None] - k_pos[None]
    mask = (rel >= 0) & (rel < WINDOW)
    o, _ = attend(q[:, None], k[:, None], v[:, None], rel_bias_heads(rel[None], rel_bias), mask[None])
    return o[:, 0]


def combine_branches(gate_logit, o_cmp, o_sel, o_win):
    B, T = gate_logit.shape[:2]
    g = jax.nn.sigmoid(gate_logit.astype(F32)).reshape(B, T, NSA_KV_HEADS, NSA_GROUP, 3).astype(o_cmp.dtype)
    o = g[..., 0:1] * o_cmp + g[..., 1:2] * o_sel + g[..., 2:3] * o_win
    return o.reshape(B, T, NSA_Q_DIM)


def causal_conv(x_hist, x_new, w):
    xc = jnp.concatenate([x_hist, x_new], axis=1)
    y = lax.conv_general_dilated(xc, w[:, None, :], window_strides=(1,), padding='VALID',
                                 dimension_numbers=('NWC', 'WIO', 'NWC'), feature_group_count=xc.shape[-1])
    return jax.nn.silu(y), xc[:, -(CONV_W - 1):]


def gated_delta_rule(q, k, v, g, beta, s0):
    B, T, H, DK = q.shape
    C = DN_CHUNK if T >= DN_CHUNK else T
    n = -(-T // C)
    pad = n * C - T

    def prep(a):
        a = jnp.pad(a.astype(F32), ((0, 0), (0, pad)) + ((0, 0),) * (a.ndim - 2))
        a = a.reshape((B, n, C) + a.shape[2:])
        return jnp.moveaxis(a, (1, 3), (0, 2))

    qc, kc, vc, gc, bc = prep(q), prep(k), prep(v), prep(g), prep(beta)
    gcum = jnp.cumsum(gc, axis=-1)
    ii = jnp.arange(C)
    incl = ii[:, None] >= ii[None, :]
    strict = ii[:, None] > ii[None, :]
    diff = gcum[..., :, None] - gcum[..., None, :]
    decay = jnp.where(incl, jnp.exp(jnp.where(incl, diff, 0.0)), 0.0)
    kb = kc * bc[..., None]
    a_mat = jnp.where(strict, jnp.einsum('...id,...jd->...ij', kb, kc) * decay, 0.0)
    eye_plus = a_mat + jnp.eye(C, dtype=F32)
    u = lax.linalg.triangular_solve(eye_plus, vc * bc[..., None], left_side=True, lower=True, unit_diagonal=True)
    w = lax.linalg.triangular_solve(eye_plus, kb * jnp.exp(gcum)[..., None], left_side=True, lower=True, unit_diagonal=True)
    qk = jnp.where(incl, jnp.einsum('...id,...jd->...ij', qc, kc) * decay, 0.0)
    qg = qc * jnp.exp(gcum)[..., None]
    kg = kc * jnp.exp(gcum[..., -1:] - gcum)[..., None]
    glast = jnp.exp(gcum[..., -1])

    def step(S, xs):
        u_i, w_i, qk_i, qg_i, kg_i, gl_i = xs
        v_new = u_i - jnp.einsum('bhck,bhkv->bhcv', w_i, S)
        o = jnp.einsum('bhck,bhkv->bhcv', qg_i, S) + jnp.einsum('bhij,bhjv->bhiv', qk_i, v_new)
        S = S * gl_i[..., None, None] + jnp.einsum('bhck,bhcv->bhkv', kg_i, v_new)
        return S, o

    S, o = lax.scan(step, s0.astype(F32), (u, w, qk, qg, kg, glast))
    o = o.transpose(1, 0, 3, 2, 4).reshape(B, n * C, H, v.shape[-1])[:, :T]
    return o, S


def deltanet_mixer(qkv_raw, b_logit, a_logit, gate, conv_hist, s0, lp):
    B, T = qkv_raw.shape[:2]
    qkv, new_hist = causal_conv(conv_hist, qkv_raw, lp['conv_w'])
    q, k, v = jnp.split(qkv, 3, axis=-1)
    q = l2norm(q.reshape(B, T, DN_HEADS, DN_HEAD_DIM)) * DN_HEAD_DIM ** -0.5
    k = l2norm(k.reshape(B, T, DN_HEADS, DN_HEAD_DIM))
    v = v.reshape(B, T, DN_HEADS, DN_HEAD_DIM)
    beta = jax.nn.sigmoid(b_logit.astype(F32))
    g = -jnp.exp(lp['dn_a_log'].astype(F32)) * jax.nn.softplus(a_logit.astype(F32) + lp['dn_dt_bias'].astype(F32))
    o, s_new = gated_delta_rule(q, k, v, g, beta, s0)
    o = rmsnorm(o, lp['dn_norm_g']) * jax.nn.silu(gate.astype(F32).reshape(B, T, DN_HEADS, DN_HEAD_DIM))
    return o.astype(qkv_raw.dtype).reshape(B, T, DN_DIM), s_new.astype(s0.dtype), new_hist


def hier_moe(h, lp):
    B, T, D = h.shape
    x = h.reshape(B * T, D)
    lg = (x @ lp['router_group_w'] + lp['router_group_b']).astype(F32)
    pg = jax.nn.softmax(lg, axis=-1)
    grp = jnp.argmax(lg, axis=-1)
    wg = jnp.take_along_axis(pg, grp[:, None], axis=-1)
    le = (x @ lp['router_expert_w'] + lp['router_expert_b']).astype(F32).reshape(-1, N_GROUPS, EXPERTS_PER_GROUP)
    le = jnp.take_along_axis(le, grp[:, None, None], axis=1)[:, 0]
    pe, ie = lax.top_k(jax.nn.softmax(le, axis=-1), TOP_K_IN_GROUP)
    pe = pe / jnp.sum(pe, axis=-1, keepdims=True)
    eid = grp[:, None] * EXPERTS_PER_GROUP + ie
    gates = jnp.sum(jax.nn.one_hot(eid, N_EXPERTS, dtype=F32) * (wg * pe)[..., None], axis=1).astype(x.dtype)
    y = jnp.zeros_like(x)
    for e in range(N_EXPERTS):
        he = jax.nn.silu(x @ lp['w_gate'][e]) * (x @ lp['w_up'][e])
        y = y + gates[:, e:e + 1] * (he @ lp['w_down'][e])
    return y.reshape(B, T, D)


def modulation(c, lp):
    mod = jax.nn.silu(c) @ lp['w_ada'] + lp['b_ada']
    return jnp.split(mod[:, None, :], 6, axis=-1)


def modulate(x, g, shift, scale):
    return rmsnorm(x, g) * (1 + scale) + shift


def split_projection(h, lp):
    z = h @ lp['w_in']
    return jnp.split(z, np.cumsum(IN_SPLITS)[:-1].tolist(), axis=-1)


def finish_layer(x, o_nsa, o_dn, g1, sh2, sc2, g2, lp):
    x = x + g1 * (jnp.concatenate([o_nsa, o_dn], axis=-1) @ lp['w_out'])
    return x + g2 * hier_moe(modulate(x, lp['norm_ffn_g'], sh2, sc2), lp)


_PK_Q = (0, NSA_Q_DIM)
_PK_KV = (_PK_Q[1], _PK_Q[1] + 6 * NSA_KV_DIM)
_PK_QKV = (_PK_KV[1], _PK_KV[1] + DN_CONV_DIM)
_PK_DG = (_PK_QKV[1], _PK_QKV[1] + DN_DIM)
_PK_SM = (_PK_DG[1], _PK_DG[1] + LANES)
PK_DIM = _PK_SM[1]
SM_GATE = 0
SM_BETA = 3 * NSA_HEADS
SM_DECAY = SM_BETA + DN_HEADS


def pack_w_in(w_in):
    offs = np.cumsum((0,) + IN_SPLITS)
    part = lambda i: w_in[:, offs[i]:offs[i + 1]]
    d = w_in.shape[0]
    q = part(0).reshape(d, NSA_KV_HEADS, NSA_GROUP, HEAD_DIM).transpose(0, 2, 1, 3).reshape(d, NSA_Q_DIM)
    q = q * HEAD_DIM ** -0.5
    small = jnp.concatenate([part(7), part(9), part(10)], axis=1)
    small = jnp.pad(small, ((0, 0), (0, LANES - small.shape[1])))
    cols = [q] + [part(i) for i in range(1, 7)] + [part(8), part(11), small]
    return jnp.concatenate(cols, axis=1).astype(BF16)


def _in_proj_body(x_ref, g_ref, sc_ref, sh_ref, w_ref, q_ref, ck_ref, cv_ref, sk_ref, sv_ref, wk_ref, wv_ref,
                  skb_ref, svb_ref, wkb_ref, wvb_ref, qkv_ref, dg_ref, sm_ref):
    x = x_ref[...]
    tm, d = x.shape
    nb = sc_ref.shape[0]
    xn = x * lax.rsqrt(jnp.mean(x * x, axis=-1, keepdims=True) + EPS) * g_ref[...]
    h = xn.reshape(nb, tm // nb, d) * (1 + sc_ref[...]) + sh_ref[...]
    h = h.reshape(tm, d).astype(BF16)

    def mm(span):
        return jnp.dot(h, w_ref[:, span[0]:span[1]], preferred_element_type=F32)

    q_ref[...] = mm(_PK_Q).astype(BF16)
    kv = mm(_PK_KV)
    for i, r in enumerate((ck_ref, cv_ref, sk_ref, sv_ref, wk_ref, wv_ref)):
        r[...] = kv[:, i * NSA_KV_DIM:(i + 1) * NSA_KV_DIM]
    for i, r in enumerate((skb_ref, svb_ref, wkb_ref, wvb_ref)):
        r[...] = kv[:, (i + 2) * NSA_KV_DIM:(i + 3) * NSA_KV_DIM].astype(BF16)
    qkv_ref[...] = mm(_PK_QKV)
    dg_ref[...] = mm(_PK_DG)
    sm_ref[...] = mm(_PK_SM)


def in_projection(x, norm_g, scale, shift, w_packed, *, tm=256):
    B, T, D = x.shape
    n = B * T
    tm = min(tm, n)
    rows = max(tm // T, 1)
    per = max(T // tm, 1)
    row = lambda i: (i, 0)
    outs = [((n, NSA_Q_DIM), BF16)] + [((n, NSA_KV_DIM), F32)] * 6 + [((n, NSA_KV_DIM), BF16)] * 4 + \
           [((n, DN_CONV_DIM), F32), ((n, DN_DIM), F32), ((n, LANES), F32)]
    return pl.pallas_call(
        _in_proj_body,
        grid=(n // tm,),
        in_specs=[pl.BlockSpec((tm, D), row),
                  pl.BlockSpec((1, D), lambda i: (0, 0)),
                  pl.BlockSpec((rows, 1, D), lambda i: (i // per, 0, 0)),
                  pl.BlockSpec((rows, 1, D), lambda i: (i // per, 0, 0)),
                  pl.BlockSpec((D, PK_DIM), lambda i: (0, 0))],
        out_specs=[pl.BlockSpec((tm, s[1]), row) for s, _ in outs],
        out_shape=[jax.ShapeDtypeStruct(s, dt) for s, dt in outs],
        compiler_params=pltpu.CompilerParams(dimension_semantics=("parallel",), vmem_limit_bytes=VMEM_LIMIT),
        name="in_projection",
    )(x.reshape(n, D), norm_g.reshape(1, D), scale, shift, w_packed)


CHUNK_W = CMP_STRIDE * NSA_KV_DIM
CMP_HALVES = CMP_LEN // CMP_STRIDE


def pack_compress(pos, w1, w2):
    eye = jnp.eye(NSA_KV_HEADS, dtype=F32)
    w1r = w1.reshape(CMP_HALVES, CMP_STRIDE, HEAD_DIM, CMP_HIDDEN)
    w1p = jnp.einsum('ijdc,kl->ijkdlc', w1r, eye).reshape(CMP_HALVES, CHUNK_W, NSA_KV_HEADS * CMP_HIDDEN)
    w2p = jnp.einsum('cd,kl->kcld', w2, eye).reshape(NSA_KV_HEADS * CMP_HIDDEN, NSA_KV_DIM)
    posp = jnp.broadcast_to(pos.reshape(CMP_HALVES, CMP_STRIDE, 1, HEAD_DIM),
                            (CMP_HALVES, CMP_STRIDE, NSA_KV_HEADS, HEAD_DIM)).reshape(CMP_HALVES, CHUNK_W)
    return posp, w1p.astype(BF16), w2p.astype(BF16)


def _compress_one(x, pos_ref, w1_ref, w2_ref):
    n_chunk = x.shape[0]
    hid = None
    for i in range(CMP_HALVES):
        part = jnp.dot((x + pos_ref[i:i + 1]).astype(BF16), w1_ref[i], preferred_element_type=F32)
        if i:
            part = pltpu.roll(part, shift=n_chunk - i, axis=0)
        hid = part if hid is None else hid + part
    hid = hid * jax.nn.sigmoid(hid)
    return jnp.dot(hid.astype(BF16), w2_ref[...], preferred_element_type=F32)


def _compress_body(xk_ref, xv_ref, pk_ref, pv_ref, w1k_ref, w1v_ref, w2k_ref, w2v_ref, kc_ref, vc_ref):
    kc_ref[0] = _compress_one(xk_ref[0], pk_ref, w1k_ref, w2k_ref).astype(BF16)
    vc_ref[0] = _compress_one(xv_ref[0], pv_ref, w1v_ref, w2v_ref).astype(BF16)


def compress_tokens(ck, cv, pk, pv, B):
    n_chunk = ck.shape[0] // B // CMP_STRIDE
    xk = ck.reshape(B, n_chunk, CHUNK_W)
    xv = cv.reshape(B, n_chunk, CHUNK_W)
    full = lambda a: pl.BlockSpec(a.shape, lambda b: (0,) * a.ndim)
    blk = lambda w: pl.BlockSpec((1, n_chunk, w), lambda b: (b, 0, 0))
    return pl.pallas_call(
        _compress_body,
        grid=(B,),
        in_specs=[blk(CHUNK_W), blk(CHUNK_W), full(pk[0]), full(pv[0]), full(pk[1]), full(pv[1]),
                  full(pk[2]), full(pv[2])],
        out_specs=[blk(NSA_KV_DIM), blk(NSA_KV_DIM)],
        out_shape=[jax.ShapeDtypeStruct((B, n_chunk, NSA_KV_DIM), BF16)] * 2,
        compiler_params=pltpu.CompilerParams(dimension_semantics=("parallel",), vmem_limit_bytes=VMEM_LIMIT),
        name="compress_tokens",
    )(xk, xv, pk[0], pv[0], pk[1], pv[1], pk[2], pv[2])


def _t5_bucket_np(rel):
    n = np.maximum(rel, 0)
    max_exact = NUM_BUCKETS // 2
    nf = np.maximum(n, 1).astype(np.float32)
    large = max_exact + (np.log(nf / max_exact) / math.log(MAX_DISTANCE / max_exact)
                         * (NUM_BUCKETS - max_exact)).astype(np.int32)
    return np.where(n < max_exact, n, np.minimum(large, NUM_BUCKETS - 1))


assert (_t5_bucket_np(np.arange(ATT_TILE + 1, 1 << 16)) == NUM_BUCKETS - 1).all()


NB_SAME, NB_PREV, NB_FAR, NB_EDGE, NB_NONE = range(5)
FAR_TILE = 512


def bias_lookup(rel_bias, rel):
    bucket = t5_bucket(rel)
    tbl = rel_bias.astype(F32)
    shape = (tbl.shape[1],) + (1,) * rel.ndim
    out = jnp.zeros((tbl.shape[1],) + rel.shape, F32)
    for b in range(NUM_BUCKETS):
        out = jnp.where(bucket == b, tbl[b].reshape(shape), out)
    return out


def near_bias_tables(rel_bias):
    t = ATT_TILE
    i = jnp.arange(t)[:, None]
    j = jnp.arange(t)[None, :]
    diag = jnp.where(i >= j, bias_lookup(rel_bias, i - j), MASKED)
    prev = bias_lookup(rel_bias, t + i - j)
    far = jnp.broadcast_to(rel_bias.astype(F32)[NUM_BUCKETS - 1][:, None, None], prev.shape)
    edge = jnp.where(j > i, far, MASKED)
    return jnp.stack([diag, prev, far, edge, jnp.full_like(far, MASKED)])


def cmp_bias_table(rel_bias, q0, tq, n_chunk):
    first_end = CMP_LEN - 1
    span = CMP_STRIDE * (n_chunk - 1)
    n_rel = tq + span
    rel = q0 - first_end - span + jnp.arange(n_rel)
    vals = jnp.where(rel >= 0, bias_lookup(rel_bias, rel), MASKED)
    vals = jnp.roll(vals, -span, axis=1)
    flat = jnp.tile(vals, (1, n_chunk))[:, :n_chunk * (n_rel - CMP_STRIDE)]
    by_token = flat.reshape(-1, n_chunk, n_rel - CMP_STRIDE)[:, :, :tq]
    return jnp.swapaxes(by_token, 1, 2)


def overlap_matrix(n_chunk, n_sel):
    c0 = np.arange(n_chunk)[:, None] * CMP_STRIDE
    s0 = np.arange(LANES)[None, :] * SEL_BLOCK
    ov = np.clip(np.minimum(c0 + CMP_LEN, s0 + SEL_BLOCK) - np.maximum(c0, s0), 0, None) / CMP_LEN
    ov = np.where(np.arange(LANES)[None, :] < n_sel, ov, 0.0)
    return jnp.asarray(ov, F32)


def expand_matrix(n_keys):
    e = (np.arange(n_keys)[None, :] // SEL_BLOCK) == np.arange(LANES)[:, None]
    return jnp.asarray(e, BF16)


def _head_queries(q_ref, tq):
    low = lax.broadcasted_iota(jnp.int32, (tq, LANES), 1) < HEAD_DIM
    zero = jnp.zeros((tq, LANES), BF16)
    parts = []
    for kv in range(NSA_KV_HEADS):
        for g in range(NSA_GROUP):
            blk = q_ref[:, g * LANES:(g + 1) * LANES]
            parts.append(jnp.where(low if kv == 0 else ~low, blk, zero))
    return parts


def _stack_queries(q_ref, tq):
    return jnp.concatenate(_head_queries(q_ref, tq), axis=0)


def _dot_nt(a, b):
    return lax.dot_general(a, b, (((1,), (1,)), ((), ())), preferred_element_type=F32)


def _select_blocks(imp, q_pos, n_sel, n_top):
    lane = lax.broadcasted_iota(jnp.int32, imp.shape, 1)
    q_blk = q_pos // SEL_BLOCK
    forced = jnp.where(lane == 0, 1.0, jnp.where(lane == q_blk, 1.0, jnp.where(lane == q_blk - 1, 1.0, 0.0)))
    score = jnp.where(lane * SEL_BLOCK <= q_pos, imp, NEG_INF)
    score = jnp.where(forced > 0.5, 1e9, score)
    score = jnp.where(lane < n_sel, score, -3e38)
    rank = jnp.zeros(imp.shape, F32)
    for s in range(n_sel):
        col = score[:, s:s + 1]
        tie = jnp.where(lane > s, 1.0, 0.0)
        rank = rank + jnp.where(col > score, 1.0, jnp.where(col == score, tie, 0.0))
    return jnp.where(rank < n_top, 1.0, 0.0)


def _gate_columns(gl_ref, j):
    g = jax.nn.sigmoid(gl_ref[...])
    cols = [g[:, SM_GATE + 3 * h + j:SM_GATE + 3 * h + j + 1] for h in range(NSA_HEADS)]
    return jnp.stack(cols, axis=0)


def _nsa_prompt_body(q_ref, kc_ref, vc_ref, sk_ref, sv_ref, wk_ref, wv_ref, bc_ref, nb_ref, gl_ref, ov_ref, ex_ref,
                     o_ref, mb_ref, m_ref, l_ref, acc_ref, out_ref, *, n_sel, n_top):
    i = pl.program_id(1)
    tq = q_ref.shape[0]
    h = NSA_HEADS
    q = _stack_queries(q_ref, tq)

    s = _dot_nt(q, kc_ref[0]).reshape(h, tq, -1) + bc_ref[...]
    m = jnp.max(s, axis=-1, keepdims=True)
    p = jnp.where(s > 0.5 * MASKED, jnp.exp(s - m), 0.0)
    l = jnp.sum(p, axis=-1, keepdims=True)
    p = p * (1.0 / jnp.where(l > 0.0, l, 1.0))
    o_cmp = jnp.dot(p.reshape(h * tq, -1).astype(BF16), vc_ref[0], preferred_element_type=F32)
    out_ref[...] = _gate_columns(gl_ref, 0) * o_cmp.reshape(h, tq, LANES)

    q_pos = i * tq + lax.broadcasted_iota(jnp.int32, (tq, LANES), 0)
    for kv in range(NSA_KV_HEADS):
        pk = p[kv * NSA_GROUP]
        for g in range(1, NSA_GROUP):
            pk = pk + p[kv * NSA_GROUP + g]
        imp = jnp.dot(pk, ov_ref[...], precision=lax.Precision.HIGHEST, preferred_element_type=F32)
        sel = _select_blocks(imp, q_pos, n_sel, n_top)
        keys = jnp.dot(sel.astype(BF16), ex_ref[...], preferred_element_type=F32)
        mb_ref[kv] = (keys - 1.0) * (-MASKED)

    tk = ATT_TILE
    kvg = lambda x: x.reshape(NSA_KV_HEADS, NSA_GROUP, tq, x.shape[-1])

    near0 = jnp.maximum(i - 1, 0)
    near_rows = pl.ds(pl.multiple_of(near0 * tk, tk), 2 * tk)
    first = i == 0
    near_bias = jnp.concatenate([nb_ref[jnp.where(first, NB_SAME, NB_PREV)],
                                 nb_ref[jnp.where(first, NB_NONE, NB_SAME)]], axis=-1)
    s = _dot_nt(q, sk_ref[near_rows, :]).reshape(h, tq, 2 * tk) + near_bias
    s = (kvg(s) + mb_ref[:, :, near_rows][:, None]).reshape(h, tq, 2 * tk)
    m = jnp.max(s, axis=-1, keepdims=True)
    p = jnp.exp(s - m)
    m_ref[...] = m
    l_ref[...] = jnp.sum(p, axis=-1, keepdims=True)
    acc_ref[...] = jnp.dot(p.reshape(h * tq, 2 * tk).astype(BF16), sv_ref[near_rows, :],
                           preferred_element_type=F32).reshape(h, tq, LANES)

    far_end = (i - 1) * tk
    far_bias = nb_ref[NB_FAR][:, 0:1, 0:1]
    col = lax.broadcasted_iota(jnp.int32, (1, 1, FAR_TILE), 2)

    def far_step(f, carry):
        rows = pl.ds(pl.multiple_of(f * FAR_TILE, FAR_TILE), FAR_TILE)
        edge = jnp.where(f * FAR_TILE + col < far_end, 0.0, MASKED)
        s = _dot_nt(q, sk_ref[rows, :]).reshape(h, tq, FAR_TILE) + (far_bias + edge)
        s = (kvg(s) + mb_ref[:, :, rows][:, None]).reshape(h, tq, FAR_TILE)
        m_old = m_ref[...]
        m_new = jnp.maximum(m_old, jnp.max(s, axis=-1, keepdims=True))
        alpha = jnp.exp(m_old - m_new)
        p = jnp.exp(s - m_new)
        l_ref[...] = alpha * l_ref[...] + jnp.sum(p, axis=-1, keepdims=True)
        pv = jnp.dot(p.reshape(h * tq, FAR_TILE).astype(BF16), sv_ref[rows, :], preferred_element_type=F32)
        acc_ref[...] = alpha * acc_ref[...] + pv.reshape(h, tq, LANES)
        m_ref[...] = m_new
        return carry
    lax.fori_loop(0, (jnp.maximum(far_end, 0) + FAR_TILE - 1) // FAR_TILE, far_step, 0)
    out_ref[...] += _gate_columns(gl_ref, 1) * (acc_ref[...] * (1.0 / l_ref[...]))

    n_back = WINDOW // tk
    win0 = jnp.maximum(i - n_back, 0)
    win_rows = pl.ds(pl.multiple_of(win0 * tk, tk), (n_back + 1) * tk)
    tiles = []
    for c in range(n_back + 1):
        r = i - (win0 + c)
        tiles.append(nb_ref[jnp.where(r < 0, NB_NONE, jnp.where(r == n_back, NB_EDGE, jnp.minimum(r, NB_FAR)))])
    s = _dot_nt(q, wk_ref[win_rows, :]).reshape(h, tq, (n_back + 1) * tk) + jnp.concatenate(tiles, axis=-1)
    p = jnp.exp(s - jnp.max(s, axis=-1, keepdims=True))
    l = jnp.sum(p, axis=-1, keepdims=True)
    o_win = jnp.dot(p.reshape(h * tq, (n_back + 1) * tk).astype(BF16), wv_ref[win_rows, :],
                    preferred_element_type=F32).reshape(h, tq, LANES)
    out_ref[...] += _gate_columns(gl_ref, 2) * (o_win * (1.0 / l))

    low = lax.broadcasted_iota(jnp.int32, (tq, LANES), 1) < HEAD_DIM
    for g in range(NSA_GROUP):
        o_ref[:, g * LANES:(g + 1) * LANES] = jnp.where(low, out_ref[g], out_ref[NSA_GROUP + g]).astype(BF16)


def nsa_prompt(q, kc, vc, skb, svb, wkb, wvb, small, rel_bias, B, T):
    tq = ATT_TILE
    nq = T // tq
    assert T % FAR_TILE == 0 and T >= WINDOW + ATT_TILE and WINDOW % ATT_TILE == 0
    n_chunk = kc.shape[1]
    n_sel = -(-T // SEL_BLOCK)
    n_top = min(SEL_TOP, n_sel)
    bias_c = cmp_bias_table(rel_bias, 0, T, n_chunk)
    nb = near_bias_tables(rel_bias)
    ov = overlap_matrix(n_chunk, n_sel)
    ex = expand_matrix(T)
    h = NSA_HEADS
    full = lambda a: pl.BlockSpec(a.shape, lambda b, i: (0,) * a.ndim)
    seq = pl.BlockSpec((T, NSA_KV_DIM), lambda b, i: (b, 0))
    tok = pl.BlockSpec((1, n_chunk, NSA_KV_DIM), lambda b, i: (b, 0, 0))
    return pl.pallas_call(
        functools.partial(_nsa_prompt_body, n_sel=n_sel, n_top=n_top),
        grid=(B, nq),
        in_specs=[pl.BlockSpec((tq, NSA_Q_DIM), lambda b, i: (b * nq + i, 0)),
                  tok, tok, seq, seq, seq, seq,
                  pl.BlockSpec((h, tq, n_chunk), lambda b, i: (0, i, 0)),
                  full(nb),
                  pl.BlockSpec((tq, LANES), lambda b, i: (b * nq + i, 0)),
                  full(ov), full(ex)],
        out_specs=pl.BlockSpec((tq, NSA_Q_DIM), lambda b, i: (b * nq + i, 0)),
        out_shape=jax.ShapeDtypeStruct((B * T, NSA_Q_DIM), BF16),
        scratch_shapes=[pltpu.VMEM((NSA_KV_HEADS, tq, T), F32),
                        pltpu.VMEM((h, tq, 1), F32), pltpu.VMEM((h, tq, 1), F32),
                        pltpu.VMEM((h, tq, LANES), F32), pltpu.VMEM((h, tq, LANES), F32)],
        compiler_params=pltpu.CompilerParams(dimension_semantics=("parallel", "arbitrary"),
                                             vmem_limit_bytes=VMEM_LIMIT),
        name="nsa_prompt",
    )(q, kc, vc, skb, svb, wkb, wvb, bias_c, nb, small, ov, ex)


def sample_bias_tables(rel_bias, past, T, W):
    t = jnp.arange(T)[:, None]
    pad = jnp.arange(ATT_TILE)[None, :]
    new = jnp.where((pad < T) & (t >= pad), bias_lookup(rel_bias, t - pad), MASKED)
    sel = bias_lookup(rel_bias, past + t - jnp.arange(past)[None, :])
    rel_w = W + t - jnp.arange(W)[None, :]
    win = jnp.where(rel_w < WINDOW, bias_lookup(rel_bias, rel_w), MASKED)
    cmp = cmp_bias_table(rel_bias, past, T, past // CMP_STRIDE)
    return cmp, jnp.concatenate([sel, new], axis=-1), jnp.concatenate([win, new], axis=-1)


def _softmax_rows(s):
    m = jnp.max(s, axis=-1, keepdims=True)
    p = jnp.exp(s - m)
    return p * (1.0 / jnp.sum(p, axis=-1, keepdims=True))


def _nsa_sample_body(pt_ref, q_ref, skn_ref, svn_ref, wkn_ref, wvn_ref, gl_ref, wink_ref, winv_ref,
                     bc_ref, bs_ref, bw_ref, ov_ref, ex_ref, pk_ref, pv_ref, w1k_ref, w1v_ref, w2k_ref, w2v_ref,
                     *rest, n_pages, n_sel, n_top, past):
    pages = [rest[i * n_pages:(i + 1) * n_pages] for i in range(4)]
    o_ref, wko_ref, wvo_ref, s_ref, new_ref = rest[4 * n_pages:]
    h = NSA_HEADS
    tq = q_ref.shape[0]
    tk = ATT_TILE
    q = _stack_queries(q_ref, tq)

    new_ref[...] = jnp.zeros(new_ref.shape, F32)
    for n, r in enumerate((skn_ref, svn_ref, wkn_ref, wvn_ref)):
        new_ref[n, 0:tq, :] = r[...]

    xk = jnp.concatenate([r[0] for r in pages[0]], axis=0)
    xv = jnp.concatenate([r[0] for r in pages[1]], axis=0)
    kc = _compress_one(xk, pk_ref, w1k_ref, w2k_ref).astype(BF16)
    vc = _compress_one(xv, pv_ref, w1v_ref, w2v_ref).astype(BF16)
    s = _dot_nt(q, kc).reshape(h, tq, -1) + bc_ref[...]
    p = _softmax_rows(s)
    o_cmp = jnp.dot(p.reshape(h * tq, -1).astype(BF16), vc, preferred_element_type=F32)
    out = _gate_columns(gl_ref, 0) * o_cmp.reshape(h, tq, LANES)

    q_pos = past + lax.broadcasted_iota(jnp.int32, (tq, LANES), 0)
    masks = []
    for kv in range(NSA_KV_HEADS):
        pkv = p[kv * NSA_GROUP]
        for g in range(1, NSA_GROUP):
            pkv = pkv + p[kv * NSA_GROUP + g]
        imp = jnp.dot(pkv, ov_ref[...], precision=lax.Precision.HIGHEST, preferred_element_type=F32)
        sel = _select_blocks(imp, q_pos, n_sel, n_top)
        keys = jnp.dot(sel.astype(BF16), ex_ref[...], preferred_element_type=F32)
        masks.append((keys - 1.0) * (-MASKED))
    mask = jnp.stack(masks, axis=0)
    for pg in range(n_pages):
        s_ref[:, pg * tk:(pg + 1) * tk] = jnp.dot(q, pages[2][pg][0].astype(BF16), preferred_element_type=F32)
    s_ref[:, n_pages * tk:(n_pages + 1) * tk] = _dot_nt(q, new_ref[0].astype(BF16))
    width = (n_pages + 1) * tk
    s = s_ref[...].reshape(h, tq, width) + bs_ref[...]
    s = (s.reshape(NSA_KV_HEADS, NSA_GROUP, tq, width) + mask[:, None]).reshape(h, tq, width)
    p = _softmax_rows(s).reshape(h * tq, width).astype(BF16)
    o_sel = jnp.dot(p[:, n_pages * tk:], new_ref[1].astype(BF16), preferred_element_type=F32)
    for pg in range(n_pages):
        o_sel = o_sel + _dot_nt(p[:, pg * tk:(pg + 1) * tk], pages[3][pg][0].astype(BF16))
    out = out + _gate_columns(gl_ref, 1) * o_sel.reshape(h, tq, LANES)

    w = wink_ref.shape[2]
    s_ref[:, 0:w] = jnp.dot(q, wink_ref[0].astype(BF16), preferred_element_type=F32)
    s_ref[:, w:w + tk] = _dot_nt(q, new_ref[2].astype(BF16))
    s = s_ref[:, 0:w + tk].reshape(h, tq, w + tk) + bw_ref[...]
    p = _softmax_rows(s).reshape(h * tq, w + tk).astype(BF16)
    o_win = jnp.dot(p[:, w:], new_ref[3].astype(BF16), preferred_element_type=F32) \
        + _dot_nt(p[:, 0:w], winv_ref[0].astype(BF16))
    out = out + _gate_columns(gl_ref, 2) * o_win.reshape(h, tq, LANES)

    low = lax.broadcasted_iota(jnp.int32, (tq, LANES), 1) < HEAD_DIM
    for g in range(NSA_GROUP):
        o_ref[:, g * LANES:(g + 1) * LANES] = jnp.where(low, out[g], out[NSA_GROUP + g])

    keep = w - tq
    lane_w = lax.broadcasted_iota(jnp.int32, (NSA_KV_DIM, w), 1)
    for old_ref, new_rows_ref, dst_ref in ((wink_ref, wkn_ref, wko_ref), (winv_ref, wvn_ref, wvo_ref)):
        shifted = pltpu.roll(old_ref[0], shift=keep, axis=1)
        tail = jnp.concatenate([jnp.zeros((tk - tq, NSA_KV_DIM), F32), new_rows_ref[...]], axis=0).T
        tail = jnp.concatenate([jnp.zeros((NSA_KV_DIM, w - tk), F32), tail], axis=1)
        dst_ref[0] = jnp.where(lane_w >= keep, tail, shifted)


def nsa_sample(q, skn, svn, wkn, wvn, small, pools, win_k, win_v, page_table, rel_bias, pk, pv, B, T):
    n_pages = page_table.shape[1]
    past = n_pages * PAGE_SIZE
    W = win_k.shape[1]
    L = past + T
    n_cmp = (L - CMP_LEN) // CMP_STRIDE + 1
    assert PAGE_SIZE == ATT_TILE and T % 8 == 0 and T <= SEL_BLOCK and past % SEL_BLOCK == 0
    assert (n_cmp - 1) * CMP_STRIDE + CMP_LEN <= past and n_cmp >= past // CMP_STRIDE - CMP_HALVES + 1
    assert W % ATT_TILE == 0 and W + T >= WINDOW
    n_sel = -(-L // SEL_BLOCK)
    n_top = min(SEL_TOP, n_sel)
    w_keep = min(WINDOW, W + T)
    bc, bs, bw = sample_bias_tables(rel_bias, past, T, W)
    ov = overlap_matrix(past // CMP_STRIDE, n_sel)
    ex = expand_matrix(past + ATT_TILE)
    n_pool = pools[0].shape[0]
    chunks = PAGE_SIZE // CMP_STRIDE
    assert W == w_keep
    feat_major = lambda a: a.transpose(0, 2, 3, 1).reshape(a.shape[0], NSA_KV_DIM, a.shape[1])
    cmp_pools = [p.reshape(n_pool, chunks, CHUNK_W) for p in pools[:2]]
    sel_pools = [feat_major(p) for p in pools[2:]]
    wk3 = feat_major(win_k)
    wv3 = feat_major(win_v)

    full = lambda a: pl.BlockSpec(a.shape, lambda b, pt: (0,) * a.ndim)
    row = lambda w: pl.BlockSpec((T, w), lambda b, pt: (b, 0))
    seq = lambda n: pl.BlockSpec((1, NSA_KV_DIM, n), lambda b, pt: (b, 0, 0))
    page = lambda shape, pg: pl.BlockSpec((1,) + shape, lambda b, pt: (pt[b, pg], 0, 0))
    page_specs, page_args = [], []
    for pool, shape in ((cmp_pools[0], (chunks, CHUNK_W)), (cmp_pools[1], (chunks, CHUNK_W)),
                        (sel_pools[0], (NSA_KV_DIM, PAGE_SIZE)), (sel_pools[1], (NSA_KV_DIM, PAGE_SIZE))):
        for pg in range(n_pages):
            page_specs.append(page(shape, pg))
            page_args.append(pool)
    tables = (bc, bs, bw, ov, ex, pk[0], pv[0], pk[1], pv[1], pk[2], pv[2])
    grid_spec = pltpu.PrefetchScalarGridSpec(
        num_scalar_prefetch=1,
        grid=(B,),
        in_specs=[row(NSA_Q_DIM), row(NSA_KV_DIM), row(NSA_KV_DIM), row(NSA_KV_DIM), row(NSA_KV_DIM), row(LANES),
                  seq(W), seq(W)] + [full(a) for a in tables] + page_specs,
        out_specs=[row(NSA_Q_DIM), seq(w_keep), seq(w_keep)],
        scratch_shapes=[pltpu.VMEM((NSA_HEADS * T, past + ATT_TILE), F32),
                        pltpu.VMEM((4, ATT_TILE, NSA_KV_DIM), F32)],
    )
    o, wk_next, wv_next = pl.pallas_call(
        functools.partial(_nsa_sample_body, n_pages=n_pages, n_sel=n_sel, n_top=n_top, past=past),
        grid_spec=grid_spec,
        out_shape=[jax.ShapeDtypeStruct((B * T, NSA_Q_DIM), F32),
                   jax.ShapeDtypeStruct((B, NSA_KV_DIM, w_keep), F32),
                   jax.ShapeDtypeStruct((B, NSA_KV_DIM, w_keep), F32)],
        compiler_params=pltpu.CompilerParams(dimension_semantics=("arbitrary",), vmem_limit_bytes=VMEM_LIMIT),
        name="nsa_sample",
    )(page_table, q, skn, svn, wkn, wvn, small, wk3, wv3, *tables, *page_args)
    row_major = lambda a: a.reshape(B, NSA_KV_HEADS, HEAD_DIM, w_keep).transpose(0, 3, 1, 2)
    return o, row_major(wk_next), row_major(wv_next)


HIST_ROWS = 8


def _split_bf16(x):
    hi = x.astype(BF16)
    lo = (x - hi.astype(F32)).astype(BF16)
    return hi, lo


def _mm_hi(a, b):
    ah, al = _split_bf16(a)
    bh, bl = _split_bf16(b)
    d = lambda x, y: jnp.dot(x, y, preferred_element_type=F32)
    return d(ah, bh) + (d(ah, bl) + d(al, bh))


def _bmm(a, b):
    return jnp.einsum('hij,hjk->hik', a, b, preferred_element_type=F32)


def _bmm_nt(a, b):
    return jnp.einsum('hid,hjd->hij', a, b, preferred_element_type=F32)


def _bmm_hi(a, b):
    ah, al = _split_bf16(a)
    bh, bl = _split_bf16(b)
    return _bmm(ah, bh) + (_bmm(ah, bl) + _bmm(al, bh))


def _cumsum_rows(tri, x):
    hi, lo = _split_bf16(x)
    lo2 = (x - hi.astype(F32) - lo.astype(F32)).astype(BF16)
    d = lambda y: jnp.dot(tri, y, preferred_element_type=F32)
    return d(hi) + (d(lo) + d(lo2))


def _softplus(x):
    return jnp.maximum(x, 0.0) + jnp.log1p(jnp.exp(-jnp.abs(x)))


def _silu(x):
    return x * jax.nn.sigmoid(x)


def _deltanet_body(x_ref, sm_ref, dg_ref, hist_ref, s0_ref, cw_ref, apar_ref, dt_ref, ng_ref,
                   o_ref, s_out_ref, hist_out_ref, xbuf_ref, s_ref):
    j = pl.program_id(1)
    c = x_ref.shape[0]
    hd = DN_HEAD_DIM
    n_hist = CONV_W - 1

    @pl.when(j == 0)
    def _():
        xbuf_ref[HIST_ROWS - n_hist:HIST_ROWS, :] = hist_ref[0]
        s_ref[...] = s0_ref[0]

    xbuf_ref[HIST_ROWS:HIST_ROWS + c, :] = x_ref[...]
    y = None
    for t in range(CONV_W):
        lo = HIST_ROWS - n_hist + t
        term = xbuf_ref[lo:lo + c, :] * cw_ref[t:t + 1, :]
        y = term if y is None else y + term
    y = _silu(y)
    tail = xbuf_ref[HIST_ROWS + c - n_hist:HIST_ROWS + c, :]
    xbuf_ref[HIST_ROWS - n_hist:HIST_ROWS, :] = tail

    sm = sm_ref[...]
    beta_all = jax.nn.sigmoid(sm)
    g_all = apar_ref[...] * _softplus(sm + dt_ref[...])
    row = lax.broadcasted_iota(jnp.int32, (c, c), 0)
    col = lax.broadcasted_iota(jnp.int32, (c, c), 1)
    incl = row >= col
    strict = row > col
    tri = jnp.where(incl, 1.0, 0.0).astype(BF16)
    gcum_all = _cumsum_rows(tri, g_all)
    gcum_t = gcum_all.T
    eye = jnp.where(row == col, 1.0, 0.0)

    heads = lambda x: jnp.stack([x[:, h * hd:(h + 1) * hd] for h in range(DN_HEADS)], axis=0)
    q = heads(y[:, 0:DN_DIM])
    k = heads(y[:, DN_DIM:2 * DN_DIM])
    v = heads(y[:, 2 * DN_DIM:3 * DN_DIM])
    q = q * lax.rsqrt(jnp.sum(q * q, axis=-1, keepdims=True) + EPS) * hd ** -0.5
    k = k * lax.rsqrt(jnp.sum(k * k, axis=-1, keepdims=True) + EPS)
    beta = jnp.stack([beta_all[:, SM_BETA + h:SM_BETA + h + 1] for h in range(DN_HEADS)], axis=0)
    gc = jnp.stack([gcum_all[:, SM_DECAY + h:SM_DECAY + h + 1] for h in range(DN_HEADS)], axis=0)
    gr = jnp.stack([gcum_t[SM_DECAY + h:SM_DECAY + h + 1, :] for h in range(DN_HEADS)], axis=0)
    g_last = gr[:, :, c - 1:c]
    decay = jnp.where(incl, jnp.exp(jnp.where(incl, gc - gr, 0.0)), 0.0)
    eg = jnp.exp(gc)
    kb = k * beta
    kbf = k.astype(BF16)
    a = jnp.where(strict, _bmm_nt(kb.astype(BF16), kbf) * decay, 0.0)
    t_inv = eye - a
    a_pow = a
    for _ in range(int(math.log2(c)) - 1):
        a_pow = _bmm_hi(a_pow, a_pow)
        t_inv = t_inv + _bmm_hi(t_inv, a_pow)
    uw = _bmm_hi(t_inv, jnp.concatenate([v * beta, kb * eg], axis=2))
    u, w = uw[:, :, :hd], uw[:, :, hd:]
    qk = jnp.where(incl, _bmm_nt(q.astype(BF16), kbf) * decay, 0.0)
    qg = q * eg
    kg = k * jnp.exp(g_last - gc)
    s_old = s_ref[...]
    ws = _bmm(jnp.concatenate([w, qg], axis=1).astype(BF16), s_old.astype(BF16))
    v_new = u - ws[:, :c]
    vb = v_new.astype(BF16)
    o = ws[:, c:] + _bmm(qk.astype(BF16), vb)
    s_ref[...] = s_old * jnp.exp(g_last) + jnp.einsum('hck,hcv->hkv', kg.astype(BF16), vb,
                                                      preferred_element_type=F32)
    o = o * lax.rsqrt(jnp.mean(o * o, axis=-1, keepdims=True) + EPS) * ng_ref[...]
    o = o * _silu(heads(dg_ref[...]))
    o_ref[...] = jnp.concatenate([o[h] for h in range(DN_HEADS)], axis=1).astype(o_ref.dtype)

    @pl.when(j == pl.num_programs(1) - 1)
    def _():
        s_out_ref[0] = s_ref[...]
        hist_out_ref[0] = tail


def deltanet(qkv_raw, small, dg, conv_hist, s0, lp, B, T):
    c = DN_CHUNK if T >= DN_CHUNK else T
    assert T % c == 0 and c % 8 == 0 and c >= CONV_W - 1 and (c & (c - 1)) == 0
    n = T // c
    lanes = jnp.arange(LANES)
    put = lambda vals, base: jnp.zeros((LANES,), F32).at[base + jnp.arange(DN_HEADS)].set(vals).reshape(1, LANES)
    apar = put(-jnp.exp(lp['dn_a_log'].astype(F32)), SM_DECAY)
    dtb = put(lp['dn_dt_bias'].astype(F32), SM_DECAY)
    del lanes
    rows = lambda w: pl.BlockSpec((c, w), lambda b, j: (b * n + j, 0))
    per_b = lambda a: pl.BlockSpec((1,) + a.shape[1:], lambda b, j: (b,) + (0,) * (a.ndim - 1))
    full = lambda a: pl.BlockSpec(a.shape, lambda b, j: (0,) * a.ndim)
    ng = lp['dn_norm_g'].reshape(1, DN_HEAD_DIM).astype(F32)
    return pl.pallas_call(
        _deltanet_body,
        grid=(B, n),
        in_specs=[rows(DN_CONV_DIM), rows(LANES), rows(DN_DIM), per_b(conv_hist), per_b(s0),
                  full(lp['conv_w']), full(apar), full(dtb), full(ng)],
        out_specs=[rows(DN_DIM), per_b(s0), per_b(conv_hist)],
        out_shape=[jax.ShapeDtypeStruct((B * T, DN_DIM), BF16 if c % 16 == 0 else F32),
                   jax.ShapeDtypeStruct(s0.shape, F32),
                   jax.ShapeDtypeStruct(conv_hist.shape, F32)],
        scratch_shapes=[pltpu.VMEM((HIST_ROWS + c, DN_CONV_DIM), F32),
                        pltpu.VMEM((DN_HEADS, DN_HEAD_DIM, DN_HEAD_DIM), F32)],
        compiler_params=pltpu.CompilerParams(dimension_semantics=("parallel", "arbitrary"),
                                             vmem_limit_bytes=VMEM_LIMIT),
        name="deltanet",
    )(qkv_raw, small, dg, conv_hist, s0, lp['conv_w'], apar, dtb, ng)


RT_GROUP = 0
RT_EXPERT = N_GROUPS


def pack_router(wg, bg, we, be):
    w = jnp.concatenate([wg, we], axis=1)
    b = jnp.concatenate([bg, be], axis=0)
    pad = LANES - w.shape[1]
    return jnp.pad(w, ((0, 0), (0, pad))).astype(BF16), jnp.pad(b, (0, pad)).reshape(1, LANES).astype(F32)


def _route(r):
    lane = lax.broadcasted_iota(jnp.int32, r.shape, 1).astype(F32)
    big = float(LANES)
    is_grp = lane < N_GROUPS
    lg = jnp.where(is_grp, r, -jnp.inf)
    mg = jnp.max(lg, axis=-1, keepdims=True)
    eg = jnp.where(is_grp, jnp.exp(lg - mg), 0.0)
    grp = jnp.min(jnp.where(lg == mg, lane, big), axis=-1, keepdims=True)
    wg = 1.0 / jnp.sum(eg, axis=-1, keepdims=True)
    first = RT_EXPERT + grp * EXPERTS_PER_GROUP
    in_grp = (lane >= first) & (lane < first + EXPERTS_PER_GROUP)
    le = jnp.where(in_grp, r, -jnp.inf)
    me = jnp.max(le, axis=-1, keepdims=True)
    ee = jnp.where(in_grp, jnp.exp(le - me), 0.0)
    pe = jnp.where(in_grp, ee / jnp.sum(ee, axis=-1, keepdims=True), -1.0)
    v1 = jnp.max(pe, axis=-1, keepdims=True)
    i1 = jnp.min(jnp.where(pe == v1, lane, big), axis=-1, keepdims=True)
    pe2 = jnp.where(lane == i1, -1.0, pe)
    v2 = jnp.max(pe2, axis=-1, keepdims=True)
    i2 = jnp.min(jnp.where(pe2 == v2, lane, big), axis=-1, keepdims=True)
    tot = v1 + v2
    return jnp.where(lane == i1, wg * (v1 / tot), jnp.where(lane == i2, wg * (v2 / tot), 0.0))


def _finish_body(x_ref, on_ref, od_ref, g1_ref, sc_ref, sh_ref, g2_ref, won_ref, wod_ref, ng_ref, wr_ref, br_ref,
                 wgu_ref, wd_ref, fg_ref, o_ref, x1_ref, h_ref, gate_ref, y_ref):
    e = pl.program_id(1)
    tm, d = x_ref.shape
    nb = g1_ref.shape[0]
    per_batch = lambda a: a.reshape(nb, tm // nb, d)

    @pl.when(e == 0)
    def _():
        mix = jnp.dot(on_ref[...].astype(BF16), won_ref[...], preferred_element_type=F32) \
            + jnp.dot(od_ref[...].astype(BF16), wod_ref[...], preferred_element_type=F32)
        x1 = per_batch(x_ref[...]) + g1_ref[...] * per_batch(mix)
        x1 = x1.reshape(tm, d)
        x1_ref[...] = x1
        xn = x1 * lax.rsqrt(jnp.mean(x1 * x1, axis=-1, keepdims=True) + EPS) * ng_ref[...]
        h = (per_batch(xn) * (1 + sc_ref[...]) + sh_ref[...]).reshape(tm, d).astype(BF16)
        h_ref[...] = h
        gate_ref[...] = _route(jnp.dot(h, wr_ref[...], preferred_element_type=F32) + br_ref[...])
        y_ref[...] = jnp.zeros(y_ref.shape, F32)

    h = h_ref[...]
    gu = jnp.dot(h, wgu_ref[0], preferred_element_type=F32)
    he = _silu(gu[:, :D_EXPERT]) * gu[:, D_EXPERT:]
    out = jnp.dot(he.astype(BF16), wd_ref[0], preferred_element_type=F32)
    lane = lax.broadcasted_iota(jnp.int32, gate_ref.shape, 1)
    g_col = jnp.sum(jnp.where(lane == RT_EXPERT + e, gate_ref[...], 0.0), axis=-1, keepdims=True)
    y_ref[...] += g_col * out

    @pl.when(e == pl.num_programs(1) - 1)
    def _():
        x2 = per_batch(x1_ref[...]) + g2_ref[...] * per_batch(y_ref[...])
        x2 = x2.reshape(tm, d)
        o_ref[...] = x2 * lax.rsqrt(jnp.mean(x2 * x2, axis=-1, keepdims=True) + EPS) * fg_ref[...]


def finish_layer(x, o_nsa, o_dn, g1, sc2, sh2, g2, packed, norm_ffn_g, final_g, *, tm=1024):
    B, T, D = x.shape
    n = B * T
    tm = min(tm, n)
    rows = max(tm // T, 1)
    per = max(T // tm, 1)
    row = lambda w: pl.BlockSpec((tm, w), lambda i, e: (i, 0))
    mod = pl.BlockSpec((rows, 1, D), lambda i, e: (i // per, 0, 0))
    full = lambda a: pl.BlockSpec(a.shape, lambda i, e: (0,) * a.ndim)
    expert = lambda a: pl.BlockSpec((1,) + a.shape[1:], lambda i, e: (e, 0, 0))
    vec = lambda v: v.reshape(1, D).astype(F32)
    args = (x.reshape(n, D), o_nsa, o_dn, g1, sc2, sh2, g2, packed['w_out_nsa'], packed['w_out_dn'],
            vec(norm_ffn_g), packed['router_w'], packed['router_b'], packed['w_gate_up'], packed['w_down'],
            vec(final_g))
    specs = [row(D), row(o_nsa.shape[1]), row(o_dn.shape[1]), mod, mod, mod, mod] + \
            [full(a) for a in args[7:12]] + [expert(args[12]), expert(args[13]), full(args[14])]
    return pl.pallas_call(
        _finish_body,
        grid=(n // tm, N_EXPERTS),
        in_specs=specs,
        out_specs=row(D),
        out_shape=jax.ShapeDtypeStruct((n, D), F32),
        scratch_shapes=[pltpu.VMEM((tm, D), F32), pltpu.VMEM((tm, D), BF16), pltpu.VMEM((tm, LANES), F32),
                        pltpu.VMEM((tm, D), F32)],
        compiler_params=pltpu.CompilerParams(dimension_semantics=("parallel", "arbitrary"),
                                             vmem_limit_bytes=VMEM_LIMIT),
        name="finish_layer",
    )(*args).reshape(B, T, D)


def _final_norm_body(x_ref, g_ref, o_ref):
    x = x_ref[...]
    o_ref[...] = x * lax.rsqrt(jnp.mean(x * x, axis=-1, keepdims=True) + EPS) * g_ref[...]


def final_rmsnorm(x, g, *, tm=512):
    shp = x.shape
    x2 = x.reshape(-1, shp[-1])
    n, d = x2.shape
    return pl.pallas_call(
        _final_norm_body,
        grid=(n // tm,),
        in_specs=[pl.BlockSpec((tm, d), lambda i: (i, 0)), pl.BlockSpec((1, d), lambda i: (0, 0))],
        out_specs=pl.BlockSpec((tm, d), lambda i: (i, 0)),
        out_shape=jax.ShapeDtypeStruct(x2.shape, x2.dtype),
        compiler_params=pltpu.CompilerParams(dimension_semantics=("parallel",)),
        name="final_rmsnorm",
    )(x2, g.reshape(1, d)).reshape(shp)


def permute_w_out_nsa(w_out):
    w = w_out[:NSA_Q_DIM].reshape(NSA_KV_HEADS, NSA_GROUP, HEAD_DIM, -1)
    return w.transpose(1, 0, 2, 3).reshape(NSA_Q_DIM, -1).astype(BF16)


def layer_prompt(x, c, lp, rel_bias, packed):
    B, T, _ = x.shape
    sh1, sc1, g1, sh2, sc2, g2 = modulation(c, lp)
    (q, ck, cv, sk, sv, wk, wv, skb, svb, wkb, wvb, qkv_raw, dg, small) = in_projection(
        x, lp['norm_mix_g'], sc1, sh1, packed['w_in'])
    kc, vc = compress_tokens(ck, cv, packed['cmp_k'], packed['cmp_v'], B)
    o_nsa = nsa_prompt(q, kc, vc, skb, svb, wkb, wvb, small, rel_bias, B, T)
    kv = lambda r: r.reshape(B, T, NSA_KV_HEADS, HEAD_DIM)
    ck, cv, sk, sv, wk, wv = kv(ck), kv(cv), kv(sk), kv(sv), kv(wk), kv(wv)
    conv_hist = jnp.zeros((B, CONV_W - 1, DN_CONV_DIM), x.dtype)
    s0 = jnp.zeros((B, DN_HEADS, DN_HEAD_DIM, DN_HEAD_DIM), x.dtype)
    o_dn, s_new, conv_new = deltanet(qkv_raw, small, dg, conv_hist, s0, lp, B, T)
    y = finish_layer(x, o_nsa, o_dn, g1, sc2, sh2, g2, packed, lp['norm_ffn_g'], packed['final_norm_g'])
    w_keep = min(WINDOW, T)
    return y, (ck, cv, sk, sv, wk[:, -w_keep:], wv[:, -w_keep:], s_new, conv_new)


def layer_sample(x, c, cmp_k_pool, cmp_v_pool, sel_k_pool, sel_v_pool, win_k, win_v, s0, conv_hist,
                 page_table, lp, rel_bias, packed):
    B, T, _ = x.shape
    sh1, sc1, g1, sh2, sc2, g2 = modulation(c, lp)
    (q, ck, cv, sk, sv, wk, wv, _, _, _, _, qkv_raw, dg, small) = in_projection(
        x, lp['norm_mix_g'], sc1, sh1, packed['w_in'])
    o_nsa, wk_next, wv_next = nsa_sample(q, sk, sv, wk, wv, small, (cmp_k_pool, cmp_v_pool, sel_k_pool, sel_v_pool),
                                         win_k, win_v, page_table, rel_bias, packed['cmp_k'], packed['cmp_v'], B, T)
    kv = lambda r: r.reshape(B, -1, NSA_KV_HEADS, HEAD_DIM)
    o_dn, s_new, conv_new = deltanet(qkv_raw, small, dg, conv_hist, s0, lp, B, T)
    y = finish_layer(x, o_nsa, o_dn, g1, sc2, sh2, g2, packed, lp['norm_ffn_g'], packed['final_norm_g'])
    return y, (kv(ck), kv(cv), kv(sk), kv(sv), kv(wk_next), kv(wv_next), s_new, conv_new)


def kernel(x_prompt, x_sample, cache_cmp_k, cache_cmp_v, cache_sel_k, cache_sel_v, cache_win_k, cache_win_v,
           state_delta, state_conv, page_table, c_prompt, c_sample, rel_bias, w_ada, b_ada, norm_mix_g,
           norm_ffn_g, w_in, cmp_pos_k, cmp_w1_k, cmp_w2_k, cmp_pos_v, cmp_w1_v, cmp_w2_v, conv_w, dn_a_log,
           dn_dt_bias, dn_norm_g, w_out, router_group_w, router_group_b, router_expert_w, router_expert_b,
           expert_w_gate, expert_w_up, expert_w_down, final_norm_g):
    xp, xs = x_prompt, x_sample
    new_p, new_s = [], []
    for l in range(DEPTH):
        lp = dict(w_ada=w_ada[l], b_ada=b_ada[l], norm_mix_g=norm_mix_g[l], norm_ffn_g=norm_ffn_g[l],
                  w_in=w_in[l], cmp_pos_k=cmp_pos_k[l], cmp_w1_k=cmp_w1_k[l], cmp_w2_k=cmp_w2_k[l],
                  cmp_pos_v=cmp_pos_v[l], cmp_w1_v=cmp_w1_v[l], cmp_w2_v=cmp_w2_v[l], conv_w=conv_w[l],
                  dn_a_log=dn_a_log[l], dn_dt_bias=dn_dt_bias[l], dn_norm_g=dn_norm_g[l], w_out=w_out[l],
                  router_group_w=router_group_w[l], router_group_b=router_group_b[l],
                  router_expert_w=router_expert_w[l], router_expert_b=router_expert_b[l],
                  w_gate=expert_w_gate[l], w_up=expert_w_up[l], w_down=expert_w_down[l])
        packed = dict(w_in=pack_w_in(w_in[l]),
                      cmp_k=pack_compress(cmp_pos_k[l], cmp_w1_k[l], cmp_w2_k[l]),
                      cmp_v=pack_compress(cmp_pos_v[l], cmp_w1_v[l], cmp_w2_v[l]),
                      w_out_nsa=permute_w_out_nsa(w_out[l]),
                      w_out_dn=w_out[l][NSA_Q_DIM:].astype(BF16),
                      w_gate_up=jnp.concatenate([expert_w_gate[l], expert_w_up[l]], axis=-1).astype(BF16),
                      w_down=expert_w_down[l].astype(BF16),
                      final_norm_g=final_norm_g)
        packed['router_w'], packed['router_b'] = pack_router(router_group_w[l], router_group_b[l],
                                                             router_expert_w[l], router_expert_b[l])
        xp, st_p = layer_prompt(xp, c_prompt, lp, rel_bias, packed)
        xs, st_s = layer_sample(xs, c_sample, cache_cmp_k[l], cache_cmp_v[l], cache_sel_k[l], cache_sel_v[l],
                                cache_win_k[l], cache_win_v[l], state_delta[l], state_conv[l], page_table,
                                lp, rel_bias, packed)
        new_p.append(st_p)
        new_s.append(st_s)
    assert DEPTH == 1
    y_prompt, y_sample = xp, xs

    def stk(states, i):
        return jnp.stack([s[i] for s in states])

    return (y_prompt, y_sample,
            stk(new_p, 0), stk(new_p, 1), stk(new_p, 2), stk(new_p, 3),
            stk(new_p, 4), stk(new_p, 5), stk(new_p, 6), stk(new_p, 7),
            stk(new_s, 0), stk(new_s, 1), stk(new_s, 2), stk(new_s, 3),
            stk(new_s, 4), stk(new_s, 5), stk(new_s, 6), stk(new_s, 7))
```

```python
import functools
import math

import jax
import jax.numpy as jnp
import numpy as np
from jax import lax
from jax.experimental import pallas as pl
from jax.experimental.pallas import tpu as pltpu

D_MODEL = 1024
DEPTH = 1
PAGE_SIZE = 128

NSA_HEADS = 8
NSA_KV_HEADS = 2
NSA_GROUP = NSA_HEADS // NSA_KV_HEADS
HEAD_DIM = 64
CMP_LEN = 32
CMP_STRIDE = 16
CMP_HIDDEN = 4 * HEAD_DIM
SEL_BLOCK = 64
SEL_TOP = 16
WINDOW = 512
WIN_Q_BLOCK = 128
SEL_Q_BLOCK = 64
DN_HEADS = 8
DN_HEAD_DIM = 64
CONV_W = 4
DN_CHUNK = 64
NUM_BUCKETS = 32
MAX_DISTANCE = 128
N_GROUPS = 4
EXPERTS_PER_GROUP = 8
N_EXPERTS = N_GROUPS * EXPERTS_PER_GROUP
TOP_K_IN_GROUP = 2
D_EXPERT = D_MODEL // 4

EPS = 1e-6
NEG_INF = -1e30
F32 = jnp.float32
BF16 = jnp.bfloat16

NSA_Q_DIM = NSA_HEADS * HEAD_DIM
NSA_KV_DIM = NSA_KV_HEADS * HEAD_DIM
DN_DIM = DN_HEADS * DN_HEAD_DIM
DN_CONV_DIM = 3 * DN_DIM
MIX_WIDTH = NSA_Q_DIM + DN_DIM
IN_SPLITS = (NSA_Q_DIM, NSA_KV_DIM, NSA_KV_DIM, NSA_KV_DIM, NSA_KV_DIM, NSA_KV_DIM, NSA_KV_DIM,
             3 * NSA_HEADS, DN_CONV_DIM, DN_HEADS, DN_HEADS, DN_DIM)
IN_DIM = sum(IN_SPLITS)

LANES = 128
VMEM_LIMIT = 56 * 1024 * 1024

ATT_TILE = 128
MASKED = -1e30


def rmsnorm(x, g):
    xf = x.astype(F32)
    y = xf * lax.rsqrt(jnp.mean(xf * xf, axis=-1, keepdims=True) + EPS)
    return (y * g.astype(F32)).astype(x.dtype)


def l2norm(x):
    xf = x.astype(F32)
    return (xf * lax.rsqrt(jnp.sum(xf * xf, axis=-1, keepdims=True) + EPS)).astype(x.dtype)


def t5_bucket(rel):
    n = jnp.maximum(rel, 0)
    max_exact = NUM_BUCKETS // 2
    nf = jnp.maximum(n, 1).astype(F32)
    large = max_exact + (jnp.log(nf / max_exact) / math.log(MAX_DISTANCE / max_exact)
                         * (NUM_BUCKETS - max_exact)).astype(jnp.int32)
    large = jnp.minimum(large, NUM_BUCKETS - 1)
    return jnp.where(n < max_exact, n, large)


def rel_bias_heads(rel, rel_bias):
    b = rel_bias.astype(F32)[t5_bucket(rel)]
    b = jnp.moveaxis(b, -1, -3)
    return b.reshape(b.shape[:-3] + (NSA_KV_HEADS, NSA_GROUP) + b.shape[-2:])


def attend(q, k, v, bias, mask):
    s = jnp.einsum('bntkgd,bnlkd->bnkgtl', q, k).astype(F32) * HEAD_DIM ** -0.5 + bias
    m = mask[:, None, None]
    p = jax.nn.softmax(jnp.where(m, s, NEG_INF), axis=-1)
    p = jnp.where(m, p, 0.0)
    o = jnp.einsum('bnkgtl,bnlkd->bntkgd', p.astype(v.dtype), v)
    return o, p


def compress_rows(rows, pos_emb, w1, w2):
    B, L = rows.shape[:2]
    n_cmp = (L - CMP_LEN) // CMP_STRIDE + 1
    idx = jnp.arange(n_cmp)[:, None] * CMP_STRIDE + jnp.arange(CMP_LEN)[None, :]
    blk = rows[:, idx] + pos_emb[:, None, :]
    blk = jnp.swapaxes(blk, 2, 3).reshape(B, n_cmp, NSA_KV_HEADS, CMP_LEN * HEAD_DIM)
    return jax.nn.silu(blk @ w1) @ w2


def nsa_cmp_sel(q, q_pos, kc_rows, vc_rows, ks_rows, vs_rows, lp, rel_bias):
    B, Tq = q.shape[:2]
    L = kc_rows.shape[1]
    kc = compress_rows(kc_rows, lp['cmp_pos_k'], lp['cmp_w1_k'], lp['cmp_w2_k'])
    vc = compress_rows(vc_rows, lp['cmp_pos_v'], lp['cmp_w1_v'], lp['cmp_w2_v'])
    n_cmp = kc.shape[1]
    cmp_end = jnp.arange(n_cmp) * CMP_STRIDE + CMP_LEN - 1
    rel_c = q_pos[:, None] - cmp_end[None, :]
    o_cmp, p_cmp = attend(q[:, None], kc[:, None], vc[:, None],
                          rel_bias_heads(rel_c[None], rel_bias), (rel_c >= 0)[None])
    o_cmp = o_cmp[:, 0]
    n_sel = -(-L // SEL_BLOCK)
    c_start = jnp.arange(n_cmp) * CMP_STRIDE
    s_start = jnp.arange(n_sel) * SEL_BLOCK
    overlap = jnp.clip(jnp.minimum(c_start[:, None] + CMP_LEN, s_start[None] + SEL_BLOCK)
                       - jnp.maximum(c_start[:, None], s_start[None]), 0, None).astype(F32) / CMP_LEN
    imp = jnp.einsum('bkgtc,cs->btks', p_cmp[:, 0], overlap)
    q_blk = q_pos // SEL_BLOCK
    sb = jnp.arange(n_sel)
    forced = (sb[None] == 0) | (sb[None] == q_blk[:, None]) | (sb[None] == q_blk[:, None] - 1)
    avail = s_start[None] <= q_pos[:, None]
    score = jnp.where(forced[None, :, None], 1e9, jnp.where(avail[None, :, None], imp, NEG_INF))
    n_top = min(SEL_TOP, n_sel)
    _, sel_idx = lax.top_k(score, n_top)
    pad = n_sel * SEL_BLOCK - L

    def to_blocks(r):
        r = jnp.pad(r, ((0, 0), (0, pad), (0, 0), (0, 0)))
        return r.reshape(B, n_sel, SEL_BLOCK, NSA_KV_HEADS, HEAD_DIM).transpose(0, 3, 1, 2, 4)
    ks_b = to_blocks(ks_rows)
    vs_b = to_blocks(vs_rows)
    bi = jnp.arange(B)[:, None, None, None]
    ki = jnp.arange(NSA_KV_HEADS)[None, None, :, None]
    tbl = rel_bias.astype(F32).reshape(NUM_BUCKETS, NSA_KV_HEADS, NSA_GROUP).transpose(1, 0, 2)
    qb = SEL_Q_BLOCK if Tq % SEL_Q_BLOCK == 0 else Tq
    nqb = Tq // qb

    def sel_block(args):
        q_i, pos_i, idx_i = args
        kg = ks_b[bi, ki, idx_i]
        vg = vs_b[bi, ki, idx_i]
        k_pos = idx_i[..., None] * SEL_BLOCK + jnp.arange(SEL_BLOCK)
        rel = pos_i[None, :, None, None, None] - k_pos
        bias = tbl[ki[..., None], t5_bucket(rel)]
        s = jnp.einsum('bqkgd,bqknsd->bqkgns', q_i, kg).astype(F32) * HEAD_DIM ** -0.5 + jnp.moveaxis(bias, -1, 3)
        mask = (rel >= 0)[:, :, :, None]
        s = jnp.where(mask, s, NEG_INF).reshape(B, qb, NSA_KV_HEADS, NSA_GROUP, n_top * SEL_BLOCK)
        p = jax.nn.softmax(s, axis=-1)
        p = jnp.where(mask.reshape(B, qb, NSA_KV_HEADS, 1, n_top * SEL_BLOCK), p, 0.0)
        return jnp.einsum('bqkgl,bqkld->bqkgd', p.astype(vg.dtype),
                          vg.reshape(B, qb, NSA_KV_HEADS, n_top * SEL_BLOCK, HEAD_DIM))

    q_m = q.reshape(B, nqb, qb, NSA_KV_HEADS, NSA_GROUP, HEAD_DIM).swapaxes(0, 1)
    pos_m = q_pos.reshape(nqb, qb)
    idx_m = sel_idx.reshape(B, nqb, qb, NSA_KV_HEADS, n_top).swapaxes(0, 1)
    o_sel = lax.map(sel_block, (q_m, pos_m, idx_m))
    o_sel = o_sel.swapaxes(0, 1).reshape(B, Tq, NSA_KV_HEADS, NSA_GROUP, HEAD_DIM)
    return o_cmp, o_sel


def window_sample(q, q_pos, k, v, k_pos, rel_bias):
    rel = q_pos[:, None] - k_pos[None]
    mask = (rel >= 0) & (rel < WINDOW)
    o, _ = attend(q[:, None], k[:, None], v[:, None], rel_bias_heads(rel[None], rel_bias), mask[None])
    return o[:, 0]


def combine_branches(gate_logit, o_cmp, o_sel, o_win):
    B, T = gate_logit.shape[:2]
    g = jax.nn.sigmoid(gate_logit.astype(F32)).reshape(B, T, NSA_KV_HEADS, NSA_GROUP, 3).astype(o_cmp.dtype)
    o = g[..., 0:1] * o_cmp + g[..., 1:2] * o_sel + g[..., 2:3] * o_win
    return o.reshape(B, T, NSA_Q_DIM)


def causal_conv(x_hist, x_new, w):
    xc = jnp.concatenate([x_hist, x_new], axis=1)
    y = lax.conv_general_dilated(xc, w[:, None, :], window_strides=(1,), padding='VALID',
                                 dimension_numbers=('NWC', 'WIO', 'NWC'), feature_group_count=xc.shape[-1])
    return jax.nn.silu(y), xc[:, -(CONV_W - 1):]


def gated_delta_rule(q, k, v, g, beta, s0):
    B, T, H, DK = q.shape
    C = DN_CHUNK if T >= DN_CHUNK else T
    n = -(-T // C)
    pad = n * C - T

    def prep(a):
        a = jnp.pad(a.astype(F32), ((0, 0), (0, pad)) + ((0, 0),) * (a.ndim - 2))
        a = a.reshape((B, n, C) + a.shape[2:])
        return jnp.moveaxis(a, (1, 3), (0, 2))

    qc, kc, vc, gc, bc = prep(q), prep(k), prep(v), prep(g), prep(beta)
    gcum = jnp.cumsum(gc, axis=-1)
    ii = jnp.arange(C)
    incl = ii[:, None] >= ii[None, :]
    strict = ii[:, None] > ii[None, :]
    diff = gcum[..., :, None] - gcum[..., None, :]
    decay = jnp.where(incl, jnp.exp(jnp.where(incl, diff, 0.0)), 0.0)
    kb = kc * bc[..., None]
    a_mat = jnp.where(strict, jnp.einsum('...id,...jd->...ij', kb, kc) * decay, 0.0)
    eye_plus = a_mat + jnp.eye(C, dtype=F32)
    u = lax.linalg.triangular_solve(eye_plus, vc * bc[..., None], left_side=True, lower=True, unit_diagonal=True)
    w = lax.linalg.triangular_solve(eye_plus, kb * jnp.exp(gcum)[..., None], left_side=True, lower=True, unit_diagonal=True)
    qk = jnp.where(incl, jnp.einsum('...id,...jd->...ij', qc, kc) * decay, 0.0)
    qg = qc * jnp.exp(gcum)[..., None]
    kg = kc * jnp.exp(gcum[..., -1:] - gcum)[..., None]
    glast = jnp.exp(gcum[..., -1])

    def step(S, xs):
        u_i, w_i, qk_i, qg_i, kg_i, gl_i = xs
        v_new = u_i - jnp.einsum('bhck,bhkv->bhcv', w_i, S)
        o = jnp.einsum('bhck,bhkv->bhcv', qg_i, S) + jnp.einsum('bhij,bhjv->bhiv', qk_i, v_new)
        S = S * gl_i[..., None, None] + jnp.einsum('bhck,bhcv->bhkv', kg_i, v_new)
        return S, o

    S, o = lax.scan(step, s0.astype(F32), (u, w, qk, qg, kg, glast))
    o = o.transpose(1, 0, 3, 2, 4).reshape(B, n * C, H, v.shape[-1])[:, :T]
    return o, S


def deltanet_mixer(qkv_raw, b_logit, a_logit, gate, conv_hist, s0, lp):
    B, T = qkv_raw.shape[:2]
    qkv, new_hist = causal_conv(conv_hist, qkv_raw, lp['conv_w'])
    q, k, v = jnp.split(qkv, 3, axis=-1)
    q = l2norm(q.reshape(B, T, DN_HEADS, DN_HEAD_DIM)) * DN_HEAD_DIM ** -0.5
    k = l2norm(k.reshape(B, T, DN_HEADS, DN_HEAD_DIM))
    v = v.reshape(B, T, DN_HEADS, DN_HEAD_DIM)
    beta = jax.nn.sigmoid(b_logit.astype(F32))
    g = -jnp.exp(lp['dn_a_log'].astype(F32)) * jax.nn.softplus(a_logit.astype(F32) + lp['dn_dt_bias'].astype(F32))
    o, s_new = gated_delta_rule(q, k, v, g, beta, s0)
    o = rmsnorm(o, lp['dn_norm_g']) * jax.nn.silu(gate.astype(F32).reshape(B, T, DN_HEADS, DN_HEAD_DIM))
    return o.astype(qkv_raw.dtype).reshape(B, T, DN_DIM), s_new.astype(s0.dtype), new_hist


def hier_moe(h, lp):
    B, T, D = h.shape
    x = h.reshape(B * T, D)
    lg = (x @ lp['router_group_w'] + lp['router_group_b']).astype(F32)
    pg = jax.nn.softmax(lg, axis=-1)
    grp = jnp.argmax(lg, axis=-1)
    wg = jnp.take_along_axis(pg, grp[:, None], axis=-1)
    le = (x @ lp['router_expert_w'] + lp['router_expert_b']).astype(F32).reshape(-1, N_GROUPS, EXPERTS_PER_GROUP)
    le = jnp.take_along_axis(le, grp[:, None, None], axis=1)[:, 0]
    pe, ie = lax.top_k(jax.nn.softmax(le, axis=-1), TOP_K_IN_GROUP)
    pe = pe / jnp.sum(pe, axis=-1, keepdims=True)
    eid = grp[:, None] * EXPERTS_PER_GROUP + ie
    gates = jnp.sum(jax.nn.one_hot(eid, N_EXPERTS, dtype=F32) * (wg * pe)[..., None], axis=1).astype(x.dtype)
    y = jnp.zeros_like(x)
    for e in range(N_EXPERTS):
        he = jax.nn.silu(x @ lp['w_gate'][e]) * (x @ lp['w_up'][e])
        y = y + gates[:, e:e + 1] * (he @ lp['w_down'][e])
    return y.reshape(B, T, D)


def modulation(c, lp):
    mod = jax.nn.silu(c) @ lp['w_ada'] + lp['b_ada']
    return jnp.split(mod[:, None, :], 6, axis=-1)


def modulate(x, g, shift, scale):
    return rmsnorm(x, g) * (1 + scale) + shift


def split_projection(h, lp):
    z = h @ lp['w_in']
    return jnp.split(z, np.cumsum(IN_SPLITS)[:-1].tolist(), axis=-1)


def finish_layer(x, o_nsa, o_dn, g1, sh2, sc2, g2, lp):
    x = x + g1 * (jnp.concatenate([o_nsa, o_dn], axis=-1) @ lp['w_out'])
    return x + g2 * hier_moe(modulate(x, lp['norm_ffn_g'], sh2, sc2), lp)


_PK_Q = (0, NSA_Q_DIM)
_PK_KV = (_PK_Q[1], _PK_Q[1] + 6 * NSA_KV_DIM)
_PK_QKV = (_PK_KV[1], _PK_KV[1] + DN_CONV_DIM)
_PK_DG = (_PK_QKV[1], _PK_QKV[1] + DN_DIM)
_PK_SM = (_PK_DG[1], _PK_DG[1] + LANES)
PK_DIM = _PK_SM[1]
SM_GATE = 0
SM_BETA = 3 * NSA_HEADS
SM_DECAY = SM_BETA + DN_HEADS


def pack_w_in(w_in):
    offs = np.cumsum((0,) + IN_SPLITS)
    part = lambda i: w_in[:, offs[i]:offs[i + 1]]
    d = w_in.shape[0]
    q = part(0).reshape(d, NSA_KV_HEADS, NSA_GROUP, HEAD_DIM).transpose(0, 2, 1, 3).reshape(d, NSA_Q_DIM)
    q = q * HEAD_DIM ** -0.5
    small = jnp.concatenate([part(7), part(9), part(10)], axis=1)
    small = jnp.pad(small, ((0, 0), (0, LANES - small.shape[1])))
    cols = [q] + [part(i) for i in range(1, 7)] + [part(8), part(11), small]
    return jnp.concatenate(cols, axis=1).astype(BF16)


def _in_proj_body(x_ref, g_ref, sc_ref, sh_ref, w_ref, q_ref, ck_ref, cv_ref, sk_ref, sv_ref, wk_ref, wv_ref,
                  skb_ref, svb_ref, wkb_ref, wvb_ref, qkv_ref, dg_ref, sm_ref):
    x = x_ref[...]
    tm, d = x.shape
    nb = sc_ref.shape[0]
    xn = x * lax.rsqrt(jnp.mean(x * x, axis=-1, keepdims=True) + EPS) * g_ref[...]
    h = xn.reshape(nb, tm // nb, d) * (1 + sc_ref[...]) + sh_ref[...]
    h = h.reshape(tm, d).astype(BF16)

    def mm(span):
        return jnp.dot(h, w_ref[:, span[0]:span[1]], preferred_element_type=F32)

    q_ref[...] = mm(_PK_Q).astype(BF16)
    kv = mm(_PK_KV)
    for i, r in enumerate((ck_ref, cv_ref, sk_ref, sv_ref, wk_ref, wv_ref)):
        r[...] = kv[:, i * NSA_KV_DIM:(i + 1) * NSA_KV_DIM]
    for i, r in enumerate((skb_ref, svb_ref, wkb_ref, wvb_ref)):
        r[...] = kv[:, (i + 2) * NSA_KV_DIM:(i + 3) * NSA_KV_DIM].astype(BF16)
    qkv_ref[...] = mm(_PK_QKV)
    dg_ref[...] = mm(_PK_DG)
    sm_ref[...] = mm(_PK_SM)


def in_projection(x, norm_g, scale, shift, w_packed, *, tm=256):
    B, T, D = x.shape
    n = B * T
    tm = min(tm, n)
    rows = max(tm // T, 1)
    per = max(T // tm, 1)
    row = lambda i: (i, 0)
    outs = [((n, NSA_Q_DIM), BF16)] + [((n, NSA_KV_DIM), F32)] * 6 + [((n, NSA_KV_DIM), BF16)] * 4 + \
           [((n, DN_CONV_DIM), F32), ((n, DN_DIM), F32), ((n, LANES), F32)]
    return pl.pallas_call(
        _in_proj_body,
        grid=(n // tm,),
        in_specs=[pl.BlockSpec((tm, D), row),
                  pl.BlockSpec((1, D), lambda i: (0, 0)),
                  pl.BlockSpec((rows, 1, D), lambda i: (i // per, 0, 0)),
                  pl.BlockSpec((rows, 1, D), lambda i: (i // per, 0, 0)),
                  pl.BlockSpec((D, PK_DIM), lambda i: (0, 0))],
        out_specs=[pl.BlockSpec((tm, s[1]), row) for s, _ in outs],
        out_shape=[jax.ShapeDtypeStruct(s, dt) for s, dt in outs],
        compiler_params=pltpu.CompilerParams(dimension_semantics=("parallel",), vmem_limit_bytes=VMEM_LIMIT),
        name="in_projection",
    )(x.reshape(n, D), norm_g.reshape(1, D), scale, shift, w_packed)


CHUNK_W = CMP_STRIDE * NSA_KV_DIM
CMP_HALVES = CMP_LEN // CMP_STRIDE


def pack_compress(pos, w1, w2):
    eye = jnp.eye(NSA_KV_HEADS, dtype=F32)
    w1r = w1.reshape(CMP_HALVES, CMP_STRIDE, HEAD_DIM, CMP_HIDDEN)
    w1p = jnp.einsum('ijdc,kl->ijkdlc', w1r, eye).reshape(CMP_HALVES, CHUNK_W, NSA_KV_HEADS * CMP_HIDDEN)
    w2p = jnp.einsum('cd,kl->kcld', w2, eye).reshape(NSA_KV_HEADS * CMP_HIDDEN, NSA_KV_DIM)
    posp = jnp.broadcast_to(pos.reshape(CMP_HALVES, CMP_STRIDE, 1, HEAD_DIM),
                            (CMP_HALVES, CMP_STRIDE, NSA_KV_HEADS, HEAD_DIM)).reshape(CMP_HALVES, CHUNK_W)
    return posp, w1p.astype(BF16), w2p.astype(BF16)


def _compress_one(rows_ref, pos_ref, w1_ref, w2_ref):
    n_chunk = rows_ref.shape[0] // CMP_STRIDE
    x = jnp.concatenate([rows_ref[pl.ds(j, n_chunk, stride=CMP_STRIDE), :] for j in range(CMP_STRIDE)], axis=1)
    hid = None
    for i in range(CMP_HALVES):
        part = jnp.dot((x + pos_ref[i:i + 1]).astype(BF16), w1_ref[i], preferred_element_type=F32)
        if i:
            part = pltpu.roll(part, shift=n_chunk - i, axis=0)
        hid = part if hid is None else hid + part
    hid = hid * jax.nn.sigmoid(hid)
    return jnp.dot(hid.astype(BF16), w2_ref[...], preferred_element_type=F32)


def _compress_body(xk_ref, xv_ref, pk_ref, pv_ref, w1k_ref, w1v_ref, w2k_ref, w2v_ref, kc_ref, vc_ref):
    kc_ref[0] = _compress_one(xk_ref, pk_ref, w1k_ref, w2k_ref).astype(BF16)
    vc_ref[0] = _compress_one(xv_ref, pv_ref, w1v_ref, w2v_ref).astype(BF16)


def compress_tokens(ck, cv, pk, pv, B):
    L = ck.shape[0] // B
    n_chunk = L // CMP_STRIDE
    full = lambda a: pl.BlockSpec(a.shape, lambda b: (0,) * a.ndim)
    rows = pl.BlockSpec((L, NSA_KV_DIM), lambda b: (b, 0))
    tok = pl.BlockSpec((1, n_chunk, NSA_KV_DIM), lambda b: (b, 0, 0))
    return pl.pallas_call(
        _compress_body,
        grid=(B,),
        in_specs=[rows, rows, full(pk[0]), full(pv[0]), full(pk[1]), full(pv[1]), full(pk[2]), full(pv[2])],
        out_specs=[tok, tok],
        out_shape=[jax.ShapeDtypeStruct((B, n_chunk, NSA_KV_DIM), BF16)] * 2,
        compiler_params=pltpu.CompilerParams(dimension_semantics=("parallel",), vmem_limit_bytes=VMEM_LIMIT),
        name="compress_tokens",
    )(ck, cv, pk[0], pv[0], pk[1], pv[1], pk[2], pv[2])


def _t5_bucket_np(rel):
    n = np.maximum(rel, 0)
    max_exact = NUM_BUCKETS // 2
    nf = np.maximum(n, 1).astype(np.float32)
    large = max_exact + (np.log(nf / max_exact) / math.log(MAX_DISTANCE / max_exact)
                         * (NUM_BUCKETS - max_exact)).astype(np.int32)
    return np.where(n < max_exact, n, np.minimum(large, NUM_BUCKETS - 1))


assert (_t5_bucket_np(np.arange(ATT_TILE + 1, 1 << 16)) == NUM_BUCKETS - 1).all()


NB_SAME, NB_PREV, NB_FAR, NB_EDGE, NB_NONE = range(5)
FAR_TILE = 512


def bias_lookup(rel_bias, rel):
    bucket = t5_bucket(rel)
    tbl = rel_bias.astype(F32)
    shape = (tbl.shape[1],) + (1,) * rel.ndim
    out = jnp.zeros((tbl.shape[1],) + rel.shape, F32)
    for b in range(NUM_BUCKETS):
        out = jnp.where(bucket == b, tbl[b].reshape(shape), out)
    return out


def near_bias_tables(rel_bias):
    t = ATT_TILE
    i = jnp.arange(t)[:, None]
    j = jnp.arange(t)[None, :]
    diag = jnp.where(i >= j, bias_lookup(rel_bias, i - j), MASKED)
    prev = bias_lookup(rel_bias, t + i - j)
    far = jnp.broadcast_to(rel_bias.astype(F32)[NUM_BUCKETS - 1][:, None, None], prev.shape)
    edge = jnp.where(j > i, far, MASKED)
    return jnp.stack([diag, prev, far, edge, jnp.full_like(far, MASKED)])


def cmp_bias_table(rel_bias, q0, tq, n_chunk):
    first_end = CMP_LEN - 1
    span = CMP_STRIDE * (n_chunk - 1)
    n_rel = tq + span
    rel = q0 - first_end - span + jnp.arange(n_rel)
    vals = jnp.where(rel >= 0, bias_lookup(rel_bias, rel), MASKED)
    vals = jnp.roll(vals, -span, axis=1)
    flat = jnp.tile(vals, (1, n_chunk))[:, :n_chunk * (n_rel - CMP_STRIDE)]
    by_token = flat.reshape(-1, n_chunk, n_rel - CMP_STRIDE)[:, :, :tq]
    return jnp.swapaxes(by_token, 1, 2)


def overlap_matrix(n_chunk, n_sel):
    c0 = np.arange(n_chunk)[:, None] * CMP_STRIDE
    s0 = np.arange(LANES)[None, :] * SEL_BLOCK
    ov = np.clip(np.minimum(c0 + CMP_LEN, s0 + SEL_BLOCK) - np.maximum(c0, s0), 0, None) / CMP_LEN
    ov = np.where(np.arange(LANES)[None, :] < n_sel, ov, 0.0)
    return jnp.asarray(ov, F32)


def expand_matrix(n_keys):
    e = (np.arange(n_keys)[None, :] // SEL_BLOCK) == np.arange(LANES)[:, None]
    return jnp.asarray(e, BF16)


def _head_queries(q_ref, tq):
    low = lax.broadcasted_iota(jnp.int32, (tq, LANES), 1) < HEAD_DIM
    zero = jnp.zeros((tq, LANES), BF16)
    parts = []
    for kv in range(NSA_KV_HEADS):
        for g in range(NSA_GROUP):
            blk = q_ref[:, g * LANES:(g + 1) * LANES]
            parts.append(jnp.where(low if kv == 0 else ~low, blk, zero))
    return parts


def _stack_queries(q_ref, tq):
    return jnp.concatenate(_head_queries(q_ref, tq), axis=0)


def _dot_nt(a, b):
    return lax.dot_general(a, b, (((1,), (1,)), ((), ())), preferred_element_type=F32)


def _select_blocks(imp, q_pos, n_sel, n_top):
    lane = lax.broadcasted_iota(jnp.int32, imp.shape, 1)
    q_blk = q_pos // SEL_BLOCK
    forced = jnp.where(lane == 0, 1.0, jnp.where(lane == q_blk, 1.0, jnp.where(lane == q_blk - 1, 1.0, 0.0)))
    score = jnp.where(lane * SEL_BLOCK <= q_pos, imp, NEG_INF)
    score = jnp.where(forced > 0.5, 1e9, score)
    score = jnp.where(lane < n_sel, score, -3e38)
    rank = jnp.zeros(imp.shape, F32)
    for s in range(n_sel):
        col = score[:, s:s + 1]
        tie = jnp.where(lane > s, 1.0, 0.0)
        rank = rank + jnp.where(col > score, 1.0, jnp.where(col == score, tie, 0.0))
    return jnp.where(rank < n_top, 1.0, 0.0)


def _gate_columns(gl_ref, j):
    g = jax.nn.sigmoid(gl_ref[...])
    cols = [g[:, SM_GATE + 3 * h + j:SM_GATE + 3 * h + j + 1] for h in range(NSA_HEADS)]
    return jnp.stack(cols, axis=0)


def _nsa_prompt_body(q_ref, kc_ref, vc_ref, sk_ref, sv_ref, wk_ref, wv_ref, bc_ref, nb_ref, gl_ref, ov_ref, ex_ref,
                     o_ref, mb_ref, m_ref, l_ref, acc_ref, out_ref, *, n_sel, n_top):
    i = pl.program_id(1)
    tq = q_ref.shape[0]
    h = NSA_HEADS
    q = _stack_queries(q_ref, tq)

    s = _dot_nt(q, kc_ref[0]).reshape(h, tq, -1) + bc_ref[...]
    m = jnp.max(s, axis=-1, keepdims=True)
    p = jnp.where(s > 0.5 * MASKED, jnp.exp(s - m), 0.0)
    l = jnp.sum(p, axis=-1, keepdims=True)
    p = p * (1.0 / jnp.where(l > 0.0, l, 1.0))
    o_cmp = jnp.dot(p.reshape(h * tq, -1).astype(BF16), vc_ref[0], preferred_element_type=F32)
    out_ref[...] = _gate_columns(gl_ref, 0) * o_cmp.reshape(h, tq, LANES)

    q_pos = i * tq + lax.broadcasted_iota(jnp.int32, (tq, LANES), 0)
    for kv in range(NSA_KV_HEADS):
        pk = p[kv * NSA_GROUP]
        for g in range(1, NSA_GROUP):
            pk = pk + p[kv * NSA_GROUP + g]
        imp = jnp.dot(pk, ov_ref[...], precision=lax.Precision.HIGHEST, preferred_element_type=F32)
        sel = _select_blocks(imp, q_pos, n_sel, n_top)
        keys = jnp.dot(sel.astype(BF16), ex_ref[...], preferred_element_type=F32)
        mb_ref[kv] = (keys - 1.0) * (-MASKED)

    tk = ATT_TILE
    kvg = lambda x: x.reshape(NSA_KV_HEADS, NSA_GROUP, tq, x.shape[-1])

    near0 = jnp.maximum(i - 1, 0)
    near_rows = pl.ds(pl.multiple_of(near0 * tk, tk), 2 * tk)
    first = i == 0
    near_bias = jnp.concatenate([nb_ref[jnp.where(first, NB_SAME, NB_PREV)],
                                 nb_ref[jnp.where(first, NB_NONE, NB_SAME)]], axis=-1)
    s = _dot_nt(q, sk_ref[near_rows, :]).reshape(h, tq, 2 * tk) + near_bias
    s = (kvg(s) + mb_ref[:, :, near_rows][:, None]).reshape(h, tq, 2 * tk)
    m = jnp.max(s, axis=-1, keepdims=True)
    p = jnp.exp(s - m)
    m_ref[...] = m
    l_ref[...] = jnp.sum(p, axis=-1, keepdims=True)
    acc_ref[...] = jnp.dot(p.reshape(h * tq, 2 * tk).astype(BF16), sv_ref[near_rows, :],
                           preferred_element_type=F32).reshape(h, tq, LANES)

    far_end = (i - 1) * tk
    far_bias = nb_ref[NB_FAR][:, 0:1, 0:1]
    col = lax.broadcasted_iota(jnp.int32, (1, 1, FAR_TILE), 2)

    def far_step(f, carry):
        rows = pl.ds(pl.multiple_of(f * FAR_TILE, FAR_TILE), FAR_TILE)
        edge = jnp.where(f * FAR_TILE + col < far_end, 0.0, MASKED)
        s = _dot_nt(q, sk_ref[rows, :]).reshape(h, tq, FAR_TILE) + (far_bias + edge)
        s = (kvg(s) + mb_ref[:, :, rows][:, None]).reshape(h, tq, FAR_TILE)
        m_old = m_ref[...]
        m_new = jnp.maximum(m_old, jnp.max(s, axis=-1, keepdims=True))
        alpha = jnp.exp(m_old - m_new)
        p = jnp.exp(s - m_new)
        l_ref[...] = alpha * l_ref[...] + jnp.sum(p, axis=-1, keepdims=True)
        pv = jnp.dot(p.reshape(h * tq, FAR_TILE).astype(BF16), sv_ref[rows, :], preferred_element_type=F32)
        acc_ref[...] = alpha * acc_ref[...] + pv.reshape(h, tq, LANES)
        m_ref[...] = m_new
        return carry
    lax.fori_loop(0, (jnp.maximum(far_end, 0) + FAR_TILE - 1) // FAR_TILE, far_step, 0)
    out_ref[...] += _gate_columns(gl_ref, 1) * (acc_ref[...] * (1.0 / l_ref[...]))

    n_back = WINDOW // tk
    win0 = jnp.maximum(i - n_back, 0)
    win_rows = pl.ds(pl.multiple_of(win0 * tk, tk), (n_back + 1) * tk)
    tiles = []
    for c in range(n_back + 1):
        r = i - (win0 + c)
        tiles.append(nb_ref[jnp.where(r < 0, NB_NONE, jnp.where(r == n_back, NB_EDGE, jnp.minimum(r, NB_FAR)))])
    s = _dot_nt(q, wk_ref[win_rows, :]).reshape(h, tq, (n_back + 1) * tk) + jnp.concatenate(tiles, axis=-1)
    p = jnp.exp(s - jnp.max(s, axis=-1, keepdims=True))
    l = jnp.sum(p, axis=-1, keepdims=True)
    o_win = jnp.dot(p.reshape(h * tq, (n_back + 1) * tk).astype(BF16), wv_ref[win_rows, :],
                    preferred_element_type=F32).reshape(h, tq, LANES)
    out_ref[...] += _gate_columns(gl_ref, 2) * (o_win * (1.0 / l))

    low = lax.broadcasted_iota(jnp.int32, (tq, LANES), 1) < HEAD_DIM
    for g in range(NSA_GROUP):
        o_ref[:, g * LANES:(g + 1) * LANES] = jnp.where(low, out_ref[g], out_ref[NSA_GROUP + g]).astype(BF16)


def nsa_prompt(q, kc, vc, skb, svb, wkb, wvb, small, rel_bias, B, T):
    tq = ATT_TILE
    nq = T // tq
    assert T % FAR_TILE == 0 and T >= WINDOW + ATT_TILE and WINDOW % ATT_TILE == 0
    n_chunk = kc.shape[1]
    n_sel = -(-T // SEL_BLOCK)
    n_top = min(SEL_TOP, n_sel)
    bias_c = cmp_bias_table(rel_bias, 0, T, n_chunk)
    nb = near_bias_tables(rel_bias)
    ov = overlap_matrix(n_chunk, n_sel)
    ex = expand_matrix(T)
    h = NSA_HEADS
    full = lambda a: pl.BlockSpec(a.shape, lambda b, i: (0,) * a.ndim)
    seq = pl.BlockSpec((T, NSA_KV_DIM), lambda b, i: (b, 0))
    tok = pl.BlockSpec((1, n_chunk, NSA_KV_DIM), lambda b, i: (b, 0, 0))
    return pl.pallas_call(
        functools.partial(_nsa_prompt_body, n_sel=n_sel, n_top=n_top),
        grid=(B, nq),
        in_specs=[pl.BlockSpec((tq, NSA_Q_DIM), lambda b, i: (b * nq + i, 0)),
                  tok, tok, seq, seq, seq, seq,
                  pl.BlockSpec((h, tq, n_chunk), lambda b, i: (0, i, 0)),
                  full(nb),
                  pl.BlockSpec((tq, LANES), lambda b, i: (b * nq + i, 0)),
                  full(ov), full(ex)],
        out_specs=pl.BlockSpec((tq, NSA_Q_DIM), lambda b, i: (b * nq + i, 0)),
        out_shape=jax.ShapeDtypeStruct((B * T, NSA_Q_DIM), BF16),
        scratch_shapes=[pltpu.VMEM((NSA_KV_HEADS, tq, T), F32),
                        pltpu.VMEM((h, tq, 1), F32), pltpu.VMEM((h, tq, 1), F32),
                        pltpu.VMEM((h, tq, LANES), F32), pltpu.VMEM((h, tq, LANES), F32)],
        compiler_params=pltpu.CompilerParams(dimension_semantics=("parallel", "arbitrary"),
                                             vmem_limit_bytes=VMEM_LIMIT),
        name="nsa_prompt",
    )(q, kc, vc, skb, svb, wkb, wvb, bias_c, nb, small, ov, ex)


def sample_bias_tables(rel_bias, past, T, W):
    t = jnp.arange(T)[:, None]
    pad = jnp.arange(ATT_TILE)[None, :]
    new = jnp.where((pad < T) & (t >= pad), bias_lookup(rel_bias, t - pad), MASKED)
    sel = bias_lookup(rel_bias, past + t - jnp.arange(past)[None, :])
    rel_w = W + t - jnp.arange(W)[None, :]
    win = jnp.where(rel_w < WINDOW, bias_lookup(rel_bias, rel_w), MASKED)
    cmp = cmp_bias_table(rel_bias, past, T, past // CMP_STRIDE)
    return cmp, jnp.concatenate([sel, new], axis=-1), jnp.concatenate([win, new], axis=-1)


def _softmax_rows(s):
    m = jnp.max(s, axis=-1, keepdims=True)
    p = jnp.exp(s - m)
    return p * (1.0 / jnp.sum(p, axis=-1, keepdims=True))


def _nsa_sample_body(pt_ref, q_ref, skn_ref, svn_ref, wkn_ref, wvn_ref, gl_ref, wink_ref, winv_ref,
                     bc_ref, bs_ref, bw_ref, ov_ref, ex_ref, pk_ref, pv_ref, w1k_ref, w1v_ref, w2k_ref, w2v_ref,
                     *rest, n_pages, n_sel, n_top, past):
    pages = [rest[i * n_pages:(i + 1) * n_pages] for i in range(4)]
    o_ref, wko_ref, wvo_ref, s_ref, new_ref, rows_ref = rest[4 * n_pages:]
    h = NSA_HEADS
    tq = q_ref.shape[0]
    tk = ATT_TILE
    q = _stack_queries(q_ref, tq)

    new_ref[...] = jnp.zeros(new_ref.shape, F32)
    for n, r in enumerate((skn_ref, svn_ref, wkn_ref, wvn_ref)):
        new_ref[n, 0:tq, :] = r[...]

    for pg in range(n_pages):
        rows_ref[0, pg * tk:(pg + 1) * tk, :] = pages[0][pg][0].T
        rows_ref[1, pg * tk:(pg + 1) * tk, :] = pages[1][pg][0].T
    kc = _compress_one(rows_ref.at[0], pk_ref, w1k_ref, w2k_ref).astype(BF16)
    vc = _compress_one(rows_ref.at[1], pv_ref, w1v_ref, w2v_ref).astype(BF16)
    s = _dot_nt(q, kc).reshape(h, tq, -1) + bc_ref[...]
    p = _softmax_rows(s)
    o_cmp = jnp.dot(p.reshape(h * tq, -1).astype(BF16), vc, preferred_element_type=F32)
    out = _gate_columns(gl_ref, 0) * o_cmp.reshape(h, tq, LANES)

    q_pos = past + lax.broadcasted_iota(jnp.int32, (tq, LANES), 0)
    masks = []
    for kv in range(NSA_KV_HEADS):
        pkv = p[kv * NSA_GROUP]
        for g in range(1, NSA_GROUP):
            pkv = pkv + p[kv * NSA_GROUP + g]
        imp = jnp.dot(pkv, ov_ref[...], precision=lax.Precision.HIGHEST, preferred_element_type=F32)
        sel = _select_blocks(imp, q_pos, n_sel, n_top)
        keys = jnp.dot(sel.astype(BF16), ex_ref[...], preferred_element_type=F32)
        masks.append((keys - 1.0) * (-MASKED))
    mask = jnp.stack(masks, axis=0)
    for pg in range(n_pages):
        s_ref[:, pg * tk:(pg + 1) * tk] = jnp.dot(q, pages[2][pg][0].astype(BF16), preferred_element_type=F32)
    s_ref[:, n_pages * tk:(n_pages + 1) * tk] = _dot_nt(q, new_ref[0].astype(BF16))
    width = (n_pages + 1) * tk
    s = s_ref[...].reshape(h, tq, width) + bs_ref[...]
    s = (s.reshape(NSA_KV_HEADS, NSA_GROUP, tq, width) + mask[:, None]).reshape(h, tq, width)
    p = _softmax_rows(s).reshape(h * tq, width).astype(BF16)
    o_sel = jnp.dot(p[:, n_pages * tk:], new_ref[1].astype(BF16), preferred_element_type=F32)
    for pg in range(n_pages):
        o_sel = o_sel + _dot_nt(p[:, pg * tk:(pg + 1) * tk], pages[3][pg][0].astype(BF16))
    out = out + _gate_columns(gl_ref, 1) * o_sel.reshape(h, tq, LANES)

    w = wink_ref.shape[2]
    s_ref[:, 0:w] = jnp.dot(q, wink_ref[0].astype(BF16), preferred_element_type=F32)
    s_ref[:, w:w + tk] = _dot_nt(q, new_ref[2].astype(BF16))
    s = s_ref[:, 0:w + tk].reshape(h, tq, w + tk) + bw_ref[...]
    p = _softmax_rows(s).reshape(h * tq, w + tk).astype(BF16)
    o_win = jnp.dot(p[:, w:], new_ref[3].astype(BF16), preferred_element_type=F32) \
        + _dot_nt(p[:, 0:w], winv_ref[0].astype(BF16))
    out = out + _gate_columns(gl_ref, 2) * o_win.reshape(h, tq, LANES)

    low = lax.broadcasted_iota(jnp.int32, (tq, LANES), 1) < HEAD_DIM
    for g in range(NSA_GROUP):
        o_ref[:, g * LANES:(g + 1) * LANES] = jnp.where(low, out[g], out[NSA_GROUP + g])

    keep = w - tq
    lane_w = lax.broadcasted_iota(jnp.int32, (NSA_KV_DIM, w), 1)
    for old_ref, new_rows_ref, dst_ref in ((wink_ref, wkn_ref, wko_ref), (winv_ref, wvn_ref, wvo_ref)):
        shifted = pltpu.roll(old_ref[0], shift=keep, axis=1)
        tail = jnp.concatenate([jnp.zeros((tk - tq, NSA_KV_DIM), F32), new_rows_ref[...]], axis=0).T
        tail = jnp.concatenate([jnp.zeros((NSA_KV_DIM, w - tk), F32), tail], axis=1)
        dst_ref[0] = jnp.where(lane_w >= keep, tail, shifted)


def nsa_sample(q, skn, svn, wkn, wvn, small, pools, win_k, win_v, page_table, rel_bias, pk, pv, B, T):
    n_pages = page_table.shape[1]
    past = n_pages * PAGE_SIZE
    W = win_k.shape[1]
    L = past + T
    n_cmp = (L - CMP_LEN) // CMP_STRIDE + 1
    assert PAGE_SIZE == ATT_TILE and T % 8 == 0 and T <= SEL_BLOCK and past % SEL_BLOCK == 0
    assert (n_cmp - 1) * CMP_STRIDE + CMP_LEN <= past and n_cmp >= past // CMP_STRIDE - CMP_HALVES + 1
    assert W % ATT_TILE == 0 and W + T >= WINDOW
    n_sel = -(-L // SEL_BLOCK)
    n_top = min(SEL_TOP, n_sel)
    w_keep = min(WINDOW, W + T)
    bc, bs, bw = sample_bias_tables(rel_bias, past, T, W)
    ov = overlap_matrix(past // CMP_STRIDE, n_sel)
    ex = expand_matrix(past + ATT_TILE)
    n_pool = pools[0].shape[0]
    chunks = PAGE_SIZE // CMP_STRIDE
    assert W == w_keep
    feat_major = lambda a: a.transpose(0, 2, 3, 1).reshape(a.shape[0], NSA_KV_DIM, a.shape[1])
    cmp_pools = [feat_major(p) for p in pools[:2]]
    sel_pools = [feat_major(p) for p in pools[2:]]
    wk3 = feat_major(win_k)
    wv3 = feat_major(win_v)

    full = lambda a: pl.BlockSpec(a.shape, lambda b, pt: (0,) * a.ndim)
    row = lambda w: pl.BlockSpec((T, w), lambda b, pt: (b, 0))
    seq = lambda n: pl.BlockSpec((1, NSA_KV_DIM, n), lambda b, pt: (b, 0, 0))
    page = lambda shape, pg: pl.BlockSpec((1,) + shape, lambda b, pt: (pt[b, pg], 0, 0))
    page_specs, page_args = [], []
    for pool in cmp_pools + sel_pools:
        for pg in range(n_pages):
            page_specs.append(page((NSA_KV_DIM, PAGE_SIZE), pg))
            page_args.append(pool)
    tables = (bc, bs, bw, ov, ex, pk[0], pv[0], pk[1], pv[1], pk[2], pv[2])
    grid_spec = pltpu.PrefetchScalarGridSpec(
        num_scalar_prefetch=1,
        grid=(B,),
        in_specs=[row(NSA_Q_DIM), row(NSA_KV_DIM), row(NSA_KV_DIM), row(NSA_KV_DIM), row(NSA_KV_DIM), row(LANES),
                  seq(W), seq(W)] + [full(a) for a in tables] + page_specs,
        out_specs=[row(NSA_Q_DIM), seq(w_keep), seq(w_keep)],
        scratch_shapes=[pltpu.VMEM((NSA_HEADS * T, past + ATT_TILE), F32),
                        pltpu.VMEM((4, ATT_TILE, NSA_KV_DIM), F32),
                        pltpu.VMEM((2, past, NSA_KV_DIM), F32)],
    )
    o, wk_next, wv_next = pl.pallas_call(
        functools.partial(_nsa_sample_body, n_pages=n_pages, n_sel=n_sel, n_top=n_top, past=past),
        grid_spec=grid_spec,
        out_shape=[jax.ShapeDtypeStruct((B * T, NSA_Q_DIM), F32),
                   jax.ShapeDtypeStruct((B, NSA_KV_DIM, w_keep), F32),
                   jax.ShapeDtypeStruct((B, NSA_KV_DIM, w_keep), F32)],
        compiler_params=pltpu.CompilerParams(dimension_semantics=("arbitrary",), vmem_limit_bytes=VMEM_LIMIT),
        name="nsa_sample",
    )(page_table, q, skn, svn, wkn, wvn, small, wk3, wv3, *tables, *page_args)
    row_major = lambda a: a.reshape(B, NSA_KV_HEADS, HEAD_DIM, w_keep).transpose(0, 3, 1, 2)
    return o, row_major(wk_next), row_major(wv_next)


HIST_ROWS = 8
DN_CHUNKS_PER_STEP = 4


def _split_bf16(x):
    hi = x.astype(BF16)
    lo = (x - hi.astype(F32)).astype(BF16)
    return hi, lo


def _mm_hi(a, b):
    ah, al = _split_bf16(a)
    bh, bl = _split_bf16(b)
    d = lambda x, y: jnp.dot(x, y, preferred_element_type=F32)
    return d(ah, bh) + (d(ah, bl) + d(al, bh))


def _bmm(a, b):
    return jnp.einsum('hij,hjk->hik', a, b, preferred_element_type=F32)


def _bmm_nt(a, b):
    return jnp.einsum('hid,hjd->hij', a, b, preferred_element_type=F32)


def _bmm_hi(a, b):
    ah, al = _split_bf16(a)
    bh, bl = _split_bf16(b)
    return _bmm(ah, bh) + (_bmm(ah, bl) + _bmm(al, bh))


def _cumsum_rows(tri, x):
    hi, lo = _split_bf16(x)
    lo2 = (x - hi.astype(F32) - lo.astype(F32)).astype(BF16)
    d = lambda y: jnp.dot(tri, y, preferred_element_type=F32)
    return d(hi) + (d(lo) + d(lo2))


def _softplus(x):
    return jnp.maximum(x, 0.0) + jnp.log1p(jnp.exp(-jnp.abs(x)))


def _silu(x):
    return x * jax.nn.sigmoid(x)


def _deltanet_body(x_ref, sm_ref, dg_ref, hist_ref, s0_ref, cw_ref, apar_ref, dt_ref, ng_ref,
                   o_ref, s_out_ref, hist_out_ref, xbuf_ref, s_ref, *, c):
    j = pl.program_id(1)
    rows = x_ref.shape[0]
    n_grp = rows // c
    hd = DN_HEAD_DIM
    n_hist = CONV_W - 1

    n_pair = DN_HEADS // 2
    zero_blk = jnp.zeros((hd, hd), F32)

    @pl.when(j == 0)
    def _():
        xbuf_ref[HIST_ROWS - n_hist:HIST_ROWS, :] = hist_ref[0]
        for p in range(n_pair):
            s_ref[p] = jnp.concatenate([jnp.concatenate([s0_ref[0, 2 * p], zero_blk], axis=1),
                                        jnp.concatenate([zero_blk, s0_ref[0, 2 * p + 1]], axis=1)], axis=0)

    xbuf_ref[HIST_ROWS:HIST_ROWS + rows, :] = x_ref[...]
    y = None
    for t in range(CONV_W):
        lo = HIST_ROWS - n_hist + t
        term = xbuf_ref[lo:lo + rows, :] * cw_ref[t:t + 1, :]
        y = term if y is None else y + term
    y = _silu(y)
    tail = xbuf_ref[HIST_ROWS + rows - n_hist:HIST_ROWS + rows, :]
    xbuf_ref[HIST_ROWS - n_hist:HIST_ROWS, :] = tail

    sm = sm_ref[...]
    beta_all = jax.nn.sigmoid(sm)
    g_all = apar_ref[...] * _softplus(sm + dt_ref[...])
    row = lax.broadcasted_iota(jnp.int32, (rows, rows), 0)
    col = lax.broadcasted_iota(jnp.int32, (rows, rows), 1)
    tri = jnp.where((row >= col) & (row // c == col // c), 1.0, 0.0).astype(BF16)
    gcum_all = _cumsum_rows(tri, g_all)
    gcum_t = gcum_all.T

    wd, wj = 2 * hd, 2 * c
    first_d = lax.broadcasted_iota(jnp.int32, (1, wd), 1) < hd
    first_j = lax.broadcasted_iota(jnp.int32, (1, wj), 1) < c
    rj = lax.broadcasted_iota(jnp.int32, (c, wj), 0)
    cj = lax.broadcasted_iota(jnp.int32, (c, wj), 1)
    cj = jnp.where(cj < c, cj, cj - c)
    incl, strict = rj >= cj, rj > cj
    eye = jnp.where(rj == cj, 1.0, 0.0)
    same_head = (lax.broadcasted_iota(jnp.int32, (wd, wd), 0) < hd) == (lax.broadcasted_iota(jnp.int32, (wd, wd), 1) < hd)
    ones_bd = jnp.where(same_head, 1.0, 0.0).astype(BF16)
    first_rows = lax.broadcasted_iota(jnp.int32, (wd, 1), 0) < hd

    items = [(g, p) for g in range(n_grp) for p in range(n_pair)]

    def block_diag(x, first):
        zero = jnp.zeros_like(x)
        return jnp.concatenate([jnp.where(first, x, zero), jnp.where(first, zero, x)], axis=1)

    def head_sums(x):
        n = x.shape[0]
        hi, lo = _split_bf16(x.reshape(n * c, wd))
        tot = jnp.dot(hi, ones_bd, preferred_element_type=F32) + jnp.dot(lo, ones_bd, preferred_element_type=F32)
        return tot.reshape(n, c, wd)

    def mm_hi(a_parts, b_parts):
        return _bmm(a_parts[0], b_parts[0]) + (_bmm(a_parts[0], b_parts[1]) + _bmm(a_parts[1], b_parts[0]))

    def per_head(cols, base, first):
        return jnp.stack([jnp.where(first, cols[g * c:(g + 1) * c, base + 2 * p:base + 2 * p + 1],
                                    cols[g * c:(g + 1) * c, base + 2 * p + 1:base + 2 * p + 2])
                          for g, p in items], axis=0)

    lanes = lambda x, off: jnp.stack([x[g * c:(g + 1) * c, off + p * wd:off + (p + 1) * wd] for g, p in items], axis=0)
    q = lanes(y, 0)
    k = lanes(y, DN_DIM)
    v = lanes(y, 2 * DN_DIM)
    q = q * lax.rsqrt(head_sums(q * q) + EPS) * hd ** -0.5
    k = k * lax.rsqrt(head_sums(k * k) + EPS)
    beta = per_head(beta_all, SM_BETA, first_d)
    gc_d = per_head(gcum_all, SM_DECAY, first_d)
    gc_j = per_head(gcum_all, SM_DECAY, first_j)
    gr_j = jnp.stack([jnp.concatenate([gcum_t[SM_DECAY + 2 * p:SM_DECAY + 2 * p + 1, g * c:(g + 1) * c],
                                       gcum_t[SM_DECAY + 2 * p + 1:SM_DECAY + 2 * p + 2, g * c:(g + 1) * c]], axis=1)
                      for g, p in items], axis=0)
    gl_d = gc_d[:, c - 1:c, :]
    decay = jnp.where(incl, jnp.exp(jnp.where(incl, gc_j - gr_j, 0.0)), 0.0)
    eg = jnp.exp(gc_d)
    kb = k * beta
    k_bd = block_diag(k.astype(BF16), first_d)
    a = jnp.where(strict, _bmm_nt(kb.astype(BF16), k_bd) * decay, 0.0)
    t_inv = eye - a
    split_bd = lambda parts, first: tuple(block_diag(x, first) for x in parts)
    pow_parts = _split_bf16(a)
    pow_bd = split_bd(pow_parts, first_j)
    for _ in range(int(math.log2(c)) - 1):
        pow_parts = _split_bf16(mm_hi(pow_parts, pow_bd))
        pow_bd = split_bd(pow_parts, first_j)
        t_inv = t_inv + mm_hi(_split_bf16(t_inv), pow_bd)
    vb_parts = split_bd(_split_bf16(v * beta), first_d)
    kbg_parts = split_bd(_split_bf16(kb * eg), first_d)
    rhs = tuple(jnp.concatenate([x, z], axis=2) for x, z in zip(vb_parts, kbg_parts))
    uw = mm_hi(_split_bf16(t_inv), rhs)
    u, w = uw[:, :, :wd], uw[:, :, wd:]
    qk = jnp.where(incl, _bmm_nt(q.astype(BF16), k_bd) * decay, 0.0).astype(BF16)
    wq = jnp.concatenate([w, q * eg], axis=1).astype(BF16)
    kg = (k * jnp.exp(gl_d - gc_d)).astype(BF16)
    gl_rows = jnp.where(first_rows, jnp.exp(gl_d[:, :, 0:1]), jnp.exp(gl_d[:, :, hd:hd + 1]))

    s = s_ref[...]
    outs = []
    for g in range(n_grp):
        sl = slice(g * n_pair, (g + 1) * n_pair)
        ws = _bmm(wq[sl], s.astype(BF16))
        vn = (u[sl] - ws[:, :c]).astype(BF16)
        outs.append(ws[:, c:] + _bmm(qk[sl], block_diag(vn, first_d)))
        kv_outer = jnp.einsum('pck,pcv->pkv', kg[sl], vn, preferred_element_type=F32)
        s = s * gl_rows[sl] + jnp.where(same_head, kv_outer, 0.0)
    s_ref[...] = s
    o = jnp.concatenate(outs, axis=0)
    ng = jnp.concatenate([ng_ref[...], ng_ref[...]], axis=1)
    o = o * lax.rsqrt(head_sums(o * o) * (1.0 / hd) + EPS) * ng
    for n, (g, p) in enumerate(items):
        gate = _silu(dg_ref[g * c:(g + 1) * c, p * wd:(p + 1) * wd])
        o_ref[g * c:(g + 1) * c, p * wd:(p + 1) * wd] = (o[n] * gate).astype(o_ref.dtype)

    @pl.when(j == pl.num_programs(1) - 1)
    def _():
        for p in range(n_pair):
            s_pair = s_ref[p]
            s_out_ref[0, 2 * p] = s_pair[0:hd, 0:hd]
            s_out_ref[0, 2 * p + 1] = s_pair[hd:2 * hd, hd:2 * hd]
        hist_out_ref[0] = tail


def deltanet(qkv_raw, small, dg, conv_hist, s0, lp, B, T):
    c = DN_CHUNK if T >= DN_CHUNK else T
    assert T % c == 0 and c % 8 == 0 and c >= CONV_W - 1 and (c & (c - 1)) == 0
    n_grp = math.gcd(T // c, DN_CHUNKS_PER_STEP)
    step = n_grp * c
    n = T // step
    put = lambda vals, base: jnp.zeros((LANES,), F32).at[base + jnp.arange(DN_HEADS)].set(vals).reshape(1, LANES)
    apar = put(-jnp.exp(lp['dn_a_log'].astype(F32)), SM_DECAY)
    dtb = put(lp['dn_dt_bias'].astype(F32), SM_DECAY)
    rows = lambda w: pl.BlockSpec((step, w), lambda b, j: (b * n + j, 0))
    per_b = lambda a: pl.BlockSpec((1,) + a.shape[1:], lambda b, j: (b,) + (0,) * (a.ndim - 1))
    full = lambda a: pl.BlockSpec(a.shape, lambda b, j: (0,) * a.ndim)
    ng = lp['dn_norm_g'].reshape(1, DN_HEAD_DIM).astype(F32)
    return pl.pallas_call(
        functools.partial(_deltanet_body, c=c),
        grid=(B, n),
        in_specs=[rows(DN_CONV_DIM), rows(LANES), rows(DN_DIM), per_b(conv_hist), per_b(s0),
                  full(lp['conv_w']), full(apar), full(dtb), full(ng)],
        out_specs=[rows(DN_DIM), per_b(s0), per_b(conv_hist)],
        out_shape=[jax.ShapeDtypeStruct((B * T, DN_DIM), BF16 if c % 16 == 0 else F32),
                   jax.ShapeDtypeStruct(s0.shape, F32),
                   jax.ShapeDtypeStruct(conv_hist.shape, F32)],
        scratch_shapes=[pltpu.VMEM((HIST_ROWS + step, DN_CONV_DIM), F32),
                        pltpu.VMEM((DN_HEADS // 2, 2 * DN_HEAD_DIM, 2 * DN_HEAD_DIM), F32)],
        compiler_params=pltpu.CompilerParams(dimension_semantics=("parallel", "arbitrary"),
                                             vmem_limit_bytes=VMEM_LIMIT),
        name="deltanet",
    )(qkv_raw, small, dg, conv_hist, s0, lp['conv_w'], apar, dtb, ng)


RT_GROUP = 0
RT_EXPERT = N_GROUPS


def pack_router(wg, bg, we, be):
    w = jnp.concatenate([wg, we], axis=1)
    b = jnp.concatenate([bg, be], axis=0)
    pad = LANES - w.shape[1]
    return jnp.pad(w, ((0, 0), (0, pad))).astype(BF16), jnp.pad(b, (0, pad)).reshape(1, LANES).astype(F32)


def _route(r):
    lane = lax.broadcasted_iota(jnp.int32, r.shape, 1).astype(F32)
    big = float(LANES)
    is_grp = lane < N_GROUPS
    lg = jnp.where(is_grp, r, -jnp.inf)
    mg = jnp.max(lg, axis=-1, keepdims=True)
    eg = jnp.where(is_grp, jnp.exp(lg - mg), 0.0)
    grp = jnp.min(jnp.where(lg == mg, lane, big), axis=-1, keepdims=True)
    wg = 1.0 / jnp.sum(eg, axis=-1, keepdims=True)
    first = RT_EXPERT + grp * EXPERTS_PER_GROUP
    in_grp = (lane >= first) & (lane < first + EXPERTS_PER_GROUP)
    le = jnp.where(in_grp, r, -jnp.inf)
    me = jnp.max(le, axis=-1, keepdims=True)
    ee = jnp.where(in_grp, jnp.exp(le - me), 0.0)
    pe = jnp.where(in_grp, ee / jnp.sum(ee, axis=-1, keepdims=True), -1.0)
    v1 = jnp.max(pe, axis=-1, keepdims=True)
    i1 = jnp.min(jnp.where(pe == v1, lane, big), axis=-1, keepdims=True)
    pe2 = jnp.where(lane == i1, -1.0, pe)
    v2 = jnp.max(pe2, axis=-1, keepdims=True)
    i2 = jnp.min(jnp.where(pe2 == v2, lane, big), axis=-1, keepdims=True)
    tot = v1 + v2
    return jnp.where(lane == i1, wg * (v1 / tot), jnp.where(lane == i2, wg * (v2 / tot), 0.0))


def _finish_body(x_ref, on_ref, od_ref, g1_ref, sc_ref, sh_ref, g2_ref, won_ref, wod_ref, ng_ref, wr_ref, br_ref,
                 wgu_ref, wd_ref, fg_ref, o_ref, x1_ref, h_ref, gate_ref, y_ref):
    e = pl.program_id(1)
    tm, d = x_ref.shape
    nb = g1_ref.shape[0]
    per_batch = lambda a: a.reshape(nb, tm // nb, d)

    @pl.when(e == 0)
    def _():
        mix = jnp.dot(on_ref[...].astype(BF16), won_ref[...], preferred_element_type=F32) \
            + jnp.dot(od_ref[...].astype(BF16), wod_ref[...], preferred_element_type=F32)
        x1 = per_batch(x_ref[...]) + g1_ref[...] * per_batch(mix)
        x1 = x1.reshape(tm, d)
        x1_ref[...] = x1
        xn = x1 * lax.rsqrt(jnp.mean(x1 * x1, axis=-1, keepdims=True) + EPS) * ng_ref[...]
        h = (per_batch(xn) * (1 + sc_ref[...]) + sh_ref[...]).reshape(tm, d).astype(BF16)
        h_ref[...] = h
        gate_ref[...] = _route(jnp.dot(h, wr_ref[...], preferred_element_type=F32) + br_ref[...])
        y_ref[...] = jnp.zeros(y_ref.shape, F32)

    h = h_ref[...]
    gu = jnp.dot(h, wgu_ref[0], preferred_element_type=F32)
    he = _silu(gu[:, :D_EXPERT]) * gu[:, D_EXPERT:]
    out = jnp.dot(he.astype(BF16), wd_ref[0], preferred_element_type=F32)
    lane = lax.broadcasted_iota(jnp.int32, gate_ref.shape, 1)
    g_col = jnp.sum(jnp.where(lane == RT_EXPERT + e, gate_ref[...], 0.0), axis=-1, keepdims=True)
    y_ref[...] += g_col * out

    @pl.when(e == pl.num_programs(1) - 1)
    def _():
        x2 = per_batch(x1_ref[...]) + g2_ref[...] * per_batch(y_ref[...])
        x2 = x2.reshape(tm, d)
        o_ref[...] = x2 * lax.rsqrt(jnp.mean(x2 * x2, axis=-1, keepdims=True) + EPS) * fg_ref[...]


def finish_layer(x, o_nsa, o_dn, g1, sc2, sh2, g2, packed, norm_ffn_g, final_g, *, tm=1024):
    B, T, D = x.shape
    n = B * T
    tm = min(tm, n)
    rows = max(tm // T, 1)
    per = max(T // tm, 1)
    row = lambda w: pl.BlockSpec((tm, w), lambda i, e: (i, 0))
    mod = pl.BlockSpec((rows, 1, D), lambda i, e: (i // per, 0, 0))
    full = lambda a: pl.BlockSpec(a.shape, lambda i, e: (0,) * a.ndim)
    expert = lambda a: pl.BlockSpec((1,) + a.shape[1:], lambda i, e: (e, 0, 0))
    vec = lambda v: v.reshape(1, D).astype(F32)
    args = (x.reshape(n, D), o_nsa, o_dn, g1, sc2, sh2, g2, packed['w_out_nsa'], packed['w_out_dn'],
            vec(norm_ffn_g), packed['router_w'], packed['router_b'], packed['w_gate_up'], packed['w_down'],
            vec(final_g))
    specs = [row(D), row(o_nsa.shape[1]), row(o_dn.shape[1]), mod, mod, mod, mod] + \
            [full(a) for a in args[7:12]] + [expert(args[12]), expert(args[13]), full(args[14])]
    return pl.pallas_call(
        _finish_body,
        grid=(n // tm, N_EXPERTS),
        in_specs=specs,
        out_specs=row(D),
        out_shape=jax.ShapeDtypeStruct((n, D), F32),
        scratch_shapes=[pltpu.VMEM((tm, D), F32), pltpu.VMEM((tm, D), BF16), pltpu.VMEM((tm, LANES), F32),
                        pltpu.VMEM((tm, D), F32)],
        compiler_params=pltpu.CompilerParams(dimension_semantics=("parallel", "arbitrary"),
                                             vmem_limit_bytes=VMEM_LIMIT),
        name="finish_layer",
    )(*args).reshape(B, T, D)


def _final_norm_body(x_ref, g_ref, o_ref):
    x = x_ref[...]
    o_ref[...] = x * lax.rsqrt(jnp.mean(x * x, axis=-1, keepdims=True) + EPS) * g_ref[...]


def final_rmsnorm(x, g, *, tm=512):
    shp = x.shape
    x2 = x.reshape(-1, shp[-1])
    n, d = x2.shape
    return pl.pallas_call(
        _final_norm_body,
        grid=(n // tm,),
        in_specs=[pl.BlockSpec((tm, d), lambda i: (i, 0)), pl.BlockSpec((1, d), lambda i: (0, 0))],
        out_specs=pl.BlockSpec((tm, d), lambda i: (i, 0)),
        out_shape=jax.ShapeDtypeStruct(x2.shape, x2.dtype),
        compiler_params=pltpu.CompilerParams(dimension_semantics=("parallel",)),
        name="final_rmsnorm",
    )(x2, g.reshape(1, d)).reshape(shp)


def permute_w_out_nsa(w_out):
    w = w_out[:NSA_Q_DIM].reshape(NSA_KV_HEADS, NSA_GROUP, HEAD_DIM, -1)
    return w.transpose(1, 0, 2, 3).reshape(NSA_Q_DIM, -1).astype(BF16)


def layer_prompt(x, c, lp, rel_bias, packed):
    B, T, _ = x.shape
    sh1, sc1, g1, sh2, sc2, g2 = modulation(c, lp)
    (q, ck, cv, sk, sv, wk, wv, skb, svb, wkb, wvb, qkv_raw, dg, small) = in_projection(
        x, lp['norm_mix_g'], sc1, sh1, packed['w_in'])
    kc, vc = compress_tokens(ck, cv, packed['cmp_k'], packed['cmp_v'], B)
    o_nsa = nsa_prompt(q, kc, vc, skb, svb, wkb, wvb, small, rel_bias, B, T)
    kv = lambda r: r.reshape(B, T, NSA_KV_HEADS, HEAD_DIM)
    ck, cv, sk, sv, wk, wv = kv(ck), kv(cv), kv(sk), kv(sv), kv(wk), kv(wv)
    conv_hist = jnp.zeros((B, CONV_W - 1, DN_CONV_DIM), x.dtype)
    s0 = jnp.zeros((B, DN_HEADS, DN_HEAD_DIM, DN_HEAD_DIM), x.dtype)
    o_dn, s_new, conv_new = deltanet(qkv_raw, small, dg, conv_hist, s0, lp, B, T)
    y = finish_layer(x, o_nsa, o_dn, g1, sc2, sh2, g2, packed, lp['norm_ffn_g'], packed['final_norm_g'])
    w_keep = min(WINDOW, T)
    return y, (ck, cv, sk, sv, wk[:, -w_keep:], wv[:, -w_keep:], s_new, conv_new)


def layer_sample(x, c, cmp_k_pool, cmp_v_pool, sel_k_pool, sel_v_pool, win_k, win_v, s0, conv_hist,
                 page_table, lp, rel_bias, packed):
    B, T, _ = x.shape
    sh1, sc1, g1, sh2, sc2, g2 = modulation(c, lp)
    (q, ck, cv, sk, sv, wk, wv, _, _, _, _, qkv_raw, dg, small) = in_projection(
        x, lp['norm_mix_g'], sc1, sh1, packed['w_in'])
    o_nsa, wk_next, wv_next = nsa_sample(q, sk, sv, wk, wv, small, (cmp_k_pool, cmp_v_pool, sel_k_pool, sel_v_pool),
                                         win_k, win_v, page_table, rel_bias, packed['cmp_k'], packed['cmp_v'], B, T)
    kv = lambda r: r.reshape(B, -1, NSA_KV_HEADS, HEAD_DIM)
    o_dn, s_new, conv_new = deltanet(qkv_raw, small, dg, conv_hist, s0, lp, B, T)
    y = finish_layer(x, o_nsa, o_dn, g1, sc2, sh2, g2, packed, lp['norm_ffn_g'], packed['final_norm_g'])
    return y, (kv(ck), kv(cv), kv(sk), kv(sv), kv(wk_next), kv(wv_next), s_new, conv_new)


def kernel(x_prompt, x_sample, cache_cmp_k, cache_cmp_v, cache_sel_k, cache_sel_v, cache_win_k, cache_win_v,
           state_delta, state_conv, page_table, c_prompt, c_sample, rel_bias, w_ada, b_ada, norm_mix_g,
           norm_ffn_g, w_in, cmp_pos_k, cmp_w1_k, cmp_w2_k, cmp_pos_v, cmp_w1_v, cmp_w2_v, conv_w, dn_a_log,
           dn_dt_bias, dn_norm_g, w_out, router_group_w, router_group_b, router_expert_w, router_expert_b,
           expert_w_gate, expert_w_up, expert_w_down, final_norm_g):
    xp, xs = x_prompt, x_sample
    new_p, new_s = [], []
    for l in range(DEPTH):
        lp = dict(w_ada=w_ada[l], b_ada=b_ada[l], norm_mix_g=norm_mix_g[l], norm_ffn_g=norm_ffn_g[l],
                  w_in=w_in[l], cmp_pos_k=cmp_pos_k[l], cmp_w1_k=cmp_w1_k[l], cmp_w2_k=cmp_w2_k[l],
                  cmp_pos_v=cmp_pos_v[l], cmp_w1_v=cmp_w1_v[l], cmp_w2_v=cmp_w2_v[l], conv_w=conv_w[l],
                  dn_a_log=dn_a_log[l], dn_dt_bias=dn_dt_bias[l], dn_norm_g=dn_norm_g[l], w_out=w_out[l],
                  router_group_w=router_group_w[l], router_group_b=router_group_b[l],
                  router_expert_w=router_expert_w[l], router_expert_b=router_expert_b[l],
                  w_gate=expert_w_gate[l], w_up=expert_w_up[l], w_down=expert_w_down[l])
        packed = dict(w_in=pack_w_in(w_in[l]),
                      cmp_k=pack_compress(cmp_pos_k[l], cmp_w1_k[l], cmp_w2_k[l]),
                      cmp_v=pack_compress(cmp_pos_v[l], cmp_w1_v[l], cmp_w2_v[l]),
                      w_out_nsa=permute_w_out_nsa(w_out[l]),
                      w_out_dn=w_out[l][NSA_Q_DIM:].astype(BF16),
                      w_gate_up=jnp.concatenate([expert_w_gate[l], expert_w_up[l]], axis=-1).astype(BF16),
                      w_down=expert_w_down[l].astype(BF16),
                      final_norm_g=final_norm_g)
        packed['router_w'], packed['router_b'] = pack_router(router_group_w[l], router_group_b[l],
                                                             router_expert_w[l], router_expert_b[l])
        xp, st_p = layer_prompt(xp, c_prompt, lp, rel_bias, packed)
        xs, st_s = layer_sample(xs, c_sample, cache_cmp_k[l], cache_cmp_v[l], cache_sel_k[l], cache_sel_v[l],
                                cache_win_k[l], cache_win_v[l], state_delta[l], state_conv[l], page_table,
                                lp, rel_bias, packed)
        new_p.append(st_p)
        new_s.append(st_s)
    assert DEPTH == 1
    y_prompt, y_sample = xp, xs

    def stk(states, i):
        return jnp.stack([s[i] for s in states])

    return (y_prompt, y_sample,
            stk(new_p, 0), stk(new_p, 1), stk(new_p, 2), stk(new_p, 3),
            stk(new_p, 4), stk(new_p, 5), stk(new_p, 6), stk(new_p, 7),
            stk(new_s, 0), stk(new_s, 1), stk(new_s, 2), stk(new_s, 3),
            stk(new_s, 4), stk(new_s, 5), stk(new_s, 6), stk(new_s, 7))
```

```python
import functools
import math

import jax
import jax.numpy as jnp
import numpy as np
from jax import lax
from jax.experimental import pallas as pl
from jax.experimental.pallas import tpu as pltpu

D_MODEL = 1024
DEPTH = 1
PAGE_SIZE = 128

NSA_HEADS = 8
NSA_KV_HEADS = 2
NSA_GROUP = NSA_HEADS // NSA_KV_HEADS
HEAD_DIM = 64
CMP_LEN = 32
CMP_STRIDE = 16
CMP_HIDDEN = 4 * HEAD_DIM
SEL_BLOCK = 64
SEL_TOP = 16
WINDOW = 512
WIN_Q_BLOCK = 128
SEL_Q_BLOCK = 64
DN_HEADS = 8
DN_HEAD_DIM = 64
CONV_W = 4
DN_CHUNK = 64
NUM_BUCKETS = 32
MAX_DISTANCE = 128
N_GROUPS = 4
EXPERTS_PER_GROUP = 8
N_EXPERTS = N_GROUPS * EXPERTS_PER_GROUP
TOP_K_IN_GROUP = 2
D_EXPERT = D_MODEL // 4

EPS = 1e-6
NEG_INF = -1e30
F32 = jnp.float32
BF16 = jnp.bfloat16

NSA_Q_DIM = NSA_HEADS * HEAD_DIM
NSA_KV_DIM = NSA_KV_HEADS * HEAD_DIM
DN_DIM = DN_HEADS * DN_HEAD_DIM
DN_CONV_DIM = 3 * DN_DIM
MIX_WIDTH = NSA_Q_DIM + DN_DIM
IN_SPLITS = (NSA_Q_DIM, NSA_KV_DIM, NSA_KV_DIM, NSA_KV_DIM, NSA_KV_DIM, NSA_KV_DIM, NSA_KV_DIM,
             3 * NSA_HEADS, DN_CONV_DIM, DN_HEADS, DN_HEADS, DN_DIM)
IN_DIM = sum(IN_SPLITS)

LANES = 128
VMEM_LIMIT = 56 * 1024 * 1024

ATT_TILE = 128
MASKED = -1e30


def rmsnorm(x, g):
    xf = x.astype(F32)
    y = xf * lax.rsqrt(jnp.mean(xf * xf, axis=-1, keepdims=True) + EPS)
    return (y * g.astype(F32)).astype(x.dtype)


def l2norm(x):
    xf = x.astype(F32)
    return (xf * lax.rsqrt(jnp.sum(xf * xf, axis=-1, keepdims=True) + EPS)).astype(x.dtype)


def t5_bucket(rel):
    n = jnp.maximum(rel, 0)
    max_exact = NUM_BUCKETS // 2
    nf = jnp.maximum(n, 1).astype(F32)
    large = max_exact + (jnp.log(nf / max_exact) / math.log(MAX_DISTANCE / max_exact)
                         * (NUM_BUCKETS - max_exact)).astype(jnp.int32)
    large = jnp.minimum(large, NUM_BUCKETS - 1)
    return jnp.where(n < max_exact, n, large)


def rel_bias_heads(rel, rel_bias):
    b = rel_bias.astype(F32)[t5_bucket(rel)]
    b = jnp.moveaxis(b, -1, -3)
    return b.reshape(b.shape[:-3] + (NSA_KV_HEADS, NSA_GROUP) + b.shape[-2:])


def attend(q, k, v, bias, mask):
    s = jnp.einsum('bntkgd,bnlkd->bnkgtl', q, k).astype(F32) * HEAD_DIM ** -0.5 + bias
    m = mask[:, None, None]
    p = jax.nn.softmax(jnp.where(m, s, NEG_INF), axis=-1)
    p = jnp.where(m, p, 0.0)
    o = jnp.einsum('bnkgtl,bnlkd->bntkgd', p.astype(v.dtype), v)
    return o, p


def compress_rows(rows, pos_emb, w1, w2):
    B, L = rows.shape[:2]
    n_cmp = (L - CMP_LEN) // CMP_STRIDE + 1
    idx = jnp.arange(n_cmp)[:, None] * CMP_STRIDE + jnp.arange(CMP_LEN)[None, :]
    blk = rows[:, idx] + pos_emb[:, None, :]
    blk = jnp.swapaxes(blk, 2, 3).reshape(B, n_cmp, NSA_KV_HEADS, CMP_LEN * HEAD_DIM)
    return jax.nn.silu(blk @ w1) @ w2


def nsa_cmp_sel(q, q_pos, kc_rows, vc_rows, ks_rows, vs_rows, lp, rel_bias):
    B, Tq = q.shape[:2]
    L = kc_rows.shape[1]
    kc = compress_rows(kc_rows, lp['cmp_pos_k'], lp['cmp_w1_k'], lp['cmp_w2_k'])
    vc = compress_rows(vc_rows, lp['cmp_pos_v'], lp['cmp_w1_v'], lp['cmp_w2_v'])
    n_cmp = kc.shape[1]
    cmp_end = jnp.arange(n_cmp) * CMP_STRIDE + CMP_LEN - 1
    rel_c = q_pos[:, None] - cmp_end[None, :]
    o_cmp, p_cmp = attend(q[:, None], kc[:, None], vc[:, None],
                          rel_bias_heads(rel_c[None], rel_bias), (rel_c >= 0)[None])
    o_cmp = o_cmp[:, 0]
    n_sel = -(-L // SEL_BLOCK)
    c_start = jnp.arange(n_cmp) * CMP_STRIDE
    s_start = jnp.arange(n_sel) * SEL_BLOCK
    overlap = jnp.clip(jnp.minimum(c_start[:, None] + CMP_LEN, s_start[None] + SEL_BLOCK)
                       - jnp.maximum(c_start[:, None], s_start[None]), 0, None).astype(F32) / CMP_LEN
    imp = jnp.einsum('bkgtc,cs->btks', p_cmp[:, 0], overlap)
    q_blk = q_pos // SEL_BLOCK
    sb = jnp.arange(n_sel)
    forced = (sb[None] == 0) | (sb[None] == q_blk[:, None]) | (sb[None] == q_blk[:, None] - 1)
    avail = s_start[None] <= q_pos[:, None]
    score = jnp.where(forced[None, :, None], 1e9, jnp.where(avail[None, :, None], imp, NEG_INF))
    n_top = min(SEL_TOP, n_sel)
    _, sel_idx = lax.top_k(score, n_top)
    pad = n_sel * SEL_BLOCK - L

    def to_blocks(r):
        r = jnp.pad(r, ((0, 0), (0, pad), (0, 0), (0, 0)))
        return r.reshape(B, n_sel, SEL_BLOCK, NSA_KV_HEADS, HEAD_DIM).transpose(0, 3, 1, 2, 4)
    ks_b = to_blocks(ks_rows)
    vs_b = to_blocks(vs_rows)
    bi = jnp.arange(B)[:, None, None, None]
    ki = jnp.arange(NSA_KV_HEADS)[None, None, :, None]
    tbl = rel_bias.astype(F32).reshape(NUM_BUCKETS, NSA_KV_HEADS, NSA_GROUP).transpose(1, 0, 2)
    qb = SEL_Q_BLOCK if Tq % SEL_Q_BLOCK == 0 else Tq
    nqb = Tq // qb

    def sel_block(args):
        q_i, pos_i, idx_i = args
        kg = ks_b[bi, ki, idx_i]
        vg = vs_b[bi, ki, idx_i]
        k_pos = idx_i[..., None] * SEL_BLOCK + jnp.arange(SEL_BLOCK)
        rel = pos_i[None, :, None, None, None] - k_pos
        bias = tbl[ki[..., None], t5_bucket(rel)]
        s = jnp.einsum('bqkgd,bqknsd->bqkgns', q_i, kg).astype(F32) * HEAD_DIM ** -0.5 + jnp.moveaxis(bias, -1, 3)
        mask = (rel >= 0)[:, :, :, None]
        s = jnp.where(mask, s, NEG_INF).reshape(B, qb, NSA_KV_HEADS, NSA_GROUP, n_top * SEL_BLOCK)
        p = jax.nn.softmax(s, axis=-1)
        p = jnp.where(mask.reshape(B, qb, NSA_KV_HEADS, 1, n_top * SEL_BLOCK), p, 0.0)
        return jnp.einsum('bqkgl,bqkld->bqkgd', p.astype(vg.dtype),
                          vg.reshape(B, qb, NSA_KV_HEADS, n_top * SEL_BLOCK, HEAD_DIM))

    q_m = q.reshape(B, nqb, qb, NSA_KV_HEADS, NSA_GROUP, HEAD_DIM).swapaxes(0, 1)
    pos_m = q_pos.reshape(nqb, qb)
    idx_m = sel_idx.reshape(B, nqb, qb, NSA_KV_HEADS, n_top).swapaxes(0, 1)
    o_sel = lax.map(sel_block, (q_m, pos_m, idx_m))
    o_sel = o_sel.swapaxes(0, 1).reshape(B, Tq, NSA_KV_HEADS, NSA_GROUP, HEAD_DIM)
    return o_cmp, o_sel


def window_sample(q, q_pos, k, v, k_pos, rel_bias):
    rel = q_pos[:, None] - k_pos[None]
    mask = (rel >= 0) & (rel < WINDOW)
    o, _ = attend(q[:, None], k[:, None], v[:, None], rel_bias_heads(rel[None], rel_bias), mask[None])
    return o[:, 0]


def combine_branches(gate_logit, o_cmp, o_sel, o_win):
    B, T = gate_logit.shape[:2]
    g = jax.nn.sigmoid(gate_logit.astype(F32)).reshape(B, T, NSA_KV_HEADS, NSA_GROUP, 3).astype(o_cmp.dtype)
    o = g[..., 0:1] * o_cmp + g[..., 1:2] * o_sel + g[..., 2:3] * o_win
    return o.reshape(B, T, NSA_Q_DIM)


def causal_conv(x_hist, x_new, w):
    xc = jnp.concatenate([x_hist, x_new], axis=1)
    y = lax.conv_general_dilated(xc, w[:, None, :], window_strides=(1,), padding='VALID',
                                 dimension_numbers=('NWC', 'WIO', 'NWC'), feature_group_count=xc.shape[-1])
    return jax.nn.silu(y), xc[:, -(CONV_W - 1):]


def gated_delta_rule(q, k, v, g, beta, s0):
    B, T, H, DK = q.shape
    C = DN_CHUNK if T >= DN_CHUNK else T
    n = -(-T // C)
    pad = n * C - T

    def prep(a):
        a = jnp.pad(a.astype(F32), ((0, 0), (0, pad)) + ((0, 0),) * (a.ndim - 2))
        a = a.reshape((B, n, C) + a.shape[2:])
        return jnp.moveaxis(a, (1, 3), (0, 2))

    qc, kc, vc, gc, bc = prep(q), prep(k), prep(v), prep(g), prep(beta)
    gcum = jnp.cumsum(gc, axis=-1)
    ii = jnp.arange(C)
    incl = ii[:, None] >= ii[None, :]
    strict = ii[:, None] > ii[None, :]
    diff = gcum[..., :, None] - gcum[..., None, :]
    decay = jnp.where(incl, jnp.exp(jnp.where(incl, diff, 0.0)), 0.0)
    kb = kc * bc[..., None]
    a_mat = jnp.where(strict, jnp.einsum('...id,...jd->...ij', kb, kc) * decay, 0.0)
    eye_plus = a_mat + jnp.eye(C, dtype=F32)
    u = lax.linalg.triangular_solve(eye_plus, vc * bc[..., None], left_side=True, lower=True, unit_diagonal=True)
    w = lax.linalg.triangular_solve(eye_plus, kb * jnp.exp(gcum)[..., None], left_side=True, lower=True, unit_diagonal=True)
    qk = jnp.where(incl, jnp.einsum('...id,...jd->...ij', qc, kc) * decay, 0.0)
    qg = qc * jnp.exp(gcum)[..., None]
    kg = kc * jnp.exp(gcum[..., -1:] - gcum)[..., None]
    glast = jnp.exp(gcum[..., -1])

    def step(S, xs):
        u_i, w_i, qk_i, qg_i, kg_i, gl_i = xs
        v_new = u_i - jnp.einsum('bhck,bhkv->bhcv', w_i, S)
        o = jnp.einsum('bhck,bhkv->bhcv', qg_i, S) + jnp.einsum('bhij,bhjv->bhiv', qk_i, v_new)
        S = S * gl_i[..., None, None] + jnp.einsum('bhck,bhcv->bhkv', kg_i, v_new)
        return S, o

    S, o = lax.scan(step, s0.astype(F32), (u, w, qk, qg, kg, glast))
    o = o.transpose(1, 0, 3, 2, 4).reshape(B, n * C, H, v.shape[-1])[:, :T]
    return o, S


def deltanet_mixer(qkv_raw, b_logit, a_logit, gate, conv_hist, s0, lp):
    B, T = qkv_raw.shape[:2]
    qkv, new_hist = causal_conv(conv_hist, qkv_raw, lp['conv_w'])
    q, k, v = jnp.split(qkv, 3, axis=-1)
    q = l2norm(q.reshape(B, T, DN_HEADS, DN_HEAD_DIM)) * DN_HEAD_DIM ** -0.5
    k = l2norm(k.reshape(B, T, DN_HEADS, DN_HEAD_DIM))
    v = v.reshape(B, T, DN_HEADS, DN_HEAD_DIM)
    beta = jax.nn.sigmoid(b_logit.astype(F32))
    g = -jnp.exp(lp['dn_a_log'].astype(F32)) * jax.nn.softplus(a_logit.astype(F32) + lp['dn_dt_bias'].astype(F32))
    o, s_new = gated_delta_rule(q, k, v, g, beta, s0)
    o = rmsnorm(o, lp['dn_norm_g']) * jax.nn.silu(gate.astype(F32).reshape(B, T, DN_HEADS, DN_HEAD_DIM))
    return o.astype(qkv_raw.dtype).reshape(B, T, DN_DIM), s_new.astype(s0.dtype), new_hist


def hier_moe(h, lp):
    B, T, D = h.shape
    x = h.reshape(B * T, D)
    lg = (x @ lp['router_group_w'] + lp['router_group_b']).astype(F32)
    pg = jax.nn.softmax(lg, axis=-1)
    grp = jnp.argmax(lg, axis=-1)
    wg = jnp.take_along_axis(pg, grp[:, None], axis=-1)
    le = (x @ lp['router_expert_w'] + lp['router_expert_b']).astype(F32).reshape(-1, N_GROUPS, EXPERTS_PER_GROUP)
    le = jnp.take_along_axis(le, grp[:, None, None], axis=1)[:, 0]
    pe, ie = lax.top_k(jax.nn.softmax(le, axis=-1), TOP_K_IN_GROUP)
    pe = pe / jnp.sum(pe, axis=-1, keepdims=True)
    eid = grp[:, None] * EXPERTS_PER_GROUP + ie
    gates = jnp.sum(jax.nn.one_hot(eid, N_EXPERTS, dtype=F32) * (wg * pe)[..., None], axis=1).astype(x.dtype)
    y = jnp.zeros_like(x)
    for e in range(N_EXPERTS):
        he = jax.nn.silu(x @ lp['w_gate'][e]) * (x @ lp['w_up'][e])
        y = y + gates[:, e:e + 1] * (he @ lp['w_down'][e])
    return y.reshape(B, T, D)


def modulation(c, lp):
    mod = jax.nn.silu(c) @ lp['w_ada'] + lp['b_ada']
    return jnp.split(mod[:, None, :], 6, axis=-1)


def modulate(x, g, shift, scale):
    return rmsnorm(x, g) * (1 + scale) + shift


def split_projection(h, lp):
    z = h @ lp['w_in']
    return jnp.split(z, np.cumsum(IN_SPLITS)[:-1].tolist(), axis=-1)


def finish_layer(x, o_nsa, o_dn, g1, sh2, sc2, g2, lp):
    x = x + g1 * (jnp.concatenate([o_nsa, o_dn], axis=-1) @ lp['w_out'])
    return x + g2 * hier_moe(modulate(x, lp['norm_ffn_g'], sh2, sc2), lp)


_PK_Q = (0, NSA_Q_DIM)
_PK_KV = (_PK_Q[1], _PK_Q[1] + 6 * NSA_KV_DIM)
_PK_QKV = (_PK_KV[1], _PK_KV[1] + DN_CONV_DIM)
_PK_DG = (_PK_QKV[1], _PK_QKV[1] + DN_DIM)
_PK_SM = (_PK_DG[1], _PK_DG[1] + LANES)
PK_DIM = _PK_SM[1]
SM_GATE = 0
SM_BETA = 3 * NSA_HEADS
SM_DECAY = SM_BETA + DN_HEADS


def pack_w_in(w_in):
    offs = np.cumsum((0,) + IN_SPLITS)
    part = lambda i: w_in[:, offs[i]:offs[i + 1]]
    d = w_in.shape[0]
    q = part(0).reshape(d, NSA_KV_HEADS, NSA_GROUP, HEAD_DIM).transpose(0, 2, 1, 3).reshape(d, NSA_Q_DIM)
    q = q * HEAD_DIM ** -0.5
    small = jnp.concatenate([part(7), part(9), part(10)], axis=1)
    small = jnp.pad(small, ((0, 0), (0, LANES - small.shape[1])))
    cols = [q] + [part(i) for i in range(1, 7)] + [part(8), part(11), small]
    return jnp.concatenate(cols, axis=1).astype(BF16)


def _in_proj_body(x_ref, g_ref, sc_ref, sh_ref, w_ref, q_ref, ck_ref, cv_ref, sk_ref, sv_ref, wk_ref, wv_ref,
                  skb_ref, svb_ref, wkb_ref, wvb_ref, qkv_ref, dg_ref, sm_ref):
    x = x_ref[...]
    tm, d = x.shape
    nb = sc_ref.shape[0]
    xn = x * lax.rsqrt(jnp.mean(x * x, axis=-1, keepdims=True) + EPS) * g_ref[...]
    h = xn.reshape(nb, tm // nb, d) * (1 + sc_ref[...]) + sh_ref[...]
    h = h.reshape(tm, d).astype(BF16)

    def mm(span):
        return jnp.dot(h, w_ref[:, span[0]:span[1]], preferred_element_type=F32)

    q_ref[...] = mm(_PK_Q).astype(BF16)
    kv = mm(_PK_KV)
    for i, r in enumerate((ck_ref, cv_ref, sk_ref, sv_ref, wk_ref, wv_ref)):
        r[...] = kv[:, i * NSA_KV_DIM:(i + 1) * NSA_KV_DIM]
    for i, r in enumerate((skb_ref, svb_ref, wkb_ref, wvb_ref)):
        r[...] = kv[:, (i + 2) * NSA_KV_DIM:(i + 3) * NSA_KV_DIM].astype(BF16)
    qkv_ref[...] = mm(_PK_QKV)
    dg_ref[...] = mm(_PK_DG)
    sm_ref[...] = mm(_PK_SM)


def in_projection(x, norm_g, scale, shift, w_packed, *, tm=256):
    B, T, D = x.shape
    n = B * T
    tm = min(tm, n)
    rows = max(tm // T, 1)
    per = max(T // tm, 1)
    row = lambda i: (i, 0)
    outs = [((n, NSA_Q_DIM), BF16)] + [((n, NSA_KV_DIM), F32)] * 6 + [((n, NSA_KV_DIM), BF16)] * 4 + \
           [((n, DN_CONV_DIM), F32), ((n, DN_DIM), F32), ((n, LANES), F32)]
    return pl.pallas_call(
        _in_proj_body,
        grid=(n // tm,),
        in_specs=[pl.BlockSpec((tm, D), row),
                  pl.BlockSpec((1, D), lambda i: (0, 0)),
                  pl.BlockSpec((rows, 1, D), lambda i: (i // per, 0, 0)),
                  pl.BlockSpec((rows, 1, D), lambda i: (i // per, 0, 0)),
                  pl.BlockSpec((D, PK_DIM), lambda i: (0, 0))],
        out_specs=[pl.BlockSpec((tm, s[1]), row) for s, _ in outs],
        out_shape=[jax.ShapeDtypeStruct(s, dt) for s, dt in outs],
        compiler_params=pltpu.CompilerParams(dimension_semantics=("parallel",), vmem_limit_bytes=VMEM_LIMIT),
        name="in_projection",
    )(x.reshape(n, D), norm_g.reshape(1, D), scale, shift, w_packed)


CHUNK_W = CMP_STRIDE * NSA_KV_DIM
CMP_HALVES = CMP_LEN // CMP_STRIDE


def pack_compress(pos, w1, w2):
    eye = jnp.eye(NSA_KV_HEADS, dtype=F32)
    w1r = w1.reshape(CMP_HALVES, CMP_STRIDE, HEAD_DIM, CMP_HIDDEN)
    w1p = jnp.einsum('ijdc,kl->ijkdlc', w1r, eye).reshape(CMP_HALVES, CHUNK_W, NSA_KV_HEADS * CMP_HIDDEN)
    w2p = jnp.einsum('cd,kl->kcld', w2, eye).reshape(NSA_KV_HEADS * CMP_HIDDEN, NSA_KV_DIM)
    posp = jnp.broadcast_to(pos.reshape(CMP_HALVES, CMP_STRIDE, 1, HEAD_DIM),
                            (CMP_HALVES, CMP_STRIDE, NSA_KV_HEADS, HEAD_DIM)).reshape(CMP_HALVES, CHUNK_W)
    return posp, w1p.astype(BF16), w2p.astype(BF16)


def _compress_one(rows_ref, pos_ref, w1_ref, w2_ref):
    n_chunk = rows_ref.shape[0] // CMP_STRIDE
    x = jnp.concatenate([rows_ref[pl.ds(j, n_chunk, stride=CMP_STRIDE), :] for j in range(CMP_STRIDE)], axis=1)
    hid = None
    for i in range(CMP_HALVES):
        part = jnp.dot((x + pos_ref[i:i + 1]).astype(BF16), w1_ref[i], preferred_element_type=F32)
        if i:
            part = pltpu.roll(part, shift=n_chunk - i, axis=0)
        hid = part if hid is None else hid + part
    hid = hid * jax.nn.sigmoid(hid)
    return jnp.dot(hid.astype(BF16), w2_ref[...], preferred_element_type=F32)


def _compress_body(xk_ref, xv_ref, pk_ref, pv_ref, w1k_ref, w1v_ref, w2k_ref, w2v_ref, kc_ref, vc_ref):
    kc_ref[0] = _compress_one(xk_ref, pk_ref, w1k_ref, w2k_ref).astype(BF16)
    vc_ref[0] = _compress_one(xv_ref, pv_ref, w1v_ref, w2v_ref).astype(BF16)


def compress_tokens(ck, cv, pk, pv, B):
    L = ck.shape[0] // B
    n_chunk = L // CMP_STRIDE
    full = lambda a: pl.BlockSpec(a.shape, lambda b: (0,) * a.ndim)
    rows = pl.BlockSpec((L, NSA_KV_DIM), lambda b: (b, 0))
    tok = pl.BlockSpec((1, n_chunk, NSA_KV_DIM), lambda b: (b, 0, 0))
    return pl.pallas_call(
        _compress_body,
        grid=(B,),
        in_specs=[rows, rows, full(pk[0]), full(pv[0]), full(pk[1]), full(pv[1]), full(pk[2]), full(pv[2])],
        out_specs=[tok, tok],
        out_shape=[jax.ShapeDtypeStruct((B, n_chunk, NSA_KV_DIM), BF16)] * 2,
        compiler_params=pltpu.CompilerParams(dimension_semantics=("parallel",), vmem_limit_bytes=VMEM_LIMIT),
        name="compress_tokens",
    )(ck, cv, pk[0], pv[0], pk[1], pv[1], pk[2], pv[2])


def _t5_bucket_np(rel):
    n = np.maximum(rel, 0)
    max_exact = NUM_BUCKETS // 2
    nf = np.maximum(n, 1).astype(np.float32)
    large = max_exact + (np.log(nf / max_exact) / math.log(MAX_DISTANCE / max_exact)
                         * (NUM_BUCKETS - max_exact)).astype(np.int32)
    return np.where(n < max_exact, n, np.minimum(large, NUM_BUCKETS - 1))


assert (_t5_bucket_np(np.arange(ATT_TILE + 1, 1 << 16)) == NUM_BUCKETS - 1).all()


NB_SAME, NB_PREV, NB_FAR, NB_EDGE, NB_NONE = range(5)
FAR_TILE = 512


def bias_lookup(rel_bias, rel):
    bucket = t5_bucket(rel)
    tbl = rel_bias.astype(F32)
    shape = (tbl.shape[1],) + (1,) * rel.ndim
    out = jnp.zeros((tbl.shape[1],) + rel.shape, F32)
    for b in range(NUM_BUCKETS):
        out = jnp.where(bucket == b, tbl[b].reshape(shape), out)
    return out


def near_bias_tables(rel_bias):
    t = ATT_TILE
    i = jnp.arange(t)[:, None]
    j = jnp.arange(t)[None, :]
    diag = jnp.where(i >= j, bias_lookup(rel_bias, i - j), MASKED)
    prev = bias_lookup(rel_bias, t + i - j)
    far = jnp.broadcast_to(rel_bias.astype(F32)[NUM_BUCKETS - 1][:, None, None], prev.shape)
    edge = jnp.where(j > i, far, MASKED)
    return jnp.stack([diag, prev, far, edge, jnp.full_like(far, MASKED)])


def cmp_bias_table(rel_bias, q0, tq, n_chunk):
    first_end = CMP_LEN - 1
    span = CMP_STRIDE * (n_chunk - 1)
    n_rel = tq + span
    rel = q0 - first_end - span + jnp.arange(n_rel)
    vals = jnp.where(rel >= 0, bias_lookup(rel_bias, rel), MASKED)
    vals = jnp.roll(vals, -span, axis=1)
    flat = jnp.tile(vals, (1, n_chunk))[:, :n_chunk * (n_rel - CMP_STRIDE)]
    by_token = flat.reshape(-1, n_chunk, n_rel - CMP_STRIDE)[:, :, :tq]
    return jnp.swapaxes(by_token, 1, 2)


def overlap_matrix(n_chunk, n_sel):
    c0 = np.arange(n_chunk)[:, None] * CMP_STRIDE
    s0 = np.arange(LANES)[None, :] * SEL_BLOCK
    ov = np.clip(np.minimum(c0 + CMP_LEN, s0 + SEL_BLOCK) - np.maximum(c0, s0), 0, None) / CMP_LEN
    ov = np.where(np.arange(LANES)[None, :] < n_sel, ov, 0.0)
    return jnp.asarray(ov, F32)


def expand_matrix(n_keys):
    e = (np.arange(n_keys)[None, :] // SEL_BLOCK) == np.arange(LANES)[:, None]
    return jnp.asarray(e, BF16)


def _head_queries(q_ref, tq):
    low = lax.broadcasted_iota(jnp.int32, (tq, LANES), 1) < HEAD_DIM
    zero = jnp.zeros((tq, LANES), BF16)
    parts = []
    for kv in range(NSA_KV_HEADS):
        for g in range(NSA_GROUP):
            blk = q_ref[:, g * LANES:(g + 1) * LANES]
            parts.append(jnp.where(low if kv == 0 else ~low, blk, zero))
    return parts


def _stack_queries(q_ref, tq):
    return jnp.concatenate(_head_queries(q_ref, tq), axis=0)


def _dot_nt(a, b):
    return lax.dot_general(a, b, (((1,), (1,)), ((), ())), preferred_element_type=F32)


def _selection_mask(p_group, ov_ref, ex_ref, q0, n_sel, n_top):
    tq = p_group.shape[0]
    n_row = -(-n_sel // 8) * 8
    imp = lax.dot_general(ov_ref[...], p_group, (((0,), (1,)), ((), ())), precision=lax.Precision.HIGHEST,
                          preferred_element_type=F32)[0:n_row]
    blk = lax.broadcasted_iota(jnp.int32, (n_row, tq), 0)
    q_pos = q0 + lax.broadcasted_iota(jnp.int32, (n_row, tq), 1)
    q_blk = q_pos // SEL_BLOCK
    forced = jnp.where(blk == 0, 1.0, jnp.where(blk == q_blk, 1.0, jnp.where(blk == q_blk - 1, 1.0, 0.0)))
    score = jnp.where(blk * SEL_BLOCK <= q_pos, imp, NEG_INF)
    score = jnp.where(forced > 0.5, 1e9, score)
    score = jnp.where(blk < n_sel, score, -3e38)
    rank = jnp.zeros((n_row, tq), F32)
    for s in range(n_sel):
        other = score[s:s + 1, :]
        tie = jnp.where(blk > s, 1.0, 0.0)
        rank = rank + jnp.where(other > score, 1.0, jnp.where(other == score, tie, 0.0))
    sel = jnp.where(rank < n_top, 1.0, 0.0).astype(BF16)
    keys = lax.dot_general(sel, ex_ref[0:n_row, :], (((0,), (0,)), ((), ())), preferred_element_type=F32)
    return (keys - 1.0) * (-MASKED)


def _gate_columns(gl_ref, j):
    g = jax.nn.sigmoid(gl_ref[...])
    cols = [g[:, SM_GATE + 3 * h + j:SM_GATE + 3 * h + j + 1] for h in range(NSA_HEADS)]
    return jnp.stack(cols, axis=0)


def _nsa_prompt_body(q_ref, kc_ref, vc_ref, sk_ref, sv_ref, wk_ref, wv_ref, bc_ref, nb_ref, gl_ref, ov_ref, ex_ref,
                     o_ref, mb_ref, s_ref, sw_ref, stat_ref, mrep_ref, acc_ref, out_ref, *, n_sel, n_top):
    i = pl.program_id(1)
    tq = q_ref.shape[0]
    h = NSA_HEADS
    q = _stack_queries(q_ref, tq)

    s = _dot_nt(q, kc_ref[0]).reshape(h, tq, -1) + bc_ref[...]
    m = jnp.max(s, axis=-1, keepdims=True)
    p = jnp.where(s > 0.5 * MASKED, jnp.exp(s - m), 0.0)
    l = jnp.sum(p, axis=-1, keepdims=True)
    p = p * (1.0 / jnp.where(l > 0.0, l, 1.0))
    o_cmp = jnp.dot(p.reshape(h * tq, -1).astype(BF16), vc_ref[0], preferred_element_type=F32)
    out_ref[...] = _gate_columns(gl_ref, 0) * o_cmp.reshape(h, tq, LANES)

    for kv in range(NSA_KV_HEADS):
        pk = p[kv * NSA_GROUP]
        for g in range(1, NSA_GROUP):
            pk = pk + p[kv * NSA_GROUP + g]
        mb_ref[kv] = _selection_mask(pk, ov_ref, ex_ref, i * tq, n_sel, n_top)

    tk = ATT_TILE
    n_back = WINDOW // tk
    wide = (n_back + 1) * tk
    far_blk = FAR_TILE // tk
    n_tile = sk_ref.shape[0] // tk
    tail_col = s_ref.shape[-1] - wide
    kvg = lambda x: x.reshape(NSA_KV_HEADS, NSA_GROUP, tq, x.shape[-1])
    blocks = lambda x: [x[..., c * LANES:(c + 1) * LANES] for c in range(x.shape[-1] // LANES)]
    lane_max = lambda x: functools.reduce(jnp.maximum, blocks(x))
    lane_sum = lambda x: functools.reduce(jnp.add, blocks(x))
    rep = lambda x, n: jnp.concatenate([x] * n, axis=-1)
    far_bias = nb_ref[NB_FAR][:, 0:1, 0:1]

    n_far = jnp.maximum(i - 1, 0) // far_blk
    tail0 = jnp.minimum(n_far * far_blk, n_tile - (n_back + 1))
    tail_rows = pl.ds(pl.multiple_of(tail0 * tk, tk), wide)
    tiles = []
    for c in range(n_back + 1):
        blk = tail0 + c
        kind = jnp.where((blk < n_far * far_blk) | (blk > i), NB_NONE, jnp.minimum(i - blk, NB_FAR))
        tiles.append(nb_ref[kind])
    s = _dot_nt(q, sk_ref[tail_rows, :]).reshape(h, tq, wide) + jnp.concatenate(tiles, axis=-1)
    s = (kvg(s) + mb_ref[:, :, tail_rows][:, None]).reshape(h, tq, wide)
    s_ref[:, :, tail_col:tail_col + wide] = s
    stat_ref[0] = lane_max(s)

    win0 = jnp.maximum(i - n_back, 0)
    win_rows = pl.ds(pl.multiple_of(win0 * tk, tk), wide)
    tiles = []
    for c in range(n_back + 1):
        r = i - (win0 + c)
        tiles.append(nb_ref[jnp.where(r < 0, NB_NONE, jnp.where(r == n_back, NB_EDGE, jnp.minimum(r, NB_FAR)))])
    s = _dot_nt(q, wk_ref[win_rows, :]).reshape(h, tq, wide) + jnp.concatenate(tiles, axis=-1)
    sw_ref[...] = s
    stat_ref[1] = lane_max(s)

    def far_scores(f, carry):
        rows = pl.ds(pl.multiple_of(f * FAR_TILE, FAR_TILE), FAR_TILE)
        s = _dot_nt(q, sk_ref[rows, :]).reshape(h, tq, FAR_TILE) + far_bias
        s = (kvg(s) + mb_ref[:, :, rows][:, None]).reshape(h, tq, FAR_TILE)
        s_ref[:, :, rows] = s
        stat_ref[0] = jnp.maximum(stat_ref[0], lane_max(s))
        return carry
    lax.fori_loop(0, n_far, far_scores, 0)

    for br in range(2):
        mrep_ref[br] = jnp.broadcast_to(jnp.max(stat_ref[br], axis=-1, keepdims=True), (h, tq, LANES))
        stat_ref[br] = jnp.zeros((h, tq, LANES), F32)
    acc_ref[...] = jnp.zeros(acc_ref.shape, F32)

    def far_values(f, carry):
        rows = pl.ds(pl.multiple_of(f * FAR_TILE, FAR_TILE), FAR_TILE)
        p = jnp.exp(s_ref[:, :, rows] - rep(mrep_ref[0], far_blk))
        stat_ref[0] += lane_sum(p)
        acc_ref[...] += jnp.dot(p.reshape(h * tq, FAR_TILE).astype(BF16), sv_ref[rows, :],
                                preferred_element_type=F32).reshape(h, tq, LANES)
        return carry
    lax.fori_loop(0, n_far, far_values, 0)

    p = jnp.exp(s_ref[:, :, tail_col:tail_col + wide] - rep(mrep_ref[0], n_back + 1))
    l = jnp.sum(stat_ref[0] + lane_sum(p), axis=-1, keepdims=True)
    o_sel = acc_ref[...] + jnp.dot(p.reshape(h * tq, wide).astype(BF16), sv_ref[tail_rows, :],
                                   preferred_element_type=F32).reshape(h, tq, LANES)
    out_ref[...] += _gate_columns(gl_ref, 1) * (o_sel * (1.0 / l))

    p = jnp.exp(sw_ref[...] - rep(mrep_ref[1], n_back + 1))
    l = jnp.sum(lane_sum(p), axis=-1, keepdims=True)
    o_win = jnp.dot(p.reshape(h * tq, wide).astype(BF16), wv_ref[win_rows, :],
                    preferred_element_type=F32).reshape(h, tq, LANES)
    out_ref[...] += _gate_columns(gl_ref, 2) * (o_win * (1.0 / l))

    low = lax.broadcasted_iota(jnp.int32, (tq, LANES), 1) < HEAD_DIM
    for g in range(NSA_GROUP):
        o_ref[:, g * LANES:(g + 1) * LANES] = jnp.where(low, out_ref[g], out_ref[NSA_GROUP + g]).astype(BF16)


def nsa_prompt(q, kc, vc, skb, svb, wkb, wvb, small, rel_bias, B, T):
    tq = ATT_TILE
    nq = T // tq
    assert T % FAR_TILE == 0 and T >= WINDOW + ATT_TILE and WINDOW % ATT_TILE == 0
    wide = WINDOW + ATT_TILE
    far_max = (nq - 2) // (FAR_TILE // ATT_TILE) * FAR_TILE
    n_chunk = kc.shape[1]
    n_sel = -(-T // SEL_BLOCK)
    n_top = min(SEL_TOP, n_sel)
    bias_c = cmp_bias_table(rel_bias, 0, T, n_chunk)
    nb = near_bias_tables(rel_bias)
    ov = overlap_matrix(n_chunk, n_sel)
    ex = expand_matrix(T)
    h = NSA_HEADS
    full = lambda a: pl.BlockSpec(a.shape, lambda b, i: (0,) * a.ndim)
    seq = pl.BlockSpec((T, NSA_KV_DIM), lambda b, i: (b, 0))
    tok = pl.BlockSpec((1, n_chunk, NSA_KV_DIM), lambda b, i: (b, 0, 0))
    return pl.pallas_call(
        functools.partial(_nsa_prompt_body, n_sel=n_sel, n_top=n_top),
        grid=(B, nq),
        in_specs=[pl.BlockSpec((tq, NSA_Q_DIM), lambda b, i: (b * nq + i, 0)),
                  tok, tok, seq, seq, seq, seq,
                  pl.BlockSpec((h, tq, n_chunk), lambda b, i: (0, i, 0)),
                  full(nb),
                  pl.BlockSpec((tq, LANES), lambda b, i: (b * nq + i, 0)),
                  full(ov), full(ex)],
        out_specs=pl.BlockSpec((tq, NSA_Q_DIM), lambda b, i: (b * nq + i, 0)),
        out_shape=jax.ShapeDtypeStruct((B * T, NSA_Q_DIM), BF16),
        scratch_shapes=[pltpu.VMEM((NSA_KV_HEADS, tq, T), F32),
                        pltpu.VMEM((h, tq, far_max + wide), F32), pltpu.VMEM((h, tq, wide), F32),
                        pltpu.VMEM((2, h, tq, LANES), F32), pltpu.VMEM((2, h, tq, LANES), F32),
                        pltpu.VMEM((h, tq, LANES), F32), pltpu.VMEM((h, tq, LANES), F32)],
        compiler_params=pltpu.CompilerParams(dimension_semantics=("parallel", "arbitrary"),
                                             vmem_limit_bytes=VMEM_LIMIT),
        name="nsa_prompt",
    )(q, kc, vc, skb, svb, wkb, wvb, bias_c, nb, small, ov, ex)


def sample_bias_tables(rel_bias, past, T, W):
    t = jnp.arange(T)[:, None]
    pad = jnp.arange(ATT_TILE)[None, :]
    new = jnp.where((pad < T) & (t >= pad), bias_lookup(rel_bias, t - pad), MASKED)
    sel = bias_lookup(rel_bias, past + t - jnp.arange(past)[None, :])
    rel_w = W + t - jnp.arange(W)[None, :]
    win = jnp.where(rel_w < WINDOW, bias_lookup(rel_bias, rel_w), MASKED)
    cmp = cmp_bias_table(rel_bias, past, T, past // CMP_STRIDE)
    return cmp, jnp.concatenate([sel, new], axis=-1), jnp.concatenate([win, new], axis=-1)


def _softmax_rows(s):
    m = jnp.max(s, axis=-1, keepdims=True)
    p = jnp.exp(s - m)
    return p * (1.0 / jnp.sum(p, axis=-1, keepdims=True))


def _nsa_sample_body(pt_ref, q_ref, skn_ref, svn_ref, wkn_ref, wvn_ref, gl_ref, wink_ref, winv_ref,
                     bc_ref, bs_ref, bw_ref, ov_ref, ex_ref, pk_ref, pv_ref, w1k_ref, w1v_ref, w2k_ref, w2v_ref,
                     *rest, n_pages, n_sel, n_top, past):
    pages = [rest[i * n_pages:(i + 1) * n_pages] for i in range(4)]
    o_ref, wko_ref, wvo_ref, s_ref, new_ref, rows_ref = rest[4 * n_pages:]
    h = NSA_HEADS
    tq = q_ref.shape[0]
    tk = ATT_TILE
    q = _stack_queries(q_ref, tq)

    new_ref[...] = jnp.zeros(new_ref.shape, F32)
    for n, r in enumerate((skn_ref, svn_ref, wkn_ref, wvn_ref)):
        new_ref[n, 0:tq, :] = r[...]

    for pg in range(n_pages):
        rows_ref[0, pg * tk:(pg + 1) * tk, :] = pages[0][pg][0].T
        rows_ref[1, pg * tk:(pg + 1) * tk, :] = pages[1][pg][0].T
    kc = _compress_one(rows_ref.at[0], pk_ref, w1k_ref, w2k_ref).astype(BF16)
    vc = _compress_one(rows_ref.at[1], pv_ref, w1v_ref, w2v_ref).astype(BF16)
    s = _dot_nt(q, kc).reshape(h, tq, -1) + bc_ref[...]
    p = _softmax_rows(s)
    o_cmp = jnp.dot(p.reshape(h * tq, -1).astype(BF16), vc, preferred_element_type=F32)
    out = _gate_columns(gl_ref, 0) * o_cmp.reshape(h, tq, LANES)

    q_pos = past + lax.broadcasted_iota(jnp.int32, (tq, LANES), 0)
    masks = []
    for kv in range(NSA_KV_HEADS):
        pkv = p[kv * NSA_GROUP]
        for g in range(1, NSA_GROUP):
            pkv = pkv + p[kv * NSA_GROUP + g]
        masks.append(_selection_mask(pkv, ov_ref, ex_ref, past, n_sel, n_top))
    mask = jnp.stack(masks, axis=0)
    for pg in range(n_pages):
        s_ref[:, pg * tk:(pg + 1) * tk] = jnp.dot(q, pages[2][pg][0].astype(BF16), preferred_element_type=F32)
    s_ref[:, n_pages * tk:(n_pages + 1) * tk] = _dot_nt(q, new_ref[0].astype(BF16))
    width = (n_pages + 1) * tk
    s = s_ref[...].reshape(h, tq, width) + bs_ref[...]
    s = (s.reshape(NSA_KV_HEADS, NSA_GROUP, tq, width) + mask[:, None]).reshape(h, tq, width)
    p = _softmax_rows(s).reshape(h * tq, width).astype(BF16)
    o_sel = jnp.dot(p[:, n_pages * tk:], new_ref[1].astype(BF16), preferred_element_type=F32)
    for pg in range(n_pages):
        o_sel = o_sel + _dot_nt(p[:, pg * tk:(pg + 1) * tk], pages[3][pg][0].astype(BF16))
    out = out + _gate_columns(gl_ref, 1) * o_sel.reshape(h, tq, LANES)

    w = wink_ref.shape[2]
    s_ref[:, 0:w] = jnp.dot(q, wink_ref[0].astype(BF16), preferred_element_type=F32)
    s_ref[:, w:w + tk] = _dot_nt(q, new_ref[2].astype(BF16))
    s = s_ref[:, 0:w + tk].reshape(h, tq, w + tk) + bw_ref[...]
    p = _softmax_rows(s).reshape(h * tq, w + tk).astype(BF16)
    o_win = jnp.dot(p[:, w:], new_ref[3].astype(BF16), preferred_element_type=F32) \
        + _dot_nt(p[:, 0:w], winv_ref[0].astype(BF16))
    out = out + _gate_columns(gl_ref, 2) * o_win.reshape(h, tq, LANES)

    low = lax.broadcasted_iota(jnp.int32, (tq, LANES), 1) < HEAD_DIM
    for g in range(NSA_GROUP):
        o_ref[:, g * LANES:(g + 1) * LANES] = jnp.where(low, out[g], out[NSA_GROUP + g])

    keep = w - tq
    lane_w = lax.broadcasted_iota(jnp.int32, (NSA_KV_DIM, w), 1)
    for old_ref, new_rows_ref, dst_ref in ((wink_ref, wkn_ref, wko_ref), (winv_ref, wvn_ref, wvo_ref)):
        shifted = pltpu.roll(old_ref[0], shift=keep, axis=1)
        tail = jnp.concatenate([jnp.zeros((tk - tq, NSA_KV_DIM), F32), new_rows_ref[...]], axis=0).T
        tail = jnp.concatenate([jnp.zeros((NSA_KV_DIM, w - tk), F32), tail], axis=1)
        dst_ref[0] = jnp.where(lane_w >= keep, tail, shifted)


def nsa_sample(q, skn, svn, wkn, wvn, small, pools, win_k, win_v, page_table, rel_bias, pk, pv, B, T):
    n_pages = page_table.shape[1]
    past = n_pages * PAGE_SIZE
    W = win_k.shape[1]
    L = past + T
    n_cmp = (L - CMP_LEN) // CMP_STRIDE + 1
    assert PAGE_SIZE == ATT_TILE and T % 8 == 0 and T <= SEL_BLOCK and past % SEL_BLOCK == 0
    assert (n_cmp - 1) * CMP_STRIDE + CMP_LEN <= past and n_cmp >= past // CMP_STRIDE - CMP_HALVES + 1
    assert W % ATT_TILE == 0 and W + T >= WINDOW
    n_sel = -(-L // SEL_BLOCK)
    n_top = min(SEL_TOP, n_sel)
    w_keep = min(WINDOW, W + T)
    bc, bs, bw = sample_bias_tables(rel_bias, past, T, W)
    ov = overlap_matrix(past // CMP_STRIDE, n_sel)
    ex = expand_matrix(past + ATT_TILE)
    n_pool = pools[0].shape[0]
    chunks = PAGE_SIZE // CMP_STRIDE
    assert W == w_keep
    feat_major = lambda a: a.transpose(0, 2, 3, 1).reshape(a.shape[0], NSA_KV_DIM, a.shape[1])
    cmp_pools = [feat_major(p) for p in pools[:2]]
    sel_pools = [feat_major(p) for p in pools[2:]]
    wk3 = feat_major(win_k)
    wv3 = feat_major(win_v)

    full = lambda a: pl.BlockSpec(a.shape, lambda b, pt: (0,) * a.ndim)
    row = lambda w: pl.BlockSpec((T, w), lambda b, pt: (b, 0))
    seq = lambda n: pl.BlockSpec((1, NSA_KV_DIM, n), lambda b, pt: (b, 0, 0))
    page = lambda shape, pg: pl.BlockSpec((1,) + shape, lambda b, pt: (pt[b, pg], 0, 0))
    page_specs, page_args = [], []
    for pool in cmp_pools + sel_pools:
        for pg in range(n_pages):
            page_specs.append(page((NSA_KV_DIM, PAGE_SIZE), pg))
            page_args.append(pool)
    tables = (bc, bs, bw, ov, ex, pk[0], pv[0], pk[1], pv[1], pk[2], pv[2])
    grid_spec = pltpu.PrefetchScalarGridSpec(
        num_scalar_prefetch=1,
        grid=(B,),
        in_specs=[row(NSA_Q_DIM), row(NSA_KV_DIM), row(NSA_KV_DIM), row(NSA_KV_DIM), row(NSA_KV_DIM), row(LANES),
                  seq(W), seq(W)] + [full(a) for a in tables] + page_specs,
        out_specs=[row(NSA_Q_DIM), seq(w_keep), seq(w_keep)],
        scratch_shapes=[pltpu.VMEM((NSA_HEADS * T, past + ATT_TILE), F32),
                        pltpu.VMEM((4, ATT_TILE, NSA_KV_DIM), F32),
                        pltpu.VMEM((2, past, NSA_KV_DIM), F32)],
    )
    o, wk_next, wv_next = pl.pallas_call(
        functools.partial(_nsa_sample_body, n_pages=n_pages, n_sel=n_sel, n_top=n_top, past=past),
        grid_spec=grid_spec,
        out_shape=[jax.ShapeDtypeStruct((B * T, NSA_Q_DIM), F32),
                   jax.ShapeDtypeStruct((B, NSA_KV_DIM, w_keep), F32),
                   jax.ShapeDtypeStruct((B, NSA_KV_DIM, w_keep), F32)],
        compiler_params=pltpu.CompilerParams(dimension_semantics=("arbitrary",), vmem_limit_bytes=VMEM_LIMIT),
        name="nsa_sample",
    )(page_table, q, skn, svn, wkn, wvn, small, wk3, wv3, *tables, *page_args)
    row_major = lambda a: a.reshape(B, NSA_KV_HEADS, HEAD_DIM, w_keep).transpose(0, 3, 1, 2)
    return o, row_major(wk_next), row_major(wv_next)


HIST_ROWS = 8
DN_CHUNKS_PER_STEP = 4


def _split_bf16(x):
    hi = x.astype(BF16)
    lo = (x - hi.astype(F32)).astype(BF16)
    return hi, lo


def _mm_hi(a, b):
    ah, al = _split_bf16(a)
    bh, bl = _split_bf16(b)
    d = lambda x, y: jnp.dot(x, y, preferred_element_type=F32)
    return d(ah, bh) + (d(ah, bl) + d(al, bh))


def _bmm(a, b):
    return jnp.einsum('hij,hjk->hik', a, b, preferred_element_type=F32)


def _bmm_nt(a, b):
    return jnp.einsum('hid,hjd->hij', a, b, preferred_element_type=F32)


def _bmm_hi(a, b):
    ah, al = _split_bf16(a)
    bh, bl = _split_bf16(b)
    return _bmm(ah, bh) + (_bmm(ah, bl) + _bmm(al, bh))


def _cumsum_rows(tri, x):
    hi, lo = _split_bf16(x)
    lo2 = (x - hi.astype(F32) - lo.astype(F32)).astype(BF16)
    d = lambda y: jnp.dot(tri, y, preferred_element_type=F32)
    return d(hi) + (d(lo) + d(lo2))


def _softplus(x):
    return jnp.maximum(x, 0.0) + jnp.log1p(jnp.exp(-jnp.abs(x)))


def _silu(x):
    return x * jax.nn.sigmoid(x)


def _deltanet_body(x_ref, sm_ref, dg_ref, hist_ref, s0_ref, cw_ref, apar_ref, dt_ref, ng_ref,
                   o_ref, s_out_ref, hist_out_ref, xbuf_ref, s_ref, *, c):
    j = pl.program_id(1)
    rows = x_ref.shape[0]
    n_grp = rows // c
    hd = DN_HEAD_DIM
    n_hist = CONV_W - 1

    n_pair = DN_HEADS // 2
    zero_blk = jnp.zeros((hd, hd), F32)

    @pl.when(j == 0)
    def _():
        xbuf_ref[HIST_ROWS - n_hist:HIST_ROWS, :] = hist_ref[0]
        for p in range(n_pair):
            s_ref[p] = jnp.concatenate([jnp.concatenate([s0_ref[0, 2 * p], zero_blk], axis=1),
                                        jnp.concatenate([zero_blk, s0_ref[0, 2 * p + 1]], axis=1)], axis=0)

    xbuf_ref[HIST_ROWS:HIST_ROWS + rows, :] = x_ref[...]
    y = None
    for t in range(CONV_W):
        lo = HIST_ROWS - n_hist + t
        term = xbuf_ref[lo:lo + rows, :] * cw_ref[t:t + 1, :]
        y = term if y is None else y + term
    y = _silu(y)
    tail = xbuf_ref[HIST_ROWS + rows - n_hist:HIST_ROWS + rows, :]
    xbuf_ref[HIST_ROWS - n_hist:HIST_ROWS, :] = tail

    sm = sm_ref[...]
    beta_all = jax.nn.sigmoid(sm)
    g_all = apar_ref[...] * _softplus(sm + dt_ref[...])
    row = lax.broadcasted_iota(jnp.int32, (rows, rows), 0)
    col = lax.broadcasted_iota(jnp.int32, (rows, rows), 1)
    tri = jnp.where((row >= col) & (row // c == col // c), 1.0, 0.0).astype(BF16)
    gcum_all = _cumsum_rows(tri, g_all)
    gcum_t = gcum_all.T

    wd, wj = 2 * hd, 2 * c
    first_d = lax.broadcasted_iota(jnp.int32, (1, wd), 1) < hd
    first_j = lax.broadcasted_iota(jnp.int32, (1, wj), 1) < c
    rj = lax.broadcasted_iota(jnp.int32, (c, wj), 0)
    cj = lax.broadcasted_iota(jnp.int32, (c, wj), 1)
    cj = jnp.where(cj < c, cj, cj - c)
    incl, strict = rj >= cj, rj > cj
    eye = jnp.where(rj == cj, 1.0, 0.0)
    same_head = (lax.broadcasted_iota(jnp.int32, (wd, wd), 0) < hd) == (lax.broadcasted_iota(jnp.int32, (wd, wd), 1) < hd)
    ones_bd = jnp.where(same_head, 1.0, 0.0).astype(BF16)
    first_rows = lax.broadcasted_iota(jnp.int32, (wd, 1), 0) < hd

    items = [(g, p) for g in range(n_grp) for p in range(n_pair)]

    def block_diag(x, first):
        zero = jnp.zeros_like(x)
        return jnp.concatenate([jnp.where(first, x, zero), jnp.where(first, zero, x)], axis=1)

    def head_sums(x):
        n = x.shape[0]
        hi, lo = _split_bf16(x.reshape(n * c, wd))
        tot = jnp.dot(hi, ones_bd, preferred_element_type=F32) + jnp.dot(lo, ones_bd, preferred_element_type=F32)
        return tot.reshape(n, c, wd)

    def mm_hi(a_parts, b_parts):
        return _bmm(a_parts[0], b_parts[0]) + (_bmm(a_parts[0], b_parts[1]) + _bmm(a_parts[1], b_parts[0]))

    def per_head(cols, base, first):
        return jnp.stack([jnp.where(first, cols[g * c:(g + 1) * c, base + 2 * p:base + 2 * p + 1],
                                    cols[g * c:(g + 1) * c, base + 2 * p + 1:base + 2 * p + 2])
                          for g, p in items], axis=0)

    lanes = lambda x, off: jnp.stack([x[g * c:(g + 1) * c, off + p * wd:off + (p + 1) * wd] for g, p in items], axis=0)
    q = lanes(y, 0)
    k = lanes(y, DN_DIM)
    v = lanes(y, 2 * DN_DIM)
    q = q * lax.rsqrt(head_sums(q * q) + EPS) * hd ** -0.5
    k = k * lax.rsqrt(head_sums(k * k) + EPS)
    beta = per_head(beta_all, SM_BETA, first_d)
    gc_d = per_head(gcum_all, SM_DECAY, first_d)
    gc_j = per_head(gcum_all, SM_DECAY, first_j)
    gr_j = jnp.stack([jnp.concatenate([gcum_t[SM_DECAY + 2 * p:SM_DECAY + 2 * p + 1, g * c:(g + 1) * c],
                                       gcum_t[SM_DECAY + 2 * p + 1:SM_DECAY + 2 * p + 2, g * c:(g + 1) * c]], axis=1)
                      for g, p in items], axis=0)
    gl_d = gc_d[:, c - 1:c, :]
    decay = jnp.where(incl, jnp.exp(jnp.where(incl, gc_j - gr_j, 0.0)), 0.0)
    eg = jnp.exp(gc_d)
    kb = k * beta
    k_bd = block_diag(k.astype(BF16), first_d)
    a = jnp.where(strict, _bmm_nt(kb.astype(BF16), k_bd) * decay, 0.0)
    t_inv = eye - a
    split_bd = lambda parts, first: tuple(block_diag(x, first) for x in parts)
    pow_parts = _split_bf16(a)
    pow_bd = split_bd(pow_parts, first_j)
    for _ in range(int(math.log2(c)) - 1):
        pow_parts = _split_bf16(mm_hi(pow_parts, pow_bd))
        pow_bd = split_bd(pow_parts, first_j)
        t_inv = t_inv + mm_hi(_split_bf16(t_inv), pow_bd)
    vb_parts = split_bd(_split_bf16(v * beta), first_d)
    kbg_parts = split_bd(_split_bf16(kb * eg), first_d)
    rhs = tuple(jnp.concatenate([x, z], axis=2) for x, z in zip(vb_parts, kbg_parts))
    uw = mm_hi(_split_bf16(t_inv), rhs)
    u, w = uw[:, :, :wd], uw[:, :, wd:]
    qk = jnp.where(incl, _bmm_nt(q.astype(BF16), k_bd) * decay, 0.0).astype(BF16)
    wq = jnp.concatenate([w, q * eg], axis=1).astype(BF16)
    kg = (k * jnp.exp(gl_d - gc_d)).astype(BF16)
    gl_rows = jnp.where(first_rows, jnp.exp(gl_d[:, :, 0:1]), jnp.exp(gl_d[:, :, hd:hd + 1]))

    s = s_ref[...]
    outs = []
    for g in range(n_grp):
        sl = slice(g * n_pair, (g + 1) * n_pair)
        ws = _bmm(wq[sl], s.astype(BF16))
        vn = (u[sl] - ws[:, :c]).astype(BF16)
        outs.append(ws[:, c:] + _bmm(qk[sl], block_diag(vn, first_d)))
        kv_outer = jnp.einsum('pck,pcv->pkv', kg[sl], vn, preferred_element_type=F32)
        s = s * gl_rows[sl] + jnp.where(same_head, kv_outer, 0.0)
    s_ref[...] = s
    o = jnp.concatenate(outs, axis=0)
    ng = jnp.concatenate([ng_ref[...], ng_ref[...]], axis=1)
    o = o * lax.rsqrt(head_sums(o * o) * (1.0 / hd) + EPS) * ng
    for n, (g, p) in enumerate(items):
        gate = _silu(dg_ref[g * c:(g + 1) * c, p * wd:(p + 1) * wd])
        o_ref[g * c:(g + 1) * c, p * wd:(p + 1) * wd] = (o[n] * gate).astype(o_ref.dtype)

    @pl.when(j == pl.num_programs(1) - 1)
    def _():
        for p in range(n_pair):
            s_pair = s_ref[p]
            s_out_ref[0, 2 * p] = s_pair[0:hd, 0:hd]
            s_out_ref[0, 2 * p + 1] = s_pair[hd:2 * hd, hd:2 * hd]
        hist_out_ref[0] = tail


def deltanet(qkv_raw, small, dg, conv_hist, s0, lp, B, T):
    c = DN_CHUNK if T >= DN_CHUNK else T
    assert T % c == 0 and c % 8 == 0 and c >= CONV_W - 1 and (c & (c - 1)) == 0
    n_grp = math.gcd(T // c, DN_CHUNKS_PER_STEP)
    step = n_grp * c
    n = T // step
    put = lambda vals, base: jnp.zeros((LANES,), F32).at[base + jnp.arange(DN_HEADS)].set(vals).reshape(1, LANES)
    apar = put(-jnp.exp(lp['dn_a_log'].astype(F32)), SM_DECAY)
    dtb = put(lp['dn_dt_bias'].astype(F32), SM_DECAY)
    rows = lambda w: pl.BlockSpec((step, w), lambda b, j: (b * n + j, 0))
    per_b = lambda a: pl.BlockSpec((1,) + a.shape[1:], lambda b, j: (b,) + (0,) * (a.ndim - 1))
    full = lambda a: pl.BlockSpec(a.shape, lambda b, j: (0,) * a.ndim)
    ng = lp['dn_norm_g'].reshape(1, DN_HEAD_DIM).astype(F32)
    return pl.pallas_call(
        functools.partial(_deltanet_body, c=c),
        grid=(B, n),
        in_specs=[rows(DN_CONV_DIM), rows(LANES), rows(DN_DIM), per_b(conv_hist), per_b(s0),
                  full(lp['conv_w']), full(apar), full(dtb), full(ng)],
        out_specs=[rows(DN_DIM), per_b(s0), per_b(conv_hist)],
        out_shape=[jax.ShapeDtypeStruct((B * T, DN_DIM), BF16 if c % 16 == 0 else F32),
                   jax.ShapeDtypeStruct(s0.shape, F32),
                   jax.ShapeDtypeStruct(conv_hist.shape, F32)],
        scratch_shapes=[pltpu.VMEM((HIST_ROWS + step, DN_CONV_DIM), F32),
                        pltpu.VMEM((DN_HEADS // 2, 2 * DN_HEAD_DIM, 2 * DN_HEAD_DIM), F32)],
        compiler_params=pltpu.CompilerParams(dimension_semantics=("parallel", "arbitrary"),
                                             vmem_limit_bytes=VMEM_LIMIT),
        name="deltanet",
    )(qkv_raw, small, dg, conv_hist, s0, lp['conv_w'], apar, dtb, ng)


RT_GROUP = 0
RT_EXPERT = N_GROUPS


def pack_router(wg, bg, we, be):
    w = jnp.concatenate([wg, we], axis=1)
    b = jnp.concatenate([bg, be], axis=0)
    pad = LANES - w.shape[1]
    return jnp.pad(w, ((0, 0), (0, pad))).astype(BF16), jnp.pad(b, (0, pad)).reshape(1, LANES).astype(F32)


def _route(r):
    lane = lax.broadcasted_iota(jnp.int32, r.shape, 1).astype(F32)
    big = float(LANES)
    is_grp = lane < N_GROUPS
    lg = jnp.where(is_grp, r, -jnp.inf)
    mg = jnp.max(lg, axis=-1, keepdims=True)
    eg = jnp.where(is_grp, jnp.exp(lg - mg), 0.0)
    grp = jnp.min(jnp.where(lg == mg, lane, big), axis=-1, keepdims=True)
    wg = 1.0 / jnp.sum(eg, axis=-1, keepdims=True)
    first = RT_EXPERT + grp * EXPERTS_PER_GROUP
    in_grp = (lane >= first) & (lane < first + EXPERTS_PER_GROUP)
    le = jnp.where(in_grp, r, -jnp.inf)
    me = jnp.max(le, axis=-1, keepdims=True)
    ee = jnp.where(in_grp, jnp.exp(le - me), 0.0)
    pe = jnp.where(in_grp, ee / jnp.sum(ee, axis=-1, keepdims=True), -1.0)
    v1 = jnp.max(pe, axis=-1, keepdims=True)
    i1 = jnp.min(jnp.where(pe == v1, lane, big), axis=-1, keepdims=True)
    pe2 = jnp.where(lane == i1, -1.0, pe)
    v2 = jnp.max(pe2, axis=-1, keepdims=True)
    i2 = jnp.min(jnp.where(pe2 == v2, lane, big), axis=-1, keepdims=True)
    tot = v1 + v2
    return jnp.where(lane == i1, wg * (v1 / tot), jnp.where(lane == i2, wg * (v2 / tot), 0.0))


def _finish_body(x_ref, on_ref, od_ref, g1_ref, sc_ref, sh_ref, g2_ref, won_ref, wod_ref, ng_ref, wr_ref, br_ref,
                 wgu_ref, wd_ref, fg_ref, o_ref, x1_ref, h_ref, gate_ref, y_ref):
    e = pl.program_id(1)
    tm, d = x_ref.shape
    nb = g1_ref.shape[0]
    per_batch = lambda a: a.reshape(nb, tm // nb, d)

    @pl.when(e == 0)
    def _():
        mix = jnp.dot(on_ref[...].astype(BF16), won_ref[...], preferred_element_type=F32) \
            + jnp.dot(od_ref[...].astype(BF16), wod_ref[...], preferred_element_type=F32)
        x1 = per_batch(x_ref[...]) + g1_ref[...] * per_batch(mix)
        x1 = x1.reshape(tm, d)
        x1_ref[...] = x1
        xn = x1 * lax.rsqrt(jnp.mean(x1 * x1, axis=-1, keepdims=True) + EPS) * ng_ref[...]
        h = (per_batch(xn) * (1 + sc_ref[...]) + sh_ref[...]).reshape(tm, d).astype(BF16)
        h_ref[...] = h
        gate_ref[...] = _route(jnp.dot(h, wr_ref[...], preferred_element_type=F32) + br_ref[...])
        y_ref[...] = jnp.zeros(y_ref.shape, F32)

    h = h_ref[...]
    gu = jnp.dot(h, wgu_ref[0], preferred_element_type=F32)
    he = _silu(gu[:, :D_EXPERT]) * gu[:, D_EXPERT:]
    out = jnp.dot(he.astype(BF16), wd_ref[0], preferred_element_type=F32)
    lane = lax.broadcasted_iota(jnp.int32, gate_ref.shape, 1)
    g_col = jnp.sum(jnp.where(lane == RT_EXPERT + e, gate_ref[...], 0.0), axis=-1, keepdims=True)
    y_ref[...] += g_col * out

    @pl.when(e == pl.num_programs(1) - 1)
    def _():
        x2 = per_batch(x1_ref[...]) + g2_ref[...] * per_batch(y_ref[...])
        x2 = x2.reshape(tm, d)
        o_ref[...] = x2 * lax.rsqrt(jnp.mean(x2 * x2, axis=-1, keepdims=True) + EPS) * fg_ref[...]


def finish_layer(x, o_nsa, o_dn, g1, sc2, sh2, g2, packed, norm_ffn_g, final_g, *, tm=1024):
    B, T, D = x.shape
    n = B * T
    tm = min(tm, n)
    rows = max(tm // T, 1)
    per = max(T // tm, 1)
    row = lambda w: pl.BlockSpec((tm, w), lambda i, e: (i, 0))
    mod = pl.BlockSpec((rows, 1, D), lambda i, e: (i // per, 0, 0))
    full = lambda a: pl.BlockSpec(a.shape, lambda i, e: (0,) * a.ndim)
    expert = lambda a: pl.BlockSpec((1,) + a.shape[1:], lambda i, e: (e, 0, 0))
    vec = lambda v: v.reshape(1, D).astype(F32)
    args = (x.reshape(n, D), o_nsa, o_dn, g1, sc2, sh2, g2, packed['w_out_nsa'], packed['w_out_dn'],
            vec(norm_ffn_g), packed['router_w'], packed['router_b'], packed['w_gate_up'], packed['w_down'],
            vec(final_g))
    specs = [row(D), row(o_nsa.shape[1]), row(o_dn.shape[1]), mod, mod, mod, mod] + \
            [full(a) for a in args[7:12]] + [expert(args[12]), expert(args[13]), full(args[14])]
    return pl.pallas_call(
        _finish_body,
        grid=(n // tm, N_EXPERTS),
        in_specs=specs,
        out_specs=row(D),
        out_shape=jax.ShapeDtypeStruct((n, D), F32),
        scratch_shapes=[pltpu.VMEM((tm, D), F32), pltpu.VMEM((tm, D), BF16), pltpu.VMEM((tm, LANES), F32),
                        pltpu.VMEM((tm, D), F32)],
        compiler_params=pltpu.CompilerParams(dimension_semantics=("parallel", "arbitrary"),
                                             vmem_limit_bytes=VMEM_LIMIT),
        name="finish_layer",
    )(*args).reshape(B, T, D)


def _final_norm_body(x_ref, g_ref, o_ref):
    x = x_ref[...]
    o_ref[...] = x * lax.rsqrt(jnp.mean(x * x, axis=-1, keepdims=True) + EPS) * g_ref[...]


def final_rmsnorm(x, g, *, tm=512):
    shp = x.shape
    x2 = x.reshape(-1, shp[-1])
    n, d = x2.shape
    return pl.pallas_call(
        _final_norm_body,
        grid=(n // tm,),
        in_specs=[pl.BlockSpec((tm, d), lambda i: (i, 0)), pl.BlockSpec((1, d), lambda i: (0, 0))],
        out_specs=pl.BlockSpec((tm, d), lambda i: (i, 0)),
        out_shape=jax.ShapeDtypeStruct(x2.shape, x2.dtype),
        compiler_params=pltpu.CompilerParams(dimension_semantics=("parallel",)),
        name="final_rmsnorm",
    )(x2, g.reshape(1, d)).reshape(shp)


def permute_w_out_nsa(w_out):
    w = w_out[:NSA_Q_DIM].reshape(NSA_KV_HEADS, NSA_GROUP, HEAD_DIM, -1)
    return w.transpose(1, 0, 2, 3).reshape(NSA_Q_DIM, -1).astype(BF16)


def layer_prompt(x, c, lp, rel_bias, packed):
    B, T, _ = x.shape
    sh1, sc1, g1, sh2, sc2, g2 = modulation(c, lp)
    (q, ck, cv, sk, sv, wk, wv, skb, svb, wkb, wvb, qkv_raw, dg, small) = in_projection(
        x, lp['norm_mix_g'], sc1, sh1, packed['w_in'])
    kc, vc = compress_tokens(ck, cv, packed['cmp_k'], packed['cmp_v'], B)
    o_nsa = nsa_prompt(q, kc, vc, skb, svb, wkb, wvb, small, rel_bias, B, T)
    kv = lambda r: r.reshape(B, T, NSA_KV_HEADS, HEAD_DIM)
    ck, cv, sk, sv, wk, wv = kv(ck), kv(cv), kv(sk), kv(sv), kv(wk), kv(wv)
    conv_hist = jnp.zeros((B, CONV_W - 1, DN_CONV_DIM), x.dtype)
    s0 = jnp.zeros((B, DN_HEADS, DN_HEAD_DIM, DN_HEAD_DIM), x.dtype)
    o_dn, s_new, conv_new = deltanet(qkv_raw, small, dg, conv_hist, s0, lp, B, T)
    y = finish_layer(x, o_nsa, o_dn, g1, sc2, sh2, g2, packed, lp['norm_ffn_g'], packed['final_norm_g'])
    w_keep = min(WINDOW, T)
    return y, (ck, cv, sk, sv, wk[:, -w_keep:], wv[:, -w_keep:], s_new, conv_new)


def layer_sample(x, c, cmp_k_pool, cmp_v_pool, sel_k_pool, sel_v_pool, win_k, win_v, s0, conv_hist,
                 page_table, lp, rel_bias, packed):
    B, T, _ = x.shape
    sh1, sc1, g1, sh2, sc2, g2 = modulation(c, lp)
    (q, ck, cv, sk, sv, wk, wv, _, _, _, _, qkv_raw, dg, small) = in_projection(
        x, lp['norm_mix_g'], sc1, sh1, packed['w_in'])
    o_nsa, wk_next, wv_next = nsa_sample(q, sk, sv, wk, wv, small, (cmp_k_pool, cmp_v_pool, sel_k_pool, sel_v_pool),
                                         win_k, win_v, page_table, rel_bias, packed['cmp_k'], packed['cmp_v'], B, T)
    kv = lambda r: r.reshape(B, -1, NSA_KV_HEADS, HEAD_DIM)
    o_dn, s_new, conv_new = deltanet(qkv_raw, small, dg, conv_hist, s0, lp, B, T)
    y = finish_layer(x, o_nsa, o_dn, g1, sc2, sh2, g2, packed, lp['norm_ffn_g'], packed['final_norm_g'])
    return y, (kv(ck), kv(cv), kv(sk), kv(sv), kv(wk_next), kv(wv_next), s_new, conv_new)


def kernel(x_prompt, x_sample, cache_cmp_k, cache_cmp_v, cache_sel_k, cache_sel_v, cache_win_k, cache_win_v,
           state_delta, state_conv, page_table, c_prompt, c_sample, rel_bias, w_ada, b_ada, norm_mix_g,
           norm_ffn_g, w_in, cmp_pos_k, cmp_w1_k, cmp_w2_k, cmp_pos_v, cmp_w1_v, cmp_w2_v, conv_w, dn_a_log,
           dn_dt_bias, dn_norm_g, w_out, router_group_w, router_group_b, router_expert_w, router_expert_b,
           expert_w_gate, expert_w_up, expert_w_down, final_norm_g):
    xp, xs = x_prompt, x_sample
    new_p, new_s = [], []
    for l in range(DEPTH):
        lp = dict(w_ada=w_ada[l], b_ada=b_ada[l], norm_mix_g=norm_mix_g[l], norm_ffn_g=norm_ffn_g[l],
                  w_in=w_in[l], cmp_pos_k=cmp_pos_k[l], cmp_w1_k=cmp_w1_k[l], cmp_w2_k=cmp_w2_k[l],
                  cmp_pos_v=cmp_pos_v[l], cmp_w1_v=cmp_w1_v[l], cmp_w2_v=cmp_w2_v[l], conv_w=conv_w[l],
                  dn_a_log=dn_a_log[l], dn_dt_bias=dn_dt_bias[l], dn_norm_g=dn_norm_g[l], w_out=w_out[l],
                  router_group_w=router_group_w[l], router_group_b=router_group_b[l],
                  router_expert_w=router_expert_w[l], router_expert_b=router_expert_b[l],
                  w_gate=expert_w_gate[l], w_up=expert_w_up[l], w_down=expert_w_down[l])
        packed = dict(w_in=pack_w_in(w_in[l]),
                      cmp_k=pack_compress(cmp_pos_k[l], cmp_w1_k[l], cmp_w2_k[l]),
                      cmp_v=pack_compress(cmp_pos_v[l], cmp_w1_v[l], cmp_w2_v[l]),
                      w_out_nsa=permute_w_out_nsa(w_out[l]),
                      w_out_dn=w_out[l][NSA_Q_DIM:].astype(BF16),
                      w_gate_up=jnp.concatenate([expert_w_gate[l], expert_w_up[l]], axis=-1).astype(BF16),
                      w_down=expert_w_down[l].astype(BF16),
                      final_norm_g=final_norm_g)
        packed['router_w'], packed['router_b'] = pack_router(router_group_w[l], router_group_b[l],
                                                             router_expert_w[l], router_expert_b[l])
        xp, st_p = layer_prompt(xp, c_prompt, lp, rel_bias, packed)
        xs, st_s = layer_sample(xs, c_sample, cache_cmp_k[l], cache_cmp_v[l], cache_sel_k[l], cache_sel_v[l],
                                cache_win_k[l], cache_win_v[l], state_delta[l], state_conv[l], page_table,
                                lp, rel_bias, packed)
        new_p.append(st_p)
        new_s.append(st_s)
    assert DEPTH == 1
    y_prompt, y_sample = xp, xs

    def stk(states, i):
        return jnp.stack([s[i] for s in states])

    return (y_prompt, y_sample,
            stk(new_p, 0), stk(new_p, 1), stk(new_p, 2), stk(new_p, 3),
            stk(new_p, 4), stk(new_p, 5), stk(new_p, 6), stk(new_p, 7),
            stk(new_s, 0), stk(new_s, 1), stk(new_s, 2), stk(new_s, 3),
            stk(new_s, 4), stk(new_s, 5), stk(new_s, 6), stk(new_s, 7))
```

```python
import functools
import math

import jax
import jax.numpy as jnp
import numpy as np
from jax import lax
from jax.experimental import pallas as pl
from jax.experimental.pallas import tpu as pltpu

D_MODEL = 1024
DEPTH = 1
PAGE_SIZE = 128

NSA_HEADS = 8
NSA_KV_HEADS = 2
NSA_GROUP = NSA_HEADS // NSA_KV_HEADS
HEAD_DIM = 64
CMP_LEN = 32
CMP_STRIDE = 16
CMP_HIDDEN = 4 * HEAD_DIM
SEL_BLOCK = 64
SEL_TOP = 16
WINDOW = 512
WIN_Q_BLOCK = 128
SEL_Q_BLOCK = 64
DN_HEADS = 8
DN_HEAD_DIM = 64
CONV_W = 4
DN_CHUNK = 64
NUM_BUCKETS = 32
MAX_DISTANCE = 128
N_GROUPS = 4
EXPERTS_PER_GROUP = 8
N_EXPERTS = N_GROUPS * EXPERTS_PER_GROUP
TOP_K_IN_GROUP = 2
D_EXPERT = D_MODEL // 4

EPS = 1e-6
NEG_INF = -1e30
F32 = jnp.float32
BF16 = jnp.bfloat16

NSA_Q_DIM = NSA_HEADS * HEAD_DIM
NSA_KV_DIM = NSA_KV_HEADS * HEAD_DIM
DN_DIM = DN_HEADS * DN_HEAD_DIM
DN_CONV_DIM = 3 * DN_DIM
MIX_WIDTH = NSA_Q_DIM + DN_DIM
IN_SPLITS = (NSA_Q_DIM, NSA_KV_DIM, NSA_KV_DIM, NSA_KV_DIM, NSA_KV_DIM, NSA_KV_DIM, NSA_KV_DIM,
             3 * NSA_HEADS, DN_CONV_DIM, DN_HEADS, DN_HEADS, DN_DIM)
IN_DIM = sum(IN_SPLITS)

LANES = 128
VMEM_LIMIT = 56 * 1024 * 1024

ATT_TILE = 128
MASKED = -1e30


def rmsnorm(x, g):
    xf = x.astype(F32)
    y = xf * lax.rsqrt(jnp.mean(xf * xf, axis=-1, keepdims=True) + EPS)
    return (y * g.astype(F32)).astype(x.dtype)


def l2norm(x):
    xf = x.astype(F32)
    return (xf * lax.rsqrt(jnp.sum(xf * xf, axis=-1, keepdims=True) + EPS)).astype(x.dtype)


def t5_bucket(rel):
    n = jnp.maximum(rel, 0)
    max_exact = NUM_BUCKETS // 2
    nf = jnp.maximum(n, 1).astype(F32)
    large = max_exact + (jnp.log(nf / max_exact) / math.log(MAX_DISTANCE / max_exact)
                         * (NUM_BUCKETS - max_exact)).astype(jnp.int32)
    large = jnp.minimum(large, NUM_BUCKETS - 1)
    return jnp.where(n < max_exact, n, large)


def rel_bias_heads(rel, rel_bias):
    b = rel_bias.astype(F32)[t5_bucket(rel)]
    b = jnp.moveaxis(b, -1, -3)
    return b.reshape(b.shape[:-3] + (NSA_KV_HEADS, NSA_GROUP) + b.shape[-2:])


def attend(q, k, v, bias, mask):
    s = jnp.einsum('bntkgd,bnlkd->bnkgtl', q, k).astype(F32) * HEAD_DIM ** -0.5 + bias
    m = mask[:, None, None]
    p = jax.nn.softmax(jnp.where(m, s, NEG_INF), axis=-1)
    p = jnp.where(m, p, 0.0)
    o = jnp.einsum('bnkgtl,bnlkd->bntkgd', p.astype(v.dtype), v)
    return o, p


def compress_rows(rows, pos_emb, w1, w2):
    B, L = rows.shape[:2]
    n_cmp = (L - CMP_LEN) // CMP_STRIDE + 1
    idx = jnp.arange(n_cmp)[:, None] * CMP_STRIDE + jnp.arange(CMP_LEN)[None, :]
    blk = rows[:, idx] + pos_emb[:, None, :]
    blk = jnp.swapaxes(blk, 2, 3).reshape(B, n_cmp, NSA_KV_HEADS, CMP_LEN * HEAD_DIM)
    return jax.nn.silu(blk @ w1) @ w2


def nsa_cmp_sel(q, q_pos, kc_rows, vc_rows, ks_rows, vs_rows, lp, rel_bias):
    B, Tq = q.shape[:2]
    L = kc_rows.shape[1]
    kc = compress_rows(kc_rows, lp['cmp_pos_k'], lp['cmp_w1_k'], lp['cmp_w2_k'])
    vc = compress_rows(vc_rows, lp['cmp_pos_v'], lp['cmp_w1_v'], lp['cmp_w2_v'])
    n_cmp = kc.shape[1]
    cmp_end = jnp.arange(n_cmp) * CMP_STRIDE + CMP_LEN - 1
    rel_c = q_pos[:, None] - cmp_end[None, :]
    o_cmp, p_cmp = attend(q[:, None], kc[:, None], vc[:, None],
                          rel_bias_heads(rel_c[None], rel_bias), (rel_c >= 0)[None])
    o_cmp = o_cmp[:, 0]
    n_sel = -(-L // SEL_BLOCK)
    c_start = jnp.arange(n_cmp) * CMP_STRIDE
    s_start = jnp.arange(n_sel) * SEL_BLOCK
    overlap = jnp.clip(jnp.minimum(c_start[:, None] + CMP_LEN, s_start[None] + SEL_BLOCK)
                       - jnp.maximum(c_start[:, None], s_start[None]), 0, None).astype(F32) / CMP_LEN
    imp = jnp.einsum('bkgtc,cs->btks', p_cmp[:, 0], overlap)
    q_blk = q_pos // SEL_BLOCK
    sb = jnp.arange(n_sel)
    forced = (sb[None] == 0) | (sb[None] == q_blk[:, None]) | (sb[None] == q_blk[:, None] - 1)
    avail = s_start[None] <= q_pos[:, None]
    score = jnp.where(forced[None, :, None], 1e9, jnp.where(avail[None, :, None], imp, NEG_INF))
    n_top = min(SEL_TOP, n_sel)
    _, sel_idx = lax.top_k(score, n_top)
    pad = n_sel * SEL_BLOCK - L

    def to_blocks(r):
        r = jnp.pad(r, ((0, 0), (0, pad), (0, 0), (0, 0)))
        return r.reshape(B, n_sel, SEL_BLOCK, NSA_KV_HEADS, HEAD_DIM).transpose(0, 3, 1, 2, 4)
    ks_b = to_blocks(ks_rows)
    vs_b = to_blocks(vs_rows)
    bi = jnp.arange(B)[:, None, None, None]
    ki = jnp.arange(NSA_KV_HEADS)[None, None, :, None]
    tbl = rel_bias.astype(F32).reshape(NUM_BUCKETS, NSA_KV_HEADS, NSA_GROUP).transpose(1, 0, 2)
    qb = SEL_Q_BLOCK if Tq % SEL_Q_BLOCK == 0 else Tq
    nqb = Tq // qb

    def sel_block(args):
        q_i, pos_i, idx_i = args
        kg = ks_b[bi, ki, idx_i]
        vg = vs_b[bi, ki, idx_i]
        k_pos = idx_i[..., None] * SEL_BLOCK + jnp.arange(SEL_BLOCK)
        rel = pos_i[None, :, None, None, None] - k_pos
        bias = tbl[ki[..., None], t5_bucket(rel)]
        s = jnp.einsum('bqkgd,bqknsd->bqkgns', q_i, kg).astype(F32) * HEAD_DIM ** -0.5 + jnp.moveaxis(bias, -1, 3)
        mask = (rel >= 0)[:, :, :, None]
        s = jnp.where(mask, s, NEG_INF).reshape(B, qb, NSA_KV_HEADS, NSA_GROUP, n_top * SEL_BLOCK)
        p = jax.nn.softmax(s, axis=-1)
        p = jnp.where(mask.reshape(B, qb, NSA_KV_HEADS, 1, n_top * SEL_BLOCK), p, 0.0)
        return jnp.einsum('bqkgl,bqkld->bqkgd', p.astype(vg.dtype),
                          vg.reshape(B, qb, NSA_KV_HEADS, n_top * SEL_BLOCK, HEAD_DIM))

    q_m = q.reshape(B, nqb, qb, NSA_KV_HEADS, NSA_GROUP, HEAD_DIM).swapaxes(0, 1)
    pos_m = q_pos.reshape(nqb, qb)
    idx_m = sel_idx.reshape(B, nqb, qb, NSA_KV_HEADS, n_top).swapaxes(0, 1)
    o_sel = lax.map(sel_block, (q_m, pos_m, idx_m))
    o_sel = o_sel.swapaxes(0, 1).reshape(B, Tq, NSA_KV_HEADS, NSA_GROUP, HEAD_DIM)
    return o_cmp, o_sel


def window_sample(q, q_pos, k, v, k_pos, rel_bias):
    rel = q_pos[:, None] - k_pos[None]
    mask = (rel >= 0) & (rel < WINDOW)
    o, _ = attend(q[:, None], k[:, None], v[:, None], rel_bias_heads(rel[None], rel_bias), mask[None])
    return o[:, 0]


def combine_branches(gate_logit, o_cmp, o_sel, o_win):
    B, T = gate_logit.shape[:2]
    g = jax.nn.sigmoid(gate_logit.astype(F32)).reshape(B, T, NSA_KV_HEADS, NSA_GROUP, 3).astype(o_cmp.dtype)
    o = g[..., 0:1] * o_cmp + g[..., 1:2] * o_sel + g[..., 2:3] * o_win
    return o.reshape(B, T, NSA_Q_DIM)


def causal_conv(x_hist, x_new, w):
    xc = jnp.concatenate([x_hist, x_new], axis=1)
    y = lax.conv_general_dilated(xc, w[:, None, :], window_strides=(1,), padding='VALID',
                                 dimension_numbers=('NWC', 'WIO', 'NWC'), feature_group_count=xc.shape[-1])
    return jax.nn.silu(y), xc[:, -(CONV_W - 1):]


def gated_delta_rule(q, k, v, g, beta, s0):
    B, T, H, DK = q.shape
    C = DN_CHUNK if T >= DN_CHUNK else T
    n = -(-T // C)
    pad = n * C - T

    def prep(a):
        a = jnp.pad(a.astype(F32), ((0, 0), (0, pad)) + ((0, 0),) * (a.ndim - 2))
        a = a.reshape((B, n, C) + a.shape[2:])
        return jnp.moveaxis(a, (1, 3), (0, 2))

    qc, kc, vc, gc, bc = prep(q), prep(k), prep(v), prep(g), prep(beta)
    gcum = jnp.cumsum(gc, axis=-1)
    ii = jnp.arange(C)
    incl = ii[:, None] >= ii[None, :]
    strict = ii[:, None] > ii[None, :]
    diff = gcum[..., :, None] - gcum[..., None, :]
    decay = jnp.where(incl, jnp.exp(jnp.where(incl, diff, 0.0)), 0.0)
    kb = kc * bc[..., None]
    a_mat = jnp.where(strict, jnp.einsum('...id,...jd->...ij', kb, kc) * decay, 0.0)
    eye_plus = a_mat + jnp.eye(C, dtype=F32)
    u = lax.linalg.triangular_solve(eye_plus, vc * bc[..., None], left_side=True, lower=True, unit_diagonal=True)
    w = lax.linalg.triangular_solve(eye_plus, kb * jnp.exp(gcum)[..., None], left_side=True, lower=True, unit_diagonal=True)
    qk = jnp.where(incl, jnp.einsum('...id,...jd->...ij', qc, kc) * decay, 0.0)
    qg = qc * jnp.exp(gcum)[..., None]
    kg = kc * jnp.exp(gcum[..., -1:] - gcum)[..., None]
    glast = jnp.exp(gcum[..., -1])

    def step(S, xs):
        u_i, w_i, qk_i, qg_i, kg_i, gl_i = xs
        v_new = u_i - jnp.einsum('bhck,bhkv->bhcv', w_i, S)
        o = jnp.einsum('bhck,bhkv->bhcv', qg_i, S) + jnp.einsum('bhij,bhjv->bhiv', qk_i, v_new)
        S = S * gl_i[..., None, None] + jnp.einsum('bhck,bhcv->bhkv', kg_i, v_new)
        return S, o

    S, o = lax.scan(step, s0.astype(F32), (u, w, qk, qg, kg, glast))
    o = o.transpose(1, 0, 3, 2, 4).reshape(B, n * C, H, v.shape[-1])[:, :T]
    return o, S


def deltanet_mixer(qkv_raw, b_logit, a_logit, gate, conv_hist, s0, lp):
    B, T = qkv_raw.shape[:2]
    qkv, new_hist = causal_conv(conv_hist, qkv_raw, lp['conv_w'])
    q, k, v = jnp.split(qkv, 3, axis=-1)
    q = l2norm(q.reshape(B, T, DN_HEADS, DN_HEAD_DIM)) * DN_HEAD_DIM ** -0.5
    k = l2norm(k.reshape(B, T, DN_HEADS, DN_HEAD_DIM))
    v = v.reshape(B, T, DN_HEADS, DN_HEAD_DIM)
    beta = jax.nn.sigmoid(b_logit.astype(F32))
    g = -jnp.exp(lp['dn_a_log'].astype(F32)) * jax.nn.softplus(a_logit.astype(F32) + lp['dn_dt_bias'].astype(F32))
    o, s_new = gated_delta_rule(q, k, v, g, beta, s0)
    o = rmsnorm(o, lp['dn_norm_g']) * jax.nn.silu(gate.astype(F32).reshape(B, T, DN_HEADS, DN_HEAD_DIM))
    return o.astype(qkv_raw.dtype).reshape(B, T, DN_DIM), s_new.astype(s0.dtype), new_hist


def hier_moe(h, lp):
    B, T, D = h.shape
    x = h.reshape(B * T, D)
    lg = (x @ lp['router_group_w'] + lp['router_group_b']).astype(F32)
    pg = jax.nn.softmax(lg, axis=-1)
    grp = jnp.argmax(lg, axis=-1)
    wg = jnp.take_along_axis(pg, grp[:, None], axis=-1)
    le = (x @ lp['router_expert_w'] + lp['router_expert_b']).astype(F32).reshape(-1, N_GROUPS, EXPERTS_PER_GROUP)
    le = jnp.take_along_axis(le, grp[:, None, None], axis=1)[:, 0]
    pe, ie = lax.top_k(jax.nn.softmax(le, axis=-1), TOP_K_IN_GROUP)
    pe = pe / jnp.sum(pe, axis=-1, keepdims=True)
    eid = grp[:, None] * EXPERTS_PER_GROUP + ie
    gates = jnp.sum(jax.nn.one_hot(eid, N_EXPERTS, dtype=F32) * (wg * pe)[..., None], axis=1).astype(x.dtype)
    y = jnp.zeros_like(x)
    for e in range(N_EXPERTS):
        he = jax.nn.silu(x @ lp['w_gate'][e]) * (x @ lp['w_up'][e])
        y = y + gates[:, e:e + 1] * (he @ lp['w_down'][e])
    return y.reshape(B, T, D)


def modulation(c, lp):
    mod = jax.nn.silu(c) @ lp['w_ada'] + lp['b_ada']
    return jnp.split(mod[:, None, :], 6, axis=-1)


def modulate(x, g, shift, scale):
    return rmsnorm(x, g) * (1 + scale) + shift


def split_projection(h, lp):
    z = h @ lp['w_in']
    return jnp.split(z, np.cumsum(IN_SPLITS)[:-1].tolist(), axis=-1)


def finish_layer(x, o_nsa, o_dn, g1, sh2, sc2, g2, lp):
    x = x + g1 * (jnp.concatenate([o_nsa, o_dn], axis=-1) @ lp['w_out'])
    return x + g2 * hier_moe(modulate(x, lp['norm_ffn_g'], sh2, sc2), lp)


_PK_Q = (0, NSA_Q_DIM)
_PK_KV = (_PK_Q[1], _PK_Q[1] + 6 * NSA_KV_DIM)
_PK_QKV = (_PK_KV[1], _PK_KV[1] + DN_CONV_DIM)
_PK_DG = (_PK_QKV[1], _PK_QKV[1] + DN_DIM)
_PK_SM = (_PK_DG[1], _PK_DG[1] + LANES)
PK_DIM = _PK_SM[1]
SM_GATE = 0
SM_BETA = 3 * NSA_HEADS
SM_DECAY = SM_BETA + DN_HEADS


def pack_w_in(w_in):
    offs = np.cumsum((0,) + IN_SPLITS)
    part = lambda i: w_in[:, offs[i]:offs[i + 1]]
    d = w_in.shape[0]
    q = part(0).reshape(d, NSA_KV_HEADS, NSA_GROUP, HEAD_DIM).transpose(0, 2, 1, 3).reshape(d, NSA_Q_DIM)
    q = q * HEAD_DIM ** -0.5
    small = jnp.concatenate([part(7), part(9), part(10)], axis=1)
    small = jnp.pad(small, ((0, 0), (0, LANES - small.shape[1])))
    cols = [q] + [part(i) for i in range(1, 7)] + [part(8), part(11), small]
    return jnp.concatenate(cols, axis=1).astype(BF16)


def _in_proj_body(x_ref, g_ref, sc_ref, sh_ref, w_ref, q_ref, ck_ref, cv_ref, sk_ref, sv_ref, wk_ref, wv_ref,
                  skb_ref, svb_ref, wkb_ref, wvb_ref, qkv_ref, dg_ref, sm_ref, *row_refs):
    x = x_ref[...]
    tm, d = x.shape
    nb = sc_ref.shape[0]
    xn = x * lax.rsqrt(jnp.mean(x * x, axis=-1, keepdims=True) + EPS) * g_ref[...]
    h = xn.reshape(nb, tm // nb, d) * (1 + sc_ref[...]) + sh_ref[...]
    h = h.reshape(tm, d).astype(BF16)

    def mm(span):
        return jnp.dot(h, w_ref[:, span[0]:span[1]], preferred_element_type=F32)

    q_ref[...] = mm(_PK_Q).astype(BF16)
    kv = mm(_PK_KV)
    for i, r in enumerate((ck_ref, cv_ref, sk_ref, sv_ref, wk_ref, wv_ref)):
        rows = kv[:, i * NSA_KV_DIM:(i + 1) * NSA_KV_DIM]
        if row_refs:
            r[0] = rows.T
        else:
            r[...] = rows
    for i, r in enumerate(row_refs):
        r[...] = kv[:, i * NSA_KV_DIM:(i + 1) * NSA_KV_DIM]
    for i, r in enumerate((skb_ref, svb_ref, wkb_ref, wvb_ref)):
        r[...] = kv[:, (i + 2) * NSA_KV_DIM:(i + 3) * NSA_KV_DIM].astype(BF16)
    qkv_ref[...] = mm(_PK_QKV)
    dg_ref[...] = mm(_PK_DG)
    sm_ref[...] = mm(_PK_SM)


def in_projection(x, norm_g, scale, shift, w_packed, *, tm=256, feature_major=False):
    B, T, D = x.shape
    n = B * T
    tm = min(tm, n)
    rows = max(tm // T, 1)
    per = max(T // tm, 1)
    row = lambda i: (i, 0)
    by_row = lambda w, dt: (((n, w), dt), pl.BlockSpec((tm, w), row))
    if feature_major:
        assert tm <= T and tm % LANES == 0
        kv_out = (((B, NSA_KV_DIM, T), F32), pl.BlockSpec((1, NSA_KV_DIM, tm), lambda i: (i // per, 0, i % per)))
    else:
        kv_out = by_row(NSA_KV_DIM, F32)
    outs = [by_row(NSA_Q_DIM, BF16)] + [kv_out] * 6 + [by_row(NSA_KV_DIM, BF16)] * 4 + \
           [by_row(DN_CONV_DIM, F32), by_row(DN_DIM, F32), by_row(LANES, F32)] + \
           ([by_row(NSA_KV_DIM, F32)] * 2 if feature_major else [])
    return pl.pallas_call(
        _in_proj_body,
        grid=(n // tm,),
        in_specs=[pl.BlockSpec((tm, D), row),
                  pl.BlockSpec((1, D), lambda i: (0, 0)),
                  pl.BlockSpec((rows, 1, D), lambda i: (i // per, 0, 0)),
                  pl.BlockSpec((rows, 1, D), lambda i: (i // per, 0, 0)),
                  pl.BlockSpec((D, PK_DIM), lambda i: (0, 0))],
        out_specs=[spec for _, spec in outs],
        out_shape=[jax.ShapeDtypeStruct(s, dt) for (s, dt), _ in outs],
        compiler_params=pltpu.CompilerParams(dimension_semantics=("parallel",), vmem_limit_bytes=VMEM_LIMIT),
        name="in_projection",
    )(x.reshape(n, D), norm_g.reshape(1, D), scale, shift, w_packed)


CHUNK_W = CMP_STRIDE * NSA_KV_DIM
CMP_HALVES = CMP_LEN // CMP_STRIDE


def pack_compress(pos, w1, w2):
    eye = jnp.eye(NSA_KV_HEADS, dtype=F32)
    w1r = w1.reshape(CMP_HALVES, CMP_STRIDE, HEAD_DIM, CMP_HIDDEN)
    w1p = jnp.einsum('ijdc,kl->ijkdlc', w1r, eye).reshape(CMP_HALVES, CHUNK_W, NSA_KV_HEADS * CMP_HIDDEN)
    w2p = jnp.einsum('cd,kl->kcld', w2, eye).reshape(NSA_KV_HEADS * CMP_HIDDEN, NSA_KV_DIM)
    posp = jnp.broadcast_to(pos.reshape(CMP_HALVES, CMP_STRIDE, 1, HEAD_DIM),
                            (CMP_HALVES, CMP_STRIDE, NSA_KV_HEADS, HEAD_DIM)).reshape(CMP_HALVES, CHUNK_W)
    return posp, w1p.astype(BF16), w2p.astype(BF16)


def _compress_one(rows_ref, pos_ref, w1_ref, w2_ref):
    n_chunk = rows_ref.shape[0] // CMP_STRIDE
    x = jnp.concatenate([rows_ref[pl.ds(j, n_chunk, stride=CMP_STRIDE), :] for j in range(CMP_STRIDE)], axis=1)
    hid = None
    for i in range(CMP_HALVES):
        part = jnp.dot((x + pos_ref[i:i + 1]).astype(BF16), w1_ref[i], preferred_element_type=F32)
        if i:
            part = pltpu.roll(part, shift=n_chunk - i, axis=0)
        hid = part if hid is None else hid + part
    hid = hid * jax.nn.sigmoid(hid)
    return jnp.dot(hid.astype(BF16), w2_ref[...], preferred_element_type=F32)


def _compress_body(xk_ref, xv_ref, pk_ref, pv_ref, w1k_ref, w1v_ref, w2k_ref, w2v_ref, kc_ref, vc_ref):
    kc_ref[0] = _compress_one(xk_ref, pk_ref, w1k_ref, w2k_ref).astype(BF16)
    vc_ref[0] = _compress_one(xv_ref, pv_ref, w1v_ref, w2v_ref).astype(BF16)


def compress_tokens(ck, cv, pk, pv, B):
    L = ck.shape[0] // B
    n_chunk = L // CMP_STRIDE
    full = lambda a: pl.BlockSpec(a.shape, lambda b: (0,) * a.ndim)
    rows = pl.BlockSpec((L, NSA_KV_DIM), lambda b: (b, 0))
    tok = pl.BlockSpec((1, n_chunk, NSA_KV_DIM), lambda b: (b, 0, 0))
    return pl.pallas_call(
        _compress_body,
        grid=(B,),
        in_specs=[rows, rows, full(pk[0]), full(pv[0]), full(pk[1]), full(pv[1]), full(pk[2]), full(pv[2])],
        out_specs=[tok, tok],
        out_shape=[jax.ShapeDtypeStruct((B, n_chunk, NSA_KV_DIM), BF16)] * 2,
        compiler_params=pltpu.CompilerParams(dimension_semantics=("parallel",), vmem_limit_bytes=VMEM_LIMIT),
        name="compress_tokens",
    )(ck, cv, pk[0], pv[0], pk[1], pv[1], pk[2], pv[2])


def _t5_bucket_np(rel):
    n = np.maximum(rel, 0)
    max_exact = NUM_BUCKETS // 2
    nf = np.maximum(n, 1).astype(np.float32)
    large = max_exact + (np.log(nf / max_exact) / math.log(MAX_DISTANCE / max_exact)
                         * (NUM_BUCKETS - max_exact)).astype(np.int32)
    return np.where(n < max_exact, n, np.minimum(large, NUM_BUCKETS - 1))


assert (_t5_bucket_np(np.arange(ATT_TILE + 1, 1 << 16)) == NUM_BUCKETS - 1).all()


NB_SAME, NB_PREV, NB_FAR, NB_EDGE, NB_NONE = range(5)
FAR_TILE = 512


def bias_lookup(rel_bias, rel):
    bucket = t5_bucket(rel)
    tbl = rel_bias.astype(F32)
    shape = (tbl.shape[1],) + (1,) * rel.ndim
    out = jnp.zeros((tbl.shape[1],) + rel.shape, F32)
    for b in range(NUM_BUCKETS):
        out = jnp.where(bucket == b, tbl[b].reshape(shape), out)
    return out


def near_bias_tables(rel_bias):
    t = ATT_TILE
    i = jnp.arange(t)[:, None]
    j = jnp.arange(t)[None, :]
    diag = jnp.where(i >= j, bias_lookup(rel_bias, i - j), MASKED)
    prev = bias_lookup(rel_bias, t + i - j)
    far = jnp.broadcast_to(rel_bias.astype(F32)[NUM_BUCKETS - 1][:, None, None], prev.shape)
    edge = jnp.where(j > i, far, MASKED)
    return jnp.stack([diag, prev, far, edge, jnp.full_like(far, MASKED)])


def cmp_bias_table(rel_bias, q0, tq, n_chunk):
    first_end = CMP_LEN - 1
    span = CMP_STRIDE * (n_chunk - 1)
    n_rel = tq + span
    rel = q0 - first_end - span + jnp.arange(n_rel)
    vals = jnp.where(rel >= 0, bias_lookup(rel_bias, rel), MASKED)
    vals = jnp.roll(vals, -span, axis=1)
    flat = jnp.tile(vals, (1, n_chunk))[:, :n_chunk * (n_rel - CMP_STRIDE)]
    by_token = flat.reshape(-1, n_chunk, n_rel - CMP_STRIDE)[:, :, :tq]
    return jnp.swapaxes(by_token, 1, 2)


def overlap_matrix(n_chunk, n_sel):
    c0 = np.arange(n_chunk)[:, None] * CMP_STRIDE
    s0 = np.arange(LANES)[None, :] * SEL_BLOCK
    ov = np.clip(np.minimum(c0 + CMP_LEN, s0 + SEL_BLOCK) - np.maximum(c0, s0), 0, None) / CMP_LEN
    ov = np.where(np.arange(LANES)[None, :] < n_sel, ov, 0.0)
    return jnp.asarray(ov, F32)


def expand_matrix(n_keys):
    e = (np.arange(n_keys)[None, :] // SEL_BLOCK) == np.arange(LANES)[:, None]
    return jnp.asarray(e, BF16)


def _head_queries(q_ref, tq):
    low = lax.broadcasted_iota(jnp.int32, (tq, LANES), 1) < HEAD_DIM
    zero = jnp.zeros((tq, LANES), BF16)
    parts = []
    for kv in range(NSA_KV_HEADS):
        for g in range(NSA_GROUP):
            blk = q_ref[:, g * LANES:(g + 1) * LANES]
            parts.append(jnp.where(low if kv == 0 else ~low, blk, zero))
    return parts


def _stack_queries(q_ref, tq):
    return jnp.concatenate(_head_queries(q_ref, tq), axis=0)


def _dot_nt(a, b):
    return lax.dot_general(a, b, (((1,), (1,)), ((), ())), preferred_element_type=F32)


def _selection_mask(p_group, ov_ref, ex_ref, q0, n_sel, n_top):
    tq = p_group.shape[0]
    n_row = -(-n_sel // 8) * 8
    imp = lax.dot_general(ov_ref[...], p_group, (((0,), (1,)), ((), ())), precision=lax.Precision.HIGHEST,
                          preferred_element_type=F32)[0:n_row]
    blk = lax.broadcasted_iota(jnp.int32, (n_row, tq), 0)
    q_pos = q0 + lax.broadcasted_iota(jnp.int32, (n_row, tq), 1)
    q_blk = q_pos // SEL_BLOCK
    forced = jnp.where(blk == 0, 1.0, jnp.where(blk == q_blk, 1.0, jnp.where(blk == q_blk - 1, 1.0, 0.0)))
    score = jnp.where(blk * SEL_BLOCK <= q_pos, imp, NEG_INF)
    score = jnp.where(forced > 0.5, 1e9, score)
    score = jnp.where(blk < n_sel, score, -3e38)
    rank = jnp.zeros((n_row, tq), F32)
    for s in range(n_sel):
        other = score[s:s + 1, :]
        tie = jnp.where(blk > s, 1.0, 0.0)
        rank = rank + jnp.where(other > score, 1.0, jnp.where(other == score, tie, 0.0))
    sel = jnp.where(rank < n_top, 1.0, 0.0).astype(BF16)
    keys = lax.dot_general(sel, ex_ref[0:n_row, :], (((0,), (0,)), ((), ())), preferred_element_type=F32)
    return (keys - 1.0) * (-MASKED)


def _gate_columns(gl_ref, j):
    g = jax.nn.sigmoid(gl_ref[...])
    cols = [g[:, SM_GATE + 3 * h + j:SM_GATE + 3 * h + j + 1] for h in range(NSA_HEADS)]
    return jnp.stack(cols, axis=0)


def _nsa_prompt_body(q_ref, kc_ref, vc_ref, sk_ref, sv_ref, wk_ref, wv_ref, bc_ref, nb_ref, gl_ref, ov_ref, ex_ref,
                     o_ref, mb_ref, s_ref, sw_ref, stat_ref, mrep_ref, acc_ref, out_ref, *, n_sel, n_top):
    i = pl.program_id(1)
    tq = q_ref.shape[0]
    h = NSA_HEADS
    q = _stack_queries(q_ref, tq)

    s = _dot_nt(q, kc_ref[0]).reshape(h, tq, -1) + bc_ref[...]
    m = jnp.max(s, axis=-1, keepdims=True)
    p = jnp.where(s > 0.5 * MASKED, jnp.exp(s - m), 0.0)
    l = jnp.sum(p, axis=-1, keepdims=True)
    p = p * (1.0 / jnp.where(l > 0.0, l, 1.0))
    o_cmp = jnp.dot(p.reshape(h * tq, -1).astype(BF16), vc_ref[0], preferred_element_type=F32)
    out_ref[...] = _gate_columns(gl_ref, 0) * o_cmp.reshape(h, tq, LANES)

    for kv in range(NSA_KV_HEADS):
        pk = p[kv * NSA_GROUP]
        for g in range(1, NSA_GROUP):
            pk = pk + p[kv * NSA_GROUP + g]
        mb_ref[kv] = _selection_mask(pk, ov_ref, ex_ref, i * tq, n_sel, n_top)

    tk = ATT_TILE
    n_back = WINDOW // tk
    wide = (n_back + 1) * tk
    far_blk = FAR_TILE // tk
    n_tile = sk_ref.shape[0] // tk
    tail_col = s_ref.shape[-1] - wide
    kvg = lambda x: x.reshape(NSA_KV_HEADS, NSA_GROUP, tq, x.shape[-1])
    blocks = lambda x: [x[..., c * LANES:(c + 1) * LANES] for c in range(x.shape[-1] // LANES)]
    lane_max = lambda x: functools.reduce(jnp.maximum, blocks(x))
    lane_sum = lambda x: functools.reduce(jnp.add, blocks(x))
    rep = lambda x, n: jnp.concatenate([x] * n, axis=-1)
    far_bias = nb_ref[NB_FAR][:, 0:1, 0:1]

    n_far = jnp.maximum(i - 1, 0) // far_blk
    tail0 = jnp.minimum(n_far * far_blk, n_tile - (n_back + 1))
    tail_rows = pl.ds(pl.multiple_of(tail0 * tk, tk), wide)
    tiles = []
    for c in range(n_back + 1):
        blk = tail0 + c
        kind = jnp.where((blk < n_far * far_blk) | (blk > i), NB_NONE, jnp.minimum(i - blk, NB_FAR))
        tiles.append(nb_ref[kind])
    s = _dot_nt(q, sk_ref[tail_rows, :]).reshape(h, tq, wide) + jnp.concatenate(tiles, axis=-1)
    s = (kvg(s) + mb_ref[:, :, tail_rows][:, None]).reshape(h, tq, wide)
    s_ref[:, :, tail_col:tail_col + wide] = s
    stat_ref[0] = lane_max(s)

    win0 = jnp.maximum(i - n_back, 0)
    win_rows = pl.ds(pl.multiple_of(win0 * tk, tk), wide)
    tiles = []
    for c in range(n_back + 1):
        r = i - (win0 + c)
        tiles.append(nb_ref[jnp.where(r < 0, NB_NONE, jnp.where(r == n_back, NB_EDGE, jnp.minimum(r, NB_FAR)))])
    s = _dot_nt(q, wk_ref[win_rows, :]).reshape(h, tq, wide) + jnp.concatenate(tiles, axis=-1)
    sw_ref[...] = s
    stat_ref[1] = lane_max(s)

    def far_scores(f, carry):
        rows = pl.ds(pl.multiple_of(f * FAR_TILE, FAR_TILE), FAR_TILE)
        s = _dot_nt(q, sk_ref[rows, :]).reshape(h, tq, FAR_TILE) + far_bias
        s = (kvg(s) + mb_ref[:, :, rows][:, None]).reshape(h, tq, FAR_TILE)
        s_ref[:, :, rows] = s
        stat_ref[0] = jnp.maximum(stat_ref[0], lane_max(s))
        return carry
    lax.fori_loop(0, n_far, far_scores, 0)

    for br in range(2):
        mrep_ref[br] = jnp.broadcast_to(jnp.max(stat_ref[br], axis=-1, keepdims=True), (h, tq, LANES))
        stat_ref[br] = jnp.zeros((h, tq, LANES), F32)
    acc_ref[...] = jnp.zeros(acc_ref.shape, F32)

    def far_values(f, carry):
        rows = pl.ds(pl.multiple_of(f * FAR_TILE, FAR_TILE), FAR_TILE)
        p = jnp.exp(s_ref[:, :, rows] - rep(mrep_ref[0], far_blk))
        stat_ref[0] += lane_sum(p)
        acc_ref[...] += jnp.dot(p.reshape(h * tq, FAR_TILE).astype(BF16), sv_ref[rows, :],
                                preferred_element_type=F32).reshape(h, tq, LANES)
        return carry
    lax.fori_loop(0, n_far, far_values, 0)

    p = jnp.exp(s_ref[:, :, tail_col:tail_col + wide] - rep(mrep_ref[0], n_back + 1))
    l = jnp.sum(stat_ref[0] + lane_sum(p), axis=-1, keepdims=True)
    o_sel = acc_ref[...] + jnp.dot(p.reshape(h * tq, wide).astype(BF16), sv_ref[tail_rows, :],
                                   preferred_element_type=F32).reshape(h, tq, LANES)
    out_ref[...] += _gate_columns(gl_ref, 1) * (o_sel * (1.0 / l))

    p = jnp.exp(sw_ref[...] - rep(mrep_ref[1], n_back + 1))
    l = jnp.sum(lane_sum(p), axis=-1, keepdims=True)
    o_win = jnp.dot(p.reshape(h * tq, wide).astype(BF16), wv_ref[win_rows, :],
                    preferred_element_type=F32).reshape(h, tq, LANES)
    out_ref[...] += _gate_columns(gl_ref, 2) * (o_win * (1.0 / l))

    low = lax.broadcasted_iota(jnp.int32, (tq, LANES), 1) < HEAD_DIM
    for g in range(NSA_GROUP):
        o_ref[:, g * LANES:(g + 1) * LANES] = jnp.where(low, out_ref[g], out_ref[NSA_GROUP + g]).astype(BF16)


def nsa_prompt(q, kc, vc, skb, svb, wkb, wvb, small, rel_bias, B, T):
    tq = ATT_TILE
    nq = T // tq
    assert T % FAR_TILE == 0 and T >= WINDOW + ATT_TILE and WINDOW % ATT_TILE == 0
    wide = WINDOW + ATT_TILE
    far_max = (nq - 2) // (FAR_TILE // ATT_TILE) * FAR_TILE
    n_chunk = kc.shape[1]
    n_sel = -(-T // SEL_BLOCK)
    n_top = min(SEL_TOP, n_sel)
    bias_c = cmp_bias_table(rel_bias, 0, T, n_chunk)
    nb = near_bias_tables(rel_bias)
    ov = overlap_matrix(n_chunk, n_sel)
    ex = expand_matrix(T)
    h = NSA_HEADS
    full = lambda a: pl.BlockSpec(a.shape, lambda b, i: (0,) * a.ndim)
    seq = pl.BlockSpec((T, NSA_KV_DIM), lambda b, i: (b, 0))
    tok = pl.BlockSpec((1, n_chunk, NSA_KV_DIM), lambda b, i: (b, 0, 0))
    return pl.pallas_call(
        functools.partial(_nsa_prompt_body, n_sel=n_sel, n_top=n_top),
        grid=(B, nq),
        in_specs=[pl.BlockSpec((tq, NSA_Q_DIM), lambda b, i: (b * nq + i, 0)),
                  tok, tok, seq, seq, seq, seq,
                  pl.BlockSpec((h, tq, n_chunk), lambda b, i: (0, i, 0)),
                  full(nb),
                  pl.BlockSpec((tq, LANES), lambda b, i: (b * nq + i, 0)),
                  full(ov), full(ex)],
        out_specs=pl.BlockSpec((tq, NSA_Q_DIM), lambda b, i: (b * nq + i, 0)),
        out_shape=jax.ShapeDtypeStruct((B * T, NSA_Q_DIM), BF16),
        scratch_shapes=[pltpu.VMEM((NSA_KV_HEADS, tq, T), F32),
                        pltpu.VMEM((h, tq, far_max + wide), F32), pltpu.VMEM((h, tq, wide), F32),
                        pltpu.VMEM((2, h, tq, LANES), F32), pltpu.VMEM((2, h, tq, LANES), F32),
                        pltpu.VMEM((h, tq, LANES), F32), pltpu.VMEM((h, tq, LANES), F32)],
        compiler_params=pltpu.CompilerParams(dimension_semantics=("parallel", "arbitrary"),
                                             vmem_limit_bytes=VMEM_LIMIT),
        name="nsa_prompt",
    )(q, kc, vc, skb, svb, wkb, wvb, bias_c, nb, small, ov, ex)


def sample_bias_tables(rel_bias, past, T, W):
    t = jnp.arange(T)[:, None]
    pad = jnp.arange(ATT_TILE)[None, :]
    new = jnp.where((pad < T) & (t >= pad), bias_lookup(rel_bias, t - pad), MASKED)
    sel = bias_lookup(rel_bias, past + t - jnp.arange(past)[None, :])
    rel_w = W + t - jnp.arange(W)[None, :]
    win = jnp.where(rel_w < WINDOW, bias_lookup(rel_bias, rel_w), MASKED)
    cmp = cmp_bias_table(rel_bias, past, T, past // CMP_STRIDE)
    return cmp, jnp.concatenate([sel, new], axis=-1), jnp.concatenate([win, new], axis=-1)


def _softmax_rows(s):
    m = jnp.max(s, axis=-1, keepdims=True)
    p = jnp.exp(s - m)
    return p * (1.0 / jnp.sum(p, axis=-1, keepdims=True))


def _nsa_sample_body(pt_ref, q_ref, skn_ref, svn_ref, wkn_ref, wvn_ref, gl_ref, wink_ref, winv_ref,
                     bc_ref, bs_ref, bw_ref, ov_ref, ex_ref, pk_ref, pv_ref, w1k_ref, w1v_ref, w2k_ref, w2v_ref,
                     *rest, n_pages, n_sel, n_top, past):
    pages = [rest[i * n_pages:(i + 1) * n_pages] for i in range(4)]
    o_ref, wko_ref, wvo_ref, s_ref, new_ref, rows_ref = rest[4 * n_pages:]
    h = NSA_HEADS
    tq = q_ref.shape[0]
    tk = ATT_TILE
    q = _stack_queries(q_ref, tq)

    new_ref[...] = jnp.zeros(new_ref.shape, F32)
    for n, r in enumerate((skn_ref, svn_ref, wkn_ref, wvn_ref)):
        new_ref[n, 0:tq, :] = r[...]

    for pg in range(n_pages):
        rows_ref[0, pg * tk:(pg + 1) * tk, :] = pages[0][pg][0].T
        rows_ref[1, pg * tk:(pg + 1) * tk, :] = pages[1][pg][0].T
    kc = _compress_one(rows_ref.at[0], pk_ref, w1k_ref, w2k_ref).astype(BF16)
    vc = _compress_one(rows_ref.at[1], pv_ref, w1v_ref, w2v_ref).astype(BF16)
    s = _dot_nt(q, kc).reshape(h, tq, -1) + bc_ref[...]
    p = _softmax_rows(s)
    o_cmp = jnp.dot(p.reshape(h * tq, -1).astype(BF16), vc, preferred_element_type=F32)
    out = _gate_columns(gl_ref, 0) * o_cmp.reshape(h, tq, LANES)

    q_pos = past + lax.broadcasted_iota(jnp.int32, (tq, LANES), 0)
    masks = []
    for kv in range(NSA_KV_HEADS):
        pkv = p[kv * NSA_GROUP]
        for g in range(1, NSA_GROUP):
            pkv = pkv + p[kv * NSA_GROUP + g]
        masks.append(_selection_mask(pkv, ov_ref, ex_ref, past, n_sel, n_top))
    mask = jnp.stack(masks, axis=0)
    for pg in range(n_pages):
        s_ref[:, pg * tk:(pg + 1) * tk] = jnp.dot(q, pages[2][pg][0].astype(BF16), preferred_element_type=F32)
    s_ref[:, n_pages * tk:(n_pages + 1) * tk] = _dot_nt(q, new_ref[0].astype(BF16))
    width = (n_pages + 1) * tk
    s = s_ref[...].reshape(h, tq, width) + bs_ref[...]
    s = (s.reshape(NSA_KV_HEADS, NSA_GROUP, tq, width) + mask[:, None]).reshape(h, tq, width)
    p = _softmax_rows(s).reshape(h * tq, width).astype(BF16)
    o_sel = jnp.dot(p[:, n_pages * tk:], new_ref[1].astype(BF16), preferred_element_type=F32)
    for pg in range(n_pages):
        o_sel = o_sel + _dot_nt(p[:, pg * tk:(pg + 1) * tk], pages[3][pg][0].astype(BF16))
    out = out + _gate_columns(gl_ref, 1) * o_sel.reshape(h, tq, LANES)

    w = wink_ref.shape[2]
    s_ref[:, 0:w] = jnp.dot(q, wink_ref[0].astype(BF16), preferred_element_type=F32)
    s_ref[:, w:w + tk] = _dot_nt(q, new_ref[2].astype(BF16))
    s = s_ref[:, 0:w + tk].reshape(h, tq, w + tk) + bw_ref[...]
    p = _softmax_rows(s).reshape(h * tq, w + tk).astype(BF16)
    o_win = jnp.dot(p[:, w:], new_ref[3].astype(BF16), preferred_element_type=F32) \
        + _dot_nt(p[:, 0:w], winv_ref[0].astype(BF16))
    out = out + _gate_columns(gl_ref, 2) * o_win.reshape(h, tq, LANES)

    low = lax.broadcasted_iota(jnp.int32, (tq, LANES), 1) < HEAD_DIM
    for g in range(NSA_GROUP):
        o_ref[:, g * LANES:(g + 1) * LANES] = jnp.where(low, out[g], out[NSA_GROUP + g])

    keep = w - tq
    lane_w = lax.broadcasted_iota(jnp.int32, (NSA_KV_DIM, w), 1)
    for old_ref, new_rows_ref, dst_ref in ((wink_ref, wkn_ref, wko_ref), (winv_ref, wvn_ref, wvo_ref)):
        shifted = pltpu.roll(old_ref[0], shift=keep, axis=1)
        tail = jnp.concatenate([jnp.zeros((tk - tq, NSA_KV_DIM), F32), new_rows_ref[...]], axis=0).T
        tail = jnp.concatenate([jnp.zeros((NSA_KV_DIM, w - tk), F32), tail], axis=1)
        dst_ref[0] = jnp.where(lane_w >= keep, tail, shifted)


def nsa_sample(q, skn, svn, wkn, wvn, small, pools, win_k, win_v, page_table, rel_bias, pk, pv, B, T):
    n_pages = page_table.shape[1]
    past = n_pages * PAGE_SIZE
    W = win_k.shape[1]
    L = past + T
    n_cmp = (L - CMP_LEN) // CMP_STRIDE + 1
    assert PAGE_SIZE == ATT_TILE and T % 8 == 0 and T <= SEL_BLOCK and past % SEL_BLOCK == 0
    assert (n_cmp - 1) * CMP_STRIDE + CMP_LEN <= past and n_cmp >= past // CMP_STRIDE - CMP_HALVES + 1
    assert W % ATT_TILE == 0 and W + T >= WINDOW
    n_sel = -(-L // SEL_BLOCK)
    n_top = min(SEL_TOP, n_sel)
    w_keep = min(WINDOW, W + T)
    bc, bs, bw = sample_bias_tables(rel_bias, past, T, W)
    ov = overlap_matrix(past // CMP_STRIDE, n_sel)
    ex = expand_matrix(past + ATT_TILE)
    n_pool = pools[0].shape[0]
    chunks = PAGE_SIZE // CMP_STRIDE
    assert W == w_keep
    feat_major = lambda a: a.transpose(0, 2, 3, 1).reshape(a.shape[0], NSA_KV_DIM, a.shape[1])
    cmp_pools = [feat_major(p) for p in pools[:2]]
    sel_pools = [feat_major(p) for p in pools[2:]]
    wk3 = feat_major(win_k)
    wv3 = feat_major(win_v)

    full = lambda a: pl.BlockSpec(a.shape, lambda b, pt: (0,) * a.ndim)
    row = lambda w: pl.BlockSpec((T, w), lambda b, pt: (b, 0))
    seq = lambda n: pl.BlockSpec((1, NSA_KV_DIM, n), lambda b, pt: (b, 0, 0))
    page = lambda shape, pg: pl.BlockSpec((1,) + shape, lambda b, pt: (pt[b, pg], 0, 0))
    page_specs, page_args = [], []
    for pool in cmp_pools + sel_pools:
        for pg in range(n_pages):
            page_specs.append(page((NSA_KV_DIM, PAGE_SIZE), pg))
            page_args.append(pool)
    tables = (bc, bs, bw, ov, ex, pk[0], pv[0], pk[1], pv[1], pk[2], pv[2])
    grid_spec = pltpu.PrefetchScalarGridSpec(
        num_scalar_prefetch=1,
        grid=(B,),
        in_specs=[row(NSA_Q_DIM), row(NSA_KV_DIM), row(NSA_KV_DIM), row(NSA_KV_DIM), row(NSA_KV_DIM), row(LANES),
                  seq(W), seq(W)] + [full(a) for a in tables] + page_specs,
        out_specs=[row(NSA_Q_DIM), seq(w_keep), seq(w_keep)],
        scratch_shapes=[pltpu.VMEM((NSA_HEADS * T, past + ATT_TILE), F32),
                        pltpu.VMEM((4, ATT_TILE, NSA_KV_DIM), F32),
                        pltpu.VMEM((2, past, NSA_KV_DIM), F32)],
    )
    o, wk_next, wv_next = pl.pallas_call(
        functools.partial(_nsa_sample_body, n_pages=n_pages, n_sel=n_sel, n_top=n_top, past=past),
        grid_spec=grid_spec,
        out_shape=[jax.ShapeDtypeStruct((B * T, NSA_Q_DIM), F32),
                   jax.ShapeDtypeStruct((B, NSA_KV_DIM, w_keep), F32),
                   jax.ShapeDtypeStruct((B, NSA_KV_DIM, w_keep), F32)],
        compiler_params=pltpu.CompilerParams(dimension_semantics=("arbitrary",), vmem_limit_bytes=VMEM_LIMIT),
        name="nsa_sample",
    )(page_table, q, skn, svn, wkn, wvn, small, wk3, wv3, *tables, *page_args)
    row_major = lambda a: a.reshape(B, NSA_KV_HEADS, HEAD_DIM, w_keep).transpose(0, 3, 1, 2)
    return o, row_major(wk_next), row_major(wv_next)


HIST_ROWS = 8
DN_CHUNKS_PER_STEP = 4
DN_SEQS_PER_STEP = 8


def _split_bf16(x):
    hi = x.astype(BF16)
    lo = (x - hi.astype(F32)).astype(BF16)
    return hi, lo


def _mm_hi(a, b):
    ah, al = _split_bf16(a)
    bh, bl = _split_bf16(b)
    d = lambda x, y: jnp.dot(x, y, preferred_element_type=F32)
    return d(ah, bh) + (d(ah, bl) + d(al, bh))


def _bmm(a, b):
    return jnp.einsum('hij,hjk->hik', a, b, preferred_element_type=F32)


def _bmm_nt(a, b):
    return jnp.einsum('hid,hjd->hij', a, b, preferred_element_type=F32)


def _bmm_hi(a, b):
    ah, al = _split_bf16(a)
    bh, bl = _split_bf16(b)
    return _bmm(ah, bh) + (_bmm(ah, bl) + _bmm(al, bh))


def _cumsum_rows(tri, x):
    hi, lo = _split_bf16(x)
    lo2 = (x - hi.astype(F32) - lo.astype(F32)).astype(BF16)
    d = lambda y: jnp.dot(tri, y, preferred_element_type=F32)
    return d(hi) + (d(lo) + d(lo2))


def _softplus(x):
    return jnp.maximum(x, 0.0) + jnp.log1p(jnp.exp(-jnp.abs(x)))


def _silu(x):
    return x * jax.nn.sigmoid(x)


def _deltanet_body(x_ref, sm_ref, dg_ref, hist_ref, s0_ref, cw_ref, apar_ref, dt_ref, ng_ref,
                   o_ref, s_out_ref, hist_out_ref, xbuf_ref, s_ref, *, c, n_seq):
    j = pl.program_id(1)
    rows = x_ref.shape[0]
    seq_rows = rows // n_seq
    n_grp = seq_rows // c
    hd = DN_HEAD_DIM
    n_hist = CONV_W - 1

    n_pair = DN_HEADS // 2
    zero_blk = jnp.zeros((hd, hd), F32)

    @pl.when(j == 0)
    def _():
        xbuf_ref[:, HIST_ROWS - n_hist:HIST_ROWS, :] = hist_ref[...]
        for sq in range(n_seq):
            for p in range(n_pair):
                s_ref[sq * n_pair + p] = jnp.concatenate(
                    [jnp.concatenate([s0_ref[sq, 2 * p], zero_blk], axis=1),
                     jnp.concatenate([zero_blk, s0_ref[sq, 2 * p + 1]], axis=1)], axis=0)

    xbuf_ref[:, HIST_ROWS:HIST_ROWS + seq_rows, :] = x_ref[...].reshape(n_seq, seq_rows, x_ref.shape[1])
    y = None
    for t in range(CONV_W):
        lo = HIST_ROWS - n_hist + t
        term = xbuf_ref[:, lo:lo + seq_rows, :] * cw_ref[t:t + 1, :]
        y = term if y is None else y + term
    y = _silu(y).reshape(rows, x_ref.shape[1])
    tail = xbuf_ref[:, HIST_ROWS + seq_rows - n_hist:HIST_ROWS + seq_rows, :]
    xbuf_ref[:, HIST_ROWS - n_hist:HIST_ROWS, :] = tail

    sm = sm_ref[...]
    beta_all = jax.nn.sigmoid(sm)
    g_all = apar_ref[...] * _softplus(sm + dt_ref[...])
    row = lax.broadcasted_iota(jnp.int32, (rows, rows), 0)
    col = lax.broadcasted_iota(jnp.int32, (rows, rows), 1)
    tri = jnp.where((row >= col) & (row // c == col // c), 1.0, 0.0).astype(BF16)
    gcum_all = _cumsum_rows(tri, g_all)
    gcum_t = gcum_all.T

    wd, wj = 2 * hd, 2 * c
    first_d = lax.broadcasted_iota(jnp.int32, (1, wd), 1) < hd
    first_j = lax.broadcasted_iota(jnp.int32, (1, wj), 1) < c
    rj = lax.broadcasted_iota(jnp.int32, (c, wj), 0)
    cj = lax.broadcasted_iota(jnp.int32, (c, wj), 1)
    cj = jnp.where(cj < c, cj, cj - c)
    incl, strict = rj >= cj, rj > cj
    eye = jnp.where(rj == cj, 1.0, 0.0)
    same_head = (lax.broadcasted_iota(jnp.int32, (wd, wd), 0) < hd) == (lax.broadcasted_iota(jnp.int32, (wd, wd), 1) < hd)
    ones_bd = jnp.where(same_head, 1.0, 0.0).astype(BF16)
    first_rows = lax.broadcasted_iota(jnp.int32, (wd, 1), 0) < hd

    items = [(g, p) for g in range(n_seq * n_grp) for p in range(n_pair)]

    def block_diag(x, first):
        zero = jnp.zeros_like(x)
        return jnp.concatenate([jnp.where(first, x, zero), jnp.where(first, zero, x)], axis=1)

    def head_sums(x):
        n = x.shape[0]
        hi, lo = _split_bf16(x.reshape(n * c, wd))
        tot = jnp.dot(hi, ones_bd, preferred_element_type=F32) + jnp.dot(lo, ones_bd, preferred_element_type=F32)
        return tot.reshape(n, c, wd)

    def mm_hi(a_parts, b_parts):
        return _bmm(a_parts[0], b_parts[0]) + (_bmm(a_parts[0], b_parts[1]) + _bmm(a_parts[1], b_parts[0]))

    def per_head(cols, base, first):
        return jnp.stack([jnp.where(first, cols[g * c:(g + 1) * c, base + 2 * p:base + 2 * p + 1],
                                    cols[g * c:(g + 1) * c, base + 2 * p + 1:base + 2 * p + 2])
                          for g, p in items], axis=0)

    lanes = lambda x, off: jnp.stack([x[g * c:(g + 1) * c, off + p * wd:off + (p + 1) * wd] for g, p in items], axis=0)
    q = lanes(y, 0)
    k = lanes(y, DN_DIM)
    v = lanes(y, 2 * DN_DIM)
    q = q * lax.rsqrt(head_sums(q * q) + EPS) * hd ** -0.5
    k = k * lax.rsqrt(head_sums(k * k) + EPS)
    beta = per_head(beta_all, SM_BETA, first_d)
    gc_d = per_head(gcum_all, SM_DECAY, first_d)
    gc_j = per_head(gcum_all, SM_DECAY, first_j)
    gr_j = jnp.stack([jnp.concatenate([gcum_t[SM_DECAY + 2 * p:SM_DECAY + 2 * p + 1, g * c:(g + 1) * c],
                                       gcum_t[SM_DECAY + 2 * p + 1:SM_DECAY + 2 * p + 2, g * c:(g + 1) * c]], axis=1)
                      for g, p in items], axis=0)
    gl_d = gc_d[:, c - 1:c, :]
    decay = jnp.where(incl, jnp.exp(jnp.where(incl, gc_j - gr_j, 0.0)), 0.0)
    eg = jnp.exp(gc_d)
    kb = k * beta
    k_bd = block_diag(k.astype(BF16), first_d)
    a = jnp.where(strict, _bmm_nt(kb.astype(BF16), k_bd) * decay, 0.0)
    t_inv = eye - a
    split_bd = lambda parts, first: tuple(block_diag(x, first) for x in parts)
    pow_parts = _split_bf16(a)
    pow_bd = split_bd(pow_parts, first_j)
    for _ in range(int(math.log2(c)) - 1):
        pow_parts = _split_bf16(mm_hi(pow_parts, pow_bd))
        pow_bd = split_bd(pow_parts, first_j)
        t_inv = t_inv + mm_hi(_split_bf16(t_inv), pow_bd)
    vb_parts = split_bd(_split_bf16(v * beta), first_d)
    kbg_parts = split_bd(_split_bf16(kb * eg), first_d)
    rhs = tuple(jnp.concatenate([x, z], axis=2) for x, z in zip(vb_parts, kbg_parts))
    uw = mm_hi(_split_bf16(t_inv), rhs)
    u, w = uw[:, :, :wd], uw[:, :, wd:]
    qk = jnp.where(incl, _bmm_nt(q.astype(BF16), k_bd) * decay, 0.0).astype(BF16)
    wq = jnp.concatenate([w, q * eg], axis=1).astype(BF16)
    kg = (k * jnp.exp(gl_d - gc_d)).astype(BF16)
    gl_rows = jnp.where(first_rows, jnp.exp(gl_d[:, :, 0:1]), jnp.exp(gl_d[:, :, hd:hd + 1]))

    s = s_ref[...]
    outs = {}
    for kc in range(n_grp):
        pick = lambda x: jnp.concatenate([x[(sq * n_grp + kc) * n_pair:(sq * n_grp + kc + 1) * n_pair]
                                          for sq in range(n_seq)], axis=0)
        ws = _bmm(pick(wq), s.astype(BF16))
        vn = (pick(u) - ws[:, :c]).astype(BF16)
        o_k = ws[:, c:] + _bmm(pick(qk), block_diag(vn, first_d))
        kv_outer = jnp.einsum('pck,pcv->pkv', pick(kg), vn, preferred_element_type=F32)
        s = s * pick(gl_rows) + jnp.where(same_head, kv_outer, 0.0)
        for sq in range(n_seq):
            outs[sq * n_grp + kc] = o_k[sq * n_pair:(sq + 1) * n_pair]
    s_ref[...] = s
    o = jnp.concatenate([outs[g] for g in range(n_seq * n_grp)], axis=0)
    ng = jnp.concatenate([ng_ref[...], ng_ref[...]], axis=1)
    o = o * lax.rsqrt(head_sums(o * o) * (1.0 / hd) + EPS) * ng
    o = jnp.concatenate([jnp.concatenate([o[g * n_pair + p] for p in range(n_pair)], axis=1)
                         for g in range(n_seq * n_grp)], axis=0)
    o_ref[...] = (o * _silu(dg_ref[...])).astype(o_ref.dtype)

    @pl.when(j == pl.num_programs(1) - 1)
    def _():
        for sq in range(n_seq):
            for p in range(n_pair):
                s_pair = s_ref[sq * n_pair + p]
                s_out_ref[sq, 2 * p] = s_pair[0:hd, 0:hd]
                s_out_ref[sq, 2 * p + 1] = s_pair[hd:2 * hd, hd:2 * hd]
        hist_out_ref[...] = tail


def deltanet(qkv_raw, small, dg, conv_hist, s0, lp, B, T):
    c = DN_CHUNK if T >= DN_CHUNK else T
    assert T % c == 0 and c % 8 == 0 and c >= CONV_W - 1 and (c & (c - 1)) == 0
    n_grp = math.gcd(T // c, DN_CHUNKS_PER_STEP)
    n_seq = math.gcd(B, DN_SEQS_PER_STEP) if T == c else 1
    seq_rows = n_grp * c
    step = n_seq * seq_rows
    n = T // seq_rows
    put = lambda vals, base: jnp.zeros((LANES,), F32).at[base + jnp.arange(DN_HEADS)].set(vals).reshape(1, LANES)
    apar = put(-jnp.exp(lp['dn_a_log'].astype(F32)), SM_DECAY)
    dtb = put(lp['dn_dt_bias'].astype(F32), SM_DECAY)
    assert n_seq == 1 or n == 1
    rows = lambda w: pl.BlockSpec((step, w), lambda b, j: (b * n + j, 0))
    per_b = lambda a: pl.BlockSpec((n_seq,) + a.shape[1:], lambda b, j: (b,) + (0,) * (a.ndim - 1))
    full = lambda a: pl.BlockSpec(a.shape, lambda b, j: (0,) * a.ndim)
    ng = lp['dn_norm_g'].reshape(1, DN_HEAD_DIM).astype(F32)
    return pl.pallas_call(
        functools.partial(_deltanet_body, c=c, n_seq=n_seq),
        grid=(B // n_seq, n),
        in_specs=[rows(DN_CONV_DIM), rows(LANES), rows(DN_DIM), per_b(conv_hist), per_b(s0),
                  full(lp['conv_w']), full(apar), full(dtb), full(ng)],
        out_specs=[rows(DN_DIM), per_b(s0), per_b(conv_hist)],
        out_shape=[jax.ShapeDtypeStruct((B * T, DN_DIM), BF16 if step % 16 == 0 else F32),
                   jax.ShapeDtypeStruct(s0.shape, F32),
                   jax.ShapeDtypeStruct(conv_hist.shape, F32)],
        scratch_shapes=[pltpu.VMEM((n_seq, HIST_ROWS + seq_rows, DN_CONV_DIM), F32),
                        pltpu.VMEM((n_seq * (DN_HEADS // 2), 2 * DN_HEAD_DIM, 2 * DN_HEAD_DIM), F32)],
        compiler_params=pltpu.CompilerParams(dimension_semantics=("parallel", "arbitrary"),
                                             vmem_limit_bytes=VMEM_LIMIT),
        name="deltanet",
    )(qkv_raw, small, dg, conv_hist, s0, lp['conv_w'], apar, dtb, ng)


RT_GROUP = 0
RT_EXPERT = N_GROUPS


def pack_router(wg, bg, we, be):
    w = jnp.concatenate([wg, we], axis=1)
    b = jnp.concatenate([bg, be], axis=0)
    pad = LANES - w.shape[1]
    return jnp.pad(w, ((0, 0), (0, pad))).astype(BF16), jnp.pad(b, (0, pad)).reshape(1, LANES).astype(F32)


def _route(r):
    lane = lax.broadcasted_iota(jnp.int32, r.shape, 1).astype(F32)
    big = float(LANES)
    is_grp = lane < N_GROUPS
    lg = jnp.where(is_grp, r, -jnp.inf)
    mg = jnp.max(lg, axis=-1, keepdims=True)
    eg = jnp.where(is_grp, jnp.exp(lg - mg), 0.0)
    grp = jnp.min(jnp.where(lg == mg, lane, big), axis=-1, keepdims=True)
    wg = 1.0 / jnp.sum(eg, axis=-1, keepdims=True)
    first = RT_EXPERT + grp * EXPERTS_PER_GROUP
    in_grp = (lane >= first) & (lane < first + EXPERTS_PER_GROUP)
    le = jnp.where(in_grp, r, -jnp.inf)
    me = jnp.max(le, axis=-1, keepdims=True)
    ee = jnp.where(in_grp, jnp.exp(le - me), 0.0)
    pe = jnp.where(in_grp, ee / jnp.sum(ee, axis=-1, keepdims=True), -1.0)
    v1 = jnp.max(pe, axis=-1, keepdims=True)
    i1 = jnp.min(jnp.where(pe == v1, lane, big), axis=-1, keepdims=True)
    pe2 = jnp.where(lane == i1, -1.0, pe)
    v2 = jnp.max(pe2, axis=-1, keepdims=True)
    i2 = jnp.min(jnp.where(pe2 == v2, lane, big), axis=-1, keepdims=True)
    tot = v1 + v2
    return jnp.where(lane == i1, wg * (v1 / tot), jnp.where(lane == i2, wg * (v2 / tot), 0.0))


def _finish_body(x_ref, on_ref, od_ref, g1_ref, sc_ref, sh_ref, g2_ref, won_ref, wod_ref, ng_ref, wr_ref, br_ref,
                 wgu_ref, wd_ref, fg_ref, o_ref, x1_ref, h_ref, gate_ref, y_ref):
    e = pl.program_id(1)
    tm, d = x_ref.shape
    nb = g1_ref.shape[0]
    per_batch = lambda a: a.reshape(nb, tm // nb, d)

    @pl.when(e == 0)
    def _():
        mix = jnp.dot(on_ref[...].astype(BF16), won_ref[...], preferred_element_type=F32) \
            + jnp.dot(od_ref[...].astype(BF16), wod_ref[...], preferred_element_type=F32)
        x1 = per_batch(x_ref[...]) + g1_ref[...] * per_batch(mix)
        x1 = x1.reshape(tm, d)
        x1_ref[...] = x1
        xn = x1 * lax.rsqrt(jnp.mean(x1 * x1, axis=-1, keepdims=True) + EPS) * ng_ref[...]
        h = (per_batch(xn) * (1 + sc_ref[...]) + sh_ref[...]).reshape(tm, d).astype(BF16)
        h_ref[...] = h
        gate_ref[...] = _route(jnp.dot(h, wr_ref[...], preferred_element_type=F32) + br_ref[...])
        y_ref[...] = jnp.zeros(y_ref.shape, F32)

    h = h_ref[...]
    gu = jnp.dot(h, wgu_ref[0], preferred_element_type=F32)
    he = _silu(gu[:, :D_EXPERT]) * gu[:, D_EXPERT:]
    out = jnp.dot(he.astype(BF16), wd_ref[0], preferred_element_type=F32)
    lane = lax.broadcasted_iota(jnp.int32, gate_ref.shape, 1)
    g_col = jnp.sum(jnp.where(lane == RT_EXPERT + e, gate_ref[...], 0.0), axis=-1, keepdims=True)
    y_ref[...] += g_col * out

    @pl.when(e == pl.num_programs(1) - 1)
    def _():
        x2 = per_batch(x1_ref[...]) + g2_ref[...] * per_batch(y_ref[...])
        x2 = x2.reshape(tm, d)
        o_ref[...] = x2 * lax.rsqrt(jnp.mean(x2 * x2, axis=-1, keepdims=True) + EPS) * fg_ref[...]


def finish_layer(x, o_nsa, o_dn, g1, sc2, sh2, g2, packed, norm_ffn_g, final_g, *, tm=1024):
    B, T, D = x.shape
    n = B * T
    tm = min(tm, n)
    rows = max(tm // T, 1)
    per = max(T // tm, 1)
    row = lambda w: pl.BlockSpec((tm, w), lambda i, e: (i, 0))
    mod = pl.BlockSpec((rows, 1, D), lambda i, e: (i // per, 0, 0))
    full = lambda a: pl.BlockSpec(a.shape, lambda i, e: (0,) * a.ndim)
    expert = lambda a: pl.BlockSpec((1,) + a.shape[1:], lambda i, e: (e, 0, 0))
    vec = lambda v: v.reshape(1, D).astype(F32)
    args = (x.reshape(n, D), o_nsa, o_dn, g1, sc2, sh2, g2, packed['w_out_nsa'], packed['w_out_dn'],
            vec(norm_ffn_g), packed['router_w'], packed['router_b'], packed['w_gate_up'], packed['w_down'],
            vec(final_g))
    specs = [row(D), row(o_nsa.shape[1]), row(o_dn.shape[1]), mod, mod, mod, mod] + \
            [full(a) for a in args[7:12]] + [expert(args[12]), expert(args[13]), full(args[14])]
    return pl.pallas_call(
        _finish_body,
        grid=(n // tm, N_EXPERTS),
        in_specs=specs,
        out_specs=row(D),
        out_shape=jax.ShapeDtypeStruct((n, D), F32),
        scratch_shapes=[pltpu.VMEM((tm, D), F32), pltpu.VMEM((tm, D), BF16), pltpu.VMEM((tm, LANES), F32),
                        pltpu.VMEM((tm, D), F32)],
        compiler_params=pltpu.CompilerParams(dimension_semantics=("parallel", "arbitrary"),
                                             vmem_limit_bytes=VMEM_LIMIT),
        name="finish_layer",
    )(*args).reshape(B, T, D)


def _final_norm_body(x_ref, g_ref, o_ref):
    x = x_ref[...]
    o_ref[...] = x * lax.rsqrt(jnp.mean(x * x, axis=-1, keepdims=True) + EPS) * g_ref[...]


def final_rmsnorm(x, g, *, tm=512):
    shp = x.shape
    x2 = x.reshape(-1, shp[-1])
    n, d = x2.shape
    return pl.pallas_call(
        _final_norm_body,
        grid=(n // tm,),
        in_specs=[pl.BlockSpec((tm, d), lambda i: (i, 0)), pl.BlockSpec((1, d), lambda i: (0, 0))],
        out_specs=pl.BlockSpec((tm, d), lambda i: (i, 0)),
        out_shape=jax.ShapeDtypeStruct(x2.shape, x2.dtype),
        compiler_params=pltpu.CompilerParams(dimension_semantics=("parallel",)),
        name="final_rmsnorm",
    )(x2, g.reshape(1, d)).reshape(shp)


def permute_w_out_nsa(w_out):
    w = w_out[:NSA_Q_DIM].reshape(NSA_KV_HEADS, NSA_GROUP, HEAD_DIM, -1)
    return w.transpose(1, 0, 2, 3).reshape(NSA_Q_DIM, -1).astype(BF16)


def layer_prompt(x, c, lp, rel_bias, packed):
    B, T, _ = x.shape
    sh1, sc1, g1, sh2, sc2, g2 = modulation(c, lp)
    (q, ck, cv, sk, sv, wk, wv, skb, svb, wkb, wvb, qkv_raw, dg, small, ck_rows, cv_rows) = in_projection(
        x, lp['norm_mix_g'], sc1, sh1, packed['w_in'], feature_major=True)
    kc, vc = compress_tokens(ck_rows, cv_rows, packed['cmp_k'], packed['cmp_v'], B)
    o_nsa = nsa_prompt(q, kc, vc, skb, svb, wkb, wvb, small, rel_bias, B, T)
    kv = lambda r: r.reshape(B, NSA_KV_HEADS, HEAD_DIM, -1).transpose(0, 3, 1, 2)
    w_keep = min(WINDOW, T)
    ck, cv, sk, sv, wk, wv = kv(ck), kv(cv), kv(sk), kv(sv), kv(wk[:, :, T - w_keep:]), kv(wv[:, :, T - w_keep:])
    conv_hist = jnp.zeros((B, CONV_W - 1, DN_CONV_DIM), x.dtype)
    s0 = jnp.zeros((B, DN_HEADS, DN_HEAD_DIM, DN_HEAD_DIM), x.dtype)
    o_dn, s_new, conv_new = deltanet(qkv_raw, small, dg, conv_hist, s0, lp, B, T)
    y = finish_layer(x, o_nsa, o_dn, g1, sc2, sh2, g2, packed, lp['norm_ffn_g'], packed['final_norm_g'])
    return y, (ck, cv, sk, sv, wk, wv, s_new, conv_new)


def layer_sample(x, c, cmp_k_pool, cmp_v_pool, sel_k_pool, sel_v_pool, win_k, win_v, s0, conv_hist,
                 page_table, lp, rel_bias, packed):
    B, T, _ = x.shape
    sh1, sc1, g1, sh2, sc2, g2 = modulation(c, lp)
    (q, ck, cv, sk, sv, wk, wv, _, _, _, _, qkv_raw, dg, small) = in_projection(
        x, lp['norm_mix_g'], sc1, sh1, packed['w_in'])
    o_nsa, wk_next, wv_next = nsa_sample(q, sk, sv, wk, wv, small, (cmp_k_pool, cmp_v_pool, sel_k_pool, sel_v_pool),
                                         win_k, win_v, page_table, rel_bias, packed['cmp_k'], packed['cmp_v'], B, T)
    kv = lambda r: r.reshape(B, -1, NSA_KV_HEADS, HEAD_DIM)
    o_dn, s_new, conv_new = deltanet(qkv_raw, small, dg, conv_hist, s0, lp, B, T)
    y = finish_layer(x, o_nsa, o_dn, g1, sc2, sh2, g2, packed, lp['norm_ffn_g'], packed['final_norm_g'])
    return y, (kv(ck), kv(cv), kv(sk), kv(sv), kv(wk_next), kv(wv_next), s_new, conv_new)


def kernel(x_prompt, x_sample, cache_cmp_k, cache_cmp_v, cache_sel_k, cache_sel_v, cache_win_k, cache_win_v,
           state_delta, state_conv, page_table, c_prompt, c_sample, rel_bias, w_ada, b_ada, norm_mix_g,
           norm_ffn_g, w_in, cmp_pos_k, cmp_w1_k, cmp_w2_k, cmp_pos_v, cmp_w1_v, cmp_w2_v, conv_w, dn_a_log,
           dn_dt_bias, dn_norm_g, w_out, router_group_w, router_group_b, router_expert_w, router_expert_b,
           expert_w_gate, expert_w_up, expert_w_down, final_norm_g):
    xp, xs = x_prompt, x_sample
    new_p, new_s = [], []
    for l in range(DEPTH):
        lp = dict(w_ada=w_ada[l], b_ada=b_ada[l], norm_mix_g=norm_mix_g[l], norm_ffn_g=norm_ffn_g[l],
                  w_in=w_in[l], cmp_pos_k=cmp_pos_k[l], cmp_w1_k=cmp_w1_k[l], cmp_w2_k=cmp_w2_k[l],
                  cmp_pos_v=cmp_pos_v[l], cmp_w1_v=cmp_w1_v[l], cmp_w2_v=cmp_w2_v[l], conv_w=conv_w[l],
                  dn_a_log=dn_a_log[l], dn_dt_bias=dn_dt_bias[l], dn_norm_g=dn_norm_g[l], w_out=w_out[l],
                  router_group_w=router_group_w[l], router_group_b=router_group_b[l],
                  router_expert_w=router_expert_w[l], router_expert_b=router_expert_b[l],
                  w_gate=expert_w_gate[l], w_up=expert_w_up[l], w_down=expert_w_down[l])
        packed = dict(w_in=pack_w_in(w_in[l]),
                      cmp_k=pack_compress(cmp_pos_k[l], cmp_w1_k[l], cmp_w2_k[l]),
                      cmp_v=pack_compress(cmp_pos_v[l], cmp_w1_v[l], cmp_w2_v[l]),
                      w_out_nsa=permute_w_out_nsa(w_out[l]),
                      w_out_dn=w_out[l][NSA_Q_DIM:].astype(BF16),
                      w_gate_up=jnp.concatenate([expert_w_gate[l], expert_w_up[l]], axis=-1).astype(BF16),
                      w_down=expert_w_down[l].astype(BF16),
                      final_norm_g=final_norm_g)
        packed['router_w'], packed['router_b'] = pack_router(router_group_w[l], router_group_b[l],
                                                             router_expert_w[l], router_expert_b[l])
        xp, st_p = layer_prompt(xp, c_prompt, lp, rel_bias, packed)
        xs, st_s = layer_sample(xs, c_sample, cache_cmp_k[l], cache_cmp_v[l], cache_sel_k[l], cache_sel_v[l],
                                cache_win_k[l], cache_win_v[l], state_delta[l], state_conv[l], page_table,
                                lp, rel_bias, packed)
        new_p.append(st_p)
        new_s.append(st_s)
    assert DEPTH == 1
    y_prompt, y_sample = xp, xs

    def stk(states, i):
        return jnp.stack([s[i] for s in states])

    return (y_prompt, y_sample,
            stk(new_p, 0), stk(new_p, 1), stk(new_p, 2), stk(new_p, 3),
            stk(new_p, 4), stk(new_p, 5), stk(new_p, 6), stk(new_p, 7),
            stk(new_s, 0), stk(new_s, 1), stk(new_s, 2), stk(new_s, 3),
            stk(new_s, 4), stk(new_s, 5), stk(new_s, 6), stk(new_s, 7))
```

```python
import functools
import math

import jax
import jax.numpy as jnp
import numpy as np
from jax import lax
from jax.experimental import pallas as pl
from jax.experimental.pallas import tpu as pltpu

D_MODEL = 1024
DEPTH = 1
PAGE_SIZE = 128

NSA_HEADS = 8
NSA_KV_HEADS = 2
NSA_GROUP = NSA_HEADS // NSA_KV_HEADS
HEAD_DIM = 64
CMP_LEN = 32
CMP_STRIDE = 16
CMP_HIDDEN = 4 * HEAD_DIM
SEL_BLOCK = 64
SEL_TOP = 16
WINDOW = 512
DN_HEADS = 8
DN_HEAD_DIM = 64
CONV_W = 4
DN_CHUNK = 64
NUM_BUCKETS = 32
MAX_DISTANCE = 128
N_GROUPS = 4
EXPERTS_PER_GROUP = 8
N_EXPERTS = N_GROUPS * EXPERTS_PER_GROUP
TOP_K_IN_GROUP = 2
D_EXPERT = D_MODEL // 4

EPS = 1e-6
NEG_INF = -1e30
F32 = jnp.float32
BF16 = jnp.bfloat16

NSA_Q_DIM = NSA_HEADS * HEAD_DIM
NSA_KV_DIM = NSA_KV_HEADS * HEAD_DIM
DN_DIM = DN_HEADS * DN_HEAD_DIM
DN_CONV_DIM = 3 * DN_DIM
MIX_WIDTH = NSA_Q_DIM + DN_DIM
IN_SPLITS = (NSA_Q_DIM, NSA_KV_DIM, NSA_KV_DIM, NSA_KV_DIM, NSA_KV_DIM, NSA_KV_DIM, NSA_KV_DIM,
             3 * NSA_HEADS, DN_CONV_DIM, DN_HEADS, DN_HEADS, DN_DIM)
IN_DIM = sum(IN_SPLITS)

LANES = 128
VMEM_LIMIT = 56 * 1024 * 1024

ATT_TILE = 128
MASKED = -1e30


def t5_bucket(rel):
    n = jnp.maximum(rel, 0)
    max_exact = NUM_BUCKETS // 2
    nf = jnp.maximum(n, 1).astype(F32)
    large = max_exact + (jnp.log(nf / max_exact) / math.log(MAX_DISTANCE / max_exact)
                         * (NUM_BUCKETS - max_exact)).astype(jnp.int32)
    large = jnp.minimum(large, NUM_BUCKETS - 1)
    return jnp.where(n < max_exact, n, large)


def _modulation_body(c_ref, w_ref, b_ref, o_ref):
    c = c_ref[...]
    x = (c * jax.nn.sigmoid(c)).astype(BF16)
    o_ref[...] = jnp.dot(x, w_ref[...].astype(BF16), preferred_element_type=F32) + b_ref[...]


def modulation(c, w_ada, b_ada, *, tn=1024):
    rows, d = c.shape
    n = w_ada.shape[1]
    mod = pl.pallas_call(
        _modulation_body,
        grid=(n // tn,),
        in_specs=[pl.BlockSpec((rows, d), lambda j: (0, 0)),
                  pl.BlockSpec((d, tn), lambda j: (0, j)),
                  pl.BlockSpec((1, tn), lambda j: (0, j))],
        out_specs=pl.BlockSpec((rows, tn), lambda j: (0, j)),
        out_shape=jax.ShapeDtypeStruct((rows, n), F32),
        compiler_params=pltpu.CompilerParams(dimension_semantics=("parallel",), vmem_limit_bytes=VMEM_LIMIT),
        name="modulation",
    )(c, w_ada, b_ada.reshape(1, n))
    return jnp.split(mod[:, None, :], 6, axis=-1)


_PK_Q = (0, NSA_Q_DIM)
_PK_KV = (_PK_Q[1], _PK_Q[1] + 6 * NSA_KV_DIM)
_PK_QKV = (_PK_KV[1], _PK_KV[1] + DN_CONV_DIM)
_PK_DG = (_PK_QKV[1], _PK_QKV[1] + DN_DIM)
_PK_SM = (_PK_DG[1], _PK_DG[1] + LANES)
PK_DIM = _PK_SM[1]
SM_GATE = 0
SM_BETA = 3 * NSA_HEADS
SM_DECAY = SM_BETA + DN_HEADS


def pack_w_in(w_in):
    offs = np.cumsum((0,) + IN_SPLITS)
    part = lambda i: w_in[:, offs[i]:offs[i + 1]]
    d = w_in.shape[0]
    q = part(0).reshape(d, NSA_KV_HEADS, NSA_GROUP, HEAD_DIM).transpose(0, 2, 1, 3).reshape(d, NSA_Q_DIM)
    q = q * HEAD_DIM ** -0.5
    small = jnp.concatenate([part(7), part(9), part(10)], axis=1)
    small = jnp.pad(small, ((0, 0), (0, LANES - small.shape[1])))
    cols = [q] + [part(i) for i in range(1, 7)] + [part(8), part(11), small]
    return jnp.concatenate(cols, axis=1).astype(BF16)


def _in_proj_body(x_ref, g_ref, sc_ref, sh_ref, w_ref, q_ref, ck_ref, cv_ref, sk_ref, sv_ref, wk_ref, wv_ref,
                  skb_ref, svb_ref, wkb_ref, wvb_ref, qkv_ref, dg_ref, sm_ref, *row_refs):
    x = x_ref[...]
    tm, d = x.shape
    nb = sc_ref.shape[0]
    xn = x * lax.rsqrt(jnp.mean(x * x, axis=-1, keepdims=True) + EPS) * g_ref[...]
    h = xn.reshape(nb, tm // nb, d) * (1 + sc_ref[...]) + sh_ref[...]
    h = h.reshape(tm, d).astype(BF16)

    def mm(span):
        return jnp.dot(h, w_ref[:, span[0]:span[1]], preferred_element_type=F32)

    q_ref[...] = mm(_PK_Q).astype(BF16)
    kv = mm(_PK_KV)
    for i, r in enumerate((ck_ref, cv_ref, sk_ref, sv_ref, wk_ref, wv_ref)):
        rows = kv[:, i * NSA_KV_DIM:(i + 1) * NSA_KV_DIM]
        if row_refs:
            r[0] = rows.T
        else:
            r[...] = rows
    for i, r in enumerate(row_refs):
        r[...] = kv[:, i * NSA_KV_DIM:(i + 1) * NSA_KV_DIM]
    for i, r in enumerate((skb_ref, svb_ref, wkb_ref, wvb_ref)):
        r[...] = kv[:, (i + 2) * NSA_KV_DIM:(i + 3) * NSA_KV_DIM].astype(BF16)
    qkv_ref[...] = mm(_PK_QKV)
    dg_ref[...] = mm(_PK_DG)
    sm_ref[...] = mm(_PK_SM)


def in_projection(x, norm_g, scale, shift, w_packed, *, tm=256, feature_major=False):
    B, T, D = x.shape
    n = B * T
    tm = min(tm, n)
    rows = max(tm // T, 1)
    per = max(T // tm, 1)
    row = lambda i: (i, 0)
    by_row = lambda w, dt: (((n, w), dt), pl.BlockSpec((tm, w), row))
    if feature_major:
        assert tm <= T and tm % LANES == 0
        kv_out = (((B, NSA_KV_DIM, T), F32), pl.BlockSpec((1, NSA_KV_DIM, tm), lambda i: (i // per, 0, i % per)))
    else:
        kv_out = by_row(NSA_KV_DIM, F32)
    outs = [by_row(NSA_Q_DIM, BF16)] + [kv_out] * 6 + [by_row(NSA_KV_DIM, BF16)] * 4 + \
           [by_row(DN_CONV_DIM, F32), by_row(DN_DIM, F32), by_row(LANES, F32)] + \
           ([by_row(NSA_KV_DIM, F32)] * 2 if feature_major else [])
    return pl.pallas_call(
        _in_proj_body,
        grid=(n // tm,),
        in_specs=[pl.BlockSpec((tm, D), row),
                  pl.BlockSpec((1, D), lambda i: (0, 0)),
                  pl.BlockSpec((rows, 1, D), lambda i: (i // per, 0, 0)),
                  pl.BlockSpec((rows, 1, D), lambda i: (i // per, 0, 0)),
                  pl.BlockSpec((D, PK_DIM), lambda i: (0, 0))],
        out_specs=[spec for _, spec in outs],
        out_shape=[jax.ShapeDtypeStruct(s, dt) for (s, dt), _ in outs],
        compiler_params=pltpu.CompilerParams(dimension_semantics=("parallel",), vmem_limit_bytes=VMEM_LIMIT),
        name="in_projection",
    )(x.reshape(n, D), norm_g.reshape(1, D), scale, shift, w_packed)


CHUNK_W = CMP_STRIDE * NSA_KV_DIM
CMP_HALVES = CMP_LEN // CMP_STRIDE


def pack_compress(pos, w1, w2):
    eye = jnp.eye(NSA_KV_HEADS, dtype=F32)
    w1r = w1.reshape(CMP_HALVES, CMP_STRIDE, HEAD_DIM, CMP_HIDDEN)
    w1p = jnp.einsum('ijdc,kl->ijkdlc', w1r, eye).reshape(CMP_HALVES, CHUNK_W, NSA_KV_HEADS * CMP_HIDDEN)
    w2p = jnp.einsum('cd,kl->kcld', w2, eye).reshape(NSA_KV_HEADS * CMP_HIDDEN, NSA_KV_DIM)
    posp = jnp.broadcast_to(pos.reshape(CMP_HALVES, CMP_STRIDE, 1, HEAD_DIM),
                            (CMP_HALVES, CMP_STRIDE, NSA_KV_HEADS, HEAD_DIM)).reshape(CMP_HALVES, CHUNK_W)
    return posp, w1p.astype(BF16), w2p.astype(BF16)


def _compress_one(rows_ref, pos_ref, w1_ref, w2_ref):
    n_chunk = rows_ref.shape[0] // CMP_STRIDE
    x = jnp.concatenate([rows_ref[pl.ds(j, n_chunk, stride=CMP_STRIDE), :] for j in range(CMP_STRIDE)], axis=1)
    hid = None
    for i in range(CMP_HALVES):
        part = jnp.dot((x + pos_ref[i:i + 1]).astype(BF16), w1_ref[i], preferred_element_type=F32)
        if i:
            part = pltpu.roll(part, shift=n_chunk - i, axis=0)
        hid = part if hid is None else hid + part
    hid = hid * jax.nn.sigmoid(hid)
    return jnp.dot(hid.astype(BF16), w2_ref[...], preferred_element_type=F32)


def _compress_body(xk_ref, xv_ref, pk_ref, pv_ref, w1k_ref, w1v_ref, w2k_ref, w2v_ref, kc_ref, vc_ref):
    kc_ref[0] = _compress_one(xk_ref, pk_ref, w1k_ref, w2k_ref).astype(BF16)
    vc_ref[0] = _compress_one(xv_ref, pv_ref, w1v_ref, w2v_ref).astype(BF16)


def compress_tokens(ck, cv, pk, pv, B):
    L = ck.shape[0] // B
    n_chunk = L // CMP_STRIDE
    full = lambda a: pl.BlockSpec(a.shape, lambda b: (0,) * a.ndim)
    rows = pl.BlockSpec((L, NSA_KV_DIM), lambda b: (b, 0))
    tok = pl.BlockSpec((1, n_chunk, NSA_KV_DIM), lambda b: (b, 0, 0))
    return pl.pallas_call(
        _compress_body,
        grid=(B,),
        in_specs=[rows, rows, full(pk[0]), full(pv[0]), full(pk[1]), full(pv[1]), full(pk[2]), full(pv[2])],
        out_specs=[tok, tok],
        out_shape=[jax.ShapeDtypeStruct((B, n_chunk, NSA_KV_DIM), BF16)] * 2,
        compiler_params=pltpu.CompilerParams(dimension_semantics=("parallel",), vmem_limit_bytes=VMEM_LIMIT),
        name="compress_tokens",
    )(ck, cv, pk[0], pv[0], pk[1], pv[1], pk[2], pv[2])


def _t5_bucket_np(rel):
    n = np.maximum(rel, 0)
    max_exact = NUM_BUCKETS // 2
    nf = np.maximum(n, 1).astype(np.float32)
    large = max_exact + (np.log(nf / max_exact) / math.log(MAX_DISTANCE / max_exact)
                         * (NUM_BUCKETS - max_exact)).astype(np.int32)
    return np.where(n < max_exact, n, np.minimum(large, NUM_BUCKETS - 1))


assert (_t5_bucket_np(np.arange(ATT_TILE + 1, 1 << 16)) == NUM_BUCKETS - 1).all()


NB_SAME, NB_PREV, NB_FAR, NB_EDGE, NB_NONE = range(5)
FAR_TILE = 512


def bias_lookup(rel_bias, rel):
    bucket = t5_bucket(rel)
    tbl = rel_bias.astype(F32)
    shape = (tbl.shape[1],) + (1,) * rel.ndim
    out = jnp.zeros((tbl.shape[1],) + rel.shape, F32)
    for b in range(NUM_BUCKETS):
        out = jnp.where(bucket == b, tbl[b].reshape(shape), out)
    return out


def near_bias_tables(rel_bias):
    t = ATT_TILE
    i = jnp.arange(t)[:, None]
    j = jnp.arange(t)[None, :]
    diag = jnp.where(i >= j, bias_lookup(rel_bias, i - j), MASKED)
    prev = bias_lookup(rel_bias, t + i - j)
    far = jnp.broadcast_to(rel_bias.astype(F32)[NUM_BUCKETS - 1][:, None, None], prev.shape)
    edge = jnp.where(j > i, far, MASKED)
    return jnp.stack([diag, prev, far, edge, jnp.full_like(far, MASKED)])


def cmp_bias_table(rel_bias, q0, tq, n_chunk):
    first_end = CMP_LEN - 1
    span = CMP_STRIDE * (n_chunk - 1)
    n_rel = tq + span
    rel = q0 - first_end - span + jnp.arange(n_rel)
    vals = jnp.where(rel >= 0, bias_lookup(rel_bias, rel), MASKED)
    vals = jnp.roll(vals, -span, axis=1)
    flat = jnp.tile(vals, (1, n_chunk))[:, :n_chunk * (n_rel - CMP_STRIDE)]
    by_token = flat.reshape(-1, n_chunk, n_rel - CMP_STRIDE)[:, :, :tq]
    return jnp.swapaxes(by_token, 1, 2)


def overlap_matrix(n_chunk, n_sel):
    c0 = np.arange(n_chunk)[:, None] * CMP_STRIDE
    s0 = np.arange(LANES)[None, :] * SEL_BLOCK
    ov = np.clip(np.minimum(c0 + CMP_LEN, s0 + SEL_BLOCK) - np.maximum(c0, s0), 0, None) / CMP_LEN
    ov = np.where(np.arange(LANES)[None, :] < n_sel, ov, 0.0)
    return jnp.asarray(ov, F32)


def expand_matrix(n_keys):
    e = (np.arange(n_keys)[None, :] // SEL_BLOCK) == np.arange(LANES)[:, None]
    return jnp.asarray(e, BF16)


def _head_queries(q_ref, tq):
    low = lax.broadcasted_iota(jnp.int32, (tq, LANES), 1) < HEAD_DIM
    zero = jnp.zeros((tq, LANES), BF16)
    parts = []
    for kv in range(NSA_KV_HEADS):
        for g in range(NSA_GROUP):
            blk = q_ref[:, g * LANES:(g + 1) * LANES]
            parts.append(jnp.where(low if kv == 0 else ~low, blk, zero))
    return parts


def _stack_queries(q_ref, tq):
    return jnp.concatenate(_head_queries(q_ref, tq), axis=0)


def _dot_nt(a, b):
    return lax.dot_general(a, b, (((1,), (1,)), ((), ())), preferred_element_type=F32)


def _selection_mask(p_group, ov_ref, ex_ref, q0, n_sel, n_top):
    tq = p_group.shape[0]
    n_row = -(-n_sel // 8) * 8
    imp = lax.dot_general(ov_ref[...], p_group, (((0,), (1,)), ((), ())), precision=lax.Precision.HIGHEST,
                          preferred_element_type=F32)[0:n_row]
    blk = lax.broadcasted_iota(jnp.int32, (n_row, tq), 0)
    q_pos = q0 + lax.broadcasted_iota(jnp.int32, (n_row, tq), 1)
    q_blk = q_pos // SEL_BLOCK
    forced = jnp.where(blk == 0, 1.0, jnp.where(blk == q_blk, 1.0, jnp.where(blk == q_blk - 1, 1.0, 0.0)))
    score = jnp.where(blk * SEL_BLOCK <= q_pos, imp, NEG_INF)
    score = jnp.where(forced > 0.5, 1e9, score)
    score = jnp.where(blk < n_sel, score, -3e38)
    rank = jnp.zeros((n_row, tq), F32)
    for s in range(n_sel):
        other = score[s:s + 1, :]
        tie = jnp.where(blk > s, 1.0, 0.0)
        rank = rank + jnp.where(other > score, 1.0, jnp.where(other == score, tie, 0.0))
    sel = jnp.where(rank < n_top, 1.0, 0.0).astype(BF16)
    keys = lax.dot_general(sel, ex_ref[0:n_row, :], (((0,), (0,)), ((), ())), preferred_element_type=F32)
    return (keys - 1.0) * (-MASKED)


def _gate_columns(gl_ref, j):
    g = jax.nn.sigmoid(gl_ref[...])
    cols = [g[:, SM_GATE + 3 * h + j:SM_GATE + 3 * h + j + 1] for h in range(NSA_HEADS)]
    return jnp.stack(cols, axis=0)


def _nsa_prompt_body(q_ref, kc_ref, vc_ref, sk_ref, sv_ref, wk_ref, wv_ref, bc_ref, nb_ref, gl_ref, ov_ref, ex_ref,
                     o_ref, mb_ref, s_ref, sw_ref, stat_ref, mrep_ref, acc_ref, out_ref, *, n_sel, n_top):
    i = pl.program_id(1)
    tq = q_ref.shape[0]
    h = NSA_HEADS
    q = _stack_queries(q_ref, tq)

    s = _dot_nt(q, kc_ref[0]).reshape(h, tq, -1) + bc_ref[...]
    m = jnp.max(s, axis=-1, keepdims=True)
    p = jnp.where(s > 0.5 * MASKED, jnp.exp(s - m), 0.0)
    l = jnp.sum(p, axis=-1, keepdims=True)
    p = p * (1.0 / jnp.where(l > 0.0, l, 1.0))
    o_cmp = jnp.dot(p.reshape(h * tq, -1).astype(BF16), vc_ref[0], preferred_element_type=F32)
    out_ref[...] = _gate_columns(gl_ref, 0) * o_cmp.reshape(h, tq, LANES)

    for kv in range(NSA_KV_HEADS):
        pk = p[kv * NSA_GROUP]
        for g in range(1, NSA_GROUP):
            pk = pk + p[kv * NSA_GROUP + g]
        mb_ref[kv] = _selection_mask(pk, ov_ref, ex_ref, i * tq, n_sel, n_top)

    tk = ATT_TILE
    n_back = WINDOW // tk
    wide = (n_back + 1) * tk
    far_blk = FAR_TILE // tk
    n_tile = sk_ref.shape[0] // tk
    tail_col = s_ref.shape[-1] - wide
    kvg = lambda x: x.reshape(NSA_KV_HEADS, NSA_GROUP, tq, x.shape[-1])
    blocks = lambda x: [x[..., c * LANES:(c + 1) * LANES] for c in range(x.shape[-1] // LANES)]
    lane_max = lambda x: functools.reduce(jnp.maximum, blocks(x))
    lane_sum = lambda x: functools.reduce(jnp.add, blocks(x))
    rep = lambda x, n: jnp.concatenate([x] * n, axis=-1)
    far_bias = nb_ref[NB_FAR][:, 0:1, 0:1]

    n_far = jnp.maximum(i - 1, 0) // far_blk
    tail0 = jnp.minimum(n_far * far_blk, n_tile - (n_back + 1))
    tail_rows = pl.ds(pl.multiple_of(tail0 * tk, tk), wide)
    tiles = []
    for c in range(n_back + 1):
        blk = tail0 + c
        kind = jnp.where((blk < n_far * far_blk) | (blk > i), NB_NONE, jnp.minimum(i - blk, NB_FAR))
        tiles.append(nb_ref[kind])
    s = _dot_nt(q, sk_ref[tail_rows, :]).reshape(h, tq, wide) + jnp.concatenate(tiles, axis=-1)
    s = (kvg(s) + mb_ref[:, :, tail_rows][:, None]).reshape(h, tq, wide)
    s_ref[:, :, tail_col:tail_col + wide] = s
    stat_ref[0] = lane_max(s)

    win0 = jnp.maximum(i - n_back, 0)
    win_rows = pl.ds(pl.multiple_of(win0 * tk, tk), wide)
    tiles = []
    for c in range(n_back + 1):
        r = i - (win0 + c)
        tiles.append(nb_ref[jnp.where(r < 0, NB_NONE, jnp.where(r == n_back, NB_EDGE, jnp.minimum(r, NB_FAR)))])
    s = _dot_nt(q, wk_ref[win_rows, :]).reshape(h, tq, wide) + jnp.concatenate(tiles, axis=-1)
    sw_ref[...] = s
    stat_ref[1] = lane_max(s)

    def far_scores(f, carry):
        rows = pl.ds(pl.multiple_of(f * FAR_TILE, FAR_TILE), FAR_TILE)
        s = _dot_nt(q, sk_ref[rows, :]).reshape(h, tq, FAR_TILE) + far_bias
        s = (kvg(s) + mb_ref[:, :, rows][:, None]).reshape(h, tq, FAR_TILE)
        s_ref[:, :, rows] = s
        stat_ref[0] = jnp.maximum(stat_ref[0], lane_max(s))
        return carry
    lax.fori_loop(0, n_far, far_scores, 0)

    for br in range(2):
        mrep_ref[br] = jnp.broadcast_to(jnp.max(stat_ref[br], axis=-1, keepdims=True), (h, tq, LANES))
        stat_ref[br] = jnp.zeros((h, tq, LANES), F32)
    acc_ref[...] = jnp.zeros(acc_ref.shape, F32)

    def far_values(f, carry):
        rows = pl.ds(pl.multiple_of(f * FAR_TILE, FAR_TILE), FAR_TILE)
        p = jnp.exp(s_ref[:, :, rows] - rep(mrep_ref[0], far_blk))
        stat_ref[0] += lane_sum(p)
        acc_ref[...] += jnp.dot(p.reshape(h * tq, FAR_TILE).astype(BF16), sv_ref[rows, :],
                                preferred_element_type=F32).reshape(h, tq, LANES)
        return carry
    lax.fori_loop(0, n_far, far_values, 0)

    p = jnp.exp(s_ref[:, :, tail_col:tail_col + wide] - rep(mrep_ref[0], n_back + 1))
    l = jnp.sum(stat_ref[0] + lane_sum(p), axis=-1, keepdims=True)
    o_sel = acc_ref[...] + jnp.dot(p.reshape(h * tq, wide).astype(BF16), sv_ref[tail_rows, :],
                                   preferred_element_type=F32).reshape(h, tq, LANES)
    out_ref[...] += (_gate_columns(gl_ref, 1) * (1.0 / l)) * o_sel

    p = jnp.exp(sw_ref[...] - rep(mrep_ref[1], n_back + 1))
    l = jnp.sum(lane_sum(p), axis=-1, keepdims=True)
    o_win = jnp.dot(p.reshape(h * tq, wide).astype(BF16), wv_ref[win_rows, :],
                    preferred_element_type=F32).reshape(h, tq, LANES)
    out_ref[...] += (_gate_columns(gl_ref, 2) * (1.0 / l)) * o_win

    low = lax.broadcasted_iota(jnp.int32, (tq, LANES), 1) < HEAD_DIM
    for g in range(NSA_GROUP):
        o_ref[:, g * LANES:(g + 1) * LANES] = jnp.where(low, out_ref[g], out_ref[NSA_GROUP + g]).astype(BF16)


def nsa_prompt(q, kc, vc, skb, svb, wkb, wvb, small, rel_bias, B, T):
    tq = ATT_TILE
    nq = T // tq
    assert T % FAR_TILE == 0 and T >= WINDOW + ATT_TILE and WINDOW % ATT_TILE == 0
    wide = WINDOW + ATT_TILE
    far_max = (nq - 2) // (FAR_TILE // ATT_TILE) * FAR_TILE
    n_chunk = kc.shape[1]
    n_sel = -(-T // SEL_BLOCK)
    n_top = min(SEL_TOP, n_sel)
    bias_c = cmp_bias_table(rel_bias, 0, T, n_chunk)
    nb = near_bias_tables(rel_bias)
    ov = overlap_matrix(n_chunk, n_sel)
    ex = expand_matrix(T)
    h = NSA_HEADS
    full = lambda a: pl.BlockSpec(a.shape, lambda b, i: (0,) * a.ndim)
    seq = pl.BlockSpec((T, NSA_KV_DIM), lambda b, i: (b, 0))
    tok = pl.BlockSpec((1, n_chunk, NSA_KV_DIM), lambda b, i: (b, 0, 0))
    return pl.pallas_call(
        functools.partial(_nsa_prompt_body, n_sel=n_sel, n_top=n_top),
        grid=(B, nq),
        in_specs=[pl.BlockSpec((tq, NSA_Q_DIM), lambda b, i: (b * nq + i, 0)),
                  tok, tok, seq, seq, seq, seq,
                  pl.BlockSpec((h, tq, n_chunk), lambda b, i: (0, i, 0)),
                  full(nb),
                  pl.BlockSpec((tq, LANES), lambda b, i: (b * nq + i, 0)),
                  full(ov), full(ex)],
        out_specs=pl.BlockSpec((tq, NSA_Q_DIM), lambda b, i: (b * nq + i, 0)),
        out_shape=jax.ShapeDtypeStruct((B * T, NSA_Q_DIM), BF16),
        scratch_shapes=[pltpu.VMEM((NSA_KV_HEADS, tq, T), F32),
                        pltpu.VMEM((h, tq, far_max + wide), F32), pltpu.VMEM((h, tq, wide), F32),
                        pltpu.VMEM((2, h, tq, LANES), F32), pltpu.VMEM((2, h, tq, LANES), F32),
                        pltpu.VMEM((h, tq, LANES), F32), pltpu.VMEM((h, tq, LANES), F32)],
        compiler_params=pltpu.CompilerParams(dimension_semantics=("parallel", "arbitrary"),
                                             vmem_limit_bytes=VMEM_LIMIT),
        name="nsa_prompt",
    )(q, kc, vc, skb, svb, wkb, wvb, bias_c, nb, small, ov, ex)


def sample_bias_tables(rel_bias, past, T, W):
    t = jnp.arange(T)[:, None]
    pad = jnp.arange(ATT_TILE)[None, :]
    new = jnp.where((pad < T) & (t >= pad), bias_lookup(rel_bias, t - pad), MASKED)
    sel = bias_lookup(rel_bias, past + t - jnp.arange(past)[None, :])
    rel_w = W + t - jnp.arange(W)[None, :]
    win = jnp.where(rel_w < WINDOW, bias_lookup(rel_bias, rel_w), MASKED)
    cmp = cmp_bias_table(rel_bias, past, T, past // CMP_STRIDE)
    return cmp, jnp.concatenate([sel, new], axis=-1), jnp.concatenate([win, new], axis=-1)


def _softmax_rows(s):
    m = jnp.max(s, axis=-1, keepdims=True)
    p = jnp.exp(s - m)
    return p * (1.0 / jnp.sum(p, axis=-1, keepdims=True))


def _nsa_sample_body(pt_ref, q_ref, skn_ref, svn_ref, wkn_ref, wvn_ref, gl_ref, wink_ref, winv_ref,
                     bc_ref, bs_ref, bw_ref, ov_ref, ex_ref, pk_ref, pv_ref, w1k_ref, w1v_ref, w2k_ref, w2v_ref,
                     *rest, n_pages, n_sel, n_top, past):
    pages = [rest[i * n_pages:(i + 1) * n_pages] for i in range(4)]
    o_ref, wko_ref, wvo_ref, s_ref, new_ref, rows_ref = rest[4 * n_pages:]
    h = NSA_HEADS
    tq = q_ref.shape[0]
    tk = ATT_TILE
    q = _stack_queries(q_ref, tq)

    new_ref[...] = jnp.zeros(new_ref.shape, F32)
    for n, r in enumerate((skn_ref, svn_ref, wkn_ref, wvn_ref)):
        new_ref[n, 0:tq, :] = r[...]

    for pg in range(n_pages):
        rows_ref[0, pg * tk:(pg + 1) * tk, :] = pages[0][pg][0].T
        rows_ref[1, pg * tk:(pg + 1) * tk, :] = pages[1][pg][0].T
    kc = _compress_one(rows_ref.at[0], pk_ref, w1k_ref, w2k_ref).astype(BF16)
    vc = _compress_one(rows_ref.at[1], pv_ref, w1v_ref, w2v_ref).astype(BF16)
    s = _dot_nt(q, kc).reshape(h, tq, -1) + bc_ref[...]
    p = _softmax_rows(s)
    o_cmp = jnp.dot(p.reshape(h * tq, -1).astype(BF16), vc, preferred_element_type=F32)
    out = _gate_columns(gl_ref, 0) * o_cmp.reshape(h, tq, LANES)

    masks = []
    for kv in range(NSA_KV_HEADS):
        pkv = p[kv * NSA_GROUP]
        for g in range(1, NSA_GROUP):
            pkv = pkv + p[kv * NSA_GROUP + g]
        masks.append(_selection_mask(pkv, ov_ref, ex_ref, past, n_sel, n_top))
    mask = jnp.stack(masks, axis=0)
    for pg in range(n_pages):
        s_ref[:, pg * tk:(pg + 1) * tk] = jnp.dot(q, pages[2][pg][0].astype(BF16), preferred_element_type=F32)
    s_ref[:, n_pages * tk:(n_pages + 1) * tk] = _dot_nt(q, new_ref[0].astype(BF16))
    width = (n_pages + 1) * tk
    s = s_ref[...].reshape(h, tq, width) + bs_ref[...]
    s = (s.reshape(NSA_KV_HEADS, NSA_GROUP, tq, width) + mask[:, None]).reshape(h, tq, width)
    p = _softmax_rows(s).reshape(h * tq, width).astype(BF16)
    o_sel = jnp.dot(p[:, n_pages * tk:], new_ref[1].astype(BF16), preferred_element_type=F32)
    for pg in range(n_pages):
        o_sel = o_sel + _dot_nt(p[:, pg * tk:(pg + 1) * tk], pages[3][pg][0].astype(BF16))
    out = out + _gate_columns(gl_ref, 1) * o_sel.reshape(h, tq, LANES)

    w = wink_ref.shape[2]
    s_ref[:, 0:w] = jnp.dot(q, wink_ref[0].astype(BF16), preferred_element_type=F32)
    s_ref[:, w:w + tk] = _dot_nt(q, new_ref[2].astype(BF16))
    s = s_ref[:, 0:w + tk].reshape(h, tq, w + tk) + bw_ref[...]
    p = _softmax_rows(s).reshape(h * tq, w + tk).astype(BF16)
    o_win = jnp.dot(p[:, w:], new_ref[3].astype(BF16), preferred_element_type=F32) \
        + _dot_nt(p[:, 0:w], winv_ref[0].astype(BF16))
    out = out + _gate_columns(gl_ref, 2) * o_win.reshape(h, tq, LANES)

    low = lax.broadcasted_iota(jnp.int32, (tq, LANES), 1) < HEAD_DIM
    for g in range(NSA_GROUP):
        o_ref[:, g * LANES:(g + 1) * LANES] = jnp.where(low, out[g], out[NSA_GROUP + g])

    keep = w - tq
    lane_w = lax.broadcasted_iota(jnp.int32, (NSA_KV_DIM, w), 1)
    for old_ref, new_rows_ref, dst_ref in ((wink_ref, wkn_ref, wko_ref), (winv_ref, wvn_ref, wvo_ref)):
        shifted = pltpu.roll(old_ref[0], shift=keep, axis=1)
        tail = jnp.concatenate([jnp.zeros((tk - tq, NSA_KV_DIM), F32), new_rows_ref[...]], axis=0).T
        tail = jnp.concatenate([jnp.zeros((NSA_KV_DIM, w - tk), F32), tail], axis=1)
        dst_ref[0] = jnp.where(lane_w >= keep, tail, shifted)


def nsa_sample(q, skn, svn, wkn, wvn, small, pools, win_k, win_v, page_table, rel_bias, pk, pv, B, T):
    n_pages = page_table.shape[1]
    past = n_pages * PAGE_SIZE
    W = win_k.shape[1]
    L = past + T
    n_cmp = (L - CMP_LEN) // CMP_STRIDE + 1
    assert PAGE_SIZE == ATT_TILE and T % 8 == 0 and T <= SEL_BLOCK and past % SEL_BLOCK == 0
    assert (n_cmp - 1) * CMP_STRIDE + CMP_LEN <= past and n_cmp >= past // CMP_STRIDE - CMP_HALVES + 1
    assert W % ATT_TILE == 0 and W + T >= WINDOW
    n_sel = -(-L // SEL_BLOCK)
    n_top = min(SEL_TOP, n_sel)
    w_keep = min(WINDOW, W + T)
    bc, bs, bw = sample_bias_tables(rel_bias, past, T, W)
    ov = overlap_matrix(past // CMP_STRIDE, n_sel)
    ex = expand_matrix(past + ATT_TILE)
    n_pool = pools[0].shape[0]
    assert W == w_keep
    feat_major = lambda a: a.transpose(0, 2, 3, 1).reshape(a.shape[0], NSA_KV_DIM, a.shape[1])
    cmp_pools = [feat_major(p) for p in pools[:2]]
    sel_pools = [feat_major(p) for p in pools[2:]]
    wk3 = feat_major(win_k)
    wv3 = feat_major(win_v)

    full = lambda a: pl.BlockSpec(a.shape, lambda b, pt: (0,) * a.ndim)
    row = lambda w: pl.BlockSpec((T, w), lambda b, pt: (b, 0))
    seq = lambda n: pl.BlockSpec((1, NSA_KV_DIM, n), lambda b, pt: (b, 0, 0))
    page = lambda shape, pg: pl.BlockSpec((1,) + shape, lambda b, pt: (pt[b, pg], 0, 0))
    page_specs, page_args = [], []
    for pool in cmp_pools + sel_pools:
        for pg in range(n_pages):
            page_specs.append(page((NSA_KV_DIM, PAGE_SIZE), pg))
            page_args.append(pool)
    tables = (bc, bs, bw, ov, ex, pk[0], pv[0], pk[1], pv[1], pk[2], pv[2])
    grid_spec = pltpu.PrefetchScalarGridSpec(
        num_scalar_prefetch=1,
        grid=(B,),
        in_specs=[row(NSA_Q_DIM), row(NSA_KV_DIM), row(NSA_KV_DIM), row(NSA_KV_DIM), row(NSA_KV_DIM), row(LANES),
                  seq(W), seq(W)] + [full(a) for a in tables] + page_specs,
        out_specs=[row(NSA_Q_DIM), seq(w_keep), seq(w_keep)],
        scratch_shapes=[pltpu.VMEM((NSA_HEADS * T, past + ATT_TILE), F32),
                        pltpu.VMEM((4, ATT_TILE, NSA_KV_DIM), F32),
                        pltpu.VMEM((2, past, NSA_KV_DIM), F32)],
    )
    o, wk_next, wv_next = pl.pallas_call(
        functools.partial(_nsa_sample_body, n_pages=n_pages, n_sel=n_sel, n_top=n_top, past=past),
        grid_spec=grid_spec,
        out_shape=[jax.ShapeDtypeStruct((B * T, NSA_Q_DIM), F32),
                   jax.ShapeDtypeStruct((B, NSA_KV_DIM, w_keep), F32),
                   jax.ShapeDtypeStruct((B, NSA_KV_DIM, w_keep), F32)],
        compiler_params=pltpu.CompilerParams(dimension_semantics=("arbitrary",), vmem_limit_bytes=VMEM_LIMIT),
        name="nsa_sample",
    )(page_table, q, skn, svn, wkn, wvn, small, wk3, wv3, *tables, *page_args)
    row_major = lambda a: a.reshape(B, NSA_KV_HEADS, HEAD_DIM, w_keep).transpose(0, 3, 1, 2)
    return o, row_major(wk_next), row_major(wv_next)


HIST_ROWS = 8
DN_CHUNKS_PER_STEP = 4
DN_SEQS_PER_STEP = 8
DN_INV_BASE = 8


def _split_bf16(x):
    hi = x.astype(BF16)
    lo = (x - hi.astype(F32)).astype(BF16)
    return hi, lo


def _bmm(a, b):
    return jnp.einsum('hij,hjk->hik', a, b, preferred_element_type=F32)


def _bmm_nt(a, b):
    return jnp.einsum('hid,hjd->hij', a, b, preferred_element_type=F32)


def _cumsum_rows(tri, x):
    hi, lo = _split_bf16(x)
    lo2 = (x - hi.astype(F32) - lo.astype(F32)).astype(BF16)
    d = lambda y: jnp.dot(tri, y, preferred_element_type=F32)
    return d(hi) + (d(lo) + d(lo2))


def _softplus(x):
    return jnp.maximum(x, 0.0) + jnp.log1p(jnp.exp(-jnp.abs(x)))


def _silu(x):
    return x * jax.nn.sigmoid(x)


def _deltanet_body(x_ref, sm_ref, dg_ref, hist_ref, s0_ref, cw_ref, apar_ref, dt_ref, ng_ref,
                   o_ref, s_out_ref, hist_out_ref, xbuf_ref, s_ref, *, c, n_seq):
    j = pl.program_id(1)
    rows = x_ref.shape[0]
    seq_rows = rows // n_seq
    n_grp = seq_rows // c
    hd = DN_HEAD_DIM
    n_hist = CONV_W - 1

    n_pair = DN_HEADS // 2
    zero_blk = jnp.zeros((hd, hd), F32)

    @pl.when(j == 0)
    def _():
        xbuf_ref[:, HIST_ROWS - n_hist:HIST_ROWS, :] = hist_ref[...]
        for sq in range(n_seq):
            for p in range(n_pair):
                s_ref[sq * n_pair + p] = jnp.concatenate(
                    [jnp.concatenate([s0_ref[sq, 2 * p], zero_blk], axis=1),
                     jnp.concatenate([zero_blk, s0_ref[sq, 2 * p + 1]], axis=1)], axis=0)

    xbuf_ref[:, HIST_ROWS:HIST_ROWS + seq_rows, :] = x_ref[...].reshape(n_seq, seq_rows, x_ref.shape[1])
    y = None
    for t in range(CONV_W):
        lo = HIST_ROWS - n_hist + t
        term = xbuf_ref[:, lo:lo + seq_rows, :] * cw_ref[t:t + 1, :]
        y = term if y is None else y + term
    y = _silu(y).reshape(rows, x_ref.shape[1])
    tail = xbuf_ref[:, HIST_ROWS + seq_rows - n_hist:HIST_ROWS + seq_rows, :]
    xbuf_ref[:, HIST_ROWS - n_hist:HIST_ROWS, :] = tail

    sm = sm_ref[...]
    beta_all = jax.nn.sigmoid(sm)
    g_all = apar_ref[...] * _softplus(sm + dt_ref[...])
    row = lax.broadcasted_iota(jnp.int32, (rows, rows), 0)
    col = lax.broadcasted_iota(jnp.int32, (rows, rows), 1)
    tri = jnp.where((row >= col) & (row // c == col // c), 1.0, 0.0).astype(BF16)
    gcum_all = _cumsum_rows(tri, g_all)
    gcum_t = gcum_all.T

    wd, wj = 2 * hd, 2 * c
    first_d = lax.broadcasted_iota(jnp.int32, (1, wd), 1) < hd
    first_j = lax.broadcasted_iota(jnp.int32, (1, wj), 1) < c
    rj = lax.broadcasted_iota(jnp.int32, (c, wj), 0)
    cj = lax.broadcasted_iota(jnp.int32, (c, wj), 1)
    cj = jnp.where(cj < c, cj, cj - c)
    incl, strict = rj >= cj, rj > cj
    eye = jnp.where(rj == cj, 1.0, 0.0)
    same_head = (lax.broadcasted_iota(jnp.int32, (wd, wd), 0) < hd) == (lax.broadcasted_iota(jnp.int32, (wd, wd), 1) < hd)
    ones_bd = jnp.where(same_head, 1.0, 0.0).astype(BF16)
    first_rows = lax.broadcasted_iota(jnp.int32, (wd, 1), 0) < hd

    items = [(g, p) for g in range(n_seq * n_grp) for p in range(n_pair)]

    def block_diag(x, first):
        zero = jnp.zeros_like(x)
        return jnp.concatenate([jnp.where(first, x, zero), jnp.where(first, zero, x)], axis=1)

    def head_sums(x):
        n = x.shape[0]
        hi, lo = _split_bf16(x.reshape(n * c, wd))
        tot = jnp.dot(hi, ones_bd, preferred_element_type=F32) + jnp.dot(lo, ones_bd, preferred_element_type=F32)
        return tot.reshape(n, c, wd)

    def mm_hi(a_parts, b_parts):
        return _bmm(a_parts[0], b_parts[0]) + (_bmm(a_parts[0], b_parts[1]) + _bmm(a_parts[1], b_parts[0]))

    def per_head(cols, base, first):
        return jnp.stack([jnp.where(first, cols[g * c:(g + 1) * c, base + 2 * p:base + 2 * p + 1],
                                    cols[g * c:(g + 1) * c, base + 2 * p + 1:base + 2 * p + 2])
                          for g, p in items], axis=0)

    lanes = lambda x, off: jnp.stack([x[g * c:(g + 1) * c, off + p * wd:off + (p + 1) * wd] for g, p in items], axis=0)
    q = lanes(y, 0)
    k = lanes(y, DN_DIM)
    v = lanes(y, 2 * DN_DIM)
    q = q * lax.rsqrt(head_sums(q * q) + EPS) * hd ** -0.5
    k = k * lax.rsqrt(head_sums(k * k) + EPS)
    beta = per_head(beta_all, SM_BETA, first_d)
    gc_d = per_head(gcum_all, SM_DECAY, first_d)
    gc_j = per_head(gcum_all, SM_DECAY, first_j)
    gr_j = jnp.stack([jnp.concatenate([gcum_t[SM_DECAY + 2 * p:SM_DECAY + 2 * p + 1, g * c:(g + 1) * c],
                                       gcum_t[SM_DECAY + 2 * p + 1:SM_DECAY + 2 * p + 2, g * c:(g + 1) * c]], axis=1)
                      for g, p in items], axis=0)
    gl_d = gc_d[:, c - 1:c, :]
    decay = jnp.where(incl, jnp.exp(jnp.where(incl, gc_j - gr_j, 0.0)), 0.0)
    eg = jnp.exp(gc_d)
    kb = k * beta
    k_bd = block_diag(k.astype(BF16), first_d)
    a = jnp.where(strict, _bmm_nt(kb.astype(BF16), k_bd) * decay, 0.0)
    split_bd = lambda parts, first: tuple(block_diag(x, first) for x in parts)
    base = min(c, DN_INV_BASE)
    same_block = lambda b: (rj // b) == (cj // b)
    diag = jnp.where(same_block(base), a, 0.0)
    t_inv = eye - diag
    pow_parts = _split_bf16(diag)
    pow_bd = split_bd(pow_parts, first_j)
    for _ in range(int(math.log2(base)) - 1):
        pow_parts = _split_bf16(mm_hi(pow_parts, pow_bd))
        pow_bd = split_bd(pow_parts, first_j)
        t_inv = t_inv + mm_hi(_split_bf16(t_inv), pow_bd)
    blk = base
    while blk < c:
        lower = jnp.where(same_block(2 * blk) & jnp.logical_not(same_block(blk)), a, 0.0)
        t_parts = _split_bf16(t_inv)
        lt = mm_hi(_split_bf16(lower), split_bd(t_parts, first_j))
        t_inv = t_inv - mm_hi(t_parts, split_bd(_split_bf16(lt), first_j))
        blk *= 2
    vb_parts = split_bd(_split_bf16(v * beta), first_d)
    kbg_parts = split_bd(_split_bf16(kb * eg), first_d)
    rhs = tuple(jnp.concatenate([x, z], axis=2) for x, z in zip(vb_parts, kbg_parts))
    uw = mm_hi(_split_bf16(t_inv), rhs)
    u, w = uw[:, :, :wd], uw[:, :, wd:]
    qk = jnp.where(incl, _bmm_nt(q.astype(BF16), k_bd) * decay, 0.0).astype(BF16)
    wq = jnp.concatenate([w, q * eg], axis=1).astype(BF16)
    kg = (k * jnp.exp(gl_d - gc_d)).astype(BF16)
    gl_rows = jnp.where(first_rows, jnp.exp(gl_d[:, :, 0:1]), jnp.exp(gl_d[:, :, hd:hd + 1]))

    s = s_ref[...]
    outs = {}
    for kc in range(n_grp):
        pick = lambda x: jnp.concatenate([x[(sq * n_grp + kc) * n_pair:(sq * n_grp + kc + 1) * n_pair]
                                          for sq in range(n_seq)], axis=0)
        ws = _bmm(pick(wq), s.astype(BF16))
        vn = (pick(u) - ws[:, :c]).astype(BF16)
        o_k = ws[:, c:] + _bmm(pick(qk), block_diag(vn, first_d))
        kv_outer = jnp.einsum('pck,pcv->pkv', pick(kg), vn, preferred_element_type=F32)
        s = s * pick(gl_rows) + jnp.where(same_head, kv_outer, 0.0)
        for sq in range(n_seq):
            outs[sq * n_grp + kc] = o_k[sq * n_pair:(sq + 1) * n_pair]
    s_ref[...] = s
    o = jnp.concatenate([outs[g] for g in range(n_seq * n_grp)], axis=0)
    ng = jnp.concatenate([ng_ref[...], ng_ref[...]], axis=1)
    o = o * lax.rsqrt(head_sums(o * o) * (1.0 / hd) + EPS) * ng
    o = jnp.concatenate([jnp.concatenate([o[g * n_pair + p] for p in range(n_pair)], axis=1)
                         for g in range(n_seq * n_grp)], axis=0)
    o_ref[...] = (o * _silu(dg_ref[...])).astype(o_ref.dtype)

    @pl.when(j == pl.num_programs(1) - 1)
    def _():
        for sq in range(n_seq):
            for p in range(n_pair):
                s_pair = s_ref[sq * n_pair + p]
                s_out_ref[sq, 2 * p] = s_pair[0:hd, 0:hd]
                s_out_ref[sq, 2 * p + 1] = s_pair[hd:2 * hd, hd:2 * hd]
        hist_out_ref[...] = tail


def deltanet(qkv_raw, small, dg, conv_hist, s0, lp, B, T):
    c = DN_CHUNK if T >= DN_CHUNK else T
    assert T % c == 0 and c % 8 == 0 and c >= CONV_W - 1 and (c & (c - 1)) == 0
    n_grp = math.gcd(T // c, DN_CHUNKS_PER_STEP)
    n_seq = math.gcd(B, DN_SEQS_PER_STEP) if T == c else 1
    seq_rows = n_grp * c
    step = n_seq * seq_rows
    n = T // seq_rows
    put = lambda vals, base: jnp.zeros((LANES,), F32).at[base + jnp.arange(DN_HEADS)].set(vals).reshape(1, LANES)
    apar = put(-jnp.exp(lp['dn_a_log'].astype(F32)), SM_DECAY)
    dtb = put(lp['dn_dt_bias'].astype(F32), SM_DECAY)
    assert n_seq == 1 or n == 1
    rows = lambda w: pl.BlockSpec((step, w), lambda b, j: (b * n + j, 0))
    per_b = lambda a: pl.BlockSpec((n_seq,) + a.shape[1:], lambda b, j: (b,) + (0,) * (a.ndim - 1))
    full = lambda a: pl.BlockSpec(a.shape, lambda b, j: (0,) * a.ndim)
    ng = lp['dn_norm_g'].reshape(1, DN_HEAD_DIM).astype(F32)
    return pl.pallas_call(
        functools.partial(_deltanet_body, c=c, n_seq=n_seq),
        grid=(B // n_seq, n),
        in_specs=[rows(DN_CONV_DIM), rows(LANES), rows(DN_DIM), per_b(conv_hist), per_b(s0),
                  full(lp['conv_w']), full(apar), full(dtb), full(ng)],
        out_specs=[rows(DN_DIM), per_b(s0), per_b(conv_hist)],
        out_shape=[jax.ShapeDtypeStruct((B * T, DN_DIM), BF16 if step % 16 == 0 else F32),
                   jax.ShapeDtypeStruct(s0.shape, F32),
                   jax.ShapeDtypeStruct(conv_hist.shape, F32)],
        scratch_shapes=[pltpu.VMEM((n_seq, HIST_ROWS + seq_rows, DN_CONV_DIM), F32),
                        pltpu.VMEM((n_seq * (DN_HEADS // 2), 2 * DN_HEAD_DIM, 2 * DN_HEAD_DIM), F32)],
        compiler_params=pltpu.CompilerParams(dimension_semantics=("parallel", "arbitrary"),
                                             vmem_limit_bytes=VMEM_LIMIT),
        name="deltanet",
    )(qkv_raw, small, dg, conv_hist, s0, lp['conv_w'], apar, dtb, ng)


RT_GROUP = 0
RT_EXPERT = N_GROUPS


def pack_router(wg, bg, we, be):
    w = jnp.concatenate([wg, we], axis=1)
    b = jnp.concatenate([bg, be], axis=0)
    pad = LANES - w.shape[1]
    return jnp.pad(w, ((0, 0), (0, pad))).astype(BF16), jnp.pad(b, (0, pad)).reshape(1, LANES).astype(F32)


def _route(r):
    lane = lax.broadcasted_iota(jnp.int32, r.shape, 1).astype(F32)
    big = float(LANES)
    is_grp = lane < N_GROUPS
    lg = jnp.where(is_grp, r, -jnp.inf)
    mg = jnp.max(lg, axis=-1, keepdims=True)
    eg = jnp.where(is_grp, jnp.exp(lg - mg), 0.0)
    grp = jnp.min(jnp.where(lg == mg, lane, big), axis=-1, keepdims=True)
    wg = 1.0 / jnp.sum(eg, axis=-1, keepdims=True)
    first = RT_EXPERT + grp * EXPERTS_PER_GROUP
    in_grp = (lane >= first) & (lane < first + EXPERTS_PER_GROUP)
    le = jnp.where(in_grp, r, -jnp.inf)
    me = jnp.max(le, axis=-1, keepdims=True)
    ee = jnp.where(in_grp, jnp.exp(le - me), 0.0)
    pe = jnp.where(in_grp, ee / jnp.sum(ee, axis=-1, keepdims=True), -1.0)
    v1 = jnp.max(pe, axis=-1, keepdims=True)
    i1 = jnp.min(jnp.where(pe == v1, lane, big), axis=-1, keepdims=True)
    pe2 = jnp.where(lane == i1, -1.0, pe)
    v2 = jnp.max(pe2, axis=-1, keepdims=True)
    i2 = jnp.min(jnp.where(pe2 == v2, lane, big), axis=-1, keepdims=True)
    tot = v1 + v2
    return jnp.where(lane == i1, wg * (v1 / tot), jnp.where(lane == i2, wg * (v2 / tot), 0.0))


def _finish_body(x_ref, on_ref, od_ref, g1_ref, sc_ref, sh_ref, g2_ref, won_ref, wod_ref, ng_ref, wr_ref, br_ref,
                 wgu_ref, wd_ref, fg_ref, o_ref, x1_ref, h_ref, gate_ref, y_ref):
    e = pl.program_id(1)
    tm, d = x_ref.shape
    nb = g1_ref.shape[0]
    per_batch = lambda a: a.reshape(nb, tm // nb, d)

    @pl.when(e == 0)
    def _():
        mix = jnp.dot(on_ref[...].astype(BF16), won_ref[...], preferred_element_type=F32) \
            + jnp.dot(od_ref[...].astype(BF16), wod_ref[...], preferred_element_type=F32)
        x1 = per_batch(x_ref[...]) + g1_ref[...] * per_batch(mix)
        x1 = x1.reshape(tm, d)
        x1_ref[...] = x1
        xn = x1 * lax.rsqrt(jnp.mean(x1 * x1, axis=-1, keepdims=True) + EPS) * ng_ref[...]
        h = (per_batch(xn) * (1 + sc_ref[...]) + sh_ref[...]).reshape(tm, d).astype(BF16)
        h_ref[...] = h
        gate_ref[...] = _route(jnp.dot(h, wr_ref[...], preferred_element_type=F32) + br_ref[...])
        y_ref[...] = jnp.zeros(y_ref.shape, F32)

    h = h_ref[...]
    gu = jnp.dot(h, wgu_ref[0], preferred_element_type=F32)
    he = _silu(gu[:, :D_EXPERT]) * gu[:, D_EXPERT:]
    out = jnp.dot(he.astype(BF16), wd_ref[0], preferred_element_type=F32)
    lane = lax.broadcasted_iota(jnp.int32, gate_ref.shape, 1)
    g_col = jnp.sum(jnp.where(lane == RT_EXPERT + e, gate_ref[...], 0.0), axis=-1, keepdims=True)
    y_ref[...] += g_col * out

    @pl.when(e == pl.num_programs(1) - 1)
    def _():
        x2 = per_batch(x1_ref[...]) + g2_ref[...] * per_batch(y_ref[...])
        x2 = x2.reshape(tm, d)
        o_ref[...] = x2 * lax.rsqrt(jnp.mean(x2 * x2, axis=-1, keepdims=True) + EPS) * fg_ref[...]


def finish_layer(x, o_nsa, o_dn, g1, sc2, sh2, g2, packed, norm_ffn_g, final_g, *, tm=1024):
    B, T, D = x.shape
    n = B * T
    tm = min(tm, n)
    rows = max(tm // T, 1)
    per = max(T // tm, 1)
    row = lambda w: pl.BlockSpec((tm, w), lambda i, e: (i, 0))
    mod = pl.BlockSpec((rows, 1, D), lambda i, e: (i // per, 0, 0))
    full = lambda a: pl.BlockSpec(a.shape, lambda i, e: (0,) * a.ndim)
    expert = lambda a: pl.BlockSpec((1,) + a.shape[1:], lambda i, e: (e, 0, 0))
    vec = lambda v: v.reshape(1, D).astype(F32)
    args = (x.reshape(n, D), o_nsa, o_dn, g1, sc2, sh2, g2, packed['w_out_nsa'], packed['w_out_dn'],
            vec(norm_ffn_g), packed['router_w'], packed['router_b'], packed['w_gate_up'], packed['w_down'],
            vec(final_g))
    specs = [row(D), row(o_nsa.shape[1]), row(o_dn.shape[1]), mod, mod, mod, mod] + \
            [full(a) for a in args[7:12]] + [expert(args[12]), expert(args[13]), full(args[14])]
    return pl.pallas_call(
        _finish_body,
        grid=(n // tm, N_EXPERTS),
        in_specs=specs,
        out_specs=row(D),
        out_shape=jax.ShapeDtypeStruct((n, D), F32),
        scratch_shapes=[pltpu.VMEM((tm, D), F32), pltpu.VMEM((tm, D), BF16), pltpu.VMEM((tm, LANES), F32),
                        pltpu.VMEM((tm, D), F32)],
        compiler_params=pltpu.CompilerParams(dimension_semantics=("parallel", "arbitrary"),
                                             vmem_limit_bytes=VMEM_LIMIT),
        name="finish_layer",
    )(*args).reshape(B, T, D)


def permute_w_out_nsa(w_out):
    w = w_out[:NSA_Q_DIM].reshape(NSA_KV_HEADS, NSA_GROUP, HEAD_DIM, -1)
    return w.transpose(1, 0, 2, 3).reshape(NSA_Q_DIM, -1).astype(BF16)


def layer_prompt(x, mod, lp, rel_bias, packed):
    B, T, _ = x.shape
    sh1, sc1, g1, sh2, sc2, g2 = mod
    (q, ck, cv, sk, sv, wk, wv, skb, svb, wkb, wvb, qkv_raw, dg, small, ck_rows, cv_rows) = in_projection(
        x, lp['norm_mix_g'], sc1, sh1, packed['w_in'], feature_major=True)
    kc, vc = compress_tokens(ck_rows, cv_rows, packed['cmp_k'], packed['cmp_v'], B)
    o_nsa = nsa_prompt(q, kc, vc, skb, svb, wkb, wvb, small, rel_bias, B, T)
    kv = lambda r: r.reshape(B, NSA_KV_HEADS, HEAD_DIM, -1).transpose(0, 3, 1, 2)
    w_keep = min(WINDOW, T)
    ck, cv, sk, sv, wk, wv = kv(ck), kv(cv), kv(sk), kv(sv), kv(wk[:, :, T - w_keep:]), kv(wv[:, :, T - w_keep:])
    conv_hist = jnp.zeros((B, CONV_W - 1, DN_CONV_DIM), x.dtype)
    s0 = jnp.zeros((B, DN_HEADS, DN_HEAD_DIM, DN_HEAD_DIM), x.dtype)
    o_dn, s_new, conv_new = deltanet(qkv_raw, small, dg, conv_hist, s0, lp, B, T)
    y = finish_layer(x, o_nsa, o_dn, g1, sc2, sh2, g2, packed, lp['norm_ffn_g'], packed['final_norm_g'])
    return y, (ck, cv, sk, sv, wk, wv, s_new, conv_new)


def layer_sample(x, mod, cmp_k_pool, cmp_v_pool, sel_k_pool, sel_v_pool, win_k, win_v, s0, conv_hist,
                 page_table, lp, rel_bias, packed):
    B, T, _ = x.shape
    sh1, sc1, g1, sh2, sc2, g2 = mod
    (q, ck, cv, sk, sv, wk, wv, _, _, _, _, qkv_raw, dg, small) = in_projection(
        x, lp['norm_mix_g'], sc1, sh1, packed['w_in'])
    o_nsa, wk_next, wv_next = nsa_sample(q, sk, sv, wk, wv, small, (cmp_k_pool, cmp_v_pool, sel_k_pool, sel_v_pool),
                                         win_k, win_v, page_table, rel_bias, packed['cmp_k'], packed['cmp_v'], B, T)
    kv = lambda r: r.reshape(B, -1, NSA_KV_HEADS, HEAD_DIM)
    o_dn, s_new, conv_new = deltanet(qkv_raw, small, dg, conv_hist, s0, lp, B, T)
    y = finish_layer(x, o_nsa, o_dn, g1, sc2, sh2, g2, packed, lp['norm_ffn_g'], packed['final_norm_g'])
    return y, (kv(ck), kv(cv), kv(sk), kv(sv), kv(wk_next), kv(wv_next), s_new, conv_new)


def kernel(x_prompt, x_sample, cache_cmp_k, cache_cmp_v, cache_sel_k, cache_sel_v, cache_win_k, cache_win_v,
           state_delta, state_conv, page_table, c_prompt, c_sample, rel_bias, w_ada, b_ada, norm_mix_g,
           norm_ffn_g, w_in, cmp_pos_k, cmp_w1_k, cmp_w2_k, cmp_pos_v, cmp_w1_v, cmp_w2_v, conv_w, dn_a_log,
           dn_dt_bias, dn_norm_g, w_out, router_group_w, router_group_b, router_expert_w, router_expert_b,
           expert_w_gate, expert_w_up, expert_w_down, final_norm_g):
    xp, xs = x_prompt, x_sample
    new_p, new_s = [], []
    for l in range(DEPTH):
        lp = dict(norm_mix_g=norm_mix_g[l], norm_ffn_g=norm_ffn_g[l], conv_w=conv_w[l],
                  dn_a_log=dn_a_log[l], dn_dt_bias=dn_dt_bias[l], dn_norm_g=dn_norm_g[l])
        packed = dict(w_in=pack_w_in(w_in[l]),
                      cmp_k=pack_compress(cmp_pos_k[l], cmp_w1_k[l], cmp_w2_k[l]),
                      cmp_v=pack_compress(cmp_pos_v[l], cmp_w1_v[l], cmp_w2_v[l]),
                      w_out_nsa=permute_w_out_nsa(w_out[l]),
                      w_out_dn=w_out[l][NSA_Q_DIM:].astype(BF16),
                      w_gate_up=jnp.concatenate([expert_w_gate[l], expert_w_up[l]], axis=-1).astype(BF16),
                      w_down=expert_w_down[l].astype(BF16),
                      final_norm_g=final_norm_g)
        packed['router_w'], packed['router_b'] = pack_router(router_group_w[l], router_group_b[l],
                                                             router_expert_w[l], router_expert_b[l])
        n_p = c_prompt.shape[0]
        mod = modulation(jnp.concatenate([c_prompt, c_sample], axis=0), w_ada[l], b_ada[l])
        xp, st_p = layer_prompt(xp, [m[:n_p] for m in mod], lp, rel_bias, packed)
        xs, st_s = layer_sample(xs, [m[n_p:] for m in mod], cache_cmp_k[l], cache_cmp_v[l], cache_sel_k[l],
                                cache_sel_v[l],
                                cache_win_k[l], cache_win_v[l], state_delta[l], state_conv[l], page_table,
                                lp, rel_bias, packed)
        new_p.append(st_p)
        new_s.append(st_s)
    assert DEPTH == 1
    y_prompt, y_sample = xp, xs

    def stk(states, i):
        return jnp.stack([s[i] for s in states])

    return (y_prompt, y_sample,
            stk(new_p, 0), stk(new_p, 1), stk(new_p, 2), stk(new_p, 3),
            stk(new_p, 4), stk(new_p, 5), stk(new_p, 6), stk(new_p, 7),
            stk(new_s, 0), stk(new_s, 1), stk(new_s, 2), stk(new_s, 3),
            stk(new_s, 4), stk(new_s, 5), stk(new_s, 6), stk(new_s, 7))
```

```python
import functools
import math

import jax
import jax.numpy as jnp
import numpy as np
from jax import lax
from jax.experimental import pallas as pl
from jax.experimental.pallas import tpu as pltpu

D_MODEL = 1024
DEPTH = 1
PAGE_SIZE = 128

NSA_HEADS = 8
NSA_KV_HEADS = 2
NSA_GROUP = NSA_HEADS // NSA_KV_HEADS
HEAD_DIM = 64
CMP_LEN = 32
CMP_STRIDE = 16
CMP_HIDDEN = 4 * HEAD_DIM
SEL_BLOCK = 64
SEL_TOP = 16
WINDOW = 512
DN_HEADS = 8
DN_HEAD_DIM = 64
CONV_W = 4
DN_CHUNK = 64
NUM_BUCKETS = 32
MAX_DISTANCE = 128
N_GROUPS = 4
EXPERTS_PER_GROUP = 8
N_EXPERTS = N_GROUPS * EXPERTS_PER_GROUP
TOP_K_IN_GROUP = 2
D_EXPERT = D_MODEL // 4

EPS = 1e-6
NEG_INF = -1e30
F32 = jnp.float32
BF16 = jnp.bfloat16

NSA_Q_DIM = NSA_HEADS * HEAD_DIM
NSA_KV_DIM = NSA_KV_HEADS * HEAD_DIM
DN_DIM = DN_HEADS * DN_HEAD_DIM
DN_CONV_DIM = 3 * DN_DIM
MIX_WIDTH = NSA_Q_DIM + DN_DIM
IN_SPLITS = (NSA_Q_DIM, NSA_KV_DIM, NSA_KV_DIM, NSA_KV_DIM, NSA_KV_DIM, NSA_KV_DIM, NSA_KV_DIM,
             3 * NSA_HEADS, DN_CONV_DIM, DN_HEADS, DN_HEADS, DN_DIM)
IN_DIM = sum(IN_SPLITS)

LANES = 128
VMEM_LIMIT = 56 * 1024 * 1024

ATT_TILE = 128
MASKED = -1e30


def t5_bucket(rel):
    n = jnp.maximum(rel, 0)
    max_exact = NUM_BUCKETS // 2
    nf = jnp.maximum(n, 1).astype(F32)
    large = max_exact + (jnp.log(nf / max_exact) / math.log(MAX_DISTANCE / max_exact)
                         * (NUM_BUCKETS - max_exact)).astype(jnp.int32)
    large = jnp.minimum(large, NUM_BUCKETS - 1)
    return jnp.where(n < max_exact, n, large)


def _modulation_body(c_ref, w_ref, b_ref, o_ref):
    c = c_ref[...]
    x = (c * jax.nn.sigmoid(c)).astype(BF16)
    o_ref[...] = jnp.dot(x, w_ref[...].astype(BF16), preferred_element_type=F32) + b_ref[...]


def modulation(c, w_ada, b_ada, *, tn=1024):
    rows, d = c.shape
    n = w_ada.shape[1]
    mod = pl.pallas_call(
        _modulation_body,
        grid=(n // tn,),
        in_specs=[pl.BlockSpec((rows, d), lambda j: (0, 0)),
                  pl.BlockSpec((d, tn), lambda j: (0, j)),
                  pl.BlockSpec((1, tn), lambda j: (0, j))],
        out_specs=pl.BlockSpec((rows, tn), lambda j: (0, j)),
        out_shape=jax.ShapeDtypeStruct((rows, n), F32),
        compiler_params=pltpu.CompilerParams(dimension_semantics=("parallel",), vmem_limit_bytes=VMEM_LIMIT),
        name="modulation",
    )(c, w_ada, b_ada.reshape(1, n))
    return jnp.split(mod[:, None, :], 6, axis=-1)


_PK_Q = (0, NSA_Q_DIM)
_PK_KV = (_PK_Q[1], _PK_Q[1] + 6 * NSA_KV_DIM)
_PK_QKV = (_PK_KV[1], _PK_KV[1] + DN_CONV_DIM)
_PK_DG = (_PK_QKV[1], _PK_QKV[1] + DN_DIM)
_PK_SM = (_PK_DG[1], _PK_DG[1] + LANES)
PK_DIM = _PK_SM[1]
SM_GATE = 0
SM_BETA = 3 * NSA_HEADS
SM_DECAY = SM_BETA + DN_HEADS


def pack_w_in(w_in):
    offs = np.cumsum((0,) + IN_SPLITS)
    part = lambda i: w_in[:, offs[i]:offs[i + 1]]
    d = w_in.shape[0]
    q = part(0).reshape(d, NSA_KV_HEADS, NSA_GROUP, HEAD_DIM).transpose(0, 2, 1, 3).reshape(d, NSA_Q_DIM)
    q = q * HEAD_DIM ** -0.5
    small = jnp.concatenate([part(7), part(9), part(10)], axis=1)
    small = jnp.pad(small, ((0, 0), (0, LANES - small.shape[1])))
    cols = [q] + [part(i) for i in range(1, 7)] + [part(8), part(11), small]
    return jnp.concatenate(cols, axis=1).astype(BF16)


def _in_proj_body(x_ref, g_ref, sc_ref, sh_ref, w_ref, q_ref, ck_ref, cv_ref, sk_ref, sv_ref, wk_ref, wv_ref,
                  skb_ref, svb_ref, wkb_ref, wvb_ref, qkv_ref, dg_ref, sm_ref, *row_refs):
    x = x_ref[...]
    tm, d = x.shape
    nb = sc_ref.shape[0]
    xn = x * lax.rsqrt(jnp.mean(x * x, axis=-1, keepdims=True) + EPS) * g_ref[...]
    h = xn.reshape(nb, tm // nb, d) * (1 + sc_ref[...]) + sh_ref[...]
    h = h.reshape(tm, d).astype(BF16)

    def mm(span):
        return jnp.dot(h, w_ref[:, span[0]:span[1]], preferred_element_type=F32)

    q_ref[...] = mm(_PK_Q).astype(BF16)
    kv = mm(_PK_KV)
    for i, r in enumerate((ck_ref, cv_ref, sk_ref, sv_ref, wk_ref, wv_ref)):
        rows = kv[:, i * NSA_KV_DIM:(i + 1) * NSA_KV_DIM]
        if row_refs:
            r[0] = rows.T
        else:
            r[...] = rows
    for i, r in enumerate(row_refs):
        r[...] = kv[:, i * NSA_KV_DIM:(i + 1) * NSA_KV_DIM]
    for i, r in enumerate((skb_ref, svb_ref, wkb_ref, wvb_ref)):
        r[...] = kv[:, (i + 2) * NSA_KV_DIM:(i + 3) * NSA_KV_DIM].astype(BF16)
    qkv_ref[...] = mm(_PK_QKV)
    dg_ref[...] = mm(_PK_DG)
    sm_ref[...] = mm(_PK_SM)


def in_projection(x, norm_g, scale, shift, w_packed, *, tm=256, feature_major=False):
    B, T, D = x.shape
    n = B * T
    tm = min(tm, n)
    rows = max(tm // T, 1)
    per = max(T // tm, 1)
    row = lambda i: (i, 0)
    by_row = lambda w, dt: (((n, w), dt), pl.BlockSpec((tm, w), row))
    if feature_major:
        assert tm <= T and tm % LANES == 0
        kv_out = (((B, NSA_KV_DIM, T), F32), pl.BlockSpec((1, NSA_KV_DIM, tm), lambda i: (i // per, 0, i % per)))
    else:
        kv_out = by_row(NSA_KV_DIM, F32)
    outs = [by_row(NSA_Q_DIM, BF16)] + [kv_out] * 6 + [by_row(NSA_KV_DIM, BF16)] * 4 + \
           [by_row(DN_CONV_DIM, F32), by_row(DN_DIM, F32), by_row(LANES, F32)] + \
           ([by_row(NSA_KV_DIM, F32)] * 2 if feature_major else [])
    return pl.pallas_call(
        _in_proj_body,
        grid=(n // tm,),
        in_specs=[pl.BlockSpec((tm, D), row),
                  pl.BlockSpec((1, D), lambda i: (0, 0)),
                  pl.BlockSpec((rows, 1, D), lambda i: (i // per, 0, 0)),
                  pl.BlockSpec((rows, 1, D), lambda i: (i // per, 0, 0)),
                  pl.BlockSpec((D, PK_DIM), lambda i: (0, 0))],
        out_specs=[spec for _, spec in outs],
        out_shape=[jax.ShapeDtypeStruct(s, dt) for (s, dt), _ in outs],
        compiler_params=pltpu.CompilerParams(dimension_semantics=("parallel",), vmem_limit_bytes=VMEM_LIMIT),
        name="in_projection",
    )(x.reshape(n, D), norm_g.reshape(1, D), scale, shift, w_packed)


CHUNK_W = CMP_STRIDE * NSA_KV_DIM
CMP_HALVES = CMP_LEN // CMP_STRIDE


def pack_compress(pos, w1, w2):
    eye = jnp.eye(NSA_KV_HEADS, dtype=F32)
    w1r = w1.reshape(CMP_HALVES, CMP_STRIDE, HEAD_DIM, CMP_HIDDEN)
    w1p = jnp.einsum('ijdc,kl->ijkdlc', w1r, eye).reshape(CMP_HALVES, CHUNK_W, NSA_KV_HEADS * CMP_HIDDEN)
    w2p = jnp.einsum('cd,kl->kcld', w2, eye).reshape(NSA_KV_HEADS * CMP_HIDDEN, NSA_KV_DIM)
    posp = jnp.broadcast_to(pos.reshape(CMP_HALVES, CMP_STRIDE, 1, HEAD_DIM),
                            (CMP_HALVES, CMP_STRIDE, NSA_KV_HEADS, HEAD_DIM)).reshape(CMP_HALVES, CHUNK_W)
    return posp, w1p.astype(BF16), w2p.astype(BF16)


def _compress_one(rows_ref, pos_ref, w1_ref, w2_ref):
    n_chunk = rows_ref.shape[0] // CMP_STRIDE
    x = jnp.concatenate([rows_ref[pl.ds(j, n_chunk, stride=CMP_STRIDE), :] for j in range(CMP_STRIDE)], axis=1)
    hid = None
    for i in range(CMP_HALVES):
        part = jnp.dot((x + pos_ref[i:i + 1]).astype(BF16), w1_ref[i], preferred_element_type=F32)
        if i:
            part = pltpu.roll(part, shift=n_chunk - i, axis=0)
        hid = part if hid is None else hid + part
    hid = hid * jax.nn.sigmoid(hid)
    return jnp.dot(hid.astype(BF16), w2_ref[...], preferred_element_type=F32)


def _compress_body(xk_ref, xv_ref, pk_ref, pv_ref, w1k_ref, w1v_ref, w2k_ref, w2v_ref, kc_ref, vc_ref):
    kc_ref[0] = _compress_one(xk_ref, pk_ref, w1k_ref, w2k_ref).astype(BF16)
    vc_ref[0] = _compress_one(xv_ref, pv_ref, w1v_ref, w2v_ref).astype(BF16)


def compress_tokens(ck, cv, pk, pv, B):
    L = ck.shape[0] // B
    n_chunk = L // CMP_STRIDE
    full = lambda a: pl.BlockSpec(a.shape, lambda b: (0,) * a.ndim)
    rows = pl.BlockSpec((L, NSA_KV_DIM), lambda b: (b, 0))
    tok = pl.BlockSpec((1, n_chunk, NSA_KV_DIM), lambda b: (b, 0, 0))
    return pl.pallas_call(
        _compress_body,
        grid=(B,),
        in_specs=[rows, rows, full(pk[0]), full(pv[0]), full(pk[1]), full(pv[1]), full(pk[2]), full(pv[2])],
        out_specs=[tok, tok],
        out_shape=[jax.ShapeDtypeStruct((B, n_chunk, NSA_KV_DIM), BF16)] * 2,
        compiler_params=pltpu.CompilerParams(dimension_semantics=("parallel",), vmem_limit_bytes=VMEM_LIMIT),
        name="compress_tokens",
    )(ck, cv, pk[0], pv[0], pk[1], pv[1], pk[2], pv[2])


def _t5_bucket_np(rel):
    n = np.maximum(rel, 0)
    max_exact = NUM_BUCKETS // 2
    nf = np.maximum(n, 1).astype(np.float32)
    large = max_exact + (np.log(nf / max_exact) / math.log(MAX_DISTANCE / max_exact)
                         * (NUM_BUCKETS - max_exact)).astype(np.int32)
    return np.where(n < max_exact, n, np.minimum(large, NUM_BUCKETS - 1))


assert (_t5_bucket_np(np.arange(ATT_TILE + 1, 1 << 16)) == NUM_BUCKETS - 1).all()


NB_SAME, NB_PREV, NB_FAR, NB_EDGE, NB_NONE = range(5)
FAR_TILE = 512


def bias_lookup(rel_bias, rel):
    bucket = t5_bucket(rel)
    tbl = rel_bias.astype(F32)
    shape = (tbl.shape[1],) + (1,) * rel.ndim
    out = jnp.zeros((tbl.shape[1],) + rel.shape, F32)
    for b in range(NUM_BUCKETS):
        out = jnp.where(bucket == b, tbl[b].reshape(shape), out)
    return out


def near_bias_tables(rel_bias):
    t = ATT_TILE
    i = jnp.arange(t)[:, None]
    j = jnp.arange(t)[None, :]
    diag = jnp.where(i >= j, bias_lookup(rel_bias, i - j), MASKED)
    prev = bias_lookup(rel_bias, t + i - j)
    far = jnp.broadcast_to(rel_bias.astype(F32)[NUM_BUCKETS - 1][:, None, None], prev.shape)
    edge = jnp.where(j > i, far, MASKED)
    return jnp.stack([diag, prev, far, edge, jnp.full_like(far, MASKED)])


def cmp_bias_table(rel_bias, q0, tq, n_chunk):
    first_end = CMP_LEN - 1
    span = CMP_STRIDE * (n_chunk - 1)
    n_rel = tq + span
    rel = q0 - first_end - span + jnp.arange(n_rel)
    vals = jnp.where(rel >= 0, bias_lookup(rel_bias, rel), MASKED)
    vals = jnp.roll(vals, -span, axis=1)
    flat = jnp.tile(vals, (1, n_chunk))[:, :n_chunk * (n_rel - CMP_STRIDE)]
    by_token = flat.reshape(-1, n_chunk, n_rel - CMP_STRIDE)[:, :, :tq]
    return jnp.swapaxes(by_token, 1, 2)


def overlap_matrix(n_chunk, n_sel):
    c0 = np.arange(n_chunk)[:, None] * CMP_STRIDE
    s0 = np.arange(LANES)[None, :] * SEL_BLOCK
    ov = np.clip(np.minimum(c0 + CMP_LEN, s0 + SEL_BLOCK) - np.maximum(c0, s0), 0, None) / CMP_LEN
    ov = np.where(np.arange(LANES)[None, :] < n_sel, ov, 0.0)
    return jnp.asarray(ov, F32)


def expand_matrix(n_keys):
    e = (np.arange(n_keys)[None, :] // SEL_BLOCK) == np.arange(LANES)[:, None]
    return jnp.asarray(e, BF16)


def _head_queries(q_ref, tq):
    low = lax.broadcasted_iota(jnp.int32, (tq, LANES), 1) < HEAD_DIM
    zero = jnp.zeros((tq, LANES), BF16)
    parts = []
    for kv in range(NSA_KV_HEADS):
        for g in range(NSA_GROUP):
            blk = q_ref[:, g * LANES:(g + 1) * LANES]
            parts.append(jnp.where(low if kv == 0 else ~low, blk, zero))
    return parts


def _stack_queries(q_ref, tq):
    return jnp.concatenate(_head_queries(q_ref, tq), axis=0)


def _dot_nt(a, b):
    return lax.dot_general(a, b, (((1,), (1,)), ((), ())), preferred_element_type=F32)


def _selection_mask(p_group, ov_ref, ex_ref, q0, n_sel, n_top):
    tq = p_group.shape[0]
    n_row = -(-n_sel // 8) * 8
    imp = lax.dot_general(ov_ref[...], p_group, (((0,), (1,)), ((), ())), precision=lax.Precision.HIGHEST,
                          preferred_element_type=F32)[0:n_row]
    blk = lax.broadcasted_iota(jnp.int32, (n_row, tq), 0)
    q_pos = q0 + lax.broadcasted_iota(jnp.int32, (n_row, tq), 1)
    q_blk = q_pos // SEL_BLOCK
    forced = jnp.where(blk == 0, 1.0, jnp.where(blk == q_blk, 1.0, jnp.where(blk == q_blk - 1, 1.0, 0.0)))
    score = jnp.where(blk * SEL_BLOCK <= q_pos, imp, NEG_INF)
    score = jnp.where(forced > 0.5, 1e9, score)
    score = jnp.where(blk < n_sel, score, -3e38)
    rank = jnp.zeros((n_row, tq), F32)
    for s in range(n_sel):
        other = score[s:s + 1, :]
        tie = jnp.where(blk > s, 1.0, 0.0)
        rank = rank + jnp.where(other > score, 1.0, jnp.where(other == score, tie, 0.0))
    sel = jnp.where(rank < n_top, 1.0, 0.0).astype(BF16)
    keys = lax.dot_general(sel, ex_ref[0:n_row, :], (((0,), (0,)), ((), ())), preferred_element_type=F32)
    return (keys - 1.0) * (-MASKED)


def _gate_columns(gl_ref, j):
    g = jax.nn.sigmoid(gl_ref[...])
    cols = [g[:, SM_GATE + 3 * h + j:SM_GATE + 3 * h + j + 1] for h in range(NSA_HEADS)]
    return jnp.stack(cols, axis=0)


def _nsa_prompt_body(q_ref, kc_ref, vc_ref, sk_ref, sv_ref, wk_ref, wv_ref, bc_ref, nb_ref, gl_ref, ov_ref, ex_ref,
                     o_ref, mb_ref, s_ref, sw_ref, stat_ref, mrep_ref, acc_ref, out_ref, *, n_sel, n_top):
    i = pl.program_id(1)
    tq = q_ref.shape[0]
    h = NSA_HEADS
    q = _stack_queries(q_ref, tq)

    s = _dot_nt(q, kc_ref[0]).reshape(h, tq, -1) + bc_ref[...]
    m = jnp.max(s, axis=-1, keepdims=True)
    p = jnp.where(s > 0.5 * MASKED, jnp.exp(s - m), 0.0)
    l = jnp.sum(p, axis=-1, keepdims=True)
    p = p * (1.0 / jnp.where(l > 0.0, l, 1.0))
    o_cmp = jnp.dot(p.reshape(h * tq, -1).astype(BF16), vc_ref[0], preferred_element_type=F32)
    out_ref[...] = _gate_columns(gl_ref, 0) * o_cmp.reshape(h, tq, LANES)

    for kv in range(NSA_KV_HEADS):
        pk = p[kv * NSA_GROUP]
        for g in range(1, NSA_GROUP):
            pk = pk + p[kv * NSA_GROUP + g]
        mb_ref[kv] = _selection_mask(pk, ov_ref, ex_ref, i * tq, n_sel, n_top)

    tk = ATT_TILE
    n_back = WINDOW // tk
    wide = (n_back + 1) * tk
    far_blk = FAR_TILE // tk
    n_tile = sk_ref.shape[0] // tk
    tail_col = s_ref.shape[-1] - wide
    kvg = lambda x: x.reshape(NSA_KV_HEADS, NSA_GROUP, tq, x.shape[-1])
    blocks = lambda x: [x[..., c * LANES:(c + 1) * LANES] for c in range(x.shape[-1] // LANES)]
    lane_max = lambda x: functools.reduce(jnp.maximum, blocks(x))
    lane_sum = lambda x: functools.reduce(jnp.add, blocks(x))
    rep = lambda x, n: jnp.concatenate([x] * n, axis=-1)
    far_bias = nb_ref[NB_FAR][:, 0:1, 0:1]

    n_far = jnp.maximum(i - 1, 0) // far_blk
    tail0 = jnp.minimum(n_far * far_blk, n_tile - (n_back + 1))
    tail_rows = pl.ds(pl.multiple_of(tail0 * tk, tk), wide)
    tiles = []
    for c in range(n_back + 1):
        blk = tail0 + c
        kind = jnp.where((blk < n_far * far_blk) | (blk > i), NB_NONE, jnp.minimum(i - blk, NB_FAR))
        tiles.append(nb_ref[kind])
    s = _dot_nt(q, sk_ref[tail_rows, :]).reshape(h, tq, wide) + jnp.concatenate(tiles, axis=-1)
    s = (kvg(s) + mb_ref[:, :, tail_rows][:, None]).reshape(h, tq, wide)
    s_ref[:, :, tail_col:tail_col + wide] = s
    stat_ref[0] = lane_max(s)

    win0 = jnp.maximum(i - n_back, 0)
    win_rows = pl.ds(pl.multiple_of(win0 * tk, tk), wide)
    tiles = []
    for c in range(n_back + 1):
        r = i - (win0 + c)
        tiles.append(nb_ref[jnp.where(r < 0, NB_NONE, jnp.where(r == n_back, NB_EDGE, jnp.minimum(r, NB_FAR)))])
    s = _dot_nt(q, wk_ref[win_rows, :]).reshape(h, tq, wide) + jnp.concatenate(tiles, axis=-1)
    sw_ref[...] = s
    stat_ref[1] = lane_max(s)

    def far_scores(f, carry):
        rows = pl.ds(pl.multiple_of(f * FAR_TILE, FAR_TILE), FAR_TILE)
        s = _dot_nt(q, sk_ref[rows, :]).reshape(h, tq, FAR_TILE) + far_bias
        s = (kvg(s) + mb_ref[:, :, rows][:, None]).reshape(h, tq, FAR_TILE)
        s_ref[:, :, rows] = s
        stat_ref[0] = jnp.maximum(stat_ref[0], lane_max(s))
        return carry
    lax.fori_loop(0, n_far, far_scores, 0)

    for br in range(2):
        mrep_ref[br] = jnp.broadcast_to(jnp.max(stat_ref[br], axis=-1, keepdims=True), (h, tq, LANES))
        stat_ref[br] = jnp.zeros((h, tq, LANES), F32)
    acc_ref[...] = jnp.zeros(acc_ref.shape, F32)

    def far_values(f, carry):
        rows = pl.ds(pl.multiple_of(f * FAR_TILE, FAR_TILE), FAR_TILE)
        p = jnp.exp(s_ref[:, :, rows] - rep(mrep_ref[0], far_blk))
        stat_ref[0] += lane_sum(p)
        acc_ref[...] += jnp.dot(p.reshape(h * tq, FAR_TILE).astype(BF16), sv_ref[rows, :],
                                preferred_element_type=F32).reshape(h, tq, LANES)
        return carry
    lax.fori_loop(0, n_far, far_values, 0)

    p = jnp.exp(s_ref[:, :, tail_col:tail_col + wide] - rep(mrep_ref[0], n_back + 1))
    l = jnp.sum(stat_ref[0] + lane_sum(p), axis=-1, keepdims=True)
    o_sel = acc_ref[...] + jnp.dot(p.reshape(h * tq, wide).astype(BF16), sv_ref[tail_rows, :],
                                   preferred_element_type=F32).reshape(h, tq, LANES)
    out_ref[...] += (_gate_columns(gl_ref, 1) * (1.0 / l)) * o_sel

    p = jnp.exp(sw_ref[...] - rep(mrep_ref[1], n_back + 1))
    l = jnp.sum(lane_sum(p), axis=-1, keepdims=True)
    o_win = jnp.dot(p.reshape(h * tq, wide).astype(BF16), wv_ref[win_rows, :],
                    preferred_element_type=F32).reshape(h, tq, LANES)
    out_ref[...] += (_gate_columns(gl_ref, 2) * (1.0 / l)) * o_win

    low = lax.broadcasted_iota(jnp.int32, (tq, LANES), 1) < HEAD_DIM
    for g in range(NSA_GROUP):
        o_ref[:, g * LANES:(g + 1) * LANES] = jnp.where(low, out_ref[g], out_ref[NSA_GROUP + g]).astype(BF16)


def nsa_prompt(q, kc, vc, skb, svb, wkb, wvb, small, rel_bias, B, T):
    tq = ATT_TILE
    nq = T // tq
    assert T % FAR_TILE == 0 and T >= WINDOW + ATT_TILE and WINDOW % ATT_TILE == 0
    wide = WINDOW + ATT_TILE
    far_max = (nq - 2) // (FAR_TILE // ATT_TILE) * FAR_TILE
    n_chunk = kc.shape[1]
    n_sel = -(-T // SEL_BLOCK)
    n_top = min(SEL_TOP, n_sel)
    bias_c = cmp_bias_table(rel_bias, 0, T, n_chunk)
    nb = near_bias_tables(rel_bias)
    ov = overlap_matrix(n_chunk, n_sel)
    ex = expand_matrix(T)
    h = NSA_HEADS
    full = lambda a: pl.BlockSpec(a.shape, lambda b, i: (0,) * a.ndim)
    seq = pl.BlockSpec((T, NSA_KV_DIM), lambda b, i: (b, 0))
    tok = pl.BlockSpec((1, n_chunk, NSA_KV_DIM), lambda b, i: (b, 0, 0))
    return pl.pallas_call(
        functools.partial(_nsa_prompt_body, n_sel=n_sel, n_top=n_top),
        grid=(B, nq),
        in_specs=[pl.BlockSpec((tq, NSA_Q_DIM), lambda b, i: (b * nq + i, 0)),
                  tok, tok, seq, seq, seq, seq,
                  pl.BlockSpec((h, tq, n_chunk), lambda b, i: (0, i, 0)),
                  full(nb),
                  pl.BlockSpec((tq, LANES), lambda b, i: (b * nq + i, 0)),
                  full(ov), full(ex)],
        out_specs=pl.BlockSpec((tq, NSA_Q_DIM), lambda b, i: (b * nq + i, 0)),
        out_shape=jax.ShapeDtypeStruct((B * T, NSA_Q_DIM), BF16),
        scratch_shapes=[pltpu.VMEM((NSA_KV_HEADS, tq, T), F32),
                        pltpu.VMEM((h, tq, far_max + wide), F32), pltpu.VMEM((h, tq, wide), F32),
                        pltpu.VMEM((2, h, tq, LANES), F32), pltpu.VMEM((2, h, tq, LANES), F32),
                        pltpu.VMEM((h, tq, LANES), F32), pltpu.VMEM((h, tq, LANES), F32)],
        compiler_params=pltpu.CompilerParams(dimension_semantics=("parallel", "arbitrary"),
                                             vmem_limit_bytes=VMEM_LIMIT),
        name="nsa_prompt",
    )(q, kc, vc, skb, svb, wkb, wvb, bias_c, nb, small, ov, ex)


def sample_bias_tables(rel_bias, past, T, W):
    t = jnp.arange(T)[:, None]
    pad = jnp.arange(ATT_TILE)[None, :]
    new = jnp.where((pad < T) & (t >= pad), bias_lookup(rel_bias, t - pad), MASKED)
    sel = bias_lookup(rel_bias, past + t - jnp.arange(past)[None, :])
    rel_w = W + t - jnp.arange(W)[None, :]
    win = jnp.where(rel_w < WINDOW, bias_lookup(rel_bias, rel_w), MASKED)
    cmp = cmp_bias_table(rel_bias, past, T, past // CMP_STRIDE)
    return cmp, jnp.concatenate([sel, new], axis=-1), jnp.concatenate([win, new], axis=-1)


def _softmax_rows(s):
    m = jnp.max(s, axis=-1, keepdims=True)
    p = jnp.exp(s - m)
    return p * (1.0 / jnp.sum(p, axis=-1, keepdims=True))


def _nsa_sample_body(pt_ref, q_ref, skn_ref, svn_ref, wkn_ref, wvn_ref, gl_ref, wink_ref, winv_ref,
                     bc_ref, bs_ref, bw_ref, ov_ref, ex_ref, pk_ref, pv_ref, w1k_ref, w1v_ref, w2k_ref, w2v_ref,
                     *rest, n_pages, n_sel, n_top, past):
    pages = [rest[i * n_pages:(i + 1) * n_pages] for i in range(4)]
    o_ref, wko_ref, wvo_ref, s_ref, new_ref, rows_ref = rest[4 * n_pages:]
    h = NSA_HEADS
    tq = q_ref.shape[0]
    tk = ATT_TILE
    q = _stack_queries(q_ref, tq)

    new_ref[...] = jnp.zeros(new_ref.shape, F32)
    for n, r in enumerate((skn_ref, svn_ref, wkn_ref, wvn_ref)):
        new_ref[n, 0:tq, :] = r[...]

    for pg in range(n_pages):
        rows_ref[0, pg * tk:(pg + 1) * tk, :] = pages[0][pg][0].T
        rows_ref[1, pg * tk:(pg + 1) * tk, :] = pages[1][pg][0].T
    kc = _compress_one(rows_ref.at[0], pk_ref, w1k_ref, w2k_ref).astype(BF16)
    vc = _compress_one(rows_ref.at[1], pv_ref, w1v_ref, w2v_ref).astype(BF16)
    s = _dot_nt(q, kc).reshape(h, tq, -1) + bc_ref[...]
    p = _softmax_rows(s)
    o_cmp = jnp.dot(p.reshape(h * tq, -1).astype(BF16), vc, preferred_element_type=F32)
    out = _gate_columns(gl_ref, 0) * o_cmp.reshape(h, tq, LANES)

    masks = []
    for kv in range(NSA_KV_HEADS):
        pkv = p[kv * NSA_GROUP]
        for g in range(1, NSA_GROUP):
            pkv = pkv + p[kv * NSA_GROUP + g]
        masks.append(_selection_mask(pkv, ov_ref, ex_ref, past, n_sel, n_top))
    mask = jnp.stack(masks, axis=0)
    for pg in range(n_pages):
        s_ref[:, pg * tk:(pg + 1) * tk] = jnp.dot(q, pages[2][pg][0].astype(BF16), preferred_element_type=F32)
    s_ref[:, n_pages * tk:(n_pages + 1) * tk] = _dot_nt(q, new_ref[0].astype(BF16))
    width = (n_pages + 1) * tk
    s = s_ref[...].reshape(h, tq, width) + bs_ref[...]
    s = (s.reshape(NSA_KV_HEADS, NSA_GROUP, tq, width) + mask[:, None]).reshape(h, tq, width)
    p = _softmax_rows(s).reshape(h * tq, width).astype(BF16)
    o_sel = jnp.dot(p[:, n_pages * tk:], new_ref[1].astype(BF16), preferred_element_type=F32)
    for pg in range(n_pages):
        o_sel = o_sel + _dot_nt(p[:, pg * tk:(pg + 1) * tk], pages[3][pg][0].astype(BF16))
    out = out + _gate_columns(gl_ref, 1) * o_sel.reshape(h, tq, LANES)

    w = wink_ref.shape[2]
    s_ref[:, 0:w] = jnp.dot(q, wink_ref[0].astype(BF16), preferred_element_type=F32)
    s_ref[:, w:w + tk] = _dot_nt(q, new_ref[2].astype(BF16))
    s = s_ref[:, 0:w + tk].reshape(h, tq, w + tk) + bw_ref[...]
    p = _softmax_rows(s).reshape(h * tq, w + tk).astype(BF16)
    o_win = jnp.dot(p[:, w:], new_ref[3].astype(BF16), preferred_element_type=F32) \
        + _dot_nt(p[:, 0:w], winv_ref[0].astype(BF16))
    out = out + _gate_columns(gl_ref, 2) * o_win.reshape(h, tq, LANES)

    low = lax.broadcasted_iota(jnp.int32, (tq, LANES), 1) < HEAD_DIM
    for g in range(NSA_GROUP):
        o_ref[:, g * LANES:(g + 1) * LANES] = jnp.where(low, out[g], out[NSA_GROUP + g])

    keep = w - tq
    lane_w = lax.broadcasted_iota(jnp.int32, (NSA_KV_DIM, w), 1)
    for old_ref, new_rows_ref, dst_ref in ((wink_ref, wkn_ref, wko_ref), (winv_ref, wvn_ref, wvo_ref)):
        shifted = pltpu.roll(old_ref[0], shift=keep, axis=1)
        tail = jnp.concatenate([jnp.zeros((tk - tq, NSA_KV_DIM), F32), new_rows_ref[...]], axis=0).T
        tail = jnp.concatenate([jnp.zeros((NSA_KV_DIM, w - tk), F32), tail], axis=1)
        dst_ref[0] = jnp.where(lane_w >= keep, tail, shifted)


def nsa_sample(q, skn, svn, wkn, wvn, small, pools, win_k, win_v, page_table, rel_bias, pk, pv, B, T):
    n_pages = page_table.shape[1]
    past = n_pages * PAGE_SIZE
    W = win_k.shape[1]
    L = past + T
    n_cmp = (L - CMP_LEN) // CMP_STRIDE + 1
    assert PAGE_SIZE == ATT_TILE and T % 8 == 0 and T <= SEL_BLOCK and past % SEL_BLOCK == 0
    assert (n_cmp - 1) * CMP_STRIDE + CMP_LEN <= past and n_cmp >= past // CMP_STRIDE - CMP_HALVES + 1
    assert W % ATT_TILE == 0 and W + T >= WINDOW
    n_sel = -(-L // SEL_BLOCK)
    n_top = min(SEL_TOP, n_sel)
    w_keep = min(WINDOW, W + T)
    bc, bs, bw = sample_bias_tables(rel_bias, past, T, W)
    ov = overlap_matrix(past // CMP_STRIDE, n_sel)
    ex = expand_matrix(past + ATT_TILE)
    n_pool = pools[0].shape[0]
    assert W == w_keep
    feat_major = lambda a: a.transpose(0, 2, 3, 1).reshape(a.shape[0], NSA_KV_DIM, a.shape[1])
    cmp_pools = [feat_major(p) for p in pools[:2]]
    sel_pools = [feat_major(p) for p in pools[2:]]
    wk3 = feat_major(win_k)
    wv3 = feat_major(win_v)

    full = lambda a: pl.BlockSpec(a.shape, lambda b, pt: (0,) * a.ndim)
    row = lambda w: pl.BlockSpec((T, w), lambda b, pt: (b, 0))
    seq = lambda n: pl.BlockSpec((1, NSA_KV_DIM, n), lambda b, pt: (b, 0, 0))
    page = lambda shape, pg: pl.BlockSpec((1,) + shape, lambda b, pt: (pt[b, pg], 0, 0))
    page_specs, page_args = [], []
    for pool in cmp_pools + sel_pools:
        for pg in range(n_pages):
            page_specs.append(page((NSA_KV_DIM, PAGE_SIZE), pg))
            page_args.append(pool)
    tables = (bc, bs, bw, ov, ex, pk[0], pv[0], pk[1], pv[1], pk[2], pv[2])
    grid_spec = pltpu.PrefetchScalarGridSpec(
        num_scalar_prefetch=1,
        grid=(B,),
        in_specs=[row(NSA_Q_DIM), row(NSA_KV_DIM), row(NSA_KV_DIM), row(NSA_KV_DIM), row(NSA_KV_DIM), row(LANES),
                  seq(W), seq(W)] + [full(a) for a in tables] + page_specs,
        out_specs=[row(NSA_Q_DIM), seq(w_keep), seq(w_keep)],
        scratch_shapes=[pltpu.VMEM((NSA_HEADS * T, past + ATT_TILE), F32),
                        pltpu.VMEM((4, ATT_TILE, NSA_KV_DIM), F32),
                        pltpu.VMEM((2, past, NSA_KV_DIM), F32)],
    )
    o, wk_next, wv_next = pl.pallas_call(
        functools.partial(_nsa_sample_body, n_pages=n_pages, n_sel=n_sel, n_top=n_top, past=past),
        grid_spec=grid_spec,
        out_shape=[jax.ShapeDtypeStruct((B * T, NSA_Q_DIM), F32),
                   jax.ShapeDtypeStruct((B, NSA_KV_DIM, w_keep), F32),
                   jax.ShapeDtypeStruct((B, NSA_KV_DIM, w_keep), F32)],
        compiler_params=pltpu.CompilerParams(dimension_semantics=("arbitrary",), vmem_limit_bytes=VMEM_LIMIT),
        name="nsa_sample",
    )(page_table, q, skn, svn, wkn, wvn, small, wk3, wv3, *tables, *page_args)
    row_major = lambda a: a.reshape(B, NSA_KV_HEADS, HEAD_DIM, w_keep).transpose(0, 3, 1, 2)
    return o, row_major(wk_next), row_major(wv_next)


HIST_ROWS = 8
DN_CHUNKS_PER_STEP = 4
DN_SEQS_PER_STEP = 8
DN_INV_BASE = 8


def _split_bf16(x):
    hi = x.astype(BF16)
    lo = (x - hi.astype(F32)).astype(BF16)
    return hi, lo


def _bmm(a, b):
    return jnp.einsum('hij,hjk->hik', a, b, preferred_element_type=F32)


def _bmm_nt(a, b):
    return jnp.einsum('hid,hjd->hij', a, b, preferred_element_type=F32)


def _cumsum_rows(tri, x):
    hi, lo = _split_bf16(x)
    lo2 = (x - hi.astype(F32) - lo.astype(F32)).astype(BF16)
    d = lambda y: jnp.dot(tri, y, preferred_element_type=F32)
    return d(hi) + (d(lo) + d(lo2))


def _softplus(x):
    return jnp.maximum(x, 0.0) + jnp.log1p(jnp.exp(-jnp.abs(x)))


def _silu(x):
    return x * jax.nn.sigmoid(x)


def _deltanet_body(x_ref, sm_ref, dg_ref, hist_ref, s0_ref, cw_ref, apar_ref, dt_ref, ng_ref,
                   o_ref, s_out_ref, hist_out_ref, xbuf_ref, s_ref, *, c, n_seq):
    j = pl.program_id(1)
    rows = x_ref.shape[0]
    seq_rows = rows // n_seq
    n_grp = seq_rows // c
    hd = DN_HEAD_DIM
    n_hist = CONV_W - 1

    n_pair = DN_HEADS // 2
    zero_blk = jnp.zeros((hd, hd), F32)

    @pl.when(j == 0)
    def _():
        xbuf_ref[:, HIST_ROWS - n_hist:HIST_ROWS, :] = hist_ref[...]
        for sq in range(n_seq):
            for p in range(n_pair):
                s_ref[sq * n_pair + p] = jnp.concatenate(
                    [jnp.concatenate([s0_ref[sq, 2 * p], zero_blk], axis=1),
                     jnp.concatenate([zero_blk, s0_ref[sq, 2 * p + 1]], axis=1)], axis=0)

    xbuf_ref[:, HIST_ROWS:HIST_ROWS + seq_rows, :] = x_ref[...].reshape(n_seq, seq_rows, x_ref.shape[1])
    y = None
    for t in range(CONV_W):
        lo = HIST_ROWS - n_hist + t
        term = xbuf_ref[:, lo:lo + seq_rows, :] * cw_ref[t:t + 1, :]
        y = term if y is None else y + term
    y = _silu(y).reshape(rows, x_ref.shape[1])
    tail = xbuf_ref[:, HIST_ROWS + seq_rows - n_hist:HIST_ROWS + seq_rows, :]
    xbuf_ref[:, HIST_ROWS - n_hist:HIST_ROWS, :] = tail

    sm = sm_ref[...]
    beta_all = jax.nn.sigmoid(sm)
    g_all = apar_ref[...] * _softplus(sm + dt_ref[...])
    row = lax.broadcasted_iota(jnp.int32, (rows, rows), 0)
    col = lax.broadcasted_iota(jnp.int32, (rows, rows), 1)
    tri = jnp.where((row >= col) & (row // c == col // c), 1.0, 0.0).astype(BF16)
    gcum_all = _cumsum_rows(tri, g_all)
    gcum_t = gcum_all.T

    wd, wj = 2 * hd, 2 * c
    first_d = lax.broadcasted_iota(jnp.int32, (1, wd), 1) < hd
    first_j = lax.broadcasted_iota(jnp.int32, (1, wj), 1) < c
    rj = lax.broadcasted_iota(jnp.int32, (c, wj), 0)
    cj = lax.broadcasted_iota(jnp.int32, (c, wj), 1)
    cj = jnp.where(cj < c, cj, cj - c)
    incl, strict = rj >= cj, rj > cj
    eye = jnp.where(rj == cj, 1.0, 0.0)
    same_head = (lax.broadcasted_iota(jnp.int32, (wd, wd), 0) < hd) == (lax.broadcasted_iota(jnp.int32, (wd, wd), 1) < hd)
    ones_bd = jnp.where(same_head, 1.0, 0.0).astype(BF16)
    first_rows = lax.broadcasted_iota(jnp.int32, (wd, 1), 0) < hd

    items = [(g, p) for g in range(n_seq * n_grp) for p in range(n_pair)]

    def block_diag(x, first):
        zero = jnp.zeros_like(x)
        return jnp.concatenate([jnp.where(first, x, zero), jnp.where(first, zero, x)], axis=1)

    def head_sums(x):
        n = x.shape[0]
        hi, lo = _split_bf16(x.reshape(n * c, wd))
        tot = jnp.dot(hi, ones_bd, preferred_element_type=F32) + jnp.dot(lo, ones_bd, preferred_element_type=F32)
        return tot.reshape(n, c, wd)

    def mm_hi(a_parts, b_parts):
        return _bmm(a_parts[0], b_parts[0]) + (_bmm(a_parts[0], b_parts[1]) + _bmm(a_parts[1], b_parts[0]))

    def per_head(cols, base, first):
        return jnp.stack([jnp.where(first, cols[g * c:(g + 1) * c, base + 2 * p:base + 2 * p + 1],
                                    cols[g * c:(g + 1) * c, base + 2 * p + 1:base + 2 * p + 2])
                          for g, p in items], axis=0)

    lanes = lambda x, off: jnp.stack([x[g * c:(g + 1) * c, off + p * wd:off + (p + 1) * wd] for g, p in items], axis=0)
    q = lanes(y, 0)
    k = lanes(y, DN_DIM)
    v = lanes(y, 2 * DN_DIM)
    q = q * lax.rsqrt(head_sums(q * q) + EPS) * hd ** -0.5
    k = k * lax.rsqrt(head_sums(k * k) + EPS)
    beta = per_head(beta_all, SM_BETA, first_d)
    gc_d = per_head(gcum_all, SM_DECAY, first_d)
    gc_j = per_head(gcum_all, SM_DECAY, first_j)
    gr_j = jnp.stack([jnp.concatenate([gcum_t[SM_DECAY + 2 * p:SM_DECAY + 2 * p + 1, g * c:(g + 1) * c],
                                       gcum_t[SM_DECAY + 2 * p + 1:SM_DECAY + 2 * p + 2, g * c:(g + 1) * c]], axis=1)
                      for g, p in items], axis=0)
    gl_d = gc_d[:, c - 1:c, :]
    decay = jnp.where(incl, jnp.exp(jnp.where(incl, gc_j - gr_j, 0.0)), 0.0)
    eg = jnp.exp(gc_d)
    kb = k * beta
    k_bd = block_diag(k.astype(BF16), first_d)
    a = jnp.where(strict, _bmm_nt(kb.astype(BF16), k_bd) * decay, 0.0)
    split_bd = lambda parts, first: tuple(block_diag(x, first) for x in parts)
    base = min(c, DN_INV_BASE)
    same_block = lambda b: (rj // b) == (cj // b)
    diag = jnp.where(same_block(base), a, 0.0)
    t_inv = eye - diag
    pow_parts = _split_bf16(diag)
    pow_bd = split_bd(pow_parts, first_j)
    for _ in range(int(math.log2(base)) - 1):
        pow_parts = _split_bf16(mm_hi(pow_parts, pow_bd))
        pow_bd = split_bd(pow_parts, first_j)
        t_inv = t_inv + mm_hi(_split_bf16(t_inv), pow_bd)
    blk = base
    while blk < c:
        lower = jnp.where(same_block(2 * blk) & jnp.logical_not(same_block(blk)), a, 0.0)
        t_parts = _split_bf16(t_inv)
        lt = mm_hi(_split_bf16(lower), split_bd(t_parts, first_j))
        t_inv = t_inv - mm_hi(t_parts, split_bd(_split_bf16(lt), first_j))
        blk *= 2
    vb_parts = split_bd(_split_bf16(v * beta), first_d)
    kbg_parts = split_bd(_split_bf16(kb * eg), first_d)
    rhs = tuple(jnp.concatenate([x, z], axis=2) for x, z in zip(vb_parts, kbg_parts))
    uw = mm_hi(_split_bf16(t_inv), rhs)
    u, w = uw[:, :, :wd], uw[:, :, wd:]
    qk = jnp.where(incl, _bmm_nt(q.astype(BF16), k_bd) * decay, 0.0).astype(BF16)
    wq = jnp.concatenate([w, q * eg], axis=1).astype(BF16)
    kg = (k * jnp.exp(gl_d - gc_d)).astype(BF16)
    gl_rows = jnp.where(first_rows, jnp.exp(gl_d[:, :, 0:1]), jnp.exp(gl_d[:, :, hd:hd + 1]))

    s = s_ref[...]
    outs = {}
    for kc in range(n_grp):
        pick = lambda x: jnp.concatenate([x[(sq * n_grp + kc) * n_pair:(sq * n_grp + kc + 1) * n_pair]
                                          for sq in range(n_seq)], axis=0)
        ws = _bmm(pick(wq), s.astype(BF16))
        vn = (pick(u) - ws[:, :c]).astype(BF16)
        o_k = ws[:, c:] + _bmm(pick(qk), block_diag(vn, first_d))
        kv_outer = jnp.einsum('pck,pcv->pkv', pick(kg), vn, preferred_element_type=F32)
        s = s * pick(gl_rows) + jnp.where(same_head, kv_outer, 0.0)
        for sq in range(n_seq):
            outs[sq * n_grp + kc] = o_k[sq * n_pair:(sq + 1) * n_pair]
    s_ref[...] = s
    o = jnp.concatenate([outs[g] for g in range(n_seq * n_grp)], axis=0)
    ng = jnp.concatenate([ng_ref[...], ng_ref[...]], axis=1)
    o = o * lax.rsqrt(head_sums(o * o) * (1.0 / hd) + EPS) * ng
    o = jnp.concatenate([jnp.concatenate([o[g * n_pair + p] for p in range(n_pair)], axis=1)
                         for g in range(n_seq * n_grp)], axis=0)
    o_ref[...] = (o * _silu(dg_ref[...])).astype(o_ref.dtype)

    @pl.when(j == pl.num_programs(1) - 1)
    def _():
        for sq in range(n_seq):
            for p in range(n_pair):
                s_pair = s_ref[sq * n_pair + p]
                s_out_ref[sq, 2 * p] = s_pair[0:hd, 0:hd]
                s_out_ref[sq, 2 * p + 1] = s_pair[hd:2 * hd, hd:2 * hd]
        hist_out_ref[...] = tail


def deltanet(qkv_raw, small, dg, conv_hist, s0, lp, B, T):
    c = DN_CHUNK if T >= DN_CHUNK else T
    assert T % c == 0 and c % 8 == 0 and c >= CONV_W - 1 and (c & (c - 1)) == 0
    n_grp = math.gcd(T // c, DN_CHUNKS_PER_STEP)
    n_seq = math.gcd(B, DN_SEQS_PER_STEP) if T == c else 1
    seq_rows = n_grp * c
    step = n_seq * seq_rows
    n = T // seq_rows
    put = lambda vals, base: jnp.zeros((LANES,), F32).at[base + jnp.arange(DN_HEADS)].set(vals).reshape(1, LANES)
    apar = put(-jnp.exp(lp['dn_a_log'].astype(F32)), SM_DECAY)
    dtb = put(lp['dn_dt_bias'].astype(F32), SM_DECAY)
    assert n_seq == 1 or n == 1
    rows = lambda w: pl.BlockSpec((step, w), lambda b, j: (b * n + j, 0))
    per_b = lambda a: pl.BlockSpec((n_seq,) + a.shape[1:], lambda b, j: (b,) + (0,) * (a.ndim - 1))
    full = lambda a: pl.BlockSpec(a.shape, lambda b, j: (0,) * a.ndim)
    ng = lp['dn_norm_g'].reshape(1, DN_HEAD_DIM).astype(F32)
    return pl.pallas_call(
        functools.partial(_deltanet_body, c=c, n_seq=n_seq),
        grid=(B // n_seq, n),
        in_specs=[rows(DN_CONV_DIM), rows(LANES), rows(DN_DIM), per_b(conv_hist), per_b(s0),
                  full(lp['conv_w']), full(apar), full(dtb), full(ng)],
        out_specs=[rows(DN_DIM), per_b(s0), per_b(conv_hist)],
        out_shape=[jax.ShapeDtypeStruct((B * T, DN_DIM), BF16 if step % 16 == 0 else F32),
                   jax.ShapeDtypeStruct(s0.shape, F32),
                   jax.ShapeDtypeStruct(conv_hist.shape, F32)],
        scratch_shapes=[pltpu.VMEM((n_seq, HIST_ROWS + seq_rows, DN_CONV_DIM), F32),
                        pltpu.VMEM((n_seq * (DN_HEADS // 2), 2 * DN_HEAD_DIM, 2 * DN_HEAD_DIM), F32)],
        compiler_params=pltpu.CompilerParams(dimension_semantics=("parallel", "arbitrary"),
                                             vmem_limit_bytes=VMEM_LIMIT),
        name="deltanet",
    )(qkv_raw, small, dg, conv_hist, s0, lp['conv_w'], apar, dtb, ng)


MOE_EXPERTS_PER_STEP = 4
RT_GROUP = 0
RT_EXPERT = N_GROUPS


def pack_router(wg, bg, we, be):
    w = jnp.concatenate([wg, we], axis=1)
    b = jnp.concatenate([bg, be], axis=0)
    pad = LANES - w.shape[1]
    return jnp.pad(w, ((0, 0), (0, pad))).astype(BF16), jnp.pad(b, (0, pad)).reshape(1, LANES).astype(F32)


def _route(r):
    lane = lax.broadcasted_iota(jnp.int32, r.shape, 1).astype(F32)
    big = float(LANES)
    is_grp = lane < N_GROUPS
    lg = jnp.where(is_grp, r, -jnp.inf)
    mg = jnp.max(lg, axis=-1, keepdims=True)
    eg = jnp.where(is_grp, jnp.exp(lg - mg), 0.0)
    grp = jnp.min(jnp.where(lg == mg, lane, big), axis=-1, keepdims=True)
    wg = 1.0 / jnp.sum(eg, axis=-1, keepdims=True)
    first = RT_EXPERT + grp * EXPERTS_PER_GROUP
    in_grp = (lane >= first) & (lane < first + EXPERTS_PER_GROUP)
    le = jnp.where(in_grp, r, -jnp.inf)
    me = jnp.max(le, axis=-1, keepdims=True)
    ee = jnp.where(in_grp, jnp.exp(le - me), 0.0)
    pe = jnp.where(in_grp, ee / jnp.sum(ee, axis=-1, keepdims=True), -1.0)
    v1 = jnp.max(pe, axis=-1, keepdims=True)
    i1 = jnp.min(jnp.where(pe == v1, lane, big), axis=-1, keepdims=True)
    pe2 = jnp.where(lane == i1, -1.0, pe)
    v2 = jnp.max(pe2, axis=-1, keepdims=True)
    i2 = jnp.min(jnp.where(pe2 == v2, lane, big), axis=-1, keepdims=True)
    tot = v1 + v2
    return jnp.where(lane == i1, wg * (v1 / tot), jnp.where(lane == i2, wg * (v2 / tot), 0.0))


def _finish_body(x_ref, on_ref, od_ref, g1_ref, sc_ref, sh_ref, g2_ref, won_ref, wod_ref, ng_ref, wr_ref, br_ref,
                 wgu_ref, wd_ref, fg_ref, o_ref, x1_ref, h_ref, gate_ref, y_ref):
    e = pl.program_id(1)
    tm, d = x_ref.shape
    nb = g1_ref.shape[0]
    per_batch = lambda a: a.reshape(nb, tm // nb, d)

    @pl.when(e == 0)
    def _():
        mix = jnp.dot(on_ref[...].astype(BF16), won_ref[...], preferred_element_type=F32) \
            + jnp.dot(od_ref[...].astype(BF16), wod_ref[...], preferred_element_type=F32)
        x1 = per_batch(x_ref[...]) + g1_ref[...] * per_batch(mix)
        x1 = x1.reshape(tm, d)
        x1_ref[...] = x1
        xn = x1 * lax.rsqrt(jnp.mean(x1 * x1, axis=-1, keepdims=True) + EPS) * ng_ref[...]
        h = (per_batch(xn) * (1 + sc_ref[...]) + sh_ref[...]).reshape(tm, d).astype(BF16)
        h_ref[...] = h
        gate_ref[...] = _route(jnp.dot(h, wr_ref[...], preferred_element_type=F32) + br_ref[...])
        y_ref[...] = jnp.zeros(y_ref.shape, F32)

    h = h_ref[...]
    lane = lax.broadcasted_iota(jnp.int32, gate_ref.shape, 1)
    y = y_ref[...]
    for k in range(wgu_ref.shape[0]):
        gu = jnp.dot(h, wgu_ref[k], preferred_element_type=F32)
        he = _silu(gu[:, :D_EXPERT]) * gu[:, D_EXPERT:]
        out = jnp.dot(he.astype(BF16), wd_ref[k], preferred_element_type=F32)
        expert_lane = RT_EXPERT + e * wgu_ref.shape[0] + k
        g_col = jnp.sum(jnp.where(lane == expert_lane, gate_ref[...], 0.0), axis=-1, keepdims=True)
        y = y + g_col * out
    y_ref[...] = y

    @pl.when(e == pl.num_programs(1) - 1)
    def _():
        x2 = per_batch(x1_ref[...]) + g2_ref[...] * per_batch(y_ref[...])
        x2 = x2.reshape(tm, d)
        o_ref[...] = x2 * lax.rsqrt(jnp.mean(x2 * x2, axis=-1, keepdims=True) + EPS) * fg_ref[...]


def finish_layer(x, o_nsa, o_dn, g1, sc2, sh2, g2, packed, norm_ffn_g, final_g, *, tm=1024):
    B, T, D = x.shape
    n = B * T
    tm = min(tm, n)
    rows = max(tm // T, 1)
    per = max(T // tm, 1)
    row = lambda w: pl.BlockSpec((tm, w), lambda i, e: (i, 0))
    mod = pl.BlockSpec((rows, 1, D), lambda i, e: (i // per, 0, 0))
    full = lambda a: pl.BlockSpec(a.shape, lambda i, e: (0,) * a.ndim)
    expert = lambda a: pl.BlockSpec((MOE_EXPERTS_PER_STEP,) + a.shape[1:], lambda i, e: (e, 0, 0))
    vec = lambda v: v.reshape(1, D).astype(F32)
    args = (x.reshape(n, D), o_nsa, o_dn, g1, sc2, sh2, g2, packed['w_out_nsa'], packed['w_out_dn'],
            vec(norm_ffn_g), packed['router_w'], packed['router_b'], packed['w_gate_up'], packed['w_down'],
            vec(final_g))
    specs = [row(D), row(o_nsa.shape[1]), row(o_dn.shape[1]), mod, mod, mod, mod] + \
            [full(a) for a in args[7:12]] + [expert(args[12]), expert(args[13]), full(args[14])]
    return pl.pallas_call(
        _finish_body,
        grid=(n // tm, N_EXPERTS // MOE_EXPERTS_PER_STEP),
        in_specs=specs,
        out_specs=row(D),
        out_shape=jax.ShapeDtypeStruct((n, D), F32),
        scratch_shapes=[pltpu.VMEM((tm, D), F32), pltpu.VMEM((tm, D), BF16), pltpu.VMEM((tm, LANES), F32),
                        pltpu.VMEM((tm, D), F32)],
        compiler_params=pltpu.CompilerParams(dimension_semantics=("parallel", "arbitrary"),
                                             vmem_limit_bytes=VMEM_LIMIT),
        name="finish_layer",
    )(*args).reshape(B, T, D)


def permute_w_out_nsa(w_out):
    w = w_out[:NSA_Q_DIM].reshape(NSA_KV_HEADS, NSA_GROUP, HEAD_DIM, -1)
    return w.transpose(1, 0, 2, 3).reshape(NSA_Q_DIM, -1).astype(BF16)


def layer_prompt(x, mod, lp, rel_bias, packed):
    B, T, _ = x.shape
    sh1, sc1, g1, sh2, sc2, g2 = mod
    (q, ck, cv, sk, sv, wk, wv, skb, svb, wkb, wvb, qkv_raw, dg, small, ck_rows, cv_rows) = in_projection(
        x, lp['norm_mix_g'], sc1, sh1, packed['w_in'], feature_major=True)
    kc, vc = compress_tokens(ck_rows, cv_rows, packed['cmp_k'], packed['cmp_v'], B)
    o_nsa = nsa_prompt(q, kc, vc, skb, svb, wkb, wvb, small, rel_bias, B, T)
    kv = lambda r: r.reshape(B, NSA_KV_HEADS, HEAD_DIM, -1).transpose(0, 3, 1, 2)
    w_keep = min(WINDOW, T)
    ck, cv, sk, sv, wk, wv = kv(ck), kv(cv), kv(sk), kv(sv), kv(wk[:, :, T - w_keep:]), kv(wv[:, :, T - w_keep:])
    conv_hist = jnp.zeros((B, CONV_W - 1, DN_CONV_DIM), x.dtype)
    s0 = jnp.zeros((B, DN_HEADS, DN_HEAD_DIM, DN_HEAD_DIM), x.dtype)
    o_dn, s_new, conv_new = deltanet(qkv_raw, small, dg, conv_hist, s0, lp, B, T)
    y = finish_layer(x, o_nsa, o_dn, g1, sc2, sh2, g2, packed, lp['norm_ffn_g'], packed['final_norm_g'])
    return y, (ck, cv, sk, sv, wk, wv, s_new, conv_new)


def layer_sample(x, mod, cmp_k_pool, cmp_v_pool, sel_k_pool, sel_v_pool, win_k, win_v, s0, conv_hist,
                 page_table, lp, rel_bias, packed):
    B, T, _ = x.shape
    sh1, sc1, g1, sh2, sc2, g2 = mod
    (q, ck, cv, sk, sv, wk, wv, _, _, _, _, qkv_raw, dg, small) = in_projection(
        x, lp['norm_mix_g'], sc1, sh1, packed['w_in'])
    o_nsa, wk_next, wv_next = nsa_sample(q, sk, sv, wk, wv, small, (cmp_k_pool, cmp_v_pool, sel_k_pool, sel_v_pool),
                                         win_k, win_v, page_table, rel_bias, packed['cmp_k'], packed['cmp_v'], B, T)
    kv = lambda r: r.reshape(B, -1, NSA_KV_HEADS, HEAD_DIM)
    o_dn, s_new, conv_new = deltanet(qkv_raw, small, dg, conv_hist, s0, lp, B, T)
    y = finish_layer(x, o_nsa, o_dn, g1, sc2, sh2, g2, packed, lp['norm_ffn_g'], packed['final_norm_g'])
    return y, (kv(ck), kv(cv), kv(sk), kv(sv), kv(wk_next), kv(wv_next), s_new, conv_new)


def kernel(x_prompt, x_sample, cache_cmp_k, cache_cmp_v, cache_sel_k, cache_sel_v, cache_win_k, cache_win_v,
           state_delta, state_conv, page_table, c_prompt, c_sample, rel_bias, w_ada, b_ada, norm_mix_g,
           norm_ffn_g, w_in, cmp_pos_k, cmp_w1_k, cmp_w2_k, cmp_pos_v, cmp_w1_v, cmp_w2_v, conv_w, dn_a_log,
           dn_dt_bias, dn_norm_g, w_out, router_group_w, router_group_b, router_expert_w, router_expert_b,
           expert_w_gate, expert_w_up, expert_w_down, final_norm_g):
    xp, xs = x_prompt, x_sample
    new_p, new_s = [], []
    for l in range(DEPTH):
        lp = dict(norm_mix_g=norm_mix_g[l], norm_ffn_g=norm_ffn_g[l], conv_w=conv_w[l],
                  dn_a_log=dn_a_log[l], dn_dt_bias=dn_dt_bias[l], dn_norm_g=dn_norm_g[l])
        packed = dict(w_in=pack_w_in(w_in[l]),
                      cmp_k=pack_compress(cmp_pos_k[l], cmp_w1_k[l], cmp_w2_k[l]),
                      cmp_v=pack_compress(cmp_pos_v[l], cmp_w1_v[l], cmp_w2_v[l]),
                      w_out_nsa=permute_w_out_nsa(w_out[l]),
                      w_out_dn=w_out[l][NSA_Q_DIM:].astype(BF16),
                      w_gate_up=jnp.concatenate([expert_w_gate[l], expert_w_up[l]], axis=-1).astype(BF16),
                      w_down=expert_w_down[l].astype(BF16),
                      final_norm_g=final_norm_g)
        packed['router_w'], packed['router_b'] = pack_router(router_group_w[l], router_group_b[l],
                                                             router_expert_w[l], router_expert_b[l])
        n_p = c_prompt.shape[0]
        mod = modulation(jnp.concatenate([c_prompt, c_sample], axis=0), w_ada[l], b_ada[l])
        xp, st_p = layer_prompt(xp, [m[:n_p] for m in mod], lp, rel_bias, packed)
        xs, st_s = layer_sample(xs, [m[n_p:] for m in mod], cache_cmp_k[l], cache_cmp_v[l], cache_sel_k[l],
                                cache_sel_v[l],
                                cache_win_k[l], cache_win_v[l], state_delta[l], state_conv[l], page_table,
                                lp, rel_bias, packed)
        new_p.append(st_p)
        new_s.append(st_s)
    assert DEPTH == 1
    y_prompt, y_sample = xp, xs

    def stk(states, i):
        return jnp.stack([s[i] for s in states])

    return (y_prompt, y_sample,
            stk(new_p, 0), stk(new_p, 1), stk(new_p, 2), stk(new_p, 3),
            stk(new_p, 4), stk(new_p, 5), stk(new_p, 6), stk(new_p, 7),
            stk(new_s, 0), stk(new_s, 1), stk(new_s, 2), stk(new_s, 3),
            stk(new_s, 4), stk(new_s, 5), stk(new_s, 6), stk(new_s, 7))
```

```python
import functools
import math

import jax
import jax.numpy as jnp
import numpy as np
from jax import lax
from jax.experimental import pallas as pl
from jax.experimental.pallas import tpu as pltpu

D_MODEL = 1024
DEPTH = 1
PAGE_SIZE = 128

NSA_HEADS = 8
NSA_KV_HEADS = 2
NSA_GROUP = NSA_HEADS // NSA_KV_HEADS
HEAD_DIM = 64
CMP_LEN = 32
CMP_STRIDE = 16
CMP_HIDDEN = 4 * HEAD_DIM
SEL_BLOCK = 64
SEL_TOP = 16
WINDOW = 512
DN_HEADS = 8
DN_HEAD_DIM = 64
CONV_W = 4
DN_CHUNK = 64
NUM_BUCKETS = 32
MAX_DISTANCE = 128
N_GROUPS = 4
EXPERTS_PER_GROUP = 8
N_EXPERTS = N_GROUPS * EXPERTS_PER_GROUP
TOP_K_IN_GROUP = 2
D_EXPERT = D_MODEL // 4

EPS = 1e-6
NEG_INF = -1e30
F32 = jnp.float32
BF16 = jnp.bfloat16

NSA_Q_DIM = NSA_HEADS * HEAD_DIM
NSA_KV_DIM = NSA_KV_HEADS * HEAD_DIM
DN_DIM = DN_HEADS * DN_HEAD_DIM
DN_CONV_DIM = 3 * DN_DIM
MIX_WIDTH = NSA_Q_DIM + DN_DIM
IN_SPLITS = (NSA_Q_DIM, NSA_KV_DIM, NSA_KV_DIM, NSA_KV_DIM, NSA_KV_DIM, NSA_KV_DIM, NSA_KV_DIM,
             3 * NSA_HEADS, DN_CONV_DIM, DN_HEADS, DN_HEADS, DN_DIM)
IN_DIM = sum(IN_SPLITS)

LANES = 128
VMEM_LIMIT = 56 * 1024 * 1024

ATT_TILE = 128
MASKED = -1e30


def t5_bucket(rel):
    n = jnp.maximum(rel, 0)
    max_exact = NUM_BUCKETS // 2
    nf = jnp.maximum(n, 1).astype(F32)
    large = max_exact + (jnp.log(nf / max_exact) / math.log(MAX_DISTANCE / max_exact)
                         * (NUM_BUCKETS - max_exact)).astype(jnp.int32)
    large = jnp.minimum(large, NUM_BUCKETS - 1)
    return jnp.where(n < max_exact, n, large)


def _modulation_body(c_ref, w_ref, b_ref, o_ref):
    c = c_ref[...]
    x = (c * jax.nn.sigmoid(c)).astype(BF16)
    o_ref[...] = jnp.dot(x, w_ref[...].astype(BF16), preferred_element_type=F32) + b_ref[...]


def modulation(c, w_ada, b_ada, *, tn=1024):
    rows, d = c.shape
    n = w_ada.shape[1]
    mod = pl.pallas_call(
        _modulation_body,
        grid=(n // tn,),
        in_specs=[pl.BlockSpec((rows, d), lambda j: (0, 0)),
                  pl.BlockSpec((d, tn), lambda j: (0, j)),
                  pl.BlockSpec((1, tn), lambda j: (0, j))],
        out_specs=pl.BlockSpec((rows, tn), lambda j: (0, j)),
        out_shape=jax.ShapeDtypeStruct((rows, n), F32),
        compiler_params=pltpu.CompilerParams(dimension_semantics=("parallel",), vmem_limit_bytes=VMEM_LIMIT),
        name="modulation",
    )(c, w_ada, b_ada.reshape(1, n))
    return jnp.split(mod[:, None, :], 6, axis=-1)


_PK_Q = (0, NSA_Q_DIM)
_PK_KV = (_PK_Q[1], _PK_Q[1] + 6 * NSA_KV_DIM)
_PK_QKV = (_PK_KV[1], _PK_KV[1] + DN_CONV_DIM)
_PK_DG = (_PK_QKV[1], _PK_QKV[1] + DN_DIM)
_PK_SM = (_PK_DG[1], _PK_DG[1] + LANES)
PK_DIM = _PK_SM[1]
SM_GATE = 0
SM_BETA = 3 * NSA_HEADS
SM_DECAY = SM_BETA + DN_HEADS


def pack_w_in(w_in):
    offs = np.cumsum((0,) + IN_SPLITS)
    part = lambda i: w_in[:, offs[i]:offs[i + 1]]
    d = w_in.shape[0]
    q = part(0).reshape(d, NSA_KV_HEADS, NSA_GROUP, HEAD_DIM).transpose(0, 2, 1, 3).reshape(d, NSA_Q_DIM)
    q = q * HEAD_DIM ** -0.5
    small = jnp.concatenate([part(7), part(9), part(10)], axis=1)
    small = jnp.pad(small, ((0, 0), (0, LANES - small.shape[1])))
    cols = [q] + [part(i) for i in range(1, 7)] + [part(8), part(11), small]
    return jnp.concatenate(cols, axis=1).astype(BF16)


def _in_proj_body(x_ref, g_ref, sc_ref, sh_ref, w_ref, q_ref, ck_ref, cv_ref, sk_ref, sv_ref, wk_ref, wv_ref,
                  skb_ref, svb_ref, wkb_ref, wvb_ref, qkv_ref, dg_ref, sm_ref, *row_refs):
    x = x_ref[...]
    tm, d = x.shape
    nb = sc_ref.shape[0]
    xn = x * lax.rsqrt(jnp.mean(x * x, axis=-1, keepdims=True) + EPS) * g_ref[...]
    h = xn.reshape(nb, tm // nb, d) * (1 + sc_ref[...]) + sh_ref[...]
    h = h.reshape(tm, d).astype(BF16)

    def mm(span):
        return jnp.dot(h, w_ref[:, span[0]:span[1]], preferred_element_type=F32)

    q_ref[...] = mm(_PK_Q).astype(BF16)
    kv = mm(_PK_KV)
    for i, r in enumerate((ck_ref, cv_ref, sk_ref, sv_ref, wk_ref, wv_ref)):
        rows = kv[:, i * NSA_KV_DIM:(i + 1) * NSA_KV_DIM]
        if row_refs:
            r[0] = rows.T
        else:
            r[...] = rows
    for i, r in enumerate(row_refs):
        r[...] = kv[:, i * NSA_KV_DIM:(i + 1) * NSA_KV_DIM]
    for i, r in enumerate((skb_ref, svb_ref, wkb_ref, wvb_ref)):
        r[...] = kv[:, (i + 2) * NSA_KV_DIM:(i + 3) * NSA_KV_DIM].astype(BF16)
    qkv_ref[...] = mm(_PK_QKV)
    dg_ref[...] = mm(_PK_DG)
    sm_ref[...] = mm(_PK_SM)


def in_projection(x, norm_g, scale, shift, w_packed, *, tm=256, feature_major=False):
    B, T, D = x.shape
    n = B * T
    tm = min(tm, n)
    rows = max(tm // T, 1)
    per = max(T // tm, 1)
    row = lambda i: (i, 0)
    by_row = lambda w, dt: (((n, w), dt), pl.BlockSpec((tm, w), row))
    if feature_major:
        assert tm <= T and tm % LANES == 0
        kv_out = (((B, NSA_KV_DIM, T), F32), pl.BlockSpec((1, NSA_KV_DIM, tm), lambda i: (i // per, 0, i % per)))
    else:
        kv_out = by_row(NSA_KV_DIM, F32)
    outs = [by_row(NSA_Q_DIM, BF16)] + [kv_out] * 6 + [by_row(NSA_KV_DIM, BF16)] * 4 + \
           [by_row(DN_CONV_DIM, F32), by_row(DN_DIM, F32), by_row(LANES, F32)] + \
           ([by_row(NSA_KV_DIM, F32)] * 2 if feature_major else [])
    return pl.pallas_call(
        _in_proj_body,
        grid=(n // tm,),
        in_specs=[pl.BlockSpec((tm, D), row),
                  pl.BlockSpec((1, D), lambda i: (0, 0)),
                  pl.BlockSpec((rows, 1, D), lambda i: (i // per, 0, 0)),
                  pl.BlockSpec((rows, 1, D), lambda i: (i // per, 0, 0)),
                  pl.BlockSpec((D, PK_DIM), lambda i: (0, 0))],
        out_specs=[spec for _, spec in outs],
        out_shape=[jax.ShapeDtypeStruct(s, dt) for (s, dt), _ in outs],
        compiler_params=pltpu.CompilerParams(dimension_semantics=("parallel",), vmem_limit_bytes=VMEM_LIMIT),
        name="in_projection",
    )(x.reshape(n, D), norm_g.reshape(1, D), scale, shift, w_packed)


CHUNK_W = CMP_STRIDE * NSA_KV_DIM
CMP_HALVES = CMP_LEN // CMP_STRIDE


def pack_compress(pos, w1, w2):
    eye = jnp.eye(NSA_KV_HEADS, dtype=F32)
    w1r = w1.reshape(CMP_HALVES, CMP_STRIDE, HEAD_DIM, CMP_HIDDEN)
    w1p = jnp.einsum('ijdc,kl->ijkdlc', w1r, eye).reshape(CMP_HALVES, CHUNK_W, NSA_KV_HEADS * CMP_HIDDEN)
    w2p = jnp.einsum('cd,kl->kcld', w2, eye).reshape(NSA_KV_HEADS * CMP_HIDDEN, NSA_KV_DIM)
    posp = jnp.broadcast_to(pos.reshape(CMP_HALVES, CMP_STRIDE, 1, HEAD_DIM),
                            (CMP_HALVES, CMP_STRIDE, NSA_KV_HEADS, HEAD_DIM)).reshape(CMP_HALVES, CHUNK_W)
    return posp, w1p.astype(BF16), w2p.astype(BF16)


def _compress_one(rows_ref, pos_ref, w1_ref, w2_ref):
    n_chunk = rows_ref.shape[0] // CMP_STRIDE
    x = jnp.concatenate([rows_ref[pl.ds(j, n_chunk, stride=CMP_STRIDE), :] for j in range(CMP_STRIDE)], axis=1)
    hid = None
    for i in range(CMP_HALVES):
        part = jnp.dot((x + pos_ref[i:i + 1]).astype(BF16), w1_ref[i], preferred_element_type=F32)
        if i:
            part = pltpu.roll(part, shift=n_chunk - i, axis=0)
        hid = part if hid is None else hid + part
    hid = hid * jax.nn.sigmoid(hid)
    return jnp.dot(hid.astype(BF16), w2_ref[...], preferred_element_type=F32)


def _compress_body(xk_ref, xv_ref, pk_ref, pv_ref, w1k_ref, w1v_ref, w2k_ref, w2v_ref, kc_ref, vc_ref):
    kc_ref[0] = _compress_one(xk_ref, pk_ref, w1k_ref, w2k_ref).astype(BF16)
    vc_ref[0] = _compress_one(xv_ref, pv_ref, w1v_ref, w2v_ref).astype(BF16)


def compress_tokens(ck, cv, pk, pv, B):
    L = ck.shape[0] // B
    n_chunk = L // CMP_STRIDE
    full = lambda a: pl.BlockSpec(a.shape, lambda b: (0,) * a.ndim)
    rows = pl.BlockSpec((L, NSA_KV_DIM), lambda b: (b, 0))
    tok = pl.BlockSpec((1, n_chunk, NSA_KV_DIM), lambda b: (b, 0, 0))
    return pl.pallas_call(
        _compress_body,
        grid=(B,),
        in_specs=[rows, rows, full(pk[0]), full(pv[0]), full(pk[1]), full(pv[1]), full(pk[2]), full(pv[2])],
        out_specs=[tok, tok],
        out_shape=[jax.ShapeDtypeStruct((B, n_chunk, NSA_KV_DIM), BF16)] * 2,
        compiler_params=pltpu.CompilerParams(dimension_semantics=("parallel",), vmem_limit_bytes=VMEM_LIMIT),
        name="compress_tokens",
    )(ck, cv, pk[0], pv[0], pk[1], pv[1], pk[2], pv[2])


def _t5_bucket_np(rel):
    n = np.maximum(rel, 0)
    max_exact = NUM_BUCKETS // 2
    nf = np.maximum(n, 1).astype(np.float32)
    large = max_exact + (np.log(nf / max_exact) / math.log(MAX_DISTANCE / max_exact)
                         * (NUM_BUCKETS - max_exact)).astype(np.int32)
    return np.where(n < max_exact, n, np.minimum(large, NUM_BUCKETS - 1))


assert (_t5_bucket_np(np.arange(ATT_TILE + 1, 1 << 16)) == NUM_BUCKETS - 1).all()


NB_SAME, NB_PREV, NB_FAR, NB_EDGE, NB_NONE = range(5)
FAR_TILE = 512


def bias_lookup(rel_bias, rel):
    bucket = t5_bucket(rel)
    tbl = rel_bias.astype(F32)
    shape = (tbl.shape[1],) + (1,) * rel.ndim
    out = jnp.zeros((tbl.shape[1],) + rel.shape, F32)
    for b in range(NUM_BUCKETS):
        out = jnp.where(bucket == b, tbl[b].reshape(shape), out)
    return out


def near_bias_tables(rel_bias):
    t = ATT_TILE
    i = jnp.arange(t)[:, None]
    j = jnp.arange(t)[None, :]
    diag = jnp.where(i >= j, bias_lookup(rel_bias, i - j), MASKED)
    prev = bias_lookup(rel_bias, t + i - j)
    far = jnp.broadcast_to(rel_bias.astype(F32)[NUM_BUCKETS - 1][:, None, None], prev.shape)
    edge = jnp.where(j > i, far, MASKED)
    return jnp.stack([diag, prev, far, edge, jnp.full_like(far, MASKED)])


def cmp_bias_table(rel_bias, q0, tq, n_chunk):
    first_end = CMP_LEN - 1
    span = CMP_STRIDE * (n_chunk - 1)
    n_rel = tq + span
    rel = q0 - first_end - span + jnp.arange(n_rel)
    vals = jnp.where(rel >= 0, bias_lookup(rel_bias, rel), MASKED)
    vals = jnp.roll(vals, -span, axis=1)
    flat = jnp.tile(vals, (1, n_chunk))[:, :n_chunk * (n_rel - CMP_STRIDE)]
    by_token = flat.reshape(-1, n_chunk, n_rel - CMP_STRIDE)[:, :, :tq]
    return jnp.swapaxes(by_token, 1, 2)


def overlap_matrix(n_chunk, n_sel):
    c0 = np.arange(n_chunk)[:, None] * CMP_STRIDE
    s0 = np.arange(LANES)[None, :] * SEL_BLOCK
    ov = np.clip(np.minimum(c0 + CMP_LEN, s0 + SEL_BLOCK) - np.maximum(c0, s0), 0, None) / CMP_LEN
    ov = np.where(np.arange(LANES)[None, :] < n_sel, ov, 0.0)
    return jnp.asarray(ov, F32)


def expand_matrix(n_keys):
    e = (np.arange(n_keys)[None, :] // SEL_BLOCK) == np.arange(LANES)[:, None]
    return jnp.asarray(e, BF16)


def _head_queries(q_ref, tq):
    low = lax.broadcasted_iota(jnp.int32, (tq, LANES), 1) < HEAD_DIM
    zero = jnp.zeros((tq, LANES), BF16)
    parts = []
    for kv in range(NSA_KV_HEADS):
        for g in range(NSA_GROUP):
            blk = q_ref[:, g * LANES:(g + 1) * LANES]
            parts.append(jnp.where(low if kv == 0 else ~low, blk, zero))
    return parts


def _stack_queries(q_ref, tq):
    return jnp.concatenate(_head_queries(q_ref, tq), axis=0)


def _dot_nt(a, b):
    return lax.dot_general(a, b, (((1,), (1,)), ((), ())), preferred_element_type=F32)


def _selection_mask(p_group, ov_ref, ex_ref, q0, n_sel, n_top):
    tq = p_group.shape[0]
    n_row = -(-n_sel // 8) * 8
    imp = lax.dot_general(ov_ref[...], p_group, (((0,), (1,)), ((), ())), precision=lax.Precision.HIGHEST,
                          preferred_element_type=F32)[0:n_row]
    blk = lax.broadcasted_iota(jnp.int32, (n_row, tq), 0)
    q_pos = q0 + lax.broadcasted_iota(jnp.int32, (n_row, tq), 1)
    q_blk = q_pos // SEL_BLOCK
    forced = jnp.where(blk == 0, 1.0, jnp.where(blk == q_blk, 1.0, jnp.where(blk == q_blk - 1, 1.0, 0.0)))
    score = jnp.where(blk * SEL_BLOCK <= q_pos, imp, NEG_INF)
    score = jnp.where(forced > 0.5, 1e9, score)
    score = jnp.where(blk < n_sel, score, -3e38)
    rank = jnp.zeros((n_row, tq), F32)
    for s in range(n_sel):
        other = score[s:s + 1, :]
        tie = jnp.where(blk > s, 1.0, 0.0)
        rank = rank + jnp.where(other > score, 1.0, jnp.where(other == score, tie, 0.0))
    sel = jnp.where(rank < n_top, 1.0, 0.0).astype(BF16)
    keys = lax.dot_general(sel, ex_ref[0:n_row, :], (((0,), (0,)), ((), ())), preferred_element_type=F32)
    return (keys - 1.0) * (-MASKED)


def _gate_columns(gl_ref, j):
    g = jax.nn.sigmoid(gl_ref[...])
    cols = [g[:, SM_GATE + 3 * h + j:SM_GATE + 3 * h + j + 1] for h in range(NSA_HEADS)]
    return jnp.stack(cols, axis=0)


def _nsa_prompt_body(q_ref, kc_ref, vc_ref, sk_ref, sv_ref, wk_ref, wv_ref, bc_ref, nb_ref, gl_ref, ov_ref, ex_ref,
                     o_ref, mb_ref, s_ref, sw_ref, stat_ref, mrep_ref, acc_ref, out_ref, *, n_sel, n_top):
    i = pl.program_id(1)
    tq = q_ref.shape[0]
    h = NSA_HEADS
    q = _stack_queries(q_ref, tq)

    s = _dot_nt(q, kc_ref[0]).reshape(h, tq, -1) + bc_ref[...]
    m = jnp.max(s, axis=-1, keepdims=True)
    p = jnp.where(s > 0.5 * MASKED, jnp.exp(s - m), 0.0)
    l = jnp.sum(p, axis=-1, keepdims=True)
    p = p * (1.0 / jnp.where(l > 0.0, l, 1.0))
    o_cmp = jnp.dot(p.reshape(h * tq, -1).astype(BF16), vc_ref[0], preferred_element_type=F32)
    out_ref[...] = _gate_columns(gl_ref, 0) * o_cmp.reshape(h, tq, LANES)

    for kv in range(NSA_KV_HEADS):
        pk = p[kv * NSA_GROUP]
        for g in range(1, NSA_GROUP):
            pk = pk + p[kv * NSA_GROUP + g]
        mb_ref[kv] = _selection_mask(pk, ov_ref, ex_ref, i * tq, n_sel, n_top)

    tk = ATT_TILE
    n_back = WINDOW // tk
    wide = (n_back + 1) * tk
    far_blk = FAR_TILE // tk
    n_tile = sk_ref.shape[0] // tk
    tail_col = s_ref.shape[-1] - wide
    kvg = lambda x: x.reshape(NSA_KV_HEADS, NSA_GROUP, tq, x.shape[-1])
    blocks = lambda x: [x[..., c * LANES:(c + 1) * LANES] for c in range(x.shape[-1] // LANES)]
    lane_max = lambda x: functools.reduce(jnp.maximum, blocks(x))
    lane_sum = lambda x: functools.reduce(jnp.add, blocks(x))
    rep = lambda x, n: jnp.concatenate([x] * n, axis=-1)
    far_bias = nb_ref[NB_FAR][:, 0:1, 0:1]

    n_far = jnp.maximum(i - 1, 0) // far_blk
    tail0 = jnp.minimum(n_far * far_blk, n_tile - (n_back + 1))
    tail_rows = pl.ds(pl.multiple_of(tail0 * tk, tk), wide)
    tiles = []
    for c in range(n_back + 1):
        blk = tail0 + c
        kind = jnp.where((blk < n_far * far_blk) | (blk > i), NB_NONE, jnp.minimum(i - blk, NB_FAR))
        tiles.append(nb_ref[kind])
    s = _dot_nt(q, sk_ref[tail_rows, :]).reshape(h, tq, wide) + jnp.concatenate(tiles, axis=-1)
    s = (kvg(s) + mb_ref[:, :, tail_rows][:, None]).reshape(h, tq, wide)
    s_ref[:, :, tail_col:tail_col + wide] = s
    stat_ref[0] = lane_max(s)

    win0 = jnp.maximum(i - n_back, 0)
    win_rows = pl.ds(pl.multiple_of(win0 * tk, tk), wide)
    tiles = []
    for c in range(n_back + 1):
        r = i - (win0 + c)
        tiles.append(nb_ref[jnp.where(r < 0, NB_NONE, jnp.where(r == n_back, NB_EDGE, jnp.minimum(r, NB_FAR)))])
    s = _dot_nt(q, wk_ref[win_rows, :]).reshape(h, tq, wide) + jnp.concatenate(tiles, axis=-1)
    sw_ref[...] = s
    stat_ref[1] = lane_max(s)

    def far_scores(f, carry):
        rows = pl.ds(pl.multiple_of(f * FAR_TILE, FAR_TILE), FAR_TILE)
        s = _dot_nt(q, sk_ref[rows, :]).reshape(h, tq, FAR_TILE) + far_bias
        s = (kvg(s) + mb_ref[:, :, rows][:, None]).reshape(h, tq, FAR_TILE)
        s_ref[:, :, rows] = s
        stat_ref[0] = jnp.maximum(stat_ref[0], lane_max(s))
        return carry
    lax.fori_loop(0, n_far, far_scores, 0)

    for br in range(2):
        mrep_ref[br] = jnp.broadcast_to(jnp.max(stat_ref[br], axis=-1, keepdims=True), (h, tq, LANES))
        stat_ref[br] = jnp.zeros((h, tq, LANES), F32)
    acc_ref[...] = jnp.zeros(acc_ref.shape, F32)

    def far_values(f, carry):
        rows = pl.ds(pl.multiple_of(f * FAR_TILE, FAR_TILE), FAR_TILE)
        p = jnp.exp(s_ref[:, :, rows] - rep(mrep_ref[0], far_blk))
        stat_ref[0] += lane_sum(p)
        acc_ref[...] += jnp.dot(p.reshape(h * tq, FAR_TILE).astype(BF16), sv_ref[rows, :],
                                preferred_element_type=F32).reshape(h, tq, LANES)
        return carry
    lax.fori_loop(0, n_far, far_values, 0)

    p = jnp.exp(s_ref[:, :, tail_col:tail_col + wide] - rep(mrep_ref[0], n_back + 1))
    l = jnp.sum(stat_ref[0] + lane_sum(p), axis=-1, keepdims=True)
    o_sel = acc_ref[...] + jnp.dot(p.reshape(h * tq, wide).astype(BF16), sv_ref[tail_rows, :],
                                   preferred_element_type=F32).reshape(h, tq, LANES)
    out_ref[...] += (_gate_columns(gl_ref, 1) * (1.0 / l)) * o_sel

    p = jnp.exp(sw_ref[...] - rep(mrep_ref[1], n_back + 1))
    l = jnp.sum(lane_sum(p), axis=-1, keepdims=True)
    o_win = jnp.dot(p.reshape(h * tq, wide).astype(BF16), wv_ref[win_rows, :],
                    preferred_element_type=F32).reshape(h, tq, LANES)
    out_ref[...] += (_gate_columns(gl_ref, 2) * (1.0 / l)) * o_win

    low = lax.broadcasted_iota(jnp.int32, (tq, LANES), 1) < HEAD_DIM
    for g in range(NSA_GROUP):
        o_ref[:, g * LANES:(g + 1) * LANES] = jnp.where(low, out_ref[g], out_ref[NSA_GROUP + g]).astype(BF16)


def nsa_prompt(q, kc, vc, skb, svb, wkb, wvb, small, rel_bias, B, T):
    tq = ATT_TILE
    nq = T // tq
    assert T % FAR_TILE == 0 and T >= WINDOW + ATT_TILE and WINDOW % ATT_TILE == 0
    wide = WINDOW + ATT_TILE
    far_max = (nq - 2) // (FAR_TILE // ATT_TILE) * FAR_TILE
    n_chunk = kc.shape[1]
    n_sel = -(-T // SEL_BLOCK)
    n_top = min(SEL_TOP, n_sel)
    bias_c = cmp_bias_table(rel_bias, 0, T, n_chunk)
    nb = near_bias_tables(rel_bias)
    ov = overlap_matrix(n_chunk, n_sel)
    ex = expand_matrix(T)
    h = NSA_HEADS
    full = lambda a: pl.BlockSpec(a.shape, lambda b, i: (0,) * a.ndim)
    seq = pl.BlockSpec((T, NSA_KV_DIM), lambda b, i: (b, 0))
    tok = pl.BlockSpec((1, n_chunk, NSA_KV_DIM), lambda b, i: (b, 0, 0))
    return pl.pallas_call(
        functools.partial(_nsa_prompt_body, n_sel=n_sel, n_top=n_top),
        grid=(B, nq),
        in_specs=[pl.BlockSpec((tq, NSA_Q_DIM), lambda b, i: (b * nq + i, 0)),
                  tok, tok, seq, seq, seq, seq,
                  pl.BlockSpec((h, tq, n_chunk), lambda b, i: (0, i, 0)),
                  full(nb),
                  pl.BlockSpec((tq, LANES), lambda b, i: (b * nq + i, 0)),
                  full(ov), full(ex)],
        out_specs=pl.BlockSpec((tq, NSA_Q_DIM), lambda b, i: (b * nq + i, 0)),
        out_shape=jax.ShapeDtypeStruct((B * T, NSA_Q_DIM), BF16),
        scratch_shapes=[pltpu.VMEM((NSA_KV_HEADS, tq, T), F32),
                        pltpu.VMEM((h, tq, far_max + wide), F32), pltpu.VMEM((h, tq, wide), F32),
                        pltpu.VMEM((2, h, tq, LANES), F32), pltpu.VMEM((2, h, tq, LANES), F32),
                        pltpu.VMEM((h, tq, LANES), F32), pltpu.VMEM((h, tq, LANES), F32)],
        compiler_params=pltpu.CompilerParams(dimension_semantics=("parallel", "arbitrary"),
                                             vmem_limit_bytes=VMEM_LIMIT),
        name="nsa_prompt",
    )(q, kc, vc, skb, svb, wkb, wvb, bias_c, nb, small, ov, ex)


def sample_bias_tables(rel_bias, past, T, W):
    t = jnp.arange(T)[:, None]
    pad = jnp.arange(ATT_TILE)[None, :]
    new = jnp.where((pad < T) & (t >= pad), bias_lookup(rel_bias, t - pad), MASKED)
    sel = bias_lookup(rel_bias, past + t - jnp.arange(past)[None, :])
    rel_w = W + t - jnp.arange(W)[None, :]
    win = jnp.where(rel_w < WINDOW, bias_lookup(rel_bias, rel_w), MASKED)
    cmp = cmp_bias_table(rel_bias, past, T, past // CMP_STRIDE)
    return cmp, jnp.concatenate([sel, new], axis=-1), jnp.concatenate([win, new], axis=-1)


def _softmax_rows(s):
    m = jnp.max(s, axis=-1, keepdims=True)
    p = jnp.exp(s - m)
    return p * (1.0 / jnp.sum(p, axis=-1, keepdims=True))


def _nsa_sample_body(pt_ref, q_ref, skn_ref, svn_ref, wkn_ref, wvn_ref, gl_ref, wink_ref, winv_ref,
                     bc_ref, bs_ref, bw_ref, ov_ref, ex_ref, pk_ref, pv_ref, w1k_ref, w1v_ref, w2k_ref, w2v_ref,
                     ck_pool, cv_pool, sk_pool, sv_pool, o_ref, wko_ref, wvo_ref,
                     s_ref, new_ref, rows_ref, page_ref, sem_ref, *, n_pages, n_sel, n_top, past):
    h = NSA_HEADS
    tq = q_ref.shape[0]
    tk = ATT_TILE
    b = pl.program_id(0)
    pools = (ck_pool, cv_pool, sk_pool, sv_pool)

    def page_copy(seq, slot, pool, pg):
        return pltpu.make_async_copy(pools[pool].at[pt_ref[seq, pg]], page_ref.at[slot, pool, pg], sem_ref.at[slot])

    def start_fetch(seq, slot):
        for pool in range(len(pools)):
            for pg in range(n_pages):
                page_copy(seq, slot, pool, pg).start()

    slot = b % 2

    @pl.when(b == 0)
    def _():
        start_fetch(0, 0)

    @pl.when(b + 1 < pl.num_programs(0))
    def _():
        start_fetch(b + 1, 1 - slot)

    for pool in range(len(pools)):
        for pg in range(n_pages):
            page_copy(b, slot, pool, pg).wait()
    pages = [[page_ref.at[slot, pool, pg] for pg in range(n_pages)] for pool in range(len(pools))]

    q = _stack_queries(q_ref, tq)

    new_ref[...] = jnp.zeros(new_ref.shape, F32)
    for n, r in enumerate((skn_ref, svn_ref, wkn_ref, wvn_ref)):
        new_ref[n, 0:tq, :] = r[...]

    for pg in range(n_pages):
        rows_ref[0, pg * tk:(pg + 1) * tk, :] = pages[0][pg][...].T
        rows_ref[1, pg * tk:(pg + 1) * tk, :] = pages[1][pg][...].T
    kc = _compress_one(rows_ref.at[0], pk_ref, w1k_ref, w2k_ref).astype(BF16)
    vc = _compress_one(rows_ref.at[1], pv_ref, w1v_ref, w2v_ref).astype(BF16)
    s = _dot_nt(q, kc).reshape(h, tq, -1) + bc_ref[...]
    p = _softmax_rows(s)
    o_cmp = jnp.dot(p.reshape(h * tq, -1).astype(BF16), vc, preferred_element_type=F32)
    out = _gate_columns(gl_ref, 0) * o_cmp.reshape(h, tq, LANES)

    masks = []
    for kv in range(NSA_KV_HEADS):
        pkv = p[kv * NSA_GROUP]
        for g in range(1, NSA_GROUP):
            pkv = pkv + p[kv * NSA_GROUP + g]
        masks.append(_selection_mask(pkv, ov_ref, ex_ref, past, n_sel, n_top))
    mask = jnp.stack(masks, axis=0)
    for pg in range(n_pages):
        s_ref[:, pg * tk:(pg + 1) * tk] = jnp.dot(q, pages[2][pg][...].astype(BF16), preferred_element_type=F32)
    s_ref[:, n_pages * tk:(n_pages + 1) * tk] = _dot_nt(q, new_ref[0].astype(BF16))
    width = (n_pages + 1) * tk
    s = s_ref[...].reshape(h, tq, width) + bs_ref[...]
    s = (s.reshape(NSA_KV_HEADS, NSA_GROUP, tq, width) + mask[:, None]).reshape(h, tq, width)
    p = _softmax_rows(s).reshape(h * tq, width).astype(BF16)
    o_sel = jnp.dot(p[:, n_pages * tk:], new_ref[1].astype(BF16), preferred_element_type=F32)
    for pg in range(n_pages):
        o_sel = o_sel + _dot_nt(p[:, pg * tk:(pg + 1) * tk], pages[3][pg][...].astype(BF16))
    out = out + _gate_columns(gl_ref, 1) * o_sel.reshape(h, tq, LANES)

    w = wink_ref.shape[2]
    s_ref[:, 0:w] = jnp.dot(q, wink_ref[0].astype(BF16), preferred_element_type=F32)
    s_ref[:, w:w + tk] = _dot_nt(q, new_ref[2].astype(BF16))
    s = s_ref[:, 0:w + tk].reshape(h, tq, w + tk) + bw_ref[...]
    p = _softmax_rows(s).reshape(h * tq, w + tk).astype(BF16)
    o_win = jnp.dot(p[:, w:], new_ref[3].astype(BF16), preferred_element_type=F32) \
        + _dot_nt(p[:, 0:w], winv_ref[0].astype(BF16))
    out = out + _gate_columns(gl_ref, 2) * o_win.reshape(h, tq, LANES)

    low = lax.broadcasted_iota(jnp.int32, (tq, LANES), 1) < HEAD_DIM
    for g in range(NSA_GROUP):
        o_ref[:, g * LANES:(g + 1) * LANES] = jnp.where(low, out[g], out[NSA_GROUP + g])

    keep = w - tq
    lane_w = lax.broadcasted_iota(jnp.int32, (NSA_KV_DIM, w), 1)
    for old_ref, new_rows_ref, dst_ref in ((wink_ref, wkn_ref, wko_ref), (winv_ref, wvn_ref, wvo_ref)):
        shifted = pltpu.roll(old_ref[0], shift=keep, axis=1)
        tail = jnp.concatenate([jnp.zeros((tk - tq, NSA_KV_DIM), F32), new_rows_ref[...]], axis=0).T
        tail = jnp.concatenate([jnp.zeros((NSA_KV_DIM, w - tk), F32), tail], axis=1)
        dst_ref[0] = jnp.where(lane_w >= keep, tail, shifted)


def nsa_sample(q, skn, svn, wkn, wvn, small, pools, win_k, win_v, page_table, rel_bias, pk, pv, B, T):
    n_pages = page_table.shape[1]
    past = n_pages * PAGE_SIZE
    W = win_k.shape[1]
    L = past + T
    n_cmp = (L - CMP_LEN) // CMP_STRIDE + 1
    assert PAGE_SIZE == ATT_TILE and T % 8 == 0 and T <= SEL_BLOCK and past % SEL_BLOCK == 0
    assert (n_cmp - 1) * CMP_STRIDE + CMP_LEN <= past and n_cmp >= past // CMP_STRIDE - CMP_HALVES + 1
    assert W % ATT_TILE == 0 and W + T >= WINDOW
    n_sel = -(-L // SEL_BLOCK)
    n_top = min(SEL_TOP, n_sel)
    w_keep = min(WINDOW, W + T)
    bc, bs, bw = sample_bias_tables(rel_bias, past, T, W)
    ov = overlap_matrix(past // CMP_STRIDE, n_sel)
    ex = expand_matrix(past + ATT_TILE)
    n_pool = pools[0].shape[0]
    assert W == w_keep
    feat_major = lambda a: a.transpose(0, 2, 3, 1).reshape(a.shape[0], NSA_KV_DIM, a.shape[1])
    cmp_pools = [feat_major(p) for p in pools[:2]]
    sel_pools = [feat_major(p) for p in pools[2:]]
    wk3 = feat_major(win_k)
    wv3 = feat_major(win_v)

    full = lambda a: pl.BlockSpec(a.shape, lambda b, pt: (0,) * a.ndim)
    row = lambda w: pl.BlockSpec((T, w), lambda b, pt: (b, 0))
    seq = lambda n: pl.BlockSpec((1, NSA_KV_DIM, n), lambda b, pt: (b, 0, 0))
    pool_args = cmp_pools + sel_pools
    pool_specs = [pl.BlockSpec(memory_space=pl.ANY)] * len(pool_args)
    tables = (bc, bs, bw, ov, ex, pk[0], pv[0], pk[1], pv[1], pk[2], pv[2])
    grid_spec = pltpu.PrefetchScalarGridSpec(
        num_scalar_prefetch=1,
        grid=(B,),
        in_specs=[row(NSA_Q_DIM), row(NSA_KV_DIM), row(NSA_KV_DIM), row(NSA_KV_DIM), row(NSA_KV_DIM), row(LANES),
                  seq(W), seq(W)] + [full(a) for a in tables] + pool_specs,
        out_specs=[row(NSA_Q_DIM), seq(w_keep), seq(w_keep)],
        scratch_shapes=[pltpu.VMEM((NSA_HEADS * T, past + ATT_TILE), F32),
                        pltpu.VMEM((4, ATT_TILE, NSA_KV_DIM), F32),
                        pltpu.VMEM((2, past, NSA_KV_DIM), F32),
                        pltpu.VMEM((2, len(pool_args), n_pages, NSA_KV_DIM, PAGE_SIZE), F32),
                        pltpu.SemaphoreType.DMA((2,))],
    )
    o, wk_next, wv_next = pl.pallas_call(
        functools.partial(_nsa_sample_body, n_pages=n_pages, n_sel=n_sel, n_top=n_top, past=past),
        grid_spec=grid_spec,
        out_shape=[jax.ShapeDtypeStruct((B * T, NSA_Q_DIM), F32),
                   jax.ShapeDtypeStruct((B, NSA_KV_DIM, w_keep), F32),
                   jax.ShapeDtypeStruct((B, NSA_KV_DIM, w_keep), F32)],
        compiler_params=pltpu.CompilerParams(dimension_semantics=("arbitrary",), vmem_limit_bytes=VMEM_LIMIT),
        name="nsa_sample",
    )(page_table, q, skn, svn, wkn, wvn, small, wk3, wv3, *tables, *pool_args)
    row_major = lambda a: a.reshape(B, NSA_KV_HEADS, HEAD_DIM, w_keep).transpose(0, 3, 1, 2)
    return o, row_major(wk_next), row_major(wv_next)


HIST_ROWS = 8
DN_CHUNKS_PER_STEP = 4
DN_SEQS_PER_STEP = 8
DN_INV_BASE = 8


def _split_bf16(x):
    hi = x.astype(BF16)
    lo = (x - hi.astype(F32)).astype(BF16)
    return hi, lo


def _bmm(a, b):
    return jnp.einsum('hij,hjk->hik', a, b, preferred_element_type=F32)


def _bmm_nt(a, b):
    return jnp.einsum('hid,hjd->hij', a, b, preferred_element_type=F32)


def _cumsum_rows(tri, x):
    hi, lo = _split_bf16(x)
    lo2 = (x - hi.astype(F32) - lo.astype(F32)).astype(BF16)
    d = lambda y: jnp.dot(tri, y, preferred_element_type=F32)
    return d(hi) + (d(lo) + d(lo2))


def _softplus(x):
    return jnp.maximum(x, 0.0) + jnp.log1p(jnp.exp(-jnp.abs(x)))


def _silu(x):
    return x * jax.nn.sigmoid(x)


def _deltanet_body(x_ref, sm_ref, dg_ref, hist_ref, s0_ref, cw_ref, apar_ref, dt_ref, ng_ref,
                   o_ref, s_out_ref, hist_out_ref, xbuf_ref, s_ref, *, c, n_seq):
    j = pl.program_id(1)
    rows = x_ref.shape[0]
    seq_rows = rows // n_seq
    n_grp = seq_rows // c
    hd = DN_HEAD_DIM
    n_hist = CONV_W - 1

    n_pair = DN_HEADS // 2
    zero_blk = jnp.zeros((hd, hd), F32)

    @pl.when(j == 0)
    def _():
        xbuf_ref[:, HIST_ROWS - n_hist:HIST_ROWS, :] = hist_ref[...]
        for sq in range(n_seq):
            for p in range(n_pair):
                s_ref[sq * n_pair + p] = jnp.concatenate(
                    [jnp.concatenate([s0_ref[sq, 2 * p], zero_blk], axis=1),
                     jnp.concatenate([zero_blk, s0_ref[sq, 2 * p + 1]], axis=1)], axis=0)

    xbuf_ref[:, HIST_ROWS:HIST_ROWS + seq_rows, :] = x_ref[...].reshape(n_seq, seq_rows, x_ref.shape[1])
    y = None
    for t in range(CONV_W):
        lo = HIST_ROWS - n_hist + t
        term = xbuf_ref[:, lo:lo + seq_rows, :] * cw_ref[t:t + 1, :]
        y = term if y is None else y + term
    y = _silu(y).reshape(rows, x_ref.shape[1])
    tail = xbuf_ref[:, HIST_ROWS + seq_rows - n_hist:HIST_ROWS + seq_rows, :]
    xbuf_ref[:, HIST_ROWS - n_hist:HIST_ROWS, :] = tail

    sm = sm_ref[...]
    beta_all = jax.nn.sigmoid(sm)
    g_all = apar_ref[...] * _softplus(sm + dt_ref[...])
    row = lax.broadcasted_iota(jnp.int32, (rows, rows), 0)
    col = lax.broadcasted_iota(jnp.int32, (rows, rows), 1)
    tri = jnp.where((row >= col) & (row // c == col // c), 1.0, 0.0).astype(BF16)
    gcum_all = _cumsum_rows(tri, g_all)
    gcum_t = gcum_all.T

    wd, wj = 2 * hd, 2 * c
    first_d = lax.broadcasted_iota(jnp.int32, (1, wd), 1) < hd
    first_j = lax.broadcasted_iota(jnp.int32, (1, wj), 1) < c
    rj = lax.broadcasted_iota(jnp.int32, (c, wj), 0)
    cj = lax.broadcasted_iota(jnp.int32, (c, wj), 1)
    cj = jnp.where(cj < c, cj, cj - c)
    incl, strict = rj >= cj, rj > cj
    eye = jnp.where(rj == cj, 1.0, 0.0)
    same_head = (lax.broadcasted_iota(jnp.int32, (wd, wd), 0) < hd) == (lax.broadcasted_iota(jnp.int32, (wd, wd), 1) < hd)
    ones_bd = jnp.where(same_head, 1.0, 0.0).astype(BF16)
    first_rows = lax.broadcasted_iota(jnp.int32, (wd, 1), 0) < hd

    items = [(g, p) for g in range(n_seq * n_grp) for p in range(n_pair)]

    def block_diag(x, first):
        zero = jnp.zeros_like(x)
        return jnp.concatenate([jnp.where(first, x, zero), jnp.where(first, zero, x)], axis=1)

    def head_sums(x):
        n = x.shape[0]
        hi, lo = _split_bf16(x.reshape(n * c, wd))
        tot = jnp.dot(hi, ones_bd, preferred_element_type=F32) + jnp.dot(lo, ones_bd, preferred_element_type=F32)
        return tot.reshape(n, c, wd)

    def mm_hi(a_parts, b_parts):
        return _bmm(a_parts[0], b_parts[0]) + (_bmm(a_parts[0], b_parts[1]) + _bmm(a_parts[1], b_parts[0]))

    def per_head(cols, base, first):
        return jnp.stack([jnp.where(first, cols[g * c:(g + 1) * c, base + 2 * p:base + 2 * p + 1],
                                    cols[g * c:(g + 1) * c, base + 2 * p + 1:base + 2 * p + 2])
                          for g, p in items], axis=0)

    lanes = lambda x, off: jnp.stack([x[g * c:(g + 1) * c, off + p * wd:off + (p + 1) * wd] for g, p in items], axis=0)
    q = lanes(y, 0)
    k = lanes(y, DN_DIM)
    v = lanes(y, 2 * DN_DIM)
    q = q * lax.rsqrt(head_sums(q * q) + EPS) * hd ** -0.5
    k = k * lax.rsqrt(head_sums(k * k) + EPS)
    beta = per_head(beta_all, SM_BETA, first_d)
    gc_d = per_head(gcum_all, SM_DECAY, first_d)
    gc_j = per_head(gcum_all, SM_DECAY, first_j)
    gr_j = jnp.stack([jnp.concatenate([gcum_t[SM_DECAY + 2 * p:SM_DECAY + 2 * p + 1, g * c:(g + 1) * c],
                                       gcum_t[SM_DECAY + 2 * p + 1:SM_DECAY + 2 * p + 2, g * c:(g + 1) * c]], axis=1)
                      for g, p in items], axis=0)
    gl_d = gc_d[:, c - 1:c, :]
    decay = jnp.where(incl, jnp.exp(jnp.where(incl, gc_j - gr_j, 0.0)), 0.0)
    eg = jnp.exp(gc_d)
    kb = k * beta
    k_bd = block_diag(k.astype(BF16), first_d)
    a = jnp.where(strict, _bmm_nt(kb.astype(BF16), k_bd) * decay, 0.0)
    split_bd = lambda parts, first: tuple(block_diag(x, first) for x in parts)
    base = min(c, DN_INV_BASE)
    same_block = lambda b: (rj // b) == (cj // b)
    diag = jnp.where(same_block(base), a, 0.0)
    t_inv = eye - diag
    pow_parts = _split_bf16(diag)
    pow_bd = split_bd(pow_parts, first_j)
    for _ in range(int(math.log2(base)) - 1):
        pow_parts = _split_bf16(mm_hi(pow_parts, pow_bd))
        pow_bd = split_bd(pow_parts, first_j)
        t_inv = t_inv + mm_hi(_split_bf16(t_inv), pow_bd)
    blk = base
    while blk < c:
        lower = jnp.where(same_block(2 * blk) & jnp.logical_not(same_block(blk)), a, 0.0)
        t_parts = _split_bf16(t_inv)
        lt = mm_hi(_split_bf16(lower), split_bd(t_parts, first_j))
        t_inv = t_inv - mm_hi(t_parts, split_bd(_split_bf16(lt), first_j))
        blk *= 2
    vb_parts = split_bd(_split_bf16(v * beta), first_d)
    kbg_parts = split_bd(_split_bf16(kb * eg), first_d)
    rhs = tuple(jnp.concatenate([x, z], axis=2) for x, z in zip(vb_parts, kbg_parts))
    uw = mm_hi(_split_bf16(t_inv), rhs)
    u, w = uw[:, :, :wd], uw[:, :, wd:]
    qk = jnp.where(incl, _bmm_nt(q.astype(BF16), k_bd) * decay, 0.0).astype(BF16)
    wq = jnp.concatenate([w, q * eg], axis=1).astype(BF16)
    kg = (k * jnp.exp(gl_d - gc_d)).astype(BF16)
    gl_rows = jnp.where(first_rows, jnp.exp(gl_d[:, :, 0:1]), jnp.exp(gl_d[:, :, hd:hd + 1]))

    s = s_ref[...]
    outs = {}
    for kc in range(n_grp):
        pick = lambda x: jnp.concatenate([x[(sq * n_grp + kc) * n_pair:(sq * n_grp + kc + 1) * n_pair]
                                          for sq in range(n_seq)], axis=0)
        ws = _bmm(pick(wq), s.astype(BF16))
        vn = (pick(u) - ws[:, :c]).astype(BF16)
        o_k = ws[:, c:] + _bmm(pick(qk), block_diag(vn, first_d))
        kv_outer = jnp.einsum('pck,pcv->pkv', pick(kg), vn, preferred_element_type=F32)
        s = s * pick(gl_rows) + jnp.where(same_head, kv_outer, 0.0)
        for sq in range(n_seq):
            outs[sq * n_grp + kc] = o_k[sq * n_pair:(sq + 1) * n_pair]
    s_ref[...] = s
    o = jnp.concatenate([outs[g] for g in range(n_seq * n_grp)], axis=0)
    ng = jnp.concatenate([ng_ref[...], ng_ref[...]], axis=1)
    o = o * lax.rsqrt(head_sums(o * o) * (1.0 / hd) + EPS) * ng
    o = jnp.concatenate([jnp.concatenate([o[g * n_pair + p] for p in range(n_pair)], axis=1)
                         for g in range(n_seq * n_grp)], axis=0)
    o_ref[...] = (o * _silu(dg_ref[...])).astype(o_ref.dtype)

    @pl.when(j == pl.num_programs(1) - 1)
    def _():
        for sq in range(n_seq):
            for p in range(n_pair):
                s_pair = s_ref[sq * n_pair + p]
                s_out_ref[sq, 2 * p] = s_pair[0:hd, 0:hd]
                s_out_ref[sq, 2 * p + 1] = s_pair[hd:2 * hd, hd:2 * hd]
        hist_out_ref[...] = tail


def deltanet(qkv_raw, small, dg, conv_hist, s0, lp, B, T):
    c = DN_CHUNK if T >= DN_CHUNK else T
    assert T % c == 0 and c % 8 == 0 and c >= CONV_W - 1 and (c & (c - 1)) == 0
    n_grp = math.gcd(T // c, DN_CHUNKS_PER_STEP)
    n_seq = math.gcd(B, DN_SEQS_PER_STEP) if T == c else 1
    seq_rows = n_grp * c
    step = n_seq * seq_rows
    n = T // seq_rows
    put = lambda vals, base: jnp.zeros((LANES,), F32).at[base + jnp.arange(DN_HEADS)].set(vals).reshape(1, LANES)
    apar = put(-jnp.exp(lp['dn_a_log'].astype(F32)), SM_DECAY)
    dtb = put(lp['dn_dt_bias'].astype(F32), SM_DECAY)
    assert n_seq == 1 or n == 1
    rows = lambda w: pl.BlockSpec((step, w), lambda b, j: (b * n + j, 0))
    per_b = lambda a: pl.BlockSpec((n_seq,) + a.shape[1:], lambda b, j: (b,) + (0,) * (a.ndim - 1))
    full = lambda a: pl.BlockSpec(a.shape, lambda b, j: (0,) * a.ndim)
    ng = lp['dn_norm_g'].reshape(1, DN_HEAD_DIM).astype(F32)
    return pl.pallas_call(
        functools.partial(_deltanet_body, c=c, n_seq=n_seq),
        grid=(B // n_seq, n),
        in_specs=[rows(DN_CONV_DIM), rows(LANES), rows(DN_DIM), per_b(conv_hist), per_b(s0),
                  full(lp['conv_w']), full(apar), full(dtb), full(ng)],
        out_specs=[rows(DN_DIM), per_b(s0), per_b(conv_hist)],
        out_shape=[jax.ShapeDtypeStruct((B * T, DN_DIM), BF16 if step % 16 == 0 else F32),
                   jax.ShapeDtypeStruct(s0.shape, F32),
                   jax.ShapeDtypeStruct(conv_hist.shape, F32)],
        scratch_shapes=[pltpu.VMEM((n_seq, HIST_ROWS + seq_rows, DN_CONV_DIM), F32),
                        pltpu.VMEM((n_seq * (DN_HEADS // 2), 2 * DN_HEAD_DIM, 2 * DN_HEAD_DIM), F32)],
        compiler_params=pltpu.CompilerParams(dimension_semantics=("parallel", "arbitrary"),
                                             vmem_limit_bytes=VMEM_LIMIT),
        name="deltanet",
    )(qkv_raw, small, dg, conv_hist, s0, lp['conv_w'], apar, dtb, ng)


MOE_EXPERTS_PER_STEP = 4
RT_GROUP = 0
RT_EXPERT = N_GROUPS


def pack_router(wg, bg, we, be):
    w = jnp.concatenate([wg, we], axis=1)
    b = jnp.concatenate([bg, be], axis=0)
    pad = LANES - w.shape[1]
    return jnp.pad(w, ((0, 0), (0, pad))).astype(BF16), jnp.pad(b, (0, pad)).reshape(1, LANES).astype(F32)


def _route(r):
    lane = lax.broadcasted_iota(jnp.int32, r.shape, 1).astype(F32)
    big = float(LANES)
    is_grp = lane < N_GROUPS
    lg = jnp.where(is_grp, r, -jnp.inf)
    mg = jnp.max(lg, axis=-1, keepdims=True)
    eg = jnp.where(is_grp, jnp.exp(lg - mg), 0.0)
    grp = jnp.min(jnp.where(lg == mg, lane, big), axis=-1, keepdims=True)
    wg = 1.0 / jnp.sum(eg, axis=-1, keepdims=True)
    first = RT_EXPERT + grp * EXPERTS_PER_GROUP
    in_grp = (lane >= first) & (lane < first + EXPERTS_PER_GROUP)
    le = jnp.where(in_grp, r, -jnp.inf)
    me = jnp.max(le, axis=-1, keepdims=True)
    ee = jnp.where(in_grp, jnp.exp(le - me), 0.0)
    pe = jnp.where(in_grp, ee / jnp.sum(ee, axis=-1, keepdims=True), -1.0)
    v1 = jnp.max(pe, axis=-1, keepdims=True)
    i1 = jnp.min(jnp.where(pe == v1, lane, big), axis=-1, keepdims=True)
    pe2 = jnp.where(lane == i1, -1.0, pe)
    v2 = jnp.max(pe2, axis=-1, keepdims=True)
    i2 = jnp.min(jnp.where(pe2 == v2, lane, big), axis=-1, keepdims=True)
    tot = v1 + v2
    return jnp.where(lane == i1, wg * (v1 / tot), jnp.where(lane == i2, wg * (v2 / tot), 0.0))


def _finish_body(x_ref, on_ref, od_ref, g1_ref, sc_ref, sh_ref, g2_ref, won_ref, wod_ref, ng_ref, wr_ref, br_ref,
                 wgu_ref, wd_ref, fg_ref, o_ref, x1_ref, h_ref, gate_ref, y_ref):
    e = pl.program_id(1)
    tm, d = x_ref.shape
    nb = g1_ref.shape[0]
    per_batch = lambda a: a.reshape(nb, tm // nb, d)

    @pl.when(e == 0)
    def _():
        mix = jnp.dot(on_ref[...].astype(BF16), won_ref[...], preferred_element_type=F32) \
            + jnp.dot(od_ref[...].astype(BF16), wod_ref[...], preferred_element_type=F32)
        x1 = per_batch(x_ref[...]) + g1_ref[...] * per_batch(mix)
        x1 = x1.reshape(tm, d)
        x1_ref[...] = x1
        xn = x1 * lax.rsqrt(jnp.mean(x1 * x1, axis=-1, keepdims=True) + EPS) * ng_ref[...]
        h = (per_batch(xn) * (1 + sc_ref[...]) + sh_ref[...]).reshape(tm, d).astype(BF16)
        h_ref[...] = h
        gate_ref[...] = _route(jnp.dot(h, wr_ref[...], preferred_element_type=F32) + br_ref[...])
        y_ref[...] = jnp.zeros(y_ref.shape, F32)

    h = h_ref[...]
    lane = lax.broadcasted_iota(jnp.int32, gate_ref.shape, 1)
    y = y_ref[...]
    for k in range(wgu_ref.shape[0]):
        gu = jnp.dot(h, wgu_ref[k], preferred_element_type=F32)
        he = _silu(gu[:, :D_EXPERT]) * gu[:, D_EXPERT:]
        out = jnp.dot(he.astype(BF16), wd_ref[k], preferred_element_type=F32)
        expert_lane = RT_EXPERT + e * wgu_ref.shape[0] + k
        g_col = jnp.sum(jnp.where(lane == expert_lane, gate_ref[...], 0.0), axis=-1, keepdims=True)
        y = y + g_col * out
    y_ref[...] = y

    @pl.when(e == pl.num_programs(1) - 1)
    def _():
        x2 = per_batch(x1_ref[...]) + g2_ref[...] * per_batch(y_ref[...])
        x2 = x2.reshape(tm, d)
        o_ref[...] = x2 * lax.rsqrt(jnp.mean(x2 * x2, axis=-1, keepdims=True) + EPS) * fg_ref[...]


def finish_layer(x, o_nsa, o_dn, g1, sc2, sh2, g2, packed, norm_ffn_g, final_g, *, tm=1024):
    B, T, D = x.shape
    n = B * T
    tm = min(tm, n)
    rows = max(tm // T, 1)
    per = max(T // tm, 1)
    row = lambda w: pl.BlockSpec((tm, w), lambda i, e: (i, 0))
    mod = pl.BlockSpec((rows, 1, D), lambda i, e: (i // per, 0, 0))
    full = lambda a: pl.BlockSpec(a.shape, lambda i, e: (0,) * a.ndim)
    expert = lambda a: pl.BlockSpec((MOE_EXPERTS_PER_STEP,) + a.shape[1:], lambda i, e: (e, 0, 0))
    vec = lambda v: v.reshape(1, D).astype(F32)
    args = (x.reshape(n, D), o_nsa, o_dn, g1, sc2, sh2, g2, packed['w_out_nsa'], packed['w_out_dn'],
            vec(norm_ffn_g), packed['router_w'], packed['router_b'], packed['w_gate_up'], packed['w_down'],
            vec(final_g))
    specs = [row(D), row(o_nsa.shape[1]), row(o_dn.shape[1]), mod, mod, mod, mod] + \
            [full(a) for a in args[7:12]] + [expert(args[12]), expert(args[13]), full(args[14])]
    return pl.pallas_call(
        _finish_body,
        grid=(n // tm, N_EXPERTS // MOE_EXPERTS_PER_STEP),
        in_specs=specs,
        out_specs=row(D),
        out_shape=jax.ShapeDtypeStruct((n, D), F32),
        scratch_shapes=[pltpu.VMEM((tm, D), F32), pltpu.VMEM((tm, D), BF16), pltpu.VMEM((tm, LANES), F32),
                        pltpu.VMEM((tm, D), F32)],
        compiler_params=pltpu.CompilerParams(dimension_semantics=("parallel", "arbitrary"),
                                             vmem_limit_bytes=VMEM_LIMIT),
        name="finish_layer",
    )(*args).reshape(B, T, D)


def permute_w_out_nsa(w_out):
    w = w_out[:NSA_Q_DIM].reshape(NSA_KV_HEADS, NSA_GROUP, HEAD_DIM, -1)
    return w.transpose(1, 0, 2, 3).reshape(NSA_Q_DIM, -1).astype(BF16)


def layer_prompt(x, mod, lp, rel_bias, packed):
    B, T, _ = x.shape
    sh1, sc1, g1, sh2, sc2, g2 = mod
    (q, ck, cv, sk, sv, wk, wv, skb, svb, wkb, wvb, qkv_raw, dg, small, ck_rows, cv_rows) = in_projection(
        x, lp['norm_mix_g'], sc1, sh1, packed['w_in'], feature_major=True)
    kc, vc = compress_tokens(ck_rows, cv_rows, packed['cmp_k'], packed['cmp_v'], B)
    o_nsa = nsa_prompt(q, kc, vc, skb, svb, wkb, wvb, small, rel_bias, B, T)
    kv = lambda r: r.reshape(B, NSA_KV_HEADS, HEAD_DIM, -1).transpose(0, 3, 1, 2)
    w_keep = min(WINDOW, T)
    ck, cv, sk, sv, wk, wv = kv(ck), kv(cv), kv(sk), kv(sv), kv(wk[:, :, T - w_keep:]), kv(wv[:, :, T - w_keep:])
    conv_hist = jnp.zeros((B, CONV_W - 1, DN_CONV_DIM), x.dtype)
    s0 = jnp.zeros((B, DN_HEADS, DN_HEAD_DIM, DN_HEAD_DIM), x.dtype)
    o_dn, s_new, conv_new = deltanet(qkv_raw, small, dg, conv_hist, s0, lp, B, T)
    y = finish_layer(x, o_nsa, o_dn, g1, sc2, sh2, g2, packed, lp['norm_ffn_g'], packed['final_norm_g'])
    return y, (ck, cv, sk, sv, wk, wv, s_new, conv_new)


def layer_sample(x, mod, cmp_k_pool, cmp_v_pool, sel_k_pool, sel_v_pool, win_k, win_v, s0, conv_hist,
                 page_table, lp, rel_bias, packed):
    B, T, _ = x.shape
    sh1, sc1, g1, sh2, sc2, g2 = mod
    (q, ck, cv, sk, sv, wk, wv, _, _, _, _, qkv_raw, dg, small) = in_projection(
        x, lp['norm_mix_g'], sc1, sh1, packed['w_in'])
    o_nsa, wk_next, wv_next = nsa_sample(q, sk, sv, wk, wv, small, (cmp_k_pool, cmp_v_pool, sel_k_pool, sel_v_pool),
                                         win_k, win_v, page_table, rel_bias, packed['cmp_k'], packed['cmp_v'], B, T)
    kv = lambda r: r.reshape(B, -1, NSA_KV_HEADS, HEAD_DIM)
    o_dn, s_new, conv_new = deltanet(qkv_raw, small, dg, conv_hist, s0, lp, B, T)
    y = finish_layer(x, o_nsa, o_dn, g1, sc2, sh2, g2, packed, lp['norm_ffn_g'], packed['final_norm_g'])
    return y, (kv(ck), kv(cv), kv(sk), kv(sv), kv(wk_next), kv(wv_next), s_new, conv_new)


def kernel(x_prompt, x_sample, cache_cmp_k, cache_cmp_v, cache_sel_k, cache_sel_v, cache_win_k, cache_win_v,
           state_delta, state_conv, page_table, c_prompt, c_sample, rel_bias, w_ada, b_ada, norm_mix_g,
           norm_ffn_g, w_in, cmp_pos_k, cmp_w1_k, cmp_w2_k, cmp_pos_v, cmp_w1_v, cmp_w2_v, conv_w, dn_a_log,
           dn_dt_bias, dn_norm_g, w_out, router_group_w, router_group_b, router_expert_w, router_expert_b,
           expert_w_gate, expert_w_up, expert_w_down, final_norm_g):
    xp, xs = x_prompt, x_sample
    new_p, new_s = [], []
    for l in range(DEPTH):
        lp = dict(norm_mix_g=norm_mix_g[l], norm_ffn_g=norm_ffn_g[l], conv_w=conv_w[l],
                  dn_a_log=dn_a_log[l], dn_dt_bias=dn_dt_bias[l], dn_norm_g=dn_norm_g[l])
        packed = dict(w_in=pack_w_in(w_in[l]),
                      cmp_k=pack_compress(cmp_pos_k[l], cmp_w1_k[l], cmp_w2_k[l]),
                      cmp_v=pack_compress(cmp_pos_v[l], cmp_w1_v[l], cmp_w2_v[l]),
                      w_out_nsa=permute_w_out_nsa(w_out[l]),
                      w_out_dn=w_out[l][NSA_Q_DIM:].astype(BF16),
                      w_gate_up=jnp.concatenate([expert_w_gate[l], expert_w_up[l]], axis=-1).astype(BF16),
                      w_down=expert_w_down[l].astype(BF16),
                      final_norm_g=final_norm_g)
        packed['router_w'], packed['router_b'] = pack_router(router_group_w[l], router_group_b[l],
                                                             router_expert_w[l], router_expert_b[l])
        n_p = c_prompt.shape[0]
        mod = modulation(jnp.concatenate([c_prompt, c_sample], axis=0), w_ada[l], b_ada[l])
        xp, st_p = layer_prompt(xp, [m[:n_p] for m in mod], lp, rel_bias, packed)
        xs, st_s = layer_sample(xs, [m[n_p:] for m in mod], cache_cmp_k[l], cache_cmp_v[l], cache_sel_k[l],
                                cache_sel_v[l],
                                cache_win_k[l], cache_win_v[l], state_delta[l], state_conv[l], page_table,
                                lp, rel_bias, packed)
        new_p.append(st_p)
        new_s.append(st_s)
    assert DEPTH == 1
    y_prompt, y_sample = xp, xs

    def stk(states, i):
        return jnp.stack([s[i] for s in states])

    return (y_prompt, y_sample,
            stk(new_p, 0), stk(new_p, 1), stk(new_p, 2), stk(new_p, 3),
            stk(new_p, 4), stk(new_p, 5), stk(new_p, 6), stk(new_p, 7),
            stk(new_s, 0), stk(new_s, 1), stk(new_s, 2), stk(new_s, 3),
            stk(new_s, 4), stk(new_s, 5), stk(new_s, 6), stk(new_s, 7))
```

```python
import functools
import math

import jax
import jax.numpy as jnp
import numpy as np
from jax import lax
from jax.experimental import pallas as pl
from jax.experimental.pallas import tpu as pltpu

D_MODEL = 1024
DEPTH = 1
PAGE_SIZE = 128

NSA_HEADS = 8
NSA_KV_HEADS = 2
NSA_GROUP = NSA_HEADS // NSA_KV_HEADS
HEAD_DIM = 64
CMP_LEN = 32
CMP_STRIDE = 16
CMP_HIDDEN = 4 * HEAD_DIM
SEL_BLOCK = 64
SEL_TOP = 16
WINDOW = 512
DN_HEADS = 8
DN_HEAD_DIM = 64
CONV_W = 4
DN_CHUNK = 64
NUM_BUCKETS = 32
MAX_DISTANCE = 128
N_GROUPS = 4
EXPERTS_PER_GROUP = 8
N_EXPERTS = N_GROUPS * EXPERTS_PER_GROUP
TOP_K_IN_GROUP = 2
D_EXPERT = D_MODEL // 4

EPS = 1e-6
NEG_INF = -1e30
F32 = jnp.float32
BF16 = jnp.bfloat16

NSA_Q_DIM = NSA_HEADS * HEAD_DIM
NSA_KV_DIM = NSA_KV_HEADS * HEAD_DIM
DN_DIM = DN_HEADS * DN_HEAD_DIM
DN_CONV_DIM = 3 * DN_DIM
MIX_WIDTH = NSA_Q_DIM + DN_DIM
IN_SPLITS = (NSA_Q_DIM, NSA_KV_DIM, NSA_KV_DIM, NSA_KV_DIM, NSA_KV_DIM, NSA_KV_DIM, NSA_KV_DIM,
             3 * NSA_HEADS, DN_CONV_DIM, DN_HEADS, DN_HEADS, DN_DIM)
IN_DIM = sum(IN_SPLITS)

LANES = 128
VMEM_LIMIT = 56 * 1024 * 1024

ATT_TILE = 128
MASKED = -1e30


def t5_bucket(rel):
    n = jnp.maximum(rel, 0)
    max_exact = NUM_BUCKETS // 2
    nf = jnp.maximum(n, 1).astype(F32)
    large = max_exact + (jnp.log(nf / max_exact) / math.log(MAX_DISTANCE / max_exact)
                         * (NUM_BUCKETS - max_exact)).astype(jnp.int32)
    large = jnp.minimum(large, NUM_BUCKETS - 1)
    return jnp.where(n < max_exact, n, large)


def _modulation_body(c_ref, w_ref, b_ref, o_ref):
    c = c_ref[...]
    x = (c * jax.nn.sigmoid(c)).astype(BF16)
    o_ref[...] = jnp.dot(x, w_ref[...].astype(BF16), preferred_element_type=F32) + b_ref[...]


def modulation(c, w_ada, b_ada, *, tn=1024):
    rows, d = c.shape
    n = w_ada.shape[1]
    mod = pl.pallas_call(
        _modulation_body,
        grid=(n // tn,),
        in_specs=[pl.BlockSpec((rows, d), lambda j: (0, 0)),
                  pl.BlockSpec((d, tn), lambda j: (0, j)),
                  pl.BlockSpec((1, tn), lambda j: (0, j))],
        out_specs=pl.BlockSpec((rows, tn), lambda j: (0, j)),
        out_shape=jax.ShapeDtypeStruct((rows, n), F32),
        compiler_params=pltpu.CompilerParams(dimension_semantics=("parallel",), vmem_limit_bytes=VMEM_LIMIT),
        name="modulation",
    )(c, w_ada, b_ada.reshape(1, n))
    return jnp.split(mod[:, None, :], 6, axis=-1)


_PK_Q = (0, NSA_Q_DIM)
_PK_KV = (_PK_Q[1], _PK_Q[1] + 6 * NSA_KV_DIM)
_PK_QKV = (_PK_KV[1], _PK_KV[1] + DN_CONV_DIM)
_PK_DG = (_PK_QKV[1], _PK_QKV[1] + DN_DIM)
_PK_SM = (_PK_DG[1], _PK_DG[1] + LANES)
PK_DIM = _PK_SM[1]
SM_GATE = 0
SM_BETA = 3 * NSA_HEADS
SM_DECAY = SM_BETA + DN_HEADS


def pack_w_in(w_in):
    offs = np.cumsum((0,) + IN_SPLITS)
    part = lambda i: w_in[:, offs[i]:offs[i + 1]]
    d = w_in.shape[0]
    q = part(0).reshape(d, NSA_KV_HEADS, NSA_GROUP, HEAD_DIM).transpose(0, 2, 1, 3).reshape(d, NSA_Q_DIM)
    q = q * HEAD_DIM ** -0.5
    small = jnp.concatenate([part(7), part(9), part(10)], axis=1)
    small = jnp.pad(small, ((0, 0), (0, LANES - small.shape[1])))
    cols = [q] + [part(i) for i in range(1, 7)] + [part(8), part(11), small]
    return jnp.concatenate(cols, axis=1).astype(BF16)


def _in_proj_body(x_ref, g_ref, sc_ref, sh_ref, w_ref, q_ref, ck_ref, cv_ref, sk_ref, sv_ref, wk_ref, wv_ref,
                  skb_ref, svb_ref, wkb_ref, wvb_ref, qkv_ref, dg_ref, sm_ref, *row_refs):
    x = x_ref[...]
    tm, d = x.shape
    nb = sc_ref.shape[0]
    xn = x * lax.rsqrt(jnp.mean(x * x, axis=-1, keepdims=True) + EPS) * g_ref[...]
    h = xn.reshape(nb, tm // nb, d) * (1 + sc_ref[...]) + sh_ref[...]
    h = h.reshape(tm, d).astype(BF16)

    def mm(span):
        return jnp.dot(h, w_ref[:, span[0]:span[1]], preferred_element_type=F32)

    q_ref[...] = mm(_PK_Q).astype(BF16)
    kv = mm(_PK_KV)
    for i, r in enumerate((ck_ref, cv_ref, sk_ref, sv_ref, wk_ref, wv_ref)):
        rows = kv[:, i * NSA_KV_DIM:(i + 1) * NSA_KV_DIM]
        if row_refs:
            r[0] = rows.T
        else:
            r[...] = rows
    for i, r in enumerate(row_refs):
        r[...] = kv[:, i * NSA_KV_DIM:(i + 1) * NSA_KV_DIM]
    for i, r in enumerate((skb_ref, svb_ref, wkb_ref, wvb_ref)):
        r[...] = kv[:, (i + 2) * NSA_KV_DIM:(i + 3) * NSA_KV_DIM].astype(BF16)
    qkv_ref[...] = mm(_PK_QKV)
    dg_ref[...] = mm(_PK_DG)
    sm_ref[...] = mm(_PK_SM)


def in_projection(x, norm_g, scale, shift, w_packed, *, tm=256, feature_major=False):
    B, T, D = x.shape
    n = B * T
    tm = min(tm, n)
    rows = max(tm // T, 1)
    per = max(T // tm, 1)
    row = lambda i: (i, 0)
    by_row = lambda w, dt: (((n, w), dt), pl.BlockSpec((tm, w), row))
    if feature_major:
        assert tm <= T and tm % LANES == 0
        kv_out = (((B, NSA_KV_DIM, T), F32), pl.BlockSpec((1, NSA_KV_DIM, tm), lambda i: (i // per, 0, i % per)))
    else:
        kv_out = by_row(NSA_KV_DIM, F32)
    outs = [by_row(NSA_Q_DIM, BF16)] + [kv_out] * 6 + [by_row(NSA_KV_DIM, BF16)] * 4 + \
           [by_row(DN_CONV_DIM, F32), by_row(DN_DIM, F32), by_row(LANES, F32)] + \
           ([by_row(NSA_KV_DIM, F32)] * 2 if feature_major else [])
    return pl.pallas_call(
        _in_proj_body,
        grid=(n // tm,),
        in_specs=[pl.BlockSpec((tm, D), row),
                  pl.BlockSpec((1, D), lambda i: (0, 0)),
                  pl.BlockSpec((rows, 1, D), lambda i: (i // per, 0, 0)),
                  pl.BlockSpec((rows, 1, D), lambda i: (i // per, 0, 0)),
                  pl.BlockSpec((D, PK_DIM), lambda i: (0, 0))],
        out_specs=[spec for _, spec in outs],
        out_shape=[jax.ShapeDtypeStruct(s, dt) for (s, dt), _ in outs],
        compiler_params=pltpu.CompilerParams(dimension_semantics=("parallel",), vmem_limit_bytes=VMEM_LIMIT),
        name="in_projection",
    )(x.reshape(n, D), norm_g.reshape(1, D), scale, shift, w_packed)


CHUNK_W = CMP_STRIDE * NSA_KV_DIM
CMP_HALVES = CMP_LEN // CMP_STRIDE


def pack_compress(pos, w1, w2):
    eye = jnp.eye(NSA_KV_HEADS, dtype=F32)
    w1r = w1.reshape(CMP_HALVES, CMP_STRIDE, HEAD_DIM, CMP_HIDDEN)
    w1p = jnp.einsum('ijdc,kl->ijkdlc', w1r, eye).reshape(CMP_HALVES, CHUNK_W, NSA_KV_HEADS * CMP_HIDDEN)
    w2p = jnp.einsum('cd,kl->kcld', w2, eye).reshape(NSA_KV_HEADS * CMP_HIDDEN, NSA_KV_DIM)
    posp = jnp.broadcast_to(pos.reshape(CMP_HALVES, CMP_STRIDE, 1, HEAD_DIM),
                            (CMP_HALVES, CMP_STRIDE, NSA_KV_HEADS, HEAD_DIM)).reshape(CMP_HALVES, CHUNK_W)
    return posp, w1p.astype(BF16), w2p.astype(BF16)


def _compress_one(rows_ref, pos_ref, w1_ref, w2_ref):
    n_chunk = rows_ref.shape[0] // CMP_STRIDE
    x = jnp.concatenate([rows_ref[pl.ds(j, n_chunk, stride=CMP_STRIDE), :] for j in range(CMP_STRIDE)], axis=1)
    hid = None
    for i in range(CMP_HALVES):
        part = jnp.dot((x + pos_ref[i:i + 1]).astype(BF16), w1_ref[i], preferred_element_type=F32)
        if i:
            part = pltpu.roll(part, shift=n_chunk - i, axis=0)
        hid = part if hid is None else hid + part
    hid = hid * jax.nn.sigmoid(hid)
    return jnp.dot(hid.astype(BF16), w2_ref[...], preferred_element_type=F32)


def _compress_body(xk_ref, xv_ref, pk_ref, pv_ref, w1k_ref, w1v_ref, w2k_ref, w2v_ref, kc_ref, vc_ref):
    kc_ref[0] = _compress_one(xk_ref, pk_ref, w1k_ref, w2k_ref).astype(BF16)
    vc_ref[0] = _compress_one(xv_ref, pv_ref, w1v_ref, w2v_ref).astype(BF16)


def compress_tokens(ck, cv, pk, pv, B):
    L = ck.shape[0] // B
    n_chunk = L // CMP_STRIDE
    full = lambda a: pl.BlockSpec(a.shape, lambda b: (0,) * a.ndim)
    rows = pl.BlockSpec((L, NSA_KV_DIM), lambda b: (b, 0))
    tok = pl.BlockSpec((1, n_chunk, NSA_KV_DIM), lambda b: (b, 0, 0))
    return pl.pallas_call(
        _compress_body,
        grid=(B,),
        in_specs=[rows, rows, full(pk[0]), full(pv[0]), full(pk[1]), full(pv[1]), full(pk[2]), full(pv[2])],
        out_specs=[tok, tok],
        out_shape=[jax.ShapeDtypeStruct((B, n_chunk, NSA_KV_DIM), BF16)] * 2,
        compiler_params=pltpu.CompilerParams(dimension_semantics=("parallel",), vmem_limit_bytes=VMEM_LIMIT),
        name="compress_tokens",
    )(ck, cv, pk[0], pv[0], pk[1], pv[1], pk[2], pv[2])


def _t5_bucket_np(rel):
    n = np.maximum(rel, 0)
    max_exact = NUM_BUCKETS // 2
    nf = np.maximum(n, 1).astype(np.float32)
    large = max_exact + (np.log(nf / max_exact) / math.log(MAX_DISTANCE / max_exact)
                         * (NUM_BUCKETS - max_exact)).astype(np.int32)
    return np.where(n < max_exact, n, np.minimum(large, NUM_BUCKETS - 1))


assert (_t5_bucket_np(np.arange(ATT_TILE + 1, 1 << 16)) == NUM_BUCKETS - 1).all()


NB_SAME, NB_PREV, NB_FAR, NB_EDGE, NB_NONE = range(5)
FAR_TILE = 512


def bias_lookup(rel_bias, rel):
    bucket = t5_bucket(rel)
    tbl = rel_bias.astype(F32)
    shape = (tbl.shape[1],) + (1,) * rel.ndim
    out = jnp.zeros((tbl.shape[1],) + rel.shape, F32)
    for b in range(NUM_BUCKETS):
        out = jnp.where(bucket == b, tbl[b].reshape(shape), out)
    return out


def near_bias_tables(rel_bias):
    t = ATT_TILE
    i = jnp.arange(t)[:, None]
    j = jnp.arange(t)[None, :]
    diag = jnp.where(i >= j, bias_lookup(rel_bias, i - j), MASKED)
    prev = bias_lookup(rel_bias, t + i - j)
    far = jnp.broadcast_to(rel_bias.astype(F32)[NUM_BUCKETS - 1][:, None, None], prev.shape)
    edge = jnp.where(j > i, far, MASKED)
    return jnp.stack([diag, prev, far, edge, jnp.full_like(far, MASKED)])


def cmp_bias_table(rel_bias, q0, tq, n_chunk):
    first_end = CMP_LEN - 1
    span = CMP_STRIDE * (n_chunk - 1)
    n_rel = tq + span
    rel = q0 - first_end - span + jnp.arange(n_rel)
    vals = jnp.where(rel >= 0, bias_lookup(rel_bias, rel), MASKED)
    vals = jnp.roll(vals, -span, axis=1)
    flat = jnp.tile(vals, (1, n_chunk))[:, :n_chunk * (n_rel - CMP_STRIDE)]
    by_token = flat.reshape(-1, n_chunk, n_rel - CMP_STRIDE)[:, :, :tq]
    return jnp.swapaxes(by_token, 1, 2)


def cmp_bias_tiles(rel_bias, n_q, tile, n_chunk):
    assert tile % CMP_STRIDE == 0 and n_q % tile == 0
    step = tile // CMP_STRIDE
    n_tile = n_q // tile
    lo = -(n_tile - 1) * step
    m = lo + jnp.arange(n_chunk - lo)[None, :]
    rel = jnp.arange(tile)[:, None] - CMP_STRIDE * m - (CMP_LEN - 1)
    template = jnp.where(rel >= 0, bias_lookup(rel_bias, rel), MASKED)
    return jnp.stack([template[:, :, -lo - i * step:-lo - i * step + n_chunk] for i in range(n_tile)], axis=0)


def overlap_matrix(n_chunk, n_sel):
    c0 = np.arange(n_chunk)[:, None] * CMP_STRIDE
    s0 = np.arange(LANES)[None, :] * SEL_BLOCK
    ov = np.clip(np.minimum(c0 + CMP_LEN, s0 + SEL_BLOCK) - np.maximum(c0, s0), 0, None) / CMP_LEN
    ov = np.where(np.arange(LANES)[None, :] < n_sel, ov, 0.0)
    return jnp.asarray(ov, F32)


def expand_matrix(n_keys):
    e = (np.arange(n_keys)[None, :] // SEL_BLOCK) == np.arange(LANES)[:, None]
    return jnp.asarray(e, BF16)


def _head_queries(q_ref, tq):
    low = lax.broadcasted_iota(jnp.int32, (tq, LANES), 1) < HEAD_DIM
    zero = jnp.zeros((tq, LANES), BF16)
    parts = []
    for kv in range(NSA_KV_HEADS):
        for g in range(NSA_GROUP):
            blk = q_ref[:, g * LANES:(g + 1) * LANES]
            parts.append(jnp.where(low if kv == 0 else ~low, blk, zero))
    return parts


def _stack_queries(q_ref, tq):
    return jnp.concatenate(_head_queries(q_ref, tq), axis=0)


def _dot_nt(a, b):
    return lax.dot_general(a, b, (((1,), (1,)), ((), ())), preferred_element_type=F32)


def _selection_mask(p_group, ov_ref, ex_ref, q0, n_sel, n_top):
    tq = p_group.shape[0]
    n_row = -(-n_sel // 8) * 8
    imp = lax.dot_general(ov_ref[...], p_group, (((0,), (1,)), ((), ())), precision=lax.Precision.HIGHEST,
                          preferred_element_type=F32)[0:n_row]
    blk = lax.broadcasted_iota(jnp.int32, (n_row, tq), 0)
    q_pos = q0 + lax.broadcasted_iota(jnp.int32, (n_row, tq), 1)
    q_blk = q_pos // SEL_BLOCK
    forced = jnp.where(blk == 0, 1.0, jnp.where(blk == q_blk, 1.0, jnp.where(blk == q_blk - 1, 1.0, 0.0)))
    score = jnp.where(blk * SEL_BLOCK <= q_pos, imp, NEG_INF)
    score = jnp.where(forced > 0.5, 1e9, score)
    score = jnp.where(blk < n_sel, score, -3e38)
    rank = jnp.zeros((n_row, tq), F32)
    for s in range(n_sel):
        other = score[s:s + 1, :]
        tie = jnp.where(blk > s, 1.0, 0.0)
        rank = rank + jnp.where(other > score, 1.0, jnp.where(other == score, tie, 0.0))
    sel = jnp.where(rank < n_top, 1.0, 0.0).astype(BF16)
    keys = lax.dot_general(sel, ex_ref[0:n_row, :], (((0,), (0,)), ((), ())), preferred_element_type=F32)
    return (keys - 1.0) * (-MASKED)


def _gate_columns(gl_ref, j):
    g = jax.nn.sigmoid(gl_ref[...])
    cols = [g[:, SM_GATE + 3 * h + j:SM_GATE + 3 * h + j + 1] for h in range(NSA_HEADS)]
    return jnp.stack(cols, axis=0)


def _nsa_prompt_body(q_ref, kc_ref, vc_ref, sk_ref, sv_ref, wk_ref, wv_ref, bc_ref, nb_ref, gl_ref, ov_ref, ex_ref,
                     o_ref, mb_ref, s_ref, sw_ref, stat_ref, mrep_ref, acc_ref, out_ref, *, n_sel, n_top):
    i = pl.program_id(1)
    tq = q_ref.shape[0]
    h = NSA_HEADS
    q = _stack_queries(q_ref, tq)

    s = _dot_nt(q, kc_ref[0]).reshape(h, tq, -1) + bc_ref[0]
    m = jnp.max(s, axis=-1, keepdims=True)
    p = jnp.where(s > 0.5 * MASKED, jnp.exp(s - m), 0.0)
    l = jnp.sum(p, axis=-1, keepdims=True)
    p = p * (1.0 / jnp.where(l > 0.0, l, 1.0))
    o_cmp = jnp.dot(p.reshape(h * tq, -1).astype(BF16), vc_ref[0], preferred_element_type=F32)
    out_ref[...] = _gate_columns(gl_ref, 0) * o_cmp.reshape(h, tq, LANES)

    for kv in range(NSA_KV_HEADS):
        pk = p[kv * NSA_GROUP]
        for g in range(1, NSA_GROUP):
            pk = pk + p[kv * NSA_GROUP + g]
        mb_ref[kv] = _selection_mask(pk, ov_ref, ex_ref, i * tq, n_sel, n_top)

    tk = ATT_TILE
    n_back = WINDOW // tk
    wide = (n_back + 1) * tk
    far_blk = FAR_TILE // tk
    n_tile = sk_ref.shape[0] // tk
    tail_col = s_ref.shape[-1] - wide
    kvg = lambda x: x.reshape(NSA_KV_HEADS, NSA_GROUP, tq, x.shape[-1])
    blocks = lambda x: [x[..., c * LANES:(c + 1) * LANES] for c in range(x.shape[-1] // LANES)]
    lane_max = lambda x: functools.reduce(jnp.maximum, blocks(x))
    lane_sum = lambda x: functools.reduce(jnp.add, blocks(x))
    rep = lambda x, n: jnp.concatenate([x] * n, axis=-1)
    far_bias = nb_ref[NB_FAR][:, 0:1, 0:1]

    n_far = jnp.maximum(i - 1, 0) // far_blk
    tail0 = jnp.minimum(n_far * far_blk, n_tile - (n_back + 1))
    tail_rows = pl.ds(pl.multiple_of(tail0 * tk, tk), wide)
    tiles = []
    for c in range(n_back + 1):
        blk = tail0 + c
        kind = jnp.where((blk < n_far * far_blk) | (blk > i), NB_NONE, jnp.minimum(i - blk, NB_FAR))
        tiles.append(nb_ref[kind])
    s = _dot_nt(q, sk_ref[tail_rows, :]).reshape(h, tq, wide) + jnp.concatenate(tiles, axis=-1)
    s = (kvg(s) + mb_ref[:, :, tail_rows][:, None]).reshape(h, tq, wide)
    s_ref[:, :, tail_col:tail_col + wide] = s
    stat_ref[0] = lane_max(s)

    win0 = jnp.maximum(i - n_back, 0)
    win_rows = pl.ds(pl.multiple_of(win0 * tk, tk), wide)
    tiles = []
    for c in range(n_back + 1):
        r = i - (win0 + c)
        tiles.append(nb_ref[jnp.where(r < 0, NB_NONE, jnp.where(r == n_back, NB_EDGE, jnp.minimum(r, NB_FAR)))])
    s = _dot_nt(q, wk_ref[win_rows, :]).reshape(h, tq, wide) + jnp.concatenate(tiles, axis=-1)
    sw_ref[...] = s
    stat_ref[1] = lane_max(s)

    def far_scores(f, carry):
        rows = pl.ds(pl.multiple_of(f * FAR_TILE, FAR_TILE), FAR_TILE)
        s = _dot_nt(q, sk_ref[rows, :]).reshape(h, tq, FAR_TILE) + far_bias
        s = (kvg(s) + mb_ref[:, :, rows][:, None]).reshape(h, tq, FAR_TILE)
        s_ref[:, :, rows] = s
        stat_ref[0] = jnp.maximum(stat_ref[0], lane_max(s))
        return carry
    lax.fori_loop(0, n_far, far_scores, 0)

    for br in range(2):
        mrep_ref[br] = jnp.broadcast_to(jnp.max(stat_ref[br], axis=-1, keepdims=True), (h, tq, LANES))
        stat_ref[br] = jnp.zeros((h, tq, LANES), F32)
    acc_ref[...] = jnp.zeros(acc_ref.shape, F32)

    def far_values(f, carry):
        rows = pl.ds(pl.multiple_of(f * FAR_TILE, FAR_TILE), FAR_TILE)
        p = jnp.exp(s_ref[:, :, rows] - rep(mrep_ref[0], far_blk))
        stat_ref[0] += lane_sum(p)
        acc_ref[...] += jnp.dot(p.reshape(h * tq, FAR_TILE).astype(BF16), sv_ref[rows, :],
                                preferred_element_type=F32).reshape(h, tq, LANES)
        return carry
    lax.fori_loop(0, n_far, far_values, 0)

    p = jnp.exp(s_ref[:, :, tail_col:tail_col + wide] - rep(mrep_ref[0], n_back + 1))
    l = jnp.sum(stat_ref[0] + lane_sum(p), axis=-1, keepdims=True)
    o_sel = acc_ref[...] + jnp.dot(p.reshape(h * tq, wide).astype(BF16), sv_ref[tail_rows, :],
                                   preferred_element_type=F32).reshape(h, tq, LANES)
    out_ref[...] += (_gate_columns(gl_ref, 1) * (1.0 / l)) * o_sel

    p = jnp.exp(sw_ref[...] - rep(mrep_ref[1], n_back + 1))
    l = jnp.sum(lane_sum(p), axis=-1, keepdims=True)
    o_win = jnp.dot(p.reshape(h * tq, wide).astype(BF16), wv_ref[win_rows, :],
                    preferred_element_type=F32).reshape(h, tq, LANES)
    out_ref[...] += (_gate_columns(gl_ref, 2) * (1.0 / l)) * o_win

    low = lax.broadcasted_iota(jnp.int32, (tq, LANES), 1) < HEAD_DIM
    for g in range(NSA_GROUP):
        o_ref[:, g * LANES:(g + 1) * LANES] = jnp.where(low, out_ref[g], out_ref[NSA_GROUP + g]).astype(BF16)


def nsa_prompt(q, kc, vc, skb, svb, wkb, wvb, small, rel_bias, B, T):
    tq = ATT_TILE
    nq = T // tq
    assert T % FAR_TILE == 0 and T >= WINDOW + ATT_TILE and WINDOW % ATT_TILE == 0
    wide = WINDOW + ATT_TILE
    far_max = (nq - 2) // (FAR_TILE // ATT_TILE) * FAR_TILE
    n_chunk = kc.shape[1]
    n_sel = -(-T // SEL_BLOCK)
    n_top = min(SEL_TOP, n_sel)
    bias_c = cmp_bias_tiles(rel_bias, T, tq, n_chunk)
    nb = near_bias_tables(rel_bias)
    ov = overlap_matrix(n_chunk, n_sel)
    ex = expand_matrix(T)
    h = NSA_HEADS
    full = lambda a: pl.BlockSpec(a.shape, lambda b, i: (0,) * a.ndim)
    seq = pl.BlockSpec((T, NSA_KV_DIM), lambda b, i: (b, 0))
    tok = pl.BlockSpec((1, n_chunk, NSA_KV_DIM), lambda b, i: (b, 0, 0))
    return pl.pallas_call(
        functools.partial(_nsa_prompt_body, n_sel=n_sel, n_top=n_top),
        grid=(B, nq),
        in_specs=[pl.BlockSpec((tq, NSA_Q_DIM), lambda b, i: (b * nq + i, 0)),
                  tok, tok, seq, seq, seq, seq,
                  pl.BlockSpec((1, h, tq, n_chunk), lambda b, i: (i, 0, 0, 0)),
                  full(nb),
                  pl.BlockSpec((tq, LANES), lambda b, i: (b * nq + i, 0)),
                  full(ov), full(ex)],
        out_specs=pl.BlockSpec((tq, NSA_Q_DIM), lambda b, i: (b * nq + i, 0)),
        out_shape=jax.ShapeDtypeStruct((B * T, NSA_Q_DIM), BF16),
        scratch_shapes=[pltpu.VMEM((NSA_KV_HEADS, tq, T), F32),
                        pltpu.VMEM((h, tq, far_max + wide), F32), pltpu.VMEM((h, tq, wide), F32),
                        pltpu.VMEM((2, h, tq, LANES), F32), pltpu.VMEM((2, h, tq, LANES), F32),
                        pltpu.VMEM((h, tq, LANES), F32), pltpu.VMEM((h, tq, LANES), F32)],
        compiler_params=pltpu.CompilerParams(dimension_semantics=("parallel", "arbitrary"),
                                             vmem_limit_bytes=VMEM_LIMIT),
        name="nsa_prompt",
    )(q, kc, vc, skb, svb, wkb, wvb, bias_c, nb, small, ov, ex)


def sample_bias_tables(rel_bias, past, T, W):
    t = jnp.arange(T)[:, None]
    pad = jnp.arange(ATT_TILE)[None, :]
    new = jnp.where((pad < T) & (t >= pad), bias_lookup(rel_bias, t - pad), MASKED)
    sel = bias_lookup(rel_bias, past + t - jnp.arange(past)[None, :])
    rel_w = W + t - jnp.arange(W)[None, :]
    win = jnp.where(rel_w < WINDOW, bias_lookup(rel_bias, rel_w), MASKED)
    cmp = cmp_bias_table(rel_bias, past, T, past // CMP_STRIDE)
    return cmp, jnp.concatenate([sel, new], axis=-1), jnp.concatenate([win, new], axis=-1)


def _softmax_rows(s):
    m = jnp.max(s, axis=-1, keepdims=True)
    p = jnp.exp(s - m)
    return p * (1.0 / jnp.sum(p, axis=-1, keepdims=True))


def _nsa_sample_body(pt_ref, q_ref, skn_ref, svn_ref, wkn_ref, wvn_ref, gl_ref, wink_ref, winv_ref,
                     bc_ref, bs_ref, bw_ref, ov_ref, ex_ref, pk_ref, pv_ref, w1k_ref, w1v_ref, w2k_ref, w2v_ref,
                     ck_pool, cv_pool, sk_pool, sv_pool, o_ref, wko_ref, wvo_ref,
                     s_ref, new_ref, rows_ref, page_ref, sem_ref, *, n_pages, n_sel, n_top, past):
    h = NSA_HEADS
    tq = q_ref.shape[0]
    tk = ATT_TILE
    b = pl.program_id(0)
    pools = (ck_pool, cv_pool, sk_pool, sv_pool)

    def page_copy(seq, slot, pool, pg):
        return pltpu.make_async_copy(pools[pool].at[pt_ref[seq, pg]], page_ref.at[slot, pool, pg], sem_ref.at[slot])

    def start_fetch(seq, slot):
        for pool in range(len(pools)):
            for pg in range(n_pages):
                page_copy(seq, slot, pool, pg).start()

    slot = b % 2

    @pl.when(b == 0)
    def _():
        start_fetch(0, 0)

    @pl.when(b + 1 < pl.num_programs(0))
    def _():
        start_fetch(b + 1, 1 - slot)

    for pool in range(len(pools)):
        for pg in range(n_pages):
            page_copy(b, slot, pool, pg).wait()
    pages = [[page_ref.at[slot, pool, pg] for pg in range(n_pages)] for pool in range(len(pools))]

    q = _stack_queries(q_ref, tq)

    new_ref[...] = jnp.zeros(new_ref.shape, F32)
    for n, r in enumerate((skn_ref, svn_ref, wkn_ref, wvn_ref)):
        new_ref[n, 0:tq, :] = r[...]

    for pg in range(n_pages):
        rows_ref[0, pg * tk:(pg + 1) * tk, :] = pages[0][pg][...].T
        rows_ref[1, pg * tk:(pg + 1) * tk, :] = pages[1][pg][...].T
    kc = _compress_one(rows_ref.at[0], pk_ref, w1k_ref, w2k_ref).astype(BF16)
    vc = _compress_one(rows_ref.at[1], pv_ref, w1v_ref, w2v_ref).astype(BF16)
    s = _dot_nt(q, kc).reshape(h, tq, -1) + bc_ref[...]
    p = _softmax_rows(s)
    o_cmp = jnp.dot(p.reshape(h * tq, -1).astype(BF16), vc, preferred_element_type=F32)
    out = _gate_columns(gl_ref, 0) * o_cmp.reshape(h, tq, LANES)

    masks = []
    for kv in range(NSA_KV_HEADS):
        pkv = p[kv * NSA_GROUP]
        for g in range(1, NSA_GROUP):
            pkv = pkv + p[kv * NSA_GROUP + g]
        masks.append(_selection_mask(pkv, ov_ref, ex_ref, past, n_sel, n_top))
    mask = jnp.stack(masks, axis=0)
    for pg in range(n_pages):
        s_ref[:, pg * tk:(pg + 1) * tk] = jnp.dot(q, pages[2][pg][...].astype(BF16), preferred_element_type=F32)
    s_ref[:, n_pages * tk:(n_pages + 1) * tk] = _dot_nt(q, new_ref[0].astype(BF16))
    width = (n_pages + 1) * tk
    s = s_ref[...].reshape(h, tq, width) + bs_ref[...]
    s = (s.reshape(NSA_KV_HEADS, NSA_GROUP, tq, width) + mask[:, None]).reshape(h, tq, width)
    p = _softmax_rows(s).reshape(h * tq, width).astype(BF16)
    o_sel = jnp.dot(p[:, n_pages * tk:], new_ref[1].astype(BF16), preferred_element_type=F32)
    for pg in range(n_pages):
        o_sel = o_sel + _dot_nt(p[:, pg * tk:(pg + 1) * tk], pages[3][pg][...].astype(BF16))
    out = out + _gate_columns(gl_ref, 1) * o_sel.reshape(h, tq, LANES)

    w = wink_ref.shape[2]
    s_ref[:, 0:w] = jnp.dot(q, wink_ref[0].astype(BF16), preferred_element_type=F32)
    s_ref[:, w:w + tk] = _dot_nt(q, new_ref[2].astype(BF16))
    s = s_ref[:, 0:w + tk].reshape(h, tq, w + tk) + bw_ref[...]
    p = _softmax_rows(s).reshape(h * tq, w + tk).astype(BF16)
    o_win = jnp.dot(p[:, w:], new_ref[3].astype(BF16), preferred_element_type=F32) \
        + _dot_nt(p[:, 0:w], winv_ref[0].astype(BF16))
    out = out + _gate_columns(gl_ref, 2) * o_win.reshape(h, tq, LANES)

    low = lax.broadcasted_iota(jnp.int32, (tq, LANES), 1) < HEAD_DIM
    for g in range(NSA_GROUP):
        o_ref[:, g * LANES:(g + 1) * LANES] = jnp.where(low, out[g], out[NSA_GROUP + g])

    keep = w - tq
    lane_w = lax.broadcasted_iota(jnp.int32, (NSA_KV_DIM, w), 1)
    for old_ref, new_rows_ref, dst_ref in ((wink_ref, wkn_ref, wko_ref), (winv_ref, wvn_ref, wvo_ref)):
        shifted = pltpu.roll(old_ref[0], shift=keep, axis=1)
        tail = jnp.concatenate([jnp.zeros((tk - tq, NSA_KV_DIM), F32), new_rows_ref[...]], axis=0).T
        tail = jnp.concatenate([jnp.zeros((NSA_KV_DIM, w - tk), F32), tail], axis=1)
        dst_ref[0] = jnp.where(lane_w >= keep, tail, shifted)


def nsa_sample(q, skn, svn, wkn, wvn, small, pools, win_k, win_v, page_table, rel_bias, pk, pv, B, T):
    n_pages = page_table.shape[1]
    past = n_pages * PAGE_SIZE
    W = win_k.shape[1]
    L = past + T
    n_cmp = (L - CMP_LEN) // CMP_STRIDE + 1
    assert PAGE_SIZE == ATT_TILE and T % 8 == 0 and T <= SEL_BLOCK and past % SEL_BLOCK == 0
    assert (n_cmp - 1) * CMP_STRIDE + CMP_LEN <= past and n_cmp >= past // CMP_STRIDE - CMP_HALVES + 1
    assert W % ATT_TILE == 0 and W + T >= WINDOW
    n_sel = -(-L // SEL_BLOCK)
    n_top = min(SEL_TOP, n_sel)
    w_keep = min(WINDOW, W + T)
    bc, bs, bw = sample_bias_tables(rel_bias, past, T, W)
    ov = overlap_matrix(past // CMP_STRIDE, n_sel)
    ex = expand_matrix(past + ATT_TILE)
    n_pool = pools[0].shape[0]
    assert W == w_keep
    feat_major = lambda a: a.transpose(0, 2, 3, 1).reshape(a.shape[0], NSA_KV_DIM, a.shape[1])
    cmp_pools = [feat_major(p) for p in pools[:2]]
    sel_pools = [feat_major(p) for p in pools[2:]]
    wk3 = feat_major(win_k)
    wv3 = feat_major(win_v)

    full = lambda a: pl.BlockSpec(a.shape, lambda b, pt: (0,) * a.ndim)
    row = lambda w: pl.BlockSpec((T, w), lambda b, pt: (b, 0))
    seq = lambda n: pl.BlockSpec((1, NSA_KV_DIM, n), lambda b, pt: (b, 0, 0))
    pool_args = cmp_pools + sel_pools
    pool_specs = [pl.BlockSpec(memory_space=pl.ANY)] * len(pool_args)
    tables = (bc, bs, bw, ov, ex, pk[0], pv[0], pk[1], pv[1], pk[2], pv[2])
    grid_spec = pltpu.PrefetchScalarGridSpec(
        num_scalar_prefetch=1,
        grid=(B,),
        in_specs=[row(NSA_Q_DIM), row(NSA_KV_DIM), row(NSA_KV_DIM), row(NSA_KV_DIM), row(NSA_KV_DIM), row(LANES),
                  seq(W), seq(W)] + [full(a) for a in tables] + pool_specs,
        out_specs=[row(NSA_Q_DIM), seq(w_keep), seq(w_keep)],
        scratch_shapes=[pltpu.VMEM((NSA_HEADS * T, past + ATT_TILE), F32),
                        pltpu.VMEM((4, ATT_TILE, NSA_KV_DIM), F32),
                        pltpu.VMEM((2, past, NSA_KV_DIM), F32),
                        pltpu.VMEM((2, len(pool_args), n_pages, NSA_KV_DIM, PAGE_SIZE), F32),
                        pltpu.SemaphoreType.DMA((2,))],
    )
    o, wk_next, wv_next = pl.pallas_call(
        functools.partial(_nsa_sample_body, n_pages=n_pages, n_sel=n_sel, n_top=n_top, past=past),
        grid_spec=grid_spec,
        out_shape=[jax.ShapeDtypeStruct((B * T, NSA_Q_DIM), F32),
                   jax.ShapeDtypeStruct((B, NSA_KV_DIM, w_keep), F32),
                   jax.ShapeDtypeStruct((B, NSA_KV_DIM, w_keep), F32)],
        compiler_params=pltpu.CompilerParams(dimension_semantics=("arbitrary",), vmem_limit_bytes=VMEM_LIMIT),
        name="nsa_sample",
    )(page_table, q, skn, svn, wkn, wvn, small, wk3, wv3, *tables, *pool_args)
    row_major = lambda a: a.reshape(B, NSA_KV_HEADS, HEAD_DIM, w_keep).transpose(0, 3, 1, 2)
    return o, row_major(wk_next), row_major(wv_next)


HIST_ROWS = 8
DN_CHUNKS_PER_STEP = 4
DN_SEQS_PER_STEP = 8
DN_INV_BASE = 8


def _split_bf16(x):
    hi = x.astype(BF16)
    lo = (x - hi.astype(F32)).astype(BF16)
    return hi, lo


def _bmm(a, b):
    return jnp.einsum('hij,hjk->hik', a, b, preferred_element_type=F32)


def _bmm_nt(a, b):
    return jnp.einsum('hid,hjd->hij', a, b, preferred_element_type=F32)


def _cumsum_rows(tri, x):
    hi, lo = _split_bf16(x)
    lo2 = (x - hi.astype(F32) - lo.astype(F32)).astype(BF16)
    d = lambda y: jnp.dot(tri, y, preferred_element_type=F32)
    return d(hi) + (d(lo) + d(lo2))


def _softplus(x):
    return jnp.maximum(x, 0.0) + jnp.log1p(jnp.exp(-jnp.abs(x)))


def _silu(x):
    return x * jax.nn.sigmoid(x)


def _deltanet_body(x_ref, sm_ref, dg_ref, hist_ref, s0_ref, cw_ref, apar_ref, dt_ref, ng_ref,
                   o_ref, s_out_ref, hist_out_ref, xbuf_ref, s_ref, *, c, n_seq):
    j = pl.program_id(1)
    rows = x_ref.shape[0]
    seq_rows = rows // n_seq
    n_grp = seq_rows // c
    hd = DN_HEAD_DIM
    n_hist = CONV_W - 1

    n_pair = DN_HEADS // 2
    zero_blk = jnp.zeros((hd, hd), F32)

    @pl.when(j == 0)
    def _():
        xbuf_ref[:, HIST_ROWS - n_hist:HIST_ROWS, :] = hist_ref[...]
        for sq in range(n_seq):
            for p in range(n_pair):
                s_ref[sq * n_pair + p] = jnp.concatenate(
                    [jnp.concatenate([s0_ref[sq, 2 * p], zero_blk], axis=1),
                     jnp.concatenate([zero_blk, s0_ref[sq, 2 * p + 1]], axis=1)], axis=0)

    xbuf_ref[:, HIST_ROWS:HIST_ROWS + seq_rows, :] = x_ref[...].reshape(n_seq, seq_rows, x_ref.shape[1])
    y = None
    for t in range(CONV_W):
        lo = HIST_ROWS - n_hist + t
        term = xbuf_ref[:, lo:lo + seq_rows, :] * cw_ref[t:t + 1, :]
        y = term if y is None else y + term
    y = _silu(y).reshape(rows, x_ref.shape[1])
    tail = xbuf_ref[:, HIST_ROWS + seq_rows - n_hist:HIST_ROWS + seq_rows, :]
    xbuf_ref[:, HIST_ROWS - n_hist:HIST_ROWS, :] = tail

    sm = sm_ref[...]
    beta_all = jax.nn.sigmoid(sm)
    g_all = apar_ref[...] * _softplus(sm + dt_ref[...])
    row = lax.broadcasted_iota(jnp.int32, (rows, rows), 0)
    col = lax.broadcasted_iota(jnp.int32, (rows, rows), 1)
    tri = jnp.where((row >= col) & (row // c == col // c), 1.0, 0.0).astype(BF16)
    gcum_all = _cumsum_rows(tri, g_all)
    gcum_t = gcum_all.T

    wd, wj = 2 * hd, 2 * c
    first_d = lax.broadcasted_iota(jnp.int32, (1, wd), 1) < hd
    first_j = lax.broadcasted_iota(jnp.int32, (1, wj), 1) < c
    rj = lax.broadcasted_iota(jnp.int32, (c, wj), 0)
    cj = lax.broadcasted_iota(jnp.int32, (c, wj), 1)
    cj = jnp.where(cj < c, cj, cj - c)
    incl, strict = rj >= cj, rj > cj
    eye = jnp.where(rj == cj, 1.0, 0.0)
    same_head = (lax.broadcasted_iota(jnp.int32, (wd, wd), 0) < hd) == (lax.broadcasted_iota(jnp.int32, (wd, wd), 1) < hd)
    ones_bd = jnp.where(same_head, 1.0, 0.0).astype(BF16)
    first_rows = lax.broadcasted_iota(jnp.int32, (wd, 1), 0) < hd

    items = [(g, p) for g in range(n_seq * n_grp) for p in range(n_pair)]

    def block_diag(x, first):
        zero = jnp.zeros_like(x)
        return jnp.concatenate([jnp.where(first, x, zero), jnp.where(first, zero, x)], axis=1)

    def head_sums(x):
        n = x.shape[0]
        hi, lo = _split_bf16(x.reshape(n * c, wd))
        tot = jnp.dot(hi, ones_bd, preferred_element_type=F32) + jnp.dot(lo, ones_bd, preferred_element_type=F32)
        return tot.reshape(n, c, wd)

    def mm_hi(a_parts, b_parts):
        return _bmm(a_parts[0], b_parts[0]) + (_bmm(a_parts[0], b_parts[1]) + _bmm(a_parts[1], b_parts[0]))

    def per_head(cols, base, first):
        return jnp.stack([jnp.where(first, cols[g * c:(g + 1) * c, base + 2 * p:base + 2 * p + 1],
                                    cols[g * c:(g + 1) * c, base + 2 * p + 1:base + 2 * p + 2])
                          for g, p in items], axis=0)

    lanes = lambda x, off: jnp.stack([x[g * c:(g + 1) * c, off + p * wd:off + (p + 1) * wd] for g, p in items], axis=0)
    q = lanes(y, 0)
    k = lanes(y, DN_DIM)
    v = lanes(y, 2 * DN_DIM)
    q = q * lax.rsqrt(head_sums(q * q) + EPS) * hd ** -0.5
    k = k * lax.rsqrt(head_sums(k * k) + EPS)
    beta = per_head(beta_all, SM_BETA, first_d)
    gc_d = per_head(gcum_all, SM_DECAY, first_d)
    gc_j = per_head(gcum_all, SM_DECAY, first_j)
    gr_j = jnp.stack([jnp.concatenate([gcum_t[SM_DECAY + 2 * p:SM_DECAY + 2 * p + 1, g * c:(g + 1) * c],
                                       gcum_t[SM_DECAY + 2 * p + 1:SM_DECAY + 2 * p + 2, g * c:(g + 1) * c]], axis=1)
                      for g, p in items], axis=0)
    gl_d = gc_d[:, c - 1:c, :]
    decay = jnp.where(incl, jnp.exp(jnp.where(incl, gc_j - gr_j, 0.0)), 0.0)
    eg = jnp.exp(gc_d)
    kb = k * beta
    k_bd = block_diag(k.astype(BF16), first_d)
    a = jnp.where(strict, _bmm_nt(kb.astype(BF16), k_bd) * decay, 0.0)
    split_bd = lambda parts, first: tuple(block_diag(x, first) for x in parts)
    base = min(c, DN_INV_BASE)
    same_block = lambda b: (rj // b) == (cj // b)
    diag = jnp.where(same_block(base), a, 0.0)
    t_inv = eye - diag
    pow_parts = _split_bf16(diag)
    pow_bd = split_bd(pow_parts, first_j)
    for _ in range(int(math.log2(base)) - 1):
        pow_parts = _split_bf16(mm_hi(pow_parts, pow_bd))
        pow_bd = split_bd(pow_parts, first_j)
        t_inv = t_inv + mm_hi(_split_bf16(t_inv), pow_bd)
    blk = base
    while blk < c:
        lower = jnp.where(same_block(2 * blk) & jnp.logical_not(same_block(blk)), a, 0.0)
        t_parts = _split_bf16(t_inv)
        lt = mm_hi(_split_bf16(lower), split_bd(t_parts, first_j))
        t_inv = t_inv - mm_hi(t_parts, split_bd(_split_bf16(lt), first_j))
        blk *= 2
    vb_parts = split_bd(_split_bf16(v * beta), first_d)
    kbg_parts = split_bd(_split_bf16(kb * eg), first_d)
    rhs = tuple(jnp.concatenate([x, z], axis=2) for x, z in zip(vb_parts, kbg_parts))
    uw = mm_hi(_split_bf16(t_inv), rhs)
    u, w = uw[:, :, :wd], uw[:, :, wd:]
    qk = jnp.where(incl, _bmm_nt(q.astype(BF16), k_bd) * decay, 0.0).astype(BF16)
    wq = jnp.concatenate([w, q * eg], axis=1).astype(BF16)
    kg = (k * jnp.exp(gl_d - gc_d)).astype(BF16)
    gl_rows = jnp.where(first_rows, jnp.exp(gl_d[:, :, 0:1]), jnp.exp(gl_d[:, :, hd:hd + 1]))

    s = s_ref[...]
    outs = {}
    for kc in range(n_grp):
        pick = lambda x: jnp.concatenate([x[(sq * n_grp + kc) * n_pair:(sq * n_grp + kc + 1) * n_pair]
                                          for sq in range(n_seq)], axis=0)
        ws = _bmm(pick(wq), s.astype(BF16))
        vn = (pick(u) - ws[:, :c]).astype(BF16)
        o_k = ws[:, c:] + _bmm(pick(qk), block_diag(vn, first_d))
        kv_outer = jnp.einsum('pck,pcv->pkv', pick(kg), vn, preferred_element_type=F32)
        s = s * pick(gl_rows) + jnp.where(same_head, kv_outer, 0.0)
        for sq in range(n_seq):
            outs[sq * n_grp + kc] = o_k[sq * n_pair:(sq + 1) * n_pair]
    s_ref[...] = s
    o = jnp.concatenate([outs[g] for g in range(n_seq * n_grp)], axis=0)
    ng = jnp.concatenate([ng_ref[...], ng_ref[...]], axis=1)
    o = o * lax.rsqrt(head_sums(o * o) * (1.0 / hd) + EPS) * ng
    o = jnp.concatenate([jnp.concatenate([o[g * n_pair + p] for p in range(n_pair)], axis=1)
                         for g in range(n_seq * n_grp)], axis=0)
    o_ref[...] = (o * _silu(dg_ref[...])).astype(o_ref.dtype)

    @pl.when(j == pl.num_programs(1) - 1)
    def _():
        for sq in range(n_seq):
            for p in range(n_pair):
                s_pair = s_ref[sq * n_pair + p]
                s_out_ref[sq, 2 * p] = s_pair[0:hd, 0:hd]
                s_out_ref[sq, 2 * p + 1] = s_pair[hd:2 * hd, hd:2 * hd]
        hist_out_ref[...] = tail


def deltanet(qkv_raw, small, dg, conv_hist, s0, lp, B, T):
    c = DN_CHUNK if T >= DN_CHUNK else T
    assert T % c == 0 and c % 8 == 0 and c >= CONV_W - 1 and (c & (c - 1)) == 0
    n_grp = math.gcd(T // c, DN_CHUNKS_PER_STEP)
    n_seq = math.gcd(B, DN_SEQS_PER_STEP) if T == c else 1
    seq_rows = n_grp * c
    step = n_seq * seq_rows
    n = T // seq_rows
    put = lambda vals, base: jnp.zeros((LANES,), F32).at[base + jnp.arange(DN_HEADS)].set(vals).reshape(1, LANES)
    apar = put(-jnp.exp(lp['dn_a_log'].astype(F32)), SM_DECAY)
    dtb = put(lp['dn_dt_bias'].astype(F32), SM_DECAY)
    assert n_seq == 1 or n == 1
    rows = lambda w: pl.BlockSpec((step, w), lambda b, j: (b * n + j, 0))
    per_b = lambda a: pl.BlockSpec((n_seq,) + a.shape[1:], lambda b, j: (b,) + (0,) * (a.ndim - 1))
    full = lambda a: pl.BlockSpec(a.shape, lambda b, j: (0,) * a.ndim)
    ng = lp['dn_norm_g'].reshape(1, DN_HEAD_DIM).astype(F32)
    return pl.pallas_call(
        functools.partial(_deltanet_body, c=c, n_seq=n_seq),
        grid=(B // n_seq, n),
        in_specs=[rows(DN_CONV_DIM), rows(LANES), rows(DN_DIM), per_b(conv_hist), per_b(s0),
                  full(lp['conv_w']), full(apar), full(dtb), full(ng)],
        out_specs=[rows(DN_DIM), per_b(s0), per_b(conv_hist)],
        out_shape=[jax.ShapeDtypeStruct((B * T, DN_DIM), BF16 if step % 16 == 0 else F32),
                   jax.ShapeDtypeStruct(s0.shape, F32),
                   jax.ShapeDtypeStruct(conv_hist.shape, F32)],
        scratch_shapes=[pltpu.VMEM((n_seq, HIST_ROWS + seq_rows, DN_CONV_DIM), F32),
                        pltpu.VMEM((n_seq * (DN_HEADS // 2), 2 * DN_HEAD_DIM, 2 * DN_HEAD_DIM), F32)],
        compiler_params=pltpu.CompilerParams(dimension_semantics=("parallel", "arbitrary"),
                                             vmem_limit_bytes=VMEM_LIMIT),
        name="deltanet",
    )(qkv_raw, small, dg, conv_hist, s0, lp['conv_w'], apar, dtb, ng)


MOE_EXPERTS_PER_STEP = 4
RT_GROUP = 0
RT_EXPERT = N_GROUPS


def pack_router(wg, bg, we, be):
    w = jnp.concatenate([wg, we], axis=1)
    b = jnp.concatenate([bg, be], axis=0)
    pad = LANES - w.shape[1]
    return jnp.pad(w, ((0, 0), (0, pad))).astype(BF16), jnp.pad(b, (0, pad)).reshape(1, LANES).astype(F32)


def _route(r):
    lane = lax.broadcasted_iota(jnp.int32, r.shape, 1).astype(F32)
    big = float(LANES)
    is_grp = lane < N_GROUPS
    lg = jnp.where(is_grp, r, -jnp.inf)
    mg = jnp.max(lg, axis=-1, keepdims=True)
    eg = jnp.where(is_grp, jnp.exp(lg - mg), 0.0)
    grp = jnp.min(jnp.where(lg == mg, lane, big), axis=-1, keepdims=True)
    wg = 1.0 / jnp.sum(eg, axis=-1, keepdims=True)
    first = RT_EXPERT + grp * EXPERTS_PER_GROUP
    in_grp = (lane >= first) & (lane < first + EXPERTS_PER_GROUP)
    le = jnp.where(in_grp, r, -jnp.inf)
    me = jnp.max(le, axis=-1, keepdims=True)
    ee = jnp.where(in_grp, jnp.exp(le - me), 0.0)
    pe = jnp.where(in_grp, ee / jnp.sum(ee, axis=-1, keepdims=True), -1.0)
    v1 = jnp.max(pe, axis=-1, keepdims=True)
    i1 = jnp.min(jnp.where(pe == v1, lane, big), axis=-1, keepdims=True)
    pe2 = jnp.where(lane == i1, -1.0, pe)
    v2 = jnp.max(pe2, axis=-1, keepdims=True)
    i2 = jnp.min(jnp.where(pe2 == v2, lane, big), axis=-1, keepdims=True)
    tot = v1 + v2
    return jnp.where(lane == i1, wg * (v1 / tot), jnp.where(lane == i2, wg * (v2 / tot), 0.0))


def _finish_body(x_ref, on_ref, od_ref, g1_ref, sc_ref, sh_ref, g2_ref, won_ref, wod_ref, ng_ref, wr_ref, br_ref,
                 wgu_ref, wd_ref, fg_ref, o_ref, x1_ref, h_ref, gate_ref, y_ref):
    e = pl.program_id(1)
    tm, d = x_ref.shape
    nb = g1_ref.shape[0]
    per_batch = lambda a: a.reshape(nb, tm // nb, d)

    @pl.when(e == 0)
    def _():
        mix = jnp.dot(on_ref[...].astype(BF16), won_ref[...], preferred_element_type=F32) \
            + jnp.dot(od_ref[...].astype(BF16), wod_ref[...], preferred_element_type=F32)
        x1 = per_batch(x_ref[...]) + g1_ref[...] * per_batch(mix)
        x1 = x1.reshape(tm, d)
        x1_ref[...] = x1
        xn = x1 * lax.rsqrt(jnp.mean(x1 * x1, axis=-1, keepdims=True) + EPS) * ng_ref[...]
        h = (per_batch(xn) * (1 + sc_ref[...]) + sh_ref[...]).reshape(tm, d).astype(BF16)
        h_ref[...] = h
        gate_ref[...] = _route(jnp.dot(h, wr_ref[...], preferred_element_type=F32) + br_ref[...])
        y_ref[...] = jnp.zeros(y_ref.shape, F32)

    h = h_ref[...]
    lane = lax.broadcasted_iota(jnp.int32, gate_ref.shape, 1)
    y = y_ref[...]
    for k in range(wgu_ref.shape[0]):
        gu = jnp.dot(h, wgu_ref[k], preferred_element_type=F32)
        he = _silu(gu[:, :D_EXPERT]) * gu[:, D_EXPERT:]
        out = jnp.dot(he.astype(BF16), wd_ref[k], preferred_element_type=F32)
        expert_lane = RT_EXPERT + e * wgu_ref.shape[0] + k
        g_col = jnp.sum(jnp.where(lane == expert_lane, gate_ref[...], 0.0), axis=-1, keepdims=True)
        y = y + g_col * out
    y_ref[...] = y

    @pl.when(e == pl.num_programs(1) - 1)
    def _():
        x2 = per_batch(x1_ref[...]) + g2_ref[...] * per_batch(y_ref[...])
        x2 = x2.reshape(tm, d)
        o_ref[...] = x2 * lax.rsqrt(jnp.mean(x2 * x2, axis=-1, keepdims=True) + EPS) * fg_ref[...]


def finish_layer(x, o_nsa, o_dn, g1, sc2, sh2, g2, packed, norm_ffn_g, final_g, *, tm=1024):
    B, T, D = x.shape
    n = B * T
    tm = min(tm, n)
    rows = max(tm // T, 1)
    per = max(T // tm, 1)
    row = lambda w: pl.BlockSpec((tm, w), lambda i, e: (i, 0))
    mod = pl.BlockSpec((rows, 1, D), lambda i, e: (i // per, 0, 0))
    full = lambda a: pl.BlockSpec(a.shape, lambda i, e: (0,) * a.ndim)
    expert = lambda a: pl.BlockSpec((MOE_EXPERTS_PER_STEP,) + a.shape[1:], lambda i, e: (e, 0, 0))
    vec = lambda v: v.reshape(1, D).astype(F32)
    args = (x.reshape(n, D), o_nsa, o_dn, g1, sc2, sh2, g2, packed['w_out_nsa'], packed['w_out_dn'],
            vec(norm_ffn_g), packed['router_w'], packed['router_b'], packed['w_gate_up'], packed['w_down'],
            vec(final_g))
    specs = [row(D), row(o_nsa.shape[1]), row(o_dn.shape[1]), mod, mod, mod, mod] + \
            [full(a) for a in args[7:12]] + [expert(args[12]), expert(args[13]), full(args[14])]
    return pl.pallas_call(
        _finish_body,
        grid=(n // tm, N_EXPERTS // MOE_EXPERTS_PER_STEP),
        in_specs=specs,
        out_specs=row(D),
        out_shape=jax.ShapeDtypeStruct((n, D), F32),
        scratch_shapes=[pltpu.VMEM((tm, D), F32), pltpu.VMEM((tm, D), BF16), pltpu.VMEM((tm, LANES), F32),
                        pltpu.VMEM((tm, D), F32)],
        compiler_params=pltpu.CompilerParams(dimension_semantics=("parallel", "arbitrary"),
                                             vmem_limit_bytes=VMEM_LIMIT),
        name="finish_layer",
    )(*args).reshape(B, T, D)


def permute_w_out_nsa(w_out):
    w = w_out[:NSA_Q_DIM].reshape(NSA_KV_HEADS, NSA_GROUP, HEAD_DIM, -1)
    return w.transpose(1, 0, 2, 3).reshape(NSA_Q_DIM, -1).astype(BF16)


def layer_prompt(x, mod, lp, rel_bias, packed):
    B, T, _ = x.shape
    sh1, sc1, g1, sh2, sc2, g2 = mod
    (q, ck, cv, sk, sv, wk, wv, skb, svb, wkb, wvb, qkv_raw, dg, small, ck_rows, cv_rows) = in_projection(
        x, lp['norm_mix_g'], sc1, sh1, packed['w_in'], feature_major=True)
    kc, vc = compress_tokens(ck_rows, cv_rows, packed['cmp_k'], packed['cmp_v'], B)
    o_nsa = nsa_prompt(q, kc, vc, skb, svb, wkb, wvb, small, rel_bias, B, T)
    kv = lambda r: r.reshape(B, NSA_KV_HEADS, HEAD_DIM, -1).transpose(0, 3, 1, 2)
    w_keep = min(WINDOW, T)
    ck, cv, sk, sv, wk, wv = kv(ck), kv(cv), kv(sk), kv(sv), kv(wk[:, :, T - w_keep:]), kv(wv[:, :, T - w_keep:])
    conv_hist = jnp.zeros((B, CONV_W - 1, DN_CONV_DIM), x.dtype)
    s0 = jnp.zeros((B, DN_HEADS, DN_HEAD_DIM, DN_HEAD_DIM), x.dtype)
    o_dn, s_new, conv_new = deltanet(qkv_raw, small, dg, conv_hist, s0, lp, B, T)
    y = finish_layer(x, o_nsa, o_dn, g1, sc2, sh2, g2, packed, lp['norm_ffn_g'], packed['final_norm_g'])
    return y, (ck, cv, sk, sv, wk, wv, s_new, conv_new)


def layer_sample(x, mod, cmp_k_pool, cmp_v_pool, sel_k_pool, sel_v_pool, win_k, win_v, s0, conv_hist,
                 page_table, lp, rel_bias, packed):
    B, T, _ = x.shape
    sh1, sc1, g1, sh2, sc2, g2 = mod
    (q, ck, cv, sk, sv, wk, wv, _, _, _, _, qkv_raw, dg, small) = in_projection(
        x, lp['norm_mix_g'], sc1, sh1, packed['w_in'])
    o_nsa, wk_next, wv_next = nsa_sample(q, sk, sv, wk, wv, small, (cmp_k_pool, cmp_v_pool, sel_k_pool, sel_v_pool),
                                         win_k, win_v, page_table, rel_bias, packed['cmp_k'], packed['cmp_v'], B, T)
    kv = lambda r: r.reshape(B, -1, NSA_KV_HEADS, HEAD_DIM)
    o_dn, s_new, conv_new = deltanet(qkv_raw, small, dg, conv_hist, s0, lp, B, T)
    y = finish_layer(x, o_nsa, o_dn, g1, sc2, sh2, g2, packed, lp['norm_ffn_g'], packed['final_norm_g'])
    return y, (kv(ck), kv(cv), kv(sk), kv(sv), kv(wk_next), kv(wv_next), s_new, conv_new)


def kernel(x_prompt, x_sample, cache_cmp_k, cache_cmp_v, cache_sel_k, cache_sel_v, cache_win_k, cache_win_v,
           state_delta, state_conv, page_table, c_prompt, c_sample, rel_bias, w_ada, b_ada, norm_mix_g,
           norm_ffn_g, w_in, cmp_pos_k, cmp_w1_k, cmp_w2_k, cmp_pos_v, cmp_w1_v, cmp_w2_v, conv_w, dn_a_log,
           dn_dt_bias, dn_norm_g, w_out, router_group_w, router_group_b, router_expert_w, router_expert_b,
           expert_w_gate, expert_w_up, expert_w_down, final_norm_g):
    xp, xs = x_prompt, x_sample
    new_p, new_s = [], []
    for l in range(DEPTH):
        lp = dict(norm_mix_g=norm_mix_g[l], norm_ffn_g=norm_ffn_g[l], conv_w=conv_w[l],
                  dn_a_log=dn_a_log[l], dn_dt_bias=dn_dt_bias[l], dn_norm_g=dn_norm_g[l])
        packed = dict(w_in=pack_w_in(w_in[l]),
                      cmp_k=pack_compress(cmp_pos_k[l], cmp_w1_k[l], cmp_w2_k[l]),
                      cmp_v=pack_compress(cmp_pos_v[l], cmp_w1_v[l], cmp_w2_v[l]),
                      w_out_nsa=permute_w_out_nsa(w_out[l]),
                      w_out_dn=w_out[l][NSA_Q_DIM:].astype(BF16),
                      w_gate_up=jnp.concatenate([expert_w_gate[l], expert_w_up[l]], axis=-1).astype(BF16),
                      w_down=expert_w_down[l].astype(BF16),
                      final_norm_g=final_norm_g)
        packed['router_w'], packed['router_b'] = pack_router(router_group_w[l], router_group_b[l],
                                                             router_expert_w[l], router_expert_b[l])
        n_p = c_prompt.shape[0]
        mod = modulation(jnp.concatenate([c_prompt, c_sample], axis=0), w_ada[l], b_ada[l])
        xp, st_p = layer_prompt(xp, [m[:n_p] for m in mod], lp, rel_bias, packed)
        xs, st_s = layer_sample(xs, [m[n_p:] for m in mod], cache_cmp_k[l], cache_cmp_v[l], cache_sel_k[l],
                                cache_sel_v[l],
                                cache_win_k[l], cache_win_v[l], state_delta[l], state_conv[l], page_table,
                                lp, rel_bias, packed)
        new_p.append(st_p)
        new_s.append(st_s)
    assert DEPTH == 1
    y_prompt, y_sample = xp, xs

    def stk(states, i):
        return jnp.stack([s[i] for s in states])

    return (y_prompt, y_sample,
            stk(new_p, 0), stk(new_p, 1), stk(new_p, 2), stk(new_p, 3),
            stk(new_p, 4), stk(new_p, 5), stk(new_p, 6), stk(new_p, 7),
            stk(new_s, 0), stk(new_s, 1), stk(new_s, 2), stk(new_s, 3),
            stk(new_s, 4), stk(new_s, 5), stk(new_s, 6), stk(new_s, 7))
```

```python
import functools
import math

import jax
import jax.numpy as jnp
import numpy as np
from jax import lax
from jax.experimental import pallas as pl
from jax.experimental.pallas import tpu as pltpu

D_MODEL = 1024
DEPTH = 1
PAGE_SIZE = 128

NSA_HEADS = 8
NSA_KV_HEADS = 2
NSA_GROUP = NSA_HEADS // NSA_KV_HEADS
HEAD_DIM = 64
CMP_LEN = 32
CMP_STRIDE = 16
CMP_HIDDEN = 4 * HEAD_DIM
SEL_BLOCK = 64
SEL_TOP = 16
WINDOW = 512
DN_HEADS = 8
DN_HEAD_DIM = 64
CONV_W = 4
DN_CHUNK = 64
NUM_BUCKETS = 32
MAX_DISTANCE = 128
N_GROUPS = 4
EXPERTS_PER_GROUP = 8
N_EXPERTS = N_GROUPS * EXPERTS_PER_GROUP
TOP_K_IN_GROUP = 2
D_EXPERT = D_MODEL // 4

EPS = 1e-6
NEG_INF = -1e30
F32 = jnp.float32
BF16 = jnp.bfloat16

NSA_Q_DIM = NSA_HEADS * HEAD_DIM
NSA_KV_DIM = NSA_KV_HEADS * HEAD_DIM
DN_DIM = DN_HEADS * DN_HEAD_DIM
DN_CONV_DIM = 3 * DN_DIM
MIX_WIDTH = NSA_Q_DIM + DN_DIM
IN_SPLITS = (NSA_Q_DIM, NSA_KV_DIM, NSA_KV_DIM, NSA_KV_DIM, NSA_KV_DIM, NSA_KV_DIM, NSA_KV_DIM,
             3 * NSA_HEADS, DN_CONV_DIM, DN_HEADS, DN_HEADS, DN_DIM)
IN_DIM = sum(IN_SPLITS)

LANES = 128
VMEM_LIMIT = 56 * 1024 * 1024

ATT_TILE = 128
MASKED = -1e30


def t5_bucket(rel):
    n = jnp.maximum(rel, 0)
    max_exact = NUM_BUCKETS // 2
    nf = jnp.maximum(n, 1).astype(F32)
    large = max_exact + (jnp.log(nf / max_exact) / math.log(MAX_DISTANCE / max_exact)
                         * (NUM_BUCKETS - max_exact)).astype(jnp.int32)
    large = jnp.minimum(large, NUM_BUCKETS - 1)
    return jnp.where(n < max_exact, n, large)


def _modulation_body(c_ref, w_ref, b_ref, o_ref):
    c = c_ref[...]
    x = (c * jax.nn.sigmoid(c)).astype(BF16)
    o_ref[...] = jnp.dot(x, w_ref[...].astype(BF16), preferred_element_type=F32) + b_ref[...]


def modulation(c, w_ada, b_ada, *, tn=1024):
    rows, d = c.shape
    n = w_ada.shape[1]
    mod = pl.pallas_call(
        _modulation_body,
        grid=(n // tn,),
        in_specs=[pl.BlockSpec((rows, d), lambda j: (0, 0)),
                  pl.BlockSpec((d, tn), lambda j: (0, j)),
                  pl.BlockSpec((1, tn), lambda j: (0, j))],
        out_specs=pl.BlockSpec((rows, tn), lambda j: (0, j)),
        out_shape=jax.ShapeDtypeStruct((rows, n), F32),
        compiler_params=pltpu.CompilerParams(dimension_semantics=("parallel",), vmem_limit_bytes=VMEM_LIMIT),
        name="modulation",
    )(c, w_ada, b_ada.reshape(1, n))
    return jnp.split(mod[:, None, :], 6, axis=-1)


_PK_Q = (0, NSA_Q_DIM)
_PK_KV = (_PK_Q[1], _PK_Q[1] + 6 * NSA_KV_DIM)
_PK_QKV = (_PK_KV[1], _PK_KV[1] + DN_CONV_DIM)
_PK_DG = (_PK_QKV[1], _PK_QKV[1] + DN_DIM)
_PK_SM = (_PK_DG[1], _PK_DG[1] + LANES)
PK_DIM = _PK_SM[1]
SM_GATE = 0
SM_BETA = 3 * NSA_HEADS
SM_DECAY = SM_BETA + DN_HEADS


def pack_w_in(w_in):
    offs = np.cumsum((0,) + IN_SPLITS)
    part = lambda i: w_in[:, offs[i]:offs[i + 1]]
    d = w_in.shape[0]
    q = part(0).reshape(d, NSA_KV_HEADS, NSA_GROUP, HEAD_DIM).transpose(0, 2, 1, 3).reshape(d, NSA_Q_DIM)
    q = q * HEAD_DIM ** -0.5
    small = jnp.concatenate([part(7), part(9), part(10)], axis=1)
    small = jnp.pad(small, ((0, 0), (0, LANES - small.shape[1])))
    cols = [q] + [part(i) for i in range(1, 7)] + [part(8), part(11), small]
    return jnp.concatenate(cols, axis=1).astype(BF16)


def _in_proj_body(x_ref, g_ref, sc_ref, sh_ref, w_ref, q_ref, ck_ref, cv_ref, sk_ref, sv_ref, wk_ref, wv_ref,
                  skb_ref, svb_ref, wkb_ref, wvb_ref, qkv_ref, dg_ref, sm_ref, *row_refs):
    x = x_ref[...]
    tm, d = x.shape
    nb = sc_ref.shape[0]
    xn = x * lax.rsqrt(jnp.mean(x * x, axis=-1, keepdims=True) + EPS) * g_ref[...]
    h = xn.reshape(nb, tm // nb, d) * (1 + sc_ref[...]) + sh_ref[...]
    h = h.reshape(tm, d).astype(BF16)

    def mm(span):
        return jnp.dot(h, w_ref[:, span[0]:span[1]], preferred_element_type=F32)

    q_ref[...] = mm(_PK_Q).astype(BF16)
    kv = mm(_PK_KV)
    for i, r in enumerate((ck_ref, cv_ref, sk_ref, sv_ref, wk_ref, wv_ref)):
        rows = kv[:, i * NSA_KV_DIM:(i + 1) * NSA_KV_DIM]
        if row_refs:
            r[0] = rows.T
        else:
            r[...] = rows
    for i, r in enumerate(row_refs):
        r[...] = kv[:, i * NSA_KV_DIM:(i + 1) * NSA_KV_DIM]
    for i, r in enumerate((skb_ref, svb_ref, wkb_ref, wvb_ref)):
        r[...] = kv[:, (i + 2) * NSA_KV_DIM:(i + 3) * NSA_KV_DIM].astype(BF16)
    qkv_ref[...] = mm(_PK_QKV)
    dg_ref[...] = mm(_PK_DG)
    sm_ref[...] = mm(_PK_SM)


def in_projection(x, norm_g, scale, shift, w_packed, *, tm=256, feature_major=False):
    B, T, D = x.shape
    n = B * T
    tm = min(tm, n)
    rows = max(tm // T, 1)
    per = max(T // tm, 1)
    row = lambda i: (i, 0)
    by_row = lambda w, dt: (((n, w), dt), pl.BlockSpec((tm, w), row))
    if feature_major:
        assert tm <= T and tm % LANES == 0
        kv_out = (((B, NSA_KV_DIM, T), F32), pl.BlockSpec((1, NSA_KV_DIM, tm), lambda i: (i // per, 0, i % per)))
    else:
        kv_out = by_row(NSA_KV_DIM, F32)
    outs = [by_row(NSA_Q_DIM, BF16)] + [kv_out] * 6 + [by_row(NSA_KV_DIM, BF16)] * 4 + \
           [by_row(DN_CONV_DIM, F32), by_row(DN_DIM, F32), by_row(LANES, F32)] + \
           ([by_row(NSA_KV_DIM, F32)] * 2 if feature_major else [])
    return pl.pallas_call(
        _in_proj_body,
        grid=(n // tm,),
        in_specs=[pl.BlockSpec((tm, D), row),
                  pl.BlockSpec((1, D), lambda i: (0, 0)),
                  pl.BlockSpec((rows, 1, D), lambda i: (i // per, 0, 0)),
                  pl.BlockSpec((rows, 1, D), lambda i: (i // per, 0, 0)),
                  pl.BlockSpec((D, PK_DIM), lambda i: (0, 0))],
        out_specs=[spec for _, spec in outs],
        out_shape=[jax.ShapeDtypeStruct(s, dt) for (s, dt), _ in outs],
        compiler_params=pltpu.CompilerParams(dimension_semantics=("parallel",), vmem_limit_bytes=VMEM_LIMIT),
        name="in_projection",
    )(x.reshape(n, D), norm_g.reshape(1, D), scale, shift, w_packed)


CHUNK_W = CMP_STRIDE * NSA_KV_DIM
CMP_HALVES = CMP_LEN // CMP_STRIDE


def pack_compress(pos, w1, w2):
    eye = jnp.eye(NSA_KV_HEADS, dtype=F32)
    w1r = w1.reshape(CMP_HALVES, CMP_STRIDE, HEAD_DIM, CMP_HIDDEN)
    w1p = jnp.einsum('ijdc,kl->ijkdlc', w1r, eye).reshape(CMP_HALVES, CHUNK_W, NSA_KV_HEADS * CMP_HIDDEN)
    w2p = jnp.einsum('cd,kl->kcld', w2, eye).reshape(NSA_KV_HEADS * CMP_HIDDEN, NSA_KV_DIM)
    posp = jnp.broadcast_to(pos.reshape(CMP_HALVES, CMP_STRIDE, 1, HEAD_DIM),
                            (CMP_HALVES, CMP_STRIDE, NSA_KV_HEADS, HEAD_DIM)).reshape(CMP_HALVES, CHUNK_W)
    return posp, w1p.astype(BF16), w2p.astype(BF16)


def _compress_one(rows_ref, pos_ref, w1_ref, w2_ref):
    n_chunk = rows_ref.shape[0] // CMP_STRIDE
    x = jnp.concatenate([rows_ref[pl.ds(j, n_chunk, stride=CMP_STRIDE), :] for j in range(CMP_STRIDE)], axis=1)
    hid = None
    for i in range(CMP_HALVES):
        part = jnp.dot((x + pos_ref[i:i + 1]).astype(BF16), w1_ref[i], preferred_element_type=F32)
        if i:
            part = pltpu.roll(part, shift=n_chunk - i, axis=0)
        hid = part if hid is None else hid + part
    hid = hid * jax.nn.sigmoid(hid)
    return jnp.dot(hid.astype(BF16), w2_ref[...], preferred_element_type=F32)


def _compress_body(xk_ref, xv_ref, pk_ref, pv_ref, w1k_ref, w1v_ref, w2k_ref, w2v_ref, kc_ref, vc_ref):
    kc_ref[0] = _compress_one(xk_ref, pk_ref, w1k_ref, w2k_ref).astype(BF16)
    vc_ref[0] = _compress_one(xv_ref, pv_ref, w1v_ref, w2v_ref).astype(BF16)


def compress_tokens(ck, cv, pk, pv, B):
    L = ck.shape[0] // B
    n_chunk = L // CMP_STRIDE
    full = lambda a: pl.BlockSpec(a.shape, lambda b: (0,) * a.ndim)
    rows = pl.BlockSpec((L, NSA_KV_DIM), lambda b: (b, 0))
    tok = pl.BlockSpec((1, n_chunk, NSA_KV_DIM), lambda b: (b, 0, 0))
    return pl.pallas_call(
        _compress_body,
        grid=(B,),
        in_specs=[rows, rows, full(pk[0]), full(pv[0]), full(pk[1]), full(pv[1]), full(pk[2]), full(pv[2])],
        out_specs=[tok, tok],
        out_shape=[jax.ShapeDtypeStruct((B, n_chunk, NSA_KV_DIM), BF16)] * 2,
        compiler_params=pltpu.CompilerParams(dimension_semantics=("parallel",), vmem_limit_bytes=VMEM_LIMIT),
        name="compress_tokens",
    )(ck, cv, pk[0], pv[0], pk[1], pv[1], pk[2], pv[2])


def _t5_bucket_np(rel):
    n = np.maximum(rel, 0)
    max_exact = NUM_BUCKETS // 2
    nf = np.maximum(n, 1).astype(np.float32)
    large = max_exact + (np.log(nf / max_exact) / math.log(MAX_DISTANCE / max_exact)
                         * (NUM_BUCKETS - max_exact)).astype(np.int32)
    return np.where(n < max_exact, n, np.minimum(large, NUM_BUCKETS - 1))


assert (_t5_bucket_np(np.arange(ATT_TILE + 1, 1 << 16)) == NUM_BUCKETS - 1).all()


NB_SAME, NB_PREV, NB_FAR, NB_EDGE, NB_NONE = range(5)
FAR_TILE = 512


def bias_lookup(rel_bias, rel):
    bucket = t5_bucket(rel)
    tbl = rel_bias.astype(F32)
    shape = (tbl.shape[1],) + (1,) * rel.ndim
    out = jnp.zeros((tbl.shape[1],) + rel.shape, F32)
    for b in range(NUM_BUCKETS):
        out = jnp.where(bucket == b, tbl[b].reshape(shape), out)
    return out


def near_bias_tables(rel_bias):
    t = ATT_TILE
    i = jnp.arange(t)[:, None]
    j = jnp.arange(t)[None, :]
    diag = jnp.where(i >= j, bias_lookup(rel_bias, i - j), MASKED)
    prev = bias_lookup(rel_bias, t + i - j)
    far = jnp.broadcast_to(rel_bias.astype(F32)[NUM_BUCKETS - 1][:, None, None], prev.shape)
    edge = jnp.where(j > i, far, MASKED)
    return jnp.stack([diag, prev, far, edge, jnp.full_like(far, MASKED)])


def cmp_bias_tiles(rel_bias, n_q, tile, n_chunk):
    assert tile % CMP_STRIDE == 0 and n_q % tile == 0
    step = tile // CMP_STRIDE
    n_tile = n_q // tile
    lo = -(n_tile - 1) * step
    m = lo + jnp.arange(n_chunk - lo)[None, :]
    rel = jnp.arange(tile)[:, None] - CMP_STRIDE * m - (CMP_LEN - 1)
    template = jnp.where(rel >= 0, bias_lookup(rel_bias, rel), MASKED)
    return jnp.stack([template[:, :, -lo - i * step:-lo - i * step + n_chunk] for i in range(n_tile)], axis=0)


def overlap_matrix(n_chunk, n_sel):
    c0 = np.arange(n_chunk)[:, None] * CMP_STRIDE
    s0 = np.arange(LANES)[None, :] * SEL_BLOCK
    ov = np.clip(np.minimum(c0 + CMP_LEN, s0 + SEL_BLOCK) - np.maximum(c0, s0), 0, None) / CMP_LEN
    ov = np.where(np.arange(LANES)[None, :] < n_sel, ov, 0.0)
    return jnp.asarray(ov, F32)


def expand_matrix(n_keys):
    e = (np.arange(n_keys)[None, :] // SEL_BLOCK) == np.arange(LANES)[:, None]
    return jnp.asarray(e, BF16)


def _head_queries(q_ref, tq):
    low = lax.broadcasted_iota(jnp.int32, (tq, LANES), 1) < HEAD_DIM
    zero = jnp.zeros((tq, LANES), BF16)
    parts = []
    for kv in range(NSA_KV_HEADS):
        for g in range(NSA_GROUP):
            blk = q_ref[:, g * LANES:(g + 1) * LANES]
            parts.append(jnp.where(low if kv == 0 else ~low, blk, zero))
    return parts


def _stack_queries(q_ref, tq):
    return jnp.concatenate(_head_queries(q_ref, tq), axis=0)


def _dot_nt(a, b):
    return lax.dot_general(a, b, (((1,), (1,)), ((), ())), preferred_element_type=F32)


def _selection_mask(p_group, ov_ref, ex_ref, q0, n_sel, n_top):
    tq = p_group.shape[0]
    n_row = -(-n_sel // 8) * 8
    imp = lax.dot_general(ov_ref[...], p_group, (((0,), (1,)), ((), ())), precision=lax.Precision.HIGHEST,
                          preferred_element_type=F32)[0:n_row]
    blk = lax.broadcasted_iota(jnp.int32, (n_row, tq), 0)
    q_pos = q0 + lax.broadcasted_iota(jnp.int32, (n_row, tq), 1)
    q_blk = q_pos // SEL_BLOCK
    forced = jnp.where(blk == 0, 1.0, jnp.where(blk == q_blk, 1.0, jnp.where(blk == q_blk - 1, 1.0, 0.0)))
    score = jnp.where(blk * SEL_BLOCK <= q_pos, imp, NEG_INF)
    score = jnp.where(forced > 0.5, 1e9, score)
    score = jnp.where(blk < n_sel, score, -3e38)
    rank = jnp.zeros((n_row, tq), F32)
    for s in range(n_sel):
        other = score[s:s + 1, :]
        tie = jnp.where(blk > s, 1.0, 0.0)
        rank = rank + jnp.where(other > score, 1.0, jnp.where(other == score, tie, 0.0))
    sel = jnp.where(rank < n_top, 1.0, 0.0).astype(BF16)
    keys = lax.dot_general(sel, ex_ref[0:n_row, :], (((0,), (0,)), ((), ())), preferred_element_type=F32)
    return (keys - 1.0) * (-MASKED)


def _gate_columns(gl_ref, j):
    g = jax.nn.sigmoid(gl_ref[...])
    cols = [g[:, SM_GATE + 3 * h + j:SM_GATE + 3 * h + j + 1] for h in range(NSA_HEADS)]
    return jnp.stack(cols, axis=0)


def _nsa_prompt_body(q_ref, kc_ref, vc_ref, sk_ref, sv_ref, wk_ref, wv_ref, bc_ref, nb_ref, gl_ref, ov_ref, ex_ref,
                     o_ref, mb_ref, s_ref, sw_ref, stat_ref, mrep_ref, acc_ref, out_ref, *, n_sel, n_top):
    i = pl.program_id(1)
    tq = q_ref.shape[0]
    h = NSA_HEADS
    q = _stack_queries(q_ref, tq)

    s = _dot_nt(q, kc_ref[0]).reshape(h, tq, -1) + bc_ref[0]
    m = jnp.max(s, axis=-1, keepdims=True)
    p = jnp.where(s > 0.5 * MASKED, jnp.exp(s - m), 0.0)
    l = jnp.sum(p, axis=-1, keepdims=True)
    p = p * (1.0 / jnp.where(l > 0.0, l, 1.0))
    o_cmp = jnp.dot(p.reshape(h * tq, -1).astype(BF16), vc_ref[0], preferred_element_type=F32)
    out_ref[...] = _gate_columns(gl_ref, 0) * o_cmp.reshape(h, tq, LANES)

    for kv in range(NSA_KV_HEADS):
        pk = p[kv * NSA_GROUP]
        for g in range(1, NSA_GROUP):
            pk = pk + p[kv * NSA_GROUP + g]
        mb_ref[kv] = _selection_mask(pk, ov_ref, ex_ref, i * tq, n_sel, n_top)

    tk = ATT_TILE
    n_back = WINDOW // tk
    wide = (n_back + 1) * tk
    far_blk = FAR_TILE // tk
    n_tile = sk_ref.shape[0] // tk
    tail_col = s_ref.shape[-1] - wide
    kvg = lambda x: x.reshape(NSA_KV_HEADS, NSA_GROUP, tq, x.shape[-1])
    blocks = lambda x: [x[..., c * LANES:(c + 1) * LANES] for c in range(x.shape[-1] // LANES)]
    lane_max = lambda x: functools.reduce(jnp.maximum, blocks(x))
    lane_sum = lambda x: functools.reduce(jnp.add, blocks(x))
    rep = lambda x, n: jnp.concatenate([x] * n, axis=-1)
    far_bias = nb_ref[NB_FAR][:, 0:1, 0:1]

    n_far = jnp.maximum(i - 1, 0) // far_blk
    tail0 = jnp.minimum(n_far * far_blk, n_tile - (n_back + 1))
    tail_rows = pl.ds(pl.multiple_of(tail0 * tk, tk), wide)
    tiles = []
    for c in range(n_back + 1):
        blk = tail0 + c
        kind = jnp.where((blk < n_far * far_blk) | (blk > i), NB_NONE, jnp.minimum(i - blk, NB_FAR))
        tiles.append(nb_ref[kind])
    s = _dot_nt(q, sk_ref[tail_rows, :]).reshape(h, tq, wide) + jnp.concatenate(tiles, axis=-1)
    s = (kvg(s) + mb_ref[:, :, tail_rows][:, None]).reshape(h, tq, wide)
    s_ref[:, :, tail_col:tail_col + wide] = s
    stat_ref[0] = lane_max(s)

    win0 = jnp.maximum(i - n_back, 0)
    win_rows = pl.ds(pl.multiple_of(win0 * tk, tk), wide)
    tiles = []
    for c in range(n_back + 1):
        r = i - (win0 + c)
        tiles.append(nb_ref[jnp.where(r < 0, NB_NONE, jnp.where(r == n_back, NB_EDGE, jnp.minimum(r, NB_FAR)))])
    s = _dot_nt(q, wk_ref[win_rows, :]).reshape(h, tq, wide) + jnp.concatenate(tiles, axis=-1)
    sw_ref[...] = s
    stat_ref[1] = lane_max(s)

    def far_scores(f, carry):
        rows = pl.ds(pl.multiple_of(f * FAR_TILE, FAR_TILE), FAR_TILE)
        s = _dot_nt(q, sk_ref[rows, :]).reshape(h, tq, FAR_TILE) + far_bias
        s = (kvg(s) + mb_ref[:, :, rows][:, None]).reshape(h, tq, FAR_TILE)
        s_ref[:, :, rows] = s
        stat_ref[0] = jnp.maximum(stat_ref[0], lane_max(s))
        return carry
    lax.fori_loop(0, n_far, far_scores, 0)

    for br in range(2):
        mrep_ref[br] = jnp.broadcast_to(jnp.max(stat_ref[br], axis=-1, keepdims=True), (h, tq, LANES))
        stat_ref[br] = jnp.zeros((h, tq, LANES), F32)
    acc_ref[...] = jnp.zeros(acc_ref.shape, F32)

    def far_values(f, carry):
        rows = pl.ds(pl.multiple_of(f * FAR_TILE, FAR_TILE), FAR_TILE)
        p = jnp.exp(s_ref[:, :, rows] - rep(mrep_ref[0], far_blk))
        stat_ref[0] += lane_sum(p)
        acc_ref[...] += jnp.dot(p.reshape(h * tq, FAR_TILE).astype(BF16), sv_ref[rows, :],
                                preferred_element_type=F32).reshape(h, tq, LANES)
        return carry
    lax.fori_loop(0, n_far, far_values, 0)

    p = jnp.exp(s_ref[:, :, tail_col:tail_col + wide] - rep(mrep_ref[0], n_back + 1))
    l = jnp.sum(stat_ref[0] + lane_sum(p), axis=-1, keepdims=True)
    o_sel = acc_ref[...] + jnp.dot(p.reshape(h * tq, wide).astype(BF16), sv_ref[tail_rows, :],
                                   preferred_element_type=F32).reshape(h, tq, LANES)
    out_ref[...] += (_gate_columns(gl_ref, 1) * (1.0 / l)) * o_sel

    p = jnp.exp(sw_ref[...] - rep(mrep_ref[1], n_back + 1))
    l = jnp.sum(lane_sum(p), axis=-1, keepdims=True)
    o_win = jnp.dot(p.reshape(h * tq, wide).astype(BF16), wv_ref[win_rows, :],
                    preferred_element_type=F32).reshape(h, tq, LANES)
    out_ref[...] += (_gate_columns(gl_ref, 2) * (1.0 / l)) * o_win

    low = lax.broadcasted_iota(jnp.int32, (tq, LANES), 1) < HEAD_DIM
    for g in range(NSA_GROUP):
        o_ref[:, g * LANES:(g + 1) * LANES] = jnp.where(low, out_ref[g], out_ref[NSA_GROUP + g]).astype(BF16)


def nsa_prompt(q, kc, vc, skb, svb, wkb, wvb, small, rel_bias, B, T):
    tq = ATT_TILE
    nq = T // tq
    assert T % FAR_TILE == 0 and T >= WINDOW + ATT_TILE and WINDOW % ATT_TILE == 0
    wide = WINDOW + ATT_TILE
    far_max = (nq - 2) // (FAR_TILE // ATT_TILE) * FAR_TILE
    n_chunk = kc.shape[1]
    n_sel = -(-T // SEL_BLOCK)
    n_top = min(SEL_TOP, n_sel)
    bias_c = cmp_bias_tiles(rel_bias, T, tq, n_chunk)
    nb = near_bias_tables(rel_bias)
    ov = overlap_matrix(n_chunk, n_sel)
    ex = expand_matrix(T)
    h = NSA_HEADS
    full = lambda a: pl.BlockSpec(a.shape, lambda b, i: (0,) * a.ndim)
    seq = pl.BlockSpec((T, NSA_KV_DIM), lambda b, i: (b, 0))
    tok = pl.BlockSpec((1, n_chunk, NSA_KV_DIM), lambda b, i: (b, 0, 0))
    return pl.pallas_call(
        functools.partial(_nsa_prompt_body, n_sel=n_sel, n_top=n_top),
        grid=(B, nq),
        in_specs=[pl.BlockSpec((tq, NSA_Q_DIM), lambda b, i: (b * nq + i, 0)),
                  tok, tok, seq, seq, seq, seq,
                  pl.BlockSpec((1, h, tq, n_chunk), lambda b, i: (i, 0, 0, 0)),
                  full(nb),
                  pl.BlockSpec((tq, LANES), lambda b, i: (b * nq + i, 0)),
                  full(ov), full(ex)],
        out_specs=pl.BlockSpec((tq, NSA_Q_DIM), lambda b, i: (b * nq + i, 0)),
        out_shape=jax.ShapeDtypeStruct((B * T, NSA_Q_DIM), BF16),
        scratch_shapes=[pltpu.VMEM((NSA_KV_HEADS, tq, T), F32),
                        pltpu.VMEM((h, tq, far_max + wide), F32), pltpu.VMEM((h, tq, wide), F32),
                        pltpu.VMEM((2, h, tq, LANES), F32), pltpu.VMEM((2, h, tq, LANES), F32),
                        pltpu.VMEM((h, tq, LANES), F32), pltpu.VMEM((h, tq, LANES), F32)],
        compiler_params=pltpu.CompilerParams(dimension_semantics=("parallel", "arbitrary"),
                                             vmem_limit_bytes=VMEM_LIMIT),
        name="nsa_prompt",
    )(q, kc, vc, skb, svb, wkb, wvb, bias_c, nb, small, ov, ex)


def sample_bias_tables(rel_bias, past, T, W):
    t = jnp.arange(T)[:, None]
    pad = jnp.arange(ATT_TILE)[None, :]
    new = jnp.where((pad < T) & (t >= pad), bias_lookup(rel_bias, t - pad), MASKED)
    far = rel_bias.astype(F32)[NUM_BUCKETS - 1][:, None, None]

    def past_rows(n):
        near = bias_lookup(rel_bias, ATT_TILE + t - pad)
        return jnp.concatenate([jnp.broadcast_to(far, (far.shape[0], T, n - ATT_TILE)), near], axis=-1)

    sel = past_rows(past)
    rel_w = W + t - jnp.arange(W)[None, :]
    win = jnp.where(rel_w < WINDOW, past_rows(W), MASKED)
    ends = jnp.arange(past // CMP_STRIDE)[None, :] * CMP_STRIDE + CMP_LEN - 1
    rel_c = past + t - ends
    cmp = jnp.where(rel_c >= 0, bias_lookup(rel_bias, rel_c), MASKED)
    return cmp, jnp.concatenate([sel, new], axis=-1), jnp.concatenate([win, new], axis=-1)


def _softmax_rows(s):
    m = jnp.max(s, axis=-1, keepdims=True)
    p = jnp.exp(s - m)
    return p * (1.0 / jnp.sum(p, axis=-1, keepdims=True))


def _nsa_sample_body(pt_ref, q_ref, skn_ref, svn_ref, wkn_ref, wvn_ref, gl_ref, wink_ref, winv_ref,
                     bc_ref, bs_ref, bw_ref, ov_ref, ex_ref, pk_ref, pv_ref, w1k_ref, w1v_ref, w2k_ref, w2v_ref,
                     ck_pool, cv_pool, sk_pool, sv_pool, o_ref, wko_ref, wvo_ref,
                     s_ref, new_ref, rows_ref, page_ref, sem_ref, *, n_pages, n_sel, n_top, past):
    h = NSA_HEADS
    tq = q_ref.shape[0]
    tk = ATT_TILE
    b = pl.program_id(0)
    pools = (ck_pool, cv_pool, sk_pool, sv_pool)

    def page_copy(seq, slot, pool, pg):
        return pltpu.make_async_copy(pools[pool].at[pt_ref[seq, pg]], page_ref.at[slot, pool, pg], sem_ref.at[slot])

    def start_fetch(seq, slot):
        for pool in range(len(pools)):
            for pg in range(n_pages):
                page_copy(seq, slot, pool, pg).start()

    slot = b % 2

    @pl.when(b == 0)
    def _():
        start_fetch(0, 0)

    @pl.when(b + 1 < pl.num_programs(0))
    def _():
        start_fetch(b + 1, 1 - slot)

    for pool in range(len(pools)):
        for pg in range(n_pages):
            page_copy(b, slot, pool, pg).wait()
    pages = [[page_ref.at[slot, pool, pg] for pg in range(n_pages)] for pool in range(len(pools))]

    q = _stack_queries(q_ref, tq)

    new_ref[...] = jnp.zeros(new_ref.shape, F32)
    for n, r in enumerate((skn_ref, svn_ref, wkn_ref, wvn_ref)):
        new_ref[n, 0:tq, :] = r[...]

    for pg in range(n_pages):
        rows_ref[0, pg * tk:(pg + 1) * tk, :] = pages[0][pg][...].T
        rows_ref[1, pg * tk:(pg + 1) * tk, :] = pages[1][pg][...].T
    kc = _compress_one(rows_ref.at[0], pk_ref, w1k_ref, w2k_ref).astype(BF16)
    vc = _compress_one(rows_ref.at[1], pv_ref, w1v_ref, w2v_ref).astype(BF16)
    s = _dot_nt(q, kc).reshape(h, tq, -1) + bc_ref[...]
    p = _softmax_rows(s)
    o_cmp = jnp.dot(p.reshape(h * tq, -1).astype(BF16), vc, preferred_element_type=F32)
    out = _gate_columns(gl_ref, 0) * o_cmp.reshape(h, tq, LANES)

    masks = []
    for kv in range(NSA_KV_HEADS):
        pkv = p[kv * NSA_GROUP]
        for g in range(1, NSA_GROUP):
            pkv = pkv + p[kv * NSA_GROUP + g]
        masks.append(_selection_mask(pkv, ov_ref, ex_ref, past, n_sel, n_top))
    mask = jnp.stack(masks, axis=0)
    for pg in range(n_pages):
        s_ref[:, pg * tk:(pg + 1) * tk] = jnp.dot(q, pages[2][pg][...].astype(BF16), preferred_element_type=F32)
    s_ref[:, n_pages * tk:(n_pages + 1) * tk] = _dot_nt(q, new_ref[0].astype(BF16))
    width = (n_pages + 1) * tk
    s = s_ref[...].reshape(h, tq, width) + bs_ref[...]
    s = (s.reshape(NSA_KV_HEADS, NSA_GROUP, tq, width) + mask[:, None]).reshape(h, tq, width)
    p = _softmax_rows(s).reshape(h * tq, width).astype(BF16)
    o_sel = jnp.dot(p[:, n_pages * tk:], new_ref[1].astype(BF16), preferred_element_type=F32)
    for pg in range(n_pages):
        o_sel = o_sel + _dot_nt(p[:, pg * tk:(pg + 1) * tk], pages[3][pg][...].astype(BF16))
    out = out + _gate_columns(gl_ref, 1) * o_sel.reshape(h, tq, LANES)

    w = wink_ref.shape[2]
    s_ref[:, 0:w] = jnp.dot(q, wink_ref[0].astype(BF16), preferred_element_type=F32)
    s_ref[:, w:w + tk] = _dot_nt(q, new_ref[2].astype(BF16))
    s = s_ref[:, 0:w + tk].reshape(h, tq, w + tk) + bw_ref[...]
    p = _softmax_rows(s).reshape(h * tq, w + tk).astype(BF16)
    o_win = jnp.dot(p[:, w:], new_ref[3].astype(BF16), preferred_element_type=F32) \
        + _dot_nt(p[:, 0:w], winv_ref[0].astype(BF16))
    out = out + _gate_columns(gl_ref, 2) * o_win.reshape(h, tq, LANES)

    low = lax.broadcasted_iota(jnp.int32, (tq, LANES), 1) < HEAD_DIM
    for g in range(NSA_GROUP):
        o_ref[:, g * LANES:(g + 1) * LANES] = jnp.where(low, out[g], out[NSA_GROUP + g])

    keep = w - tq
    lane_w = lax.broadcasted_iota(jnp.int32, (NSA_KV_DIM, w), 1)
    for old_ref, new_rows_ref, dst_ref in ((wink_ref, wkn_ref, wko_ref), (winv_ref, wvn_ref, wvo_ref)):
        shifted = pltpu.roll(old_ref[0], shift=keep, axis=1)
        tail = jnp.concatenate([jnp.zeros((tk - tq, NSA_KV_DIM), F32), new_rows_ref[...]], axis=0).T
        tail = jnp.concatenate([jnp.zeros((NSA_KV_DIM, w - tk), F32), tail], axis=1)
        dst_ref[0] = jnp.where(lane_w >= keep, tail, shifted)


def nsa_sample(q, skn, svn, wkn, wvn, small, pools, win_k, win_v, page_table, rel_bias, pk, pv, B, T):
    n_pages = page_table.shape[1]
    past = n_pages * PAGE_SIZE
    W = win_k.shape[1]
    L = past + T
    n_cmp = (L - CMP_LEN) // CMP_STRIDE + 1
    assert PAGE_SIZE == ATT_TILE and T % 8 == 0 and T <= SEL_BLOCK and past % SEL_BLOCK == 0
    assert (n_cmp - 1) * CMP_STRIDE + CMP_LEN <= past and n_cmp >= past // CMP_STRIDE - CMP_HALVES + 1
    assert W % ATT_TILE == 0 and W + T >= WINDOW
    n_sel = -(-L // SEL_BLOCK)
    n_top = min(SEL_TOP, n_sel)
    w_keep = min(WINDOW, W + T)
    bc, bs, bw = sample_bias_tables(rel_bias, past, T, W)
    ov = overlap_matrix(past // CMP_STRIDE, n_sel)
    ex = expand_matrix(past + ATT_TILE)
    n_pool = pools[0].shape[0]
    assert W == w_keep
    feat_major = lambda a: a.transpose(0, 2, 3, 1).reshape(a.shape[0], NSA_KV_DIM, a.shape[1])
    cmp_pools = [feat_major(p) for p in pools[:2]]
    sel_pools = [feat_major(p) for p in pools[2:]]
    wk3 = feat_major(win_k)
    wv3 = feat_major(win_v)

    full = lambda a: pl.BlockSpec(a.shape, lambda b, pt: (0,) * a.ndim)
    row = lambda w: pl.BlockSpec((T, w), lambda b, pt: (b, 0))
    seq = lambda n: pl.BlockSpec((1, NSA_KV_DIM, n), lambda b, pt: (b, 0, 0))
    pool_args = cmp_pools + sel_pools
    pool_specs = [pl.BlockSpec(memory_space=pl.ANY)] * len(pool_args)
    tables = (bc, bs, bw, ov, ex, pk[0], pv[0], pk[1], pv[1], pk[2], pv[2])
    grid_spec = pltpu.PrefetchScalarGridSpec(
        num_scalar_prefetch=1,
        grid=(B,),
        in_specs=[row(NSA_Q_DIM), row(NSA_KV_DIM), row(NSA_KV_DIM), row(NSA_KV_DIM), row(NSA_KV_DIM), row(LANES),
                  seq(W), seq(W)] + [full(a) for a in tables] + pool_specs,
        out_specs=[row(NSA_Q_DIM), seq(w_keep), seq(w_keep)],
        scratch_shapes=[pltpu.VMEM((NSA_HEADS * T, past + ATT_TILE), F32),
                        pltpu.VMEM((4, ATT_TILE, NSA_KV_DIM), F32),
                        pltpu.VMEM((2, past, NSA_KV_DIM), F32),
                        pltpu.VMEM((2, len(pool_args), n_pages, NSA_KV_DIM, PAGE_SIZE), F32),
                        pltpu.SemaphoreType.DMA((2,))],
    )
    o, wk_next, wv_next = pl.pallas_call(
        functools.partial(_nsa_sample_body, n_pages=n_pages, n_sel=n_sel, n_top=n_top, past=past),
        grid_spec=grid_spec,
        out_shape=[jax.ShapeDtypeStruct((B * T, NSA_Q_DIM), F32),
                   jax.ShapeDtypeStruct((B, NSA_KV_DIM, w_keep), F32),
                   jax.ShapeDtypeStruct((B, NSA_KV_DIM, w_keep), F32)],
        compiler_params=pltpu.CompilerParams(dimension_semantics=("arbitrary",), vmem_limit_bytes=VMEM_LIMIT),
        name="nsa_sample",
    )(page_table, q, skn, svn, wkn, wvn, small, wk3, wv3, *tables, *pool_args)
    row_major = lambda a: a.reshape(B, NSA_KV_HEADS, HEAD_DIM, w_keep).transpose(0, 3, 1, 2)
    return o, row_major(wk_next), row_major(wv_next)


HIST_ROWS = 8
DN_CHUNKS_PER_STEP = 4
DN_SEQS_PER_STEP = 8
DN_INV_BASE = 8


def _split_bf16(x):
    hi = x.astype(BF16)
    lo = (x - hi.astype(F32)).astype(BF16)
    return hi, lo


def _bmm(a, b):
    return jnp.einsum('hij,hjk->hik', a, b, preferred_element_type=F32)


def _bmm_nt(a, b):
    return jnp.einsum('hid,hjd->hij', a, b, preferred_element_type=F32)


def _cumsum_rows(tri, x):
    hi, lo = _split_bf16(x)
    lo2 = (x - hi.astype(F32) - lo.astype(F32)).astype(BF16)
    d = lambda y: jnp.dot(tri, y, preferred_element_type=F32)
    return d(hi) + (d(lo) + d(lo2))


def _softplus(x):
    return jnp.maximum(x, 0.0) + jnp.log1p(jnp.exp(-jnp.abs(x)))


def _silu(x):
    return x * jax.nn.sigmoid(x)


def _deltanet_body(x_ref, sm_ref, dg_ref, hist_ref, s0_ref, cw_ref, apar_ref, dt_ref, ng_ref,
                   o_ref, s_out_ref, hist_out_ref, xbuf_ref, s_ref, *, c, n_seq):
    j = pl.program_id(1)
    rows = x_ref.shape[0]
    seq_rows = rows // n_seq
    n_grp = seq_rows // c
    hd = DN_HEAD_DIM
    n_hist = CONV_W - 1

    n_pair = DN_HEADS // 2
    zero_blk = jnp.zeros((hd, hd), F32)

    @pl.when(j == 0)
    def _():
        xbuf_ref[:, HIST_ROWS - n_hist:HIST_ROWS, :] = hist_ref[...]
        for sq in range(n_seq):
            for p in range(n_pair):
                s_ref[sq * n_pair + p] = jnp.concatenate(
                    [jnp.concatenate([s0_ref[sq, 2 * p], zero_blk], axis=1),
                     jnp.concatenate([zero_blk, s0_ref[sq, 2 * p + 1]], axis=1)], axis=0)

    xbuf_ref[:, HIST_ROWS:HIST_ROWS + seq_rows, :] = x_ref[...].reshape(n_seq, seq_rows, x_ref.shape[1])
    y = None
    for t in range(CONV_W):
        lo = HIST_ROWS - n_hist + t
        term = xbuf_ref[:, lo:lo + seq_rows, :] * cw_ref[t:t + 1, :]
        y = term if y is None else y + term
    y = _silu(y).reshape(rows, x_ref.shape[1])
    tail = xbuf_ref[:, HIST_ROWS + seq_rows - n_hist:HIST_ROWS + seq_rows, :]
    xbuf_ref[:, HIST_ROWS - n_hist:HIST_ROWS, :] = tail

    sm = sm_ref[...]
    beta_all = jax.nn.sigmoid(sm)
    g_all = apar_ref[...] * _softplus(sm + dt_ref[...])
    row = lax.broadcasted_iota(jnp.int32, (rows, rows), 0)
    col = lax.broadcasted_iota(jnp.int32, (rows, rows), 1)
    tri = jnp.where((row >= col) & (row // c == col // c), 1.0, 0.0).astype(BF16)
    gcum_all = _cumsum_rows(tri, g_all)
    gcum_t = gcum_all.T

    wd, wj = 2 * hd, 2 * c
    first_d = lax.broadcasted_iota(jnp.int32, (1, wd), 1) < hd
    first_j = lax.broadcasted_iota(jnp.int32, (1, wj), 1) < c
    rj = lax.broadcasted_iota(jnp.int32, (c, wj), 0)
    cj = lax.broadcasted_iota(jnp.int32, (c, wj), 1)
    cj = jnp.where(cj < c, cj, cj - c)
    incl, strict = rj >= cj, rj > cj
    eye = jnp.where(rj == cj, 1.0, 0.0)
    same_head = (lax.broadcasted_iota(jnp.int32, (wd, wd), 0) < hd) == (lax.broadcasted_iota(jnp.int32, (wd, wd), 1) < hd)
    ones_bd = jnp.where(same_head, 1.0, 0.0).astype(BF16)
    first_rows = lax.broadcasted_iota(jnp.int32, (wd, 1), 0) < hd

    items = [(g, p) for g in range(n_seq * n_grp) for p in range(n_pair)]

    def block_diag(x, first):
        zero = jnp.zeros_like(x)
        return jnp.concatenate([jnp.where(first, x, zero), jnp.where(first, zero, x)], axis=1)

    def head_sums(x):
        n = x.shape[0]
        hi, lo = _split_bf16(x.reshape(n * c, wd))
        tot = jnp.dot(hi, ones_bd, preferred_element_type=F32) + jnp.dot(lo, ones_bd, preferred_element_type=F32)
        return tot.reshape(n, c, wd)

    def mm_hi(a_parts, b_parts):
        return _bmm(a_parts[0], b_parts[0]) + (_bmm(a_parts[0], b_parts[1]) + _bmm(a_parts[1], b_parts[0]))

    def per_head(cols, base, first):
        return jnp.stack([jnp.where(first, cols[g * c:(g + 1) * c, base + 2 * p:base + 2 * p + 1],
                                    cols[g * c:(g + 1) * c, base + 2 * p + 1:base + 2 * p + 2])
                          for g, p in items], axis=0)

    lanes = lambda x, off: jnp.stack([x[g * c:(g + 1) * c, off + p * wd:off + (p + 1) * wd] for g, p in items], axis=0)
    q = lanes(y, 0)
    k = lanes(y, DN_DIM)
    v = lanes(y, 2 * DN_DIM)
    q = q * lax.rsqrt(head_sums(q * q) + EPS) * hd ** -0.5
    k = k * lax.rsqrt(head_sums(k * k) + EPS)
    beta = per_head(beta_all, SM_BETA, first_d)
    gc_d = per_head(gcum_all, SM_DECAY, first_d)
    gc_j = per_head(gcum_all, SM_DECAY, first_j)
    gr_j = jnp.stack([jnp.concatenate([gcum_t[SM_DECAY + 2 * p:SM_DECAY + 2 * p + 1, g * c:(g + 1) * c],
                                       gcum_t[SM_DECAY + 2 * p + 1:SM_DECAY + 2 * p + 2, g * c:(g + 1) * c]], axis=1)
                      for g, p in items], axis=0)
    gl_d = gc_d[:, c - 1:c, :]
    decay = jnp.where(incl, jnp.exp(jnp.where(incl, gc_j - gr_j, 0.0)), 0.0)
    eg = jnp.exp(gc_d)
    kb = k * beta
    k_bd = block_diag(k.astype(BF16), first_d)
    a = jnp.where(strict, _bmm_nt(kb.astype(BF16), k_bd) * decay, 0.0)
    split_bd = lambda parts, first: tuple(block_diag(x, first) for x in parts)
    base = min(c, DN_INV_BASE)
    same_block = lambda b: (rj // b) == (cj // b)
    diag = jnp.where(same_block(base), a, 0.0)
    t_inv = eye - diag
    pow_parts = _split_bf16(diag)
    pow_bd = split_bd(pow_parts, first_j)
    for _ in range(int(math.log2(base)) - 1):
        pow_parts = _split_bf16(mm_hi(pow_parts, pow_bd))
        pow_bd = split_bd(pow_parts, first_j)
        t_inv = t_inv + mm_hi(_split_bf16(t_inv), pow_bd)
    blk = base
    while blk < c:
        lower = jnp.where(same_block(2 * blk) & jnp.logical_not(same_block(blk)), a, 0.0)
        t_parts = _split_bf16(t_inv)
        lt = mm_hi(_split_bf16(lower), split_bd(t_parts, first_j))
        t_inv = t_inv - mm_hi(t_parts, split_bd(_split_bf16(lt), first_j))
        blk *= 2
    vb_parts = split_bd(_split_bf16(v * beta), first_d)
    kbg_parts = split_bd(_split_bf16(kb * eg), first_d)
    rhs = tuple(jnp.concatenate([x, z], axis=2) for x, z in zip(vb_parts, kbg_parts))
    uw = mm_hi(_split_bf16(t_inv), rhs)
    u, w = uw[:, :, :wd], uw[:, :, wd:]
    qk = jnp.where(incl, _bmm_nt(q.astype(BF16), k_bd) * decay, 0.0).astype(BF16)
    wq = jnp.concatenate([w, q * eg], axis=1).astype(BF16)
    kg = (k * jnp.exp(gl_d - gc_d)).astype(BF16)
    gl_rows = jnp.where(first_rows, jnp.exp(gl_d[:, :, 0:1]), jnp.exp(gl_d[:, :, hd:hd + 1]))

    s = s_ref[...]
    outs = {}
    for kc in range(n_grp):
        pick = lambda x: jnp.concatenate([x[(sq * n_grp + kc) * n_pair:(sq * n_grp + kc + 1) * n_pair]
                                          for sq in range(n_seq)], axis=0)
        ws = _bmm(pick(wq), s.astype(BF16))
        vn = (pick(u) - ws[:, :c]).astype(BF16)
        o_k = ws[:, c:] + _bmm(pick(qk), block_diag(vn, first_d))
        kv_outer = jnp.einsum('pck,pcv->pkv', pick(kg), vn, preferred_element_type=F32)
        s = s * pick(gl_rows) + jnp.where(same_head, kv_outer, 0.0)
        for sq in range(n_seq):
            outs[sq * n_grp + kc] = o_k[sq * n_pair:(sq + 1) * n_pair]
    s_ref[...] = s
    o = jnp.concatenate([outs[g] for g in range(n_seq * n_grp)], axis=0)
    ng = jnp.concatenate([ng_ref[...], ng_ref[...]], axis=1)
    o = o * lax.rsqrt(head_sums(o * o) * (1.0 / hd) + EPS) * ng
    o = jnp.concatenate([jnp.concatenate([o[g * n_pair + p] for p in range(n_pair)], axis=1)
                         for g in range(n_seq * n_grp)], axis=0)
    o_ref[...] = (o * _silu(dg_ref[...])).astype(o_ref.dtype)

    @pl.when(j == pl.num_programs(1) - 1)
    def _():
        for sq in range(n_seq):
            for p in range(n_pair):
                s_pair = s_ref[sq * n_pair + p]
                s_out_ref[sq, 2 * p] = s_pair[0:hd, 0:hd]
                s_out_ref[sq, 2 * p + 1] = s_pair[hd:2 * hd, hd:2 * hd]
        hist_out_ref[...] = tail


def deltanet(qkv_raw, small, dg, conv_hist, s0, lp, B, T):
    c = DN_CHUNK if T >= DN_CHUNK else T
    assert T % c == 0 and c % 8 == 0 and c >= CONV_W - 1 and (c & (c - 1)) == 0
    n_grp = math.gcd(T // c, DN_CHUNKS_PER_STEP)
    n_seq = math.gcd(B, DN_SEQS_PER_STEP) if T == c else 1
    seq_rows = n_grp * c
    step = n_seq * seq_rows
    n = T // seq_rows
    put = lambda vals, base: jnp.zeros((LANES,), F32).at[base + jnp.arange(DN_HEADS)].set(vals).reshape(1, LANES)
    apar = put(-jnp.exp(lp['dn_a_log'].astype(F32)), SM_DECAY)
    dtb = put(lp['dn_dt_bias'].astype(F32), SM_DECAY)
    assert n_seq == 1 or n == 1
    rows = lambda w: pl.BlockSpec((step, w), lambda b, j: (b * n + j, 0))
    per_b = lambda a: pl.BlockSpec((n_seq,) + a.shape[1:], lambda b, j: (b,) + (0,) * (a.ndim - 1))
    full = lambda a: pl.BlockSpec(a.shape, lambda b, j: (0,) * a.ndim)
    ng = lp['dn_norm_g'].reshape(1, DN_HEAD_DIM).astype(F32)
    return pl.pallas_call(
        functools.partial(_deltanet_body, c=c, n_seq=n_seq),
        grid=(B // n_seq, n),
        in_specs=[rows(DN_CONV_DIM), rows(LANES), rows(DN_DIM), per_b(conv_hist), per_b(s0),
                  full(lp['conv_w']), full(apar), full(dtb), full(ng)],
        out_specs=[rows(DN_DIM), per_b(s0), per_b(conv_hist)],
        out_shape=[jax.ShapeDtypeStruct((B * T, DN_DIM), BF16 if step % 16 == 0 else F32),
                   jax.ShapeDtypeStruct(s0.shape, F32),
                   jax.ShapeDtypeStruct(conv_hist.shape, F32)],
        scratch_shapes=[pltpu.VMEM((n_seq, HIST_ROWS + seq_rows, DN_CONV_DIM), F32),
                        pltpu.VMEM((n_seq * (DN_HEADS // 2), 2 * DN_HEAD_DIM, 2 * DN_HEAD_DIM), F32)],
        compiler_params=pltpu.CompilerParams(dimension_semantics=("parallel", "arbitrary"),
                                             vmem_limit_bytes=VMEM_LIMIT),
        name="deltanet",
    )(qkv_raw, small, dg, conv_hist, s0, lp['conv_w'], apar, dtb, ng)


MOE_EXPERTS_PER_STEP = 4
RT_GROUP = 0
RT_EXPERT = N_GROUPS


def pack_router(wg, bg, we, be):
    w = jnp.concatenate([wg, we], axis=1)
    b = jnp.concatenate([bg, be], axis=0)
    pad = LANES - w.shape[1]
    return jnp.pad(w, ((0, 0), (0, pad))).astype(BF16), jnp.pad(b, (0, pad)).reshape(1, LANES).astype(F32)


def _route(r):
    lane = lax.broadcasted_iota(jnp.int32, r.shape, 1).astype(F32)
    big = float(LANES)
    is_grp = lane < N_GROUPS
    lg = jnp.where(is_grp, r, -jnp.inf)
    mg = jnp.max(lg, axis=-1, keepdims=True)
    eg = jnp.where(is_grp, jnp.exp(lg - mg), 0.0)
    grp = jnp.min(jnp.where(lg == mg, lane, big), axis=-1, keepdims=True)
    wg = 1.0 / jnp.sum(eg, axis=-1, keepdims=True)
    first = RT_EXPERT + grp * EXPERTS_PER_GROUP
    in_grp = (lane >= first) & (lane < first + EXPERTS_PER_GROUP)
    le = jnp.where(in_grp, r, -jnp.inf)
    me = jnp.max(le, axis=-1, keepdims=True)
    ee = jnp.where(in_grp, jnp.exp(le - me), 0.0)
    pe = jnp.where(in_grp, ee / jnp.sum(ee, axis=-1, keepdims=True), -1.0)
    v1 = jnp.max(pe, axis=-1, keepdims=True)
    i1 = jnp.min(jnp.where(pe == v1, lane, big), axis=-1, keepdims=True)
    pe2 = jnp.where(lane == i1, -1.0, pe)
    v2 = jnp.max(pe2, axis=-1, keepdims=True)
    i2 = jnp.min(jnp.where(pe2 == v2, lane, big), axis=-1, keepdims=True)
    tot = v1 + v2
    return jnp.where(lane == i1, wg * (v1 / tot), jnp.where(lane == i2, wg * (v2 / tot), 0.0))


def _finish_body(x_ref, on_ref, od_ref, g1_ref, sc_ref, sh_ref, g2_ref, won_ref, wod_ref, ng_ref, wr_ref, br_ref,
                 wgu_ref, wd_ref, fg_ref, o_ref, x1_ref, h_ref, gate_ref, y_ref):
    e = pl.program_id(1)
    tm, d = x_ref.shape
    nb = g1_ref.shape[0]
    per_batch = lambda a: a.reshape(nb, tm // nb, d)

    @pl.when(e == 0)
    def _():
        mix = jnp.dot(on_ref[...].astype(BF16), won_ref[...], preferred_element_type=F32) \
            + jnp.dot(od_ref[...].astype(BF16), wod_ref[...], preferred_element_type=F32)
        x1 = per_batch(x_ref[...]) + g1_ref[...] * per_batch(mix)
        x1 = x1.reshape(tm, d)
        x1_ref[...] = x1
        xn = x1 * lax.rsqrt(jnp.mean(x1 * x1, axis=-1, keepdims=True) + EPS) * ng_ref[...]
        h = (per_batch(xn) * (1 + sc_ref[...]) + sh_ref[...]).reshape(tm, d).astype(BF16)
        h_ref[...] = h
        gate_ref[...] = _route(jnp.dot(h, wr_ref[...], preferred_element_type=F32) + br_ref[...])
        y_ref[...] = jnp.zeros(y_ref.shape, F32)

    h = h_ref[...]
    lane = lax.broadcasted_iota(jnp.int32, gate_ref.shape, 1)
    y = y_ref[...]
    for k in range(wgu_ref.shape[0]):
        gu = jnp.dot(h, wgu_ref[k], preferred_element_type=F32)
        he = _silu(gu[:, :D_EXPERT]) * gu[:, D_EXPERT:]
        out = jnp.dot(he.astype(BF16), wd_ref[k], preferred_element_type=F32)
        expert_lane = RT_EXPERT + e * wgu_ref.shape[0] + k
        g_col = jnp.sum(jnp.where(lane == expert_lane, gate_ref[...], 0.0), axis=-1, keepdims=True)
        y = y + g_col * out
    y_ref[...] = y

    @pl.when(e == pl.num_programs(1) - 1)
    def _():
        x2 = per_batch(x1_ref[...]) + g2_ref[...] * per_batch(y_ref[...])
        x2 = x2.reshape(tm, d)
        o_ref[...] = x2 * lax.rsqrt(jnp.mean(x2 * x2, axis=-1, keepdims=True) + EPS) * fg_ref[...]


def finish_layer(x, o_nsa, o_dn, g1, sc2, sh2, g2, packed, norm_ffn_g, final_g, *, tm=1024):
    B, T, D = x.shape
    n = B * T
    tm = min(tm, n)
    rows = max(tm // T, 1)
    per = max(T // tm, 1)
    row = lambda w: pl.BlockSpec((tm, w), lambda i, e: (i, 0))
    mod = pl.BlockSpec((rows, 1, D), lambda i, e: (i // per, 0, 0))
    full = lambda a: pl.BlockSpec(a.shape, lambda i, e: (0,) * a.ndim)
    expert = lambda a: pl.BlockSpec((MOE_EXPERTS_PER_STEP,) + a.shape[1:], lambda i, e: (e, 0, 0))
    vec = lambda v: v.reshape(1, D).astype(F32)
    args = (x.reshape(n, D), o_nsa, o_dn, g1, sc2, sh2, g2, packed['w_out_nsa'], packed['w_out_dn'],
            vec(norm_ffn_g), packed['router_w'], packed['router_b'], packed['w_gate_up'], packed['w_down'],
            vec(final_g))
    specs = [row(D), row(o_nsa.shape[1]), row(o_dn.shape[1]), mod, mod, mod, mod] + \
            [full(a) for a in args[7:12]] + [expert(args[12]), expert(args[13]), full(args[14])]
    return pl.pallas_call(
        _finish_body,
        grid=(n // tm, N_EXPERTS // MOE_EXPERTS_PER_STEP),
        in_specs=specs,
        out_specs=row(D),
        out_shape=jax.ShapeDtypeStruct((n, D), F32),
        scratch_shapes=[pltpu.VMEM((tm, D), F32), pltpu.VMEM((tm, D), BF16), pltpu.VMEM((tm, LANES), F32),
                        pltpu.VMEM((tm, D), F32)],
        compiler_params=pltpu.CompilerParams(dimension_semantics=("parallel", "arbitrary"),
                                             vmem_limit_bytes=VMEM_LIMIT),
        name="finish_layer",
    )(*args).reshape(B, T, D)


def permute_w_out_nsa(w_out):
    w = w_out[:NSA_Q_DIM].reshape(NSA_KV_HEADS, NSA_GROUP, HEAD_DIM, -1)
    return w.transpose(1, 0, 2, 3).reshape(NSA_Q_DIM, -1).astype(BF16)


def layer_prompt(x, mod, lp, rel_bias, packed):
    B, T, _ = x.shape
    sh1, sc1, g1, sh2, sc2, g2 = mod
    (q, ck, cv, sk, sv, wk, wv, skb, svb, wkb, wvb, qkv_raw, dg, small, ck_rows, cv_rows) = in_projection(
        x, lp['norm_mix_g'], sc1, sh1, packed['w_in'], feature_major=True)
    kc, vc = compress_tokens(ck_rows, cv_rows, packed['cmp_k'], packed['cmp_v'], B)
    o_nsa = nsa_prompt(q, kc, vc, skb, svb, wkb, wvb, small, rel_bias, B, T)
    kv = lambda r: r.reshape(B, NSA_KV_HEADS, HEAD_DIM, -1).transpose(0, 3, 1, 2)
    w_keep = min(WINDOW, T)
    ck, cv, sk, sv, wk, wv = kv(ck), kv(cv), kv(sk), kv(sv), kv(wk[:, :, T - w_keep:]), kv(wv[:, :, T - w_keep:])
    conv_hist = jnp.zeros((B, CONV_W - 1, DN_CONV_DIM), x.dtype)
    s0 = jnp.zeros((B, DN_HEADS, DN_HEAD_DIM, DN_HEAD_DIM), x.dtype)
    o_dn, s_new, conv_new = deltanet(qkv_raw, small, dg, conv_hist, s0, lp, B, T)
    y = finish_layer(x, o_nsa, o_dn, g1, sc2, sh2, g2, packed, lp['norm_ffn_g'], packed['final_norm_g'])
    return y, (ck, cv, sk, sv, wk, wv, s_new, conv_new)


def layer_sample(x, mod, cmp_k_pool, cmp_v_pool, sel_k_pool, sel_v_pool, win_k, win_v, s0, conv_hist,
                 page_table, lp, rel_bias, packed):
    B, T, _ = x.shape
    sh1, sc1, g1, sh2, sc2, g2 = mod
    (q, ck, cv, sk, sv, wk, wv, _, _, _, _, qkv_raw, dg, small) = in_projection(
        x, lp['norm_mix_g'], sc1, sh1, packed['w_in'])
    o_nsa, wk_next, wv_next = nsa_sample(q, sk, sv, wk, wv, small, (cmp_k_pool, cmp_v_pool, sel_k_pool, sel_v_pool),
                                         win_k, win_v, page_table, rel_bias, packed['cmp_k'], packed['cmp_v'], B, T)
    kv = lambda r: r.reshape(B, -1, NSA_KV_HEADS, HEAD_DIM)
    o_dn, s_new, conv_new = deltanet(qkv_raw, small, dg, conv_hist, s0, lp, B, T)
    y = finish_layer(x, o_nsa, o_dn, g1, sc2, sh2, g2, packed, lp['norm_ffn_g'], packed['final_norm_g'])
    return y, (kv(ck), kv(cv), kv(sk), kv(sv), kv(wk_next), kv(wv_next), s_new, conv_new)


def kernel(x_prompt, x_sample, cache_cmp_k, cache_cmp_v, cache_sel_k, cache_sel_v, cache_win_k, cache_win_v,
           state_delta, state_conv, page_table, c_prompt, c_sample, rel_bias, w_ada, b_ada, norm_mix_g,
           norm_ffn_g, w_in, cmp_pos_k, cmp_w1_k, cmp_w2_k, cmp_pos_v, cmp_w1_v, cmp_w2_v, conv_w, dn_a_log,
           dn_dt_bias, dn_norm_g, w_out, router_group_w, router_group_b, router_expert_w, router_expert_b,
           expert_w_gate, expert_w_up, expert_w_down, final_norm_g):
    xp, xs = x_prompt, x_sample
    new_p, new_s = [], []
    for l in range(DEPTH):
        lp = dict(norm_mix_g=norm_mix_g[l], norm_ffn_g=norm_ffn_g[l], conv_w=conv_w[l],
                  dn_a_log=dn_a_log[l], dn_dt_bias=dn_dt_bias[l], dn_norm_g=dn_norm_g[l])
        packed = dict(w_in=pack_w_in(w_in[l]),
                      cmp_k=pack_compress(cmp_pos_k[l], cmp_w1_k[l], cmp_w2_k[l]),
                      cmp_v=pack_compress(cmp_pos_v[l], cmp_w1_v[l], cmp_w2_v[l]),
                      w_out_nsa=permute_w_out_nsa(w_out[l]),
                      w_out_dn=w_out[l][NSA_Q_DIM:].astype(BF16),
                      w_gate_up=jnp.concatenate([expert_w_gate[l], expert_w_up[l]], axis=-1).astype(BF16),
                      w_down=expert_w_down[l].astype(BF16),
                      final_norm_g=final_norm_g)
        packed['router_w'], packed['router_b'] = pack_router(router_group_w[l], router_group_b[l],
                                                             router_expert_w[l], router_expert_b[l])
        n_p = c_prompt.shape[0]
        mod = modulation(jnp.concatenate([c_prompt, c_sample], axis=0), w_ada[l], b_ada[l])
        xp, st_p = layer_prompt(xp, [m[:n_p] for m in mod], lp, rel_bias, packed)
        xs, st_s = layer_sample(xs, [m[n_p:] for m in mod], cache_cmp_k[l], cache_cmp_v[l], cache_sel_k[l],
                                cache_sel_v[l],
                                cache_win_k[l], cache_win_v[l], state_delta[l], state_conv[l], page_table,
                                lp, rel_bias, packed)
        new_p.append(st_p)
        new_s.append(st_s)
    assert DEPTH == 1
    y_prompt, y_sample = xp, xs

    def stk(states, i):
        return jnp.stack([s[i] for s in states])

    return (y_prompt, y_sample,
            stk(new_p, 0), stk(new_p, 1), stk(new_p, 2), stk(new_p, 3),
            stk(new_p, 4), stk(new_p, 5), stk(new_p, 6), stk(new_p, 7),
            stk(new_s, 0), stk(new_s, 1), stk(new_s, 2), stk(new_s, 3),
            stk(new_s, 4), stk(new_s, 5), stk(new_s, 6), stk(new_s, 7))
```

```python
import functools
import math

import jax
import jax.numpy as jnp
import numpy as np
from jax import lax
from jax.experimental import pallas as pl
from jax.experimental.pallas import tpu as pltpu

D_MODEL = 1024
DEPTH = 1
PAGE_SIZE = 128

NSA_HEADS = 8
NSA_KV_HEADS = 2
NSA_GROUP = NSA_HEADS // NSA_KV_HEADS
HEAD_DIM = 64
CMP_LEN = 32
CMP_STRIDE = 16
CMP_HIDDEN = 4 * HEAD_DIM
SEL_BLOCK = 64
SEL_TOP = 16
WINDOW = 512
DN_HEADS = 8
DN_HEAD_DIM = 64
CONV_W = 4
DN_CHUNK = 64
NUM_BUCKETS = 32
MAX_DISTANCE = 128
N_GROUPS = 4
EXPERTS_PER_GROUP = 8
N_EXPERTS = N_GROUPS * EXPERTS_PER_GROUP
TOP_K_IN_GROUP = 2
D_EXPERT = D_MODEL // 4

EPS = 1e-6
NEG_INF = -1e30
F32 = jnp.float32
BF16 = jnp.bfloat16

NSA_Q_DIM = NSA_HEADS * HEAD_DIM
NSA_KV_DIM = NSA_KV_HEADS * HEAD_DIM
DN_DIM = DN_HEADS * DN_HEAD_DIM
DN_CONV_DIM = 3 * DN_DIM
MIX_WIDTH = NSA_Q_DIM + DN_DIM
IN_SPLITS = (NSA_Q_DIM, NSA_KV_DIM, NSA_KV_DIM, NSA_KV_DIM, NSA_KV_DIM, NSA_KV_DIM, NSA_KV_DIM,
             3 * NSA_HEADS, DN_CONV_DIM, DN_HEADS, DN_HEADS, DN_DIM)
IN_DIM = sum(IN_SPLITS)

LANES = 128
VMEM_LIMIT = 56 * 1024 * 1024

ATT_TILE = 128
MASKED = -1e30


def t5_bucket(rel):
    n = jnp.maximum(rel, 0)
    max_exact = NUM_BUCKETS // 2
    nf = jnp.maximum(n, 1).astype(F32)
    large = max_exact + (jnp.log(nf / max_exact) / math.log(MAX_DISTANCE / max_exact)
                         * (NUM_BUCKETS - max_exact)).astype(jnp.int32)
    large = jnp.minimum(large, NUM_BUCKETS - 1)
    return jnp.where(n < max_exact, n, large)


def _modulation_body(c_ref, w_ref, b_ref, o_ref):
    c = c_ref[...]
    x = (c * jax.nn.sigmoid(c)).astype(BF16)
    o_ref[...] = jnp.dot(x, w_ref[...].astype(BF16), preferred_element_type=F32) + b_ref[...]


def modulation(c, w_ada, b_ada, *, tn=1024):
    rows, d = c.shape
    n = w_ada.shape[1]
    mod = pl.pallas_call(
        _modulation_body,
        grid=(n // tn,),
        in_specs=[pl.BlockSpec((rows, d), lambda j: (0, 0)),
                  pl.BlockSpec((d, tn), lambda j: (0, j)),
                  pl.BlockSpec((1, tn), lambda j: (0, j))],
        out_specs=pl.BlockSpec((rows, tn), lambda j: (0, j)),
        out_shape=jax.ShapeDtypeStruct((rows, n), F32),
        compiler_params=pltpu.CompilerParams(dimension_semantics=("parallel",), vmem_limit_bytes=VMEM_LIMIT),
        name="modulation",
    )(c, w_ada, b_ada.reshape(1, n))
    return jnp.split(mod[:, None, :], 6, axis=-1)


_PK_Q = (0, NSA_Q_DIM)
_PK_KV = (_PK_Q[1], _PK_Q[1] + 6 * NSA_KV_DIM)
_PK_QKV = (_PK_KV[1], _PK_KV[1] + DN_CONV_DIM)
_PK_DG = (_PK_QKV[1], _PK_QKV[1] + DN_DIM)
_PK_SM = (_PK_DG[1], _PK_DG[1] + LANES)
PK_DIM = _PK_SM[1]
SM_GATE = 0
SM_BETA = 3 * NSA_HEADS
SM_DECAY = SM_BETA + DN_HEADS


def pack_w_in(w_in):
    offs = np.cumsum((0,) + IN_SPLITS)
    part = lambda i: w_in[:, offs[i]:offs[i + 1]]
    d = w_in.shape[0]
    q = part(0).reshape(d, NSA_KV_HEADS, NSA_GROUP, HEAD_DIM).transpose(0, 2, 1, 3).reshape(d, NSA_Q_DIM)
    q = q * HEAD_DIM ** -0.5
    small = jnp.concatenate([part(7), part(9), part(10)], axis=1)
    small = jnp.pad(small, ((0, 0), (0, LANES - small.shape[1])))
    cols = [q] + [part(i) for i in range(1, 7)] + [part(8), part(11), small]
    return jnp.concatenate(cols, axis=1).astype(BF16)


def _in_proj_body(x_ref, g_ref, sc_ref, sh_ref, w_ref, q_ref, ck_ref, cv_ref, sk_ref, sv_ref, wk_ref, wv_ref,
                  skb_ref, svb_ref, wkb_ref, wvb_ref, qkv_ref, dg_ref, sm_ref, *row_refs):
    x = x_ref[...]
    tm, d = x.shape
    nb = sc_ref.shape[0]
    xn = x * lax.rsqrt(jnp.mean(x * x, axis=-1, keepdims=True) + EPS) * g_ref[...]
    h = xn.reshape(nb, tm // nb, d) * (1 + sc_ref[...]) + sh_ref[...]
    h = h.reshape(tm, d).astype(BF16)

    def mm(span):
        return jnp.dot(h, w_ref[:, span[0]:span[1]], preferred_element_type=F32)

    q_ref[...] = mm(_PK_Q).astype(BF16)
    kv = mm(_PK_KV)
    for i, r in enumerate((ck_ref, cv_ref, sk_ref, sv_ref, wk_ref, wv_ref)):
        rows = kv[:, i * NSA_KV_DIM:(i + 1) * NSA_KV_DIM]
        if row_refs:
            r[0] = rows.T
        else:
            r[...] = rows
    for i, r in enumerate(row_refs):
        r[...] = kv[:, i * NSA_KV_DIM:(i + 1) * NSA_KV_DIM]
    for i, r in enumerate((skb_ref, svb_ref, wkb_ref, wvb_ref)):
        r[...] = kv[:, (i + 2) * NSA_KV_DIM:(i + 3) * NSA_KV_DIM].astype(BF16)
    qkv_ref[...] = mm(_PK_QKV)
    dg_ref[...] = mm(_PK_DG)
    sm_ref[...] = mm(_PK_SM)


def in_projection(x, norm_g, scale, shift, w_packed, *, tm=256, feature_major=False):
    B, T, D = x.shape
    n = B * T
    tm = min(tm, n)
    rows = max(tm // T, 1)
    per = max(T // tm, 1)
    row = lambda i: (i, 0)
    by_row = lambda w, dt: (((n, w), dt), pl.BlockSpec((tm, w), row))
    if feature_major:
        assert tm <= T and tm % LANES == 0
        kv_out = (((B, NSA_KV_DIM, T), F32), pl.BlockSpec((1, NSA_KV_DIM, tm), lambda i: (i // per, 0, i % per)))
    else:
        kv_out = by_row(NSA_KV_DIM, F32)
    outs = [by_row(NSA_Q_DIM, BF16)] + [kv_out] * 6 + [by_row(NSA_KV_DIM, BF16)] * 4 + \
           [by_row(DN_CONV_DIM, F32), by_row(DN_DIM, F32), by_row(LANES, F32)] + \
           ([by_row(NSA_KV_DIM, F32)] * 2 if feature_major else [])
    return pl.pallas_call(
        _in_proj_body,
        grid=(n // tm,),
        in_specs=[pl.BlockSpec((tm, D), row),
                  pl.BlockSpec((1, D), lambda i: (0, 0)),
                  pl.BlockSpec((rows, 1, D), lambda i: (i // per, 0, 0)),
                  pl.BlockSpec((rows, 1, D), lambda i: (i // per, 0, 0)),
                  pl.BlockSpec((D, PK_DIM), lambda i: (0, 0))],
        out_specs=[spec for _, spec in outs],
        out_shape=[jax.ShapeDtypeStruct(s, dt) for (s, dt), _ in outs],
        compiler_params=pltpu.CompilerParams(dimension_semantics=("parallel",), vmem_limit_bytes=VMEM_LIMIT),
        name="in_projection",
    )(x.reshape(n, D), norm_g.reshape(1, D), scale, shift, w_packed)


CHUNK_W = CMP_STRIDE * NSA_KV_DIM
CMP_HALVES = CMP_LEN // CMP_STRIDE


def pack_compress(pos, w1, w2):
    eye = jnp.eye(NSA_KV_HEADS, dtype=F32)
    w1r = w1.reshape(CMP_HALVES, CMP_STRIDE, HEAD_DIM, CMP_HIDDEN)
    w1p = jnp.einsum('ijdc,kl->ijkdlc', w1r, eye).reshape(CMP_HALVES, CHUNK_W, NSA_KV_HEADS * CMP_HIDDEN)
    w2p = jnp.einsum('cd,kl->kcld', w2, eye).reshape(NSA_KV_HEADS * CMP_HIDDEN, NSA_KV_DIM)
    posp = jnp.broadcast_to(pos.reshape(CMP_HALVES, CMP_STRIDE, 1, HEAD_DIM),
                            (CMP_HALVES, CMP_STRIDE, NSA_KV_HEADS, HEAD_DIM)).reshape(CMP_HALVES, CHUNK_W)
    return posp, w1p.astype(BF16), w2p.astype(BF16)


def _compress_one(rows_ref, pos_ref, w1_ref, w2_ref):
    n_chunk = rows_ref.shape[0] // CMP_STRIDE
    x = jnp.concatenate([rows_ref[pl.ds(j, n_chunk, stride=CMP_STRIDE), :] for j in range(CMP_STRIDE)], axis=1)
    hid = None
    for i in range(CMP_HALVES):
        part = jnp.dot((x + pos_ref[i:i + 1]).astype(BF16), w1_ref[i], preferred_element_type=F32)
        if i:
            part = pltpu.roll(part, shift=n_chunk - i, axis=0)
        hid = part if hid is None else hid + part
    hid = hid * jax.nn.sigmoid(hid)
    return jnp.dot(hid.astype(BF16), w2_ref[...], preferred_element_type=F32)


def _compress_body(xk_ref, xv_ref, pk_ref, pv_ref, w1k_ref, w1v_ref, w2k_ref, w2v_ref, kc_ref, vc_ref):
    kc_ref[0] = _compress_one(xk_ref, pk_ref, w1k_ref, w2k_ref).astype(BF16)
    vc_ref[0] = _compress_one(xv_ref, pv_ref, w1v_ref, w2v_ref).astype(BF16)


def compress_tokens(ck, cv, pk, pv, B):
    L = ck.shape[0] // B
    n_chunk = L // CMP_STRIDE
    full = lambda a: pl.BlockSpec(a.shape, lambda b: (0,) * a.ndim)
    rows = pl.BlockSpec((L, NSA_KV_DIM), lambda b: (b, 0))
    tok = pl.BlockSpec((1, n_chunk, NSA_KV_DIM), lambda b: (b, 0, 0))
    return pl.pallas_call(
        _compress_body,
        grid=(B,),
        in_specs=[rows, rows, full(pk[0]), full(pv[0]), full(pk[1]), full(pv[1]), full(pk[2]), full(pv[2])],
        out_specs=[tok, tok],
        out_shape=[jax.ShapeDtypeStruct((B, n_chunk, NSA_KV_DIM), BF16)] * 2,
        compiler_params=pltpu.CompilerParams(dimension_semantics=("parallel",), vmem_limit_bytes=VMEM_LIMIT),
        name="compress_tokens",
    )(ck, cv, pk[0], pv[0], pk[1], pv[1], pk[2], pv[2])


def _t5_bucket_np(rel):
    n = np.maximum(rel, 0)
    max_exact = NUM_BUCKETS // 2
    nf = np.maximum(n, 1).astype(np.float32)
    large = max_exact + (np.log(nf / max_exact) / math.log(MAX_DISTANCE / max_exact)
                         * (NUM_BUCKETS - max_exact)).astype(np.int32)
    return np.where(n < max_exact, n, np.minimum(large, NUM_BUCKETS - 1))


assert (_t5_bucket_np(np.arange(ATT_TILE + 1, 1 << 16)) == NUM_BUCKETS - 1).all()


NB_SAME, NB_PREV, NB_FAR, NB_EDGE, NB_NONE = range(5)
FAR_TILE = 512


def bias_lookup(rel_bias, rel):
    bucket = t5_bucket(rel)
    tbl = rel_bias.astype(F32)
    shape = (tbl.shape[1],) + (1,) * rel.ndim
    out = jnp.zeros((tbl.shape[1],) + rel.shape, F32)
    for b in range(NUM_BUCKETS):
        out = jnp.where(bucket == b, tbl[b].reshape(shape), out)
    return out


def near_bias_tables(rel_bias):
    t = ATT_TILE
    i = jnp.arange(t)[:, None]
    j = jnp.arange(t)[None, :]
    diag = jnp.where(i >= j, bias_lookup(rel_bias, i - j), MASKED)
    prev = bias_lookup(rel_bias, t + i - j)
    far = jnp.broadcast_to(rel_bias.astype(F32)[NUM_BUCKETS - 1][:, None, None], prev.shape)
    edge = jnp.where(j > i, far, MASKED)
    return jnp.stack([diag, prev, far, edge, jnp.full_like(far, MASKED)])


def cmp_bias_tiles(rel_bias, n_q, tile, n_chunk):
    assert tile % CMP_STRIDE == 0 and n_q % tile == 0
    step = tile // CMP_STRIDE
    n_tile = n_q // tile
    lo = -(n_tile - 1) * step
    m = lo + jnp.arange(n_chunk - lo)[None, :]
    rel = jnp.arange(tile)[:, None] - CMP_STRIDE * m - (CMP_LEN - 1)
    template = jnp.where(rel >= 0, bias_lookup(rel_bias, rel), MASKED)
    return jnp.stack([template[:, :, -lo - i * step:-lo - i * step + n_chunk] for i in range(n_tile)], axis=0)


def overlap_matrix(n_chunk, n_sel):
    c0 = np.arange(n_chunk)[:, None] * CMP_STRIDE
    s0 = np.arange(LANES)[None, :] * SEL_BLOCK
    ov = np.clip(np.minimum(c0 + CMP_LEN, s0 + SEL_BLOCK) - np.maximum(c0, s0), 0, None) / CMP_LEN
    ov = np.where(np.arange(LANES)[None, :] < n_sel, ov, 0.0)
    return jnp.asarray(ov, F32)


def expand_matrix(n_keys):
    e = (np.arange(n_keys)[None, :] // SEL_BLOCK) == np.arange(LANES)[:, None]
    return jnp.asarray(e, BF16)


def _head_queries(q_ref, tq):
    low = lax.broadcasted_iota(jnp.int32, (tq, LANES), 1) < HEAD_DIM
    zero = jnp.zeros((tq, LANES), BF16)
    parts = []
    for kv in range(NSA_KV_HEADS):
        for g in range(NSA_GROUP):
            blk = q_ref[:, g * LANES:(g + 1) * LANES]
            parts.append(jnp.where(low if kv == 0 else ~low, blk, zero))
    return parts


def _stack_queries(q_ref, tq):
    return jnp.concatenate(_head_queries(q_ref, tq), axis=0)


def _dot_nt(a, b):
    return lax.dot_general(a, b, (((1,), (1,)), ((), ())), preferred_element_type=F32)


def _selection_mask(p_group, ov_ref, ex_ref, q0, n_sel, n_top):
    tq = p_group.shape[0]
    n_row = -(-n_sel // 8) * 8
    imp = lax.dot_general(ov_ref[...], p_group, (((0,), (1,)), ((), ())), precision=lax.Precision.HIGHEST,
                          preferred_element_type=F32)[0:n_row]
    blk = lax.broadcasted_iota(jnp.int32, (n_row, tq), 0)
    q_pos = q0 + lax.broadcasted_iota(jnp.int32, (n_row, tq), 1)
    q_blk = q_pos // SEL_BLOCK
    forced = jnp.where(blk == 0, 1.0, jnp.where(blk == q_blk, 1.0, jnp.where(blk == q_blk - 1, 1.0, 0.0)))
    score = jnp.where(blk * SEL_BLOCK <= q_pos, imp, NEG_INF)
    score = jnp.where(forced > 0.5, 1e9, score)
    score = jnp.where(blk < n_sel, score, -3e38)
    rank = jnp.zeros((n_row, tq), F32)
    for s in range(n_sel):
        other = score[s:s + 1, :]
        tie = jnp.where(blk > s, 1.0, 0.0)
        rank = rank + jnp.where(other > score, 1.0, jnp.where(other == score, tie, 0.0))
    sel = jnp.where(rank < n_top, 1.0, 0.0).astype(BF16)
    keys = lax.dot_general(sel, ex_ref[0:n_row, :], (((0,), (0,)), ((), ())), preferred_element_type=F32)
    return (keys - 1.0) * (-MASKED)


def _gate_columns(gl_ref, j):
    g = jax.nn.sigmoid(gl_ref[...])
    cols = [g[:, SM_GATE + 3 * h + j:SM_GATE + 3 * h + j + 1] for h in range(NSA_HEADS)]
    return jnp.stack(cols, axis=0)


def _nsa_prompt_body(q_ref, kc_ref, vc_ref, sk_ref, sv_ref, wk_ref, wv_ref, bc_ref, nb_ref, gl_ref, ov_ref, ex_ref,
                     o_ref, mb_ref, s_ref, sw_ref, stat_ref, mrep_ref, acc_ref, out_ref, *, n_sel, n_top):
    i = pl.program_id(1)
    tq = q_ref.shape[0]
    h = NSA_HEADS
    q = _stack_queries(q_ref, tq)

    s = _dot_nt(q, kc_ref[0]).reshape(h, tq, -1) + bc_ref[0]
    m = jnp.max(s, axis=-1, keepdims=True)
    p = jnp.where(s > 0.5 * MASKED, jnp.exp(s - m), 0.0)
    l = jnp.sum(p, axis=-1, keepdims=True)
    p = p * (1.0 / jnp.where(l > 0.0, l, 1.0))
    o_cmp = jnp.dot(p.reshape(h * tq, -1).astype(BF16), vc_ref[0], preferred_element_type=F32)
    out_ref[...] = _gate_columns(gl_ref, 0) * o_cmp.reshape(h, tq, LANES)

    for kv in range(NSA_KV_HEADS):
        pk = p[kv * NSA_GROUP]
        for g in range(1, NSA_GROUP):
            pk = pk + p[kv * NSA_GROUP + g]
        mb_ref[kv] = _selection_mask(pk, ov_ref, ex_ref, i * tq, n_sel, n_top)

    tk = ATT_TILE
    n_back = WINDOW // tk
    wide = (n_back + 1) * tk
    far_blk = FAR_TILE // tk
    n_tile = sk_ref.shape[0] // tk
    tail_col = s_ref.shape[-1] - wide
    kvg = lambda x: x.reshape(NSA_KV_HEADS, NSA_GROUP, tq, x.shape[-1])
    blocks = lambda x: [x[..., c * LANES:(c + 1) * LANES] for c in range(x.shape[-1] // LANES)]
    lane_max = lambda x: functools.reduce(jnp.maximum, blocks(x))
    lane_sum = lambda x: functools.reduce(jnp.add, blocks(x))
    rep = lambda x, n: jnp.concatenate([x] * n, axis=-1)
    far_bias = nb_ref[NB_FAR][:, 0:1, 0:1]

    n_far = jnp.maximum(i - 1, 0) // far_blk
    tail0 = jnp.minimum(n_far * far_blk, n_tile - (n_back + 1))
    tail_rows = pl.ds(pl.multiple_of(tail0 * tk, tk), wide)
    tiles = []
    for c in range(n_back + 1):
        blk = tail0 + c
        kind = jnp.where((blk < n_far * far_blk) | (blk > i), NB_NONE, jnp.minimum(i - blk, NB_FAR))
        tiles.append(nb_ref[kind])
    s = _dot_nt(q, sk_ref[tail_rows, :]).reshape(h, tq, wide) + jnp.concatenate(tiles, axis=-1)
    s = (kvg(s) + mb_ref[:, :, tail_rows][:, None]).reshape(h, tq, wide)
    s_ref[:, :, tail_col:tail_col + wide] = s
    stat_ref[0] = lane_max(s)

    win0 = jnp.maximum(i - n_back, 0)
    win_rows = pl.ds(pl.multiple_of(win0 * tk, tk), wide)
    tiles = []
    for c in range(n_back + 1):
        r = i - (win0 + c)
        tiles.append(nb_ref[jnp.where(r < 0, NB_NONE, jnp.where(r == n_back, NB_EDGE, jnp.minimum(r, NB_FAR)))])
    s = _dot_nt(q, wk_ref[win_rows, :]).reshape(h, tq, wide) + jnp.concatenate(tiles, axis=-1)
    sw_ref[...] = s
    stat_ref[1] = lane_max(s)

    def far_scores(f, carry):
        rows = pl.ds(pl.multiple_of(f * FAR_TILE, FAR_TILE), FAR_TILE)
        s = _dot_nt(q, sk_ref[rows, :]).reshape(h, tq, FAR_TILE) + far_bias
        s = (kvg(s) + mb_ref[:, :, rows][:, None]).reshape(h, tq, FAR_TILE)
        s_ref[:, :, rows] = s
        stat_ref[0] = jnp.maximum(stat_ref[0], lane_max(s))
        return carry
    lax.fori_loop(0, n_far, far_scores, 0)

    for br in range(2):
        mrep_ref[br] = jnp.broadcast_to(jnp.max(stat_ref[br], axis=-1, keepdims=True), (h, tq, LANES))
        stat_ref[br] = jnp.zeros((h, tq, LANES), F32)
    acc_ref[...] = jnp.zeros(acc_ref.shape, F32)

    def far_values(f, carry):
        rows = pl.ds(pl.multiple_of(f * FAR_TILE, FAR_TILE), FAR_TILE)
        p = jnp.exp(s_ref[:, :, rows] - rep(mrep_ref[0], far_blk))
        stat_ref[0] += lane_sum(p)
        acc_ref[...] += jnp.dot(p.reshape(h * tq, FAR_TILE).astype(BF16), sv_ref[rows, :],
                                preferred_element_type=F32).reshape(h, tq, LANES)
        return carry
    lax.fori_loop(0, n_far, far_values, 0)

    p = jnp.exp(s_ref[:, :, tail_col:tail_col + wide] - rep(mrep_ref[0], n_back + 1))
    l = jnp.sum(stat_ref[0] + lane_sum(p), axis=-1, keepdims=True)
    o_sel = acc_ref[...] + jnp.dot(p.reshape(h * tq, wide).astype(BF16), sv_ref[tail_rows, :],
                                   preferred_element_type=F32).reshape(h, tq, LANES)
    out_ref[...] += (_gate_columns(gl_ref, 1) * (1.0 / l)) * o_sel

    p = jnp.exp(sw_ref[...] - rep(mrep_ref[1], n_back + 1))
    l = jnp.sum(lane_sum(p), axis=-1, keepdims=True)
    o_win = jnp.dot(p.reshape(h * tq, wide).astype(BF16), wv_ref[win_rows, :],
                    preferred_element_type=F32).reshape(h, tq, LANES)
    out_ref[...] += (_gate_columns(gl_ref, 2) * (1.0 / l)) * o_win

    low = lax.broadcasted_iota(jnp.int32, (tq, LANES), 1) < HEAD_DIM
    for g in range(NSA_GROUP):
        o_ref[:, g * LANES:(g + 1) * LANES] = jnp.where(low, out_ref[g], out_ref[NSA_GROUP + g]).astype(BF16)


def nsa_prompt(q, kc, vc, skb, svb, wkb, wvb, small, rel_bias, B, T):
    tq = ATT_TILE
    nq = T // tq
    assert T % FAR_TILE == 0 and T >= WINDOW + ATT_TILE and WINDOW % ATT_TILE == 0
    wide = WINDOW + ATT_TILE
    far_max = (nq - 2) // (FAR_TILE // ATT_TILE) * FAR_TILE
    n_chunk = kc.shape[1]
    n_sel = -(-T // SEL_BLOCK)
    n_top = min(SEL_TOP, n_sel)
    bias_c = cmp_bias_tiles(rel_bias, T, tq, n_chunk)
    nb = near_bias_tables(rel_bias)
    ov = overlap_matrix(n_chunk, n_sel)
    ex = expand_matrix(T)
    h = NSA_HEADS
    full = lambda a: pl.BlockSpec(a.shape, lambda b, i: (0,) * a.ndim)
    seq = pl.BlockSpec((T, NSA_KV_DIM), lambda b, i: (b, 0))
    tok = pl.BlockSpec((1, n_chunk, NSA_KV_DIM), lambda b, i: (b, 0, 0))
    return pl.pallas_call(
        functools.partial(_nsa_prompt_body, n_sel=n_sel, n_top=n_top),
        grid=(B, nq),
        in_specs=[pl.BlockSpec((tq, NSA_Q_DIM), lambda b, i: (b * nq + i, 0)),
                  tok, tok, seq, seq, seq, seq,
                  pl.BlockSpec((1, h, tq, n_chunk), lambda b, i: (i, 0, 0, 0)),
                  full(nb),
                  pl.BlockSpec((tq, LANES), lambda b, i: (b * nq + i, 0)),
                  full(ov), full(ex)],
        out_specs=pl.BlockSpec((tq, NSA_Q_DIM), lambda b, i: (b * nq + i, 0)),
        out_shape=jax.ShapeDtypeStruct((B * T, NSA_Q_DIM), BF16),
        scratch_shapes=[pltpu.VMEM((NSA_KV_HEADS, tq, T), F32),
                        pltpu.VMEM((h, tq, far_max + wide), F32), pltpu.VMEM((h, tq, wide), F32),
                        pltpu.VMEM((2, h, tq, LANES), F32), pltpu.VMEM((2, h, tq, LANES), F32),
                        pltpu.VMEM((h, tq, LANES), F32), pltpu.VMEM((h, tq, LANES), F32)],
        compiler_params=pltpu.CompilerParams(dimension_semantics=("parallel", "arbitrary"),
                                             vmem_limit_bytes=VMEM_LIMIT),
        name="nsa_prompt",
    )(q, kc, vc, skb, svb, wkb, wvb, bias_c, nb, small, ov, ex)


def sample_bias_tables(rel_bias, past, T, W):
    t = jnp.arange(T)[:, None]
    pad = jnp.arange(ATT_TILE)[None, :]
    new = jnp.where((pad < T) & (t >= pad), bias_lookup(rel_bias, t - pad), MASKED)
    far = rel_bias.astype(F32)[NUM_BUCKETS - 1][:, None, None]

    def past_rows(n):
        near = bias_lookup(rel_bias, ATT_TILE + t - pad)
        return jnp.concatenate([jnp.broadcast_to(far, (far.shape[0], T, n - ATT_TILE)), near], axis=-1)

    sel = past_rows(past)
    rel_w = W + t - jnp.arange(W)[None, :]
    win = jnp.where(rel_w < WINDOW, past_rows(W), MASKED)
    ends = jnp.arange(past // CMP_STRIDE)[None, :] * CMP_STRIDE + CMP_LEN - 1
    rel_c = past + t - ends
    cmp = jnp.where(rel_c >= 0, bias_lookup(rel_bias, rel_c), MASKED)
    return cmp, jnp.concatenate([sel, new], axis=-1), jnp.concatenate([win, new], axis=-1)


def _softmax_rows(s):
    m = jnp.max(s, axis=-1, keepdims=True)
    p = jnp.exp(s - m)
    return p * (1.0 / jnp.sum(p, axis=-1, keepdims=True))


def _nsa_sample_body(pt_ref, q_ref, skn_ref, svn_ref, wkn_ref, wvn_ref, gl_ref, wink_ref, winv_ref,
                     bc_ref, bs_ref, bw_ref, ov_ref, ex_ref, pk_ref, pv_ref, w1k_ref, w1v_ref, w2k_ref, w2v_ref,
                     ck_pool, cv_pool, sk_pool, sv_pool, o_ref, wko_ref, wvo_ref,
                     s_ref, new_ref, rows_ref, page_ref, sem_ref, *, n_pages, n_sel, n_top, past):
    h = NSA_HEADS
    tq = q_ref.shape[0]
    tk = ATT_TILE
    b = pl.program_id(0)
    pools = (ck_pool, cv_pool, sk_pool, sv_pool)

    def page_copy(seq, slot, pool, pg):
        return pltpu.make_async_copy(pools[pool].at[pt_ref[seq, pg]], page_ref.at[slot, pool, pg], sem_ref.at[slot])

    def start_fetch(seq, slot):
        for pool in range(len(pools)):
            for pg in range(n_pages):
                page_copy(seq, slot, pool, pg).start()

    slot = b % 2

    @pl.when(b == 0)
    def _():
        start_fetch(0, 0)

    @pl.when(b + 1 < pl.num_programs(0))
    def _():
        start_fetch(b + 1, 1 - slot)

    for pool in range(len(pools)):
        for pg in range(n_pages):
            page_copy(b, slot, pool, pg).wait()
    pages = [[page_ref.at[slot, pool, pg] for pg in range(n_pages)] for pool in range(len(pools))]

    q = _stack_queries(q_ref, tq)

    new_ref[...] = jnp.zeros(new_ref.shape, F32)
    for n, r in enumerate((skn_ref, svn_ref, wkn_ref, wvn_ref)):
        new_ref[n, 0:tq, :] = r[...]

    for pg in range(n_pages):
        rows_ref[0, pg * tk:(pg + 1) * tk, :] = pages[0][pg][...].T
        rows_ref[1, pg * tk:(pg + 1) * tk, :] = pages[1][pg][...].T
    kc = _compress_one(rows_ref.at[0], pk_ref, w1k_ref, w2k_ref).astype(BF16)
    vc = _compress_one(rows_ref.at[1], pv_ref, w1v_ref, w2v_ref).astype(BF16)
    s = _dot_nt(q, kc).reshape(h, tq, -1) + bc_ref[...]
    p = _softmax_rows(s)
    o_cmp = jnp.dot(p.reshape(h * tq, -1).astype(BF16), vc, preferred_element_type=F32)
    out = _gate_columns(gl_ref, 0) * o_cmp.reshape(h, tq, LANES)

    masks = []
    for kv in range(NSA_KV_HEADS):
        pkv = p[kv * NSA_GROUP]
        for g in range(1, NSA_GROUP):
            pkv = pkv + p[kv * NSA_GROUP + g]
        masks.append(_selection_mask(pkv, ov_ref, ex_ref, past, n_sel, n_top))
    mask = jnp.stack(masks, axis=0)
    for pg in range(n_pages):
        s_ref[:, pg * tk:(pg + 1) * tk] = jnp.dot(q, pages[2][pg][...].astype(BF16), preferred_element_type=F32)
    s_ref[:, n_pages * tk:(n_pages + 1) * tk] = _dot_nt(q, new_ref[0].astype(BF16))
    width = (n_pages + 1) * tk
    s = s_ref[...].reshape(h, tq, width) + bs_ref[...]
    s = (s.reshape(NSA_KV_HEADS, NSA_GROUP, tq, width) + mask[:, None]).reshape(h, tq, width)
    p = _softmax_rows(s).reshape(h * tq, width).astype(BF16)
    o_sel = jnp.dot(p[:, n_pages * tk:], new_ref[1].astype(BF16), preferred_element_type=F32)
    for pg in range(n_pages):
        o_sel = o_sel + _dot_nt(p[:, pg * tk:(pg + 1) * tk], pages[3][pg][...].astype(BF16))
    out = out + _gate_columns(gl_ref, 1) * o_sel.reshape(h, tq, LANES)

    w = wink_ref.shape[2]
    s_ref[:, 0:w] = jnp.dot(q, wink_ref[0].astype(BF16), preferred_element_type=F32)
    s_ref[:, w:w + tk] = _dot_nt(q, new_ref[2].astype(BF16))
    s = s_ref[:, 0:w + tk].reshape(h, tq, w + tk) + bw_ref[...]
    p = _softmax_rows(s).reshape(h * tq, w + tk).astype(BF16)
    o_win = jnp.dot(p[:, w:], new_ref[3].astype(BF16), preferred_element_type=F32) \
        + _dot_nt(p[:, 0:w], winv_ref[0].astype(BF16))
    out = out + _gate_columns(gl_ref, 2) * o_win.reshape(h, tq, LANES)

    low = lax.broadcasted_iota(jnp.int32, (tq, LANES), 1) < HEAD_DIM
    for g in range(NSA_GROUP):
        o_ref[:, g * LANES:(g + 1) * LANES] = jnp.where(low, out[g], out[NSA_GROUP + g])

    keep = w - tq
    lane_w = lax.broadcasted_iota(jnp.int32, (NSA_KV_DIM, w), 1)
    for old_ref, new_rows_ref, dst_ref in ((wink_ref, wkn_ref, wko_ref), (winv_ref, wvn_ref, wvo_ref)):
        shifted = pltpu.roll(old_ref[0], shift=keep, axis=1)
        tail = jnp.concatenate([jnp.zeros((tk - tq, NSA_KV_DIM), F32), new_rows_ref[...]], axis=0).T
        tail = jnp.concatenate([jnp.zeros((NSA_KV_DIM, w - tk), F32), tail], axis=1)
        dst_ref[0] = jnp.where(lane_w >= keep, tail, shifted)


def nsa_sample(q, skn, svn, wkn, wvn, small, pools, win_k, win_v, page_table, rel_bias, pk, pv, B, T):
    n_pages = page_table.shape[1]
    past = n_pages * PAGE_SIZE
    W = win_k.shape[1]
    L = past + T
    n_cmp = (L - CMP_LEN) // CMP_STRIDE + 1
    assert PAGE_SIZE == ATT_TILE and T % 8 == 0 and T <= SEL_BLOCK and past % SEL_BLOCK == 0
    assert (n_cmp - 1) * CMP_STRIDE + CMP_LEN <= past and n_cmp >= past // CMP_STRIDE - CMP_HALVES + 1
    assert W % ATT_TILE == 0 and W + T >= WINDOW
    n_sel = -(-L // SEL_BLOCK)
    n_top = min(SEL_TOP, n_sel)
    w_keep = min(WINDOW, W + T)
    bc, bs, bw = sample_bias_tables(rel_bias, past, T, W)
    ov = overlap_matrix(past // CMP_STRIDE, n_sel)
    ex = expand_matrix(past + ATT_TILE)
    n_pool = pools[0].shape[0]
    assert W == w_keep
    feat_major = lambda a: a.transpose(0, 2, 3, 1).reshape(a.shape[0], NSA_KV_DIM, a.shape[1])
    cmp_pools = [feat_major(p) for p in pools[:2]]
    sel_pools = [feat_major(p) for p in pools[2:]]
    wk3 = feat_major(win_k)
    wv3 = feat_major(win_v)

    full = lambda a: pl.BlockSpec(a.shape, lambda b, pt: (0,) * a.ndim)
    row = lambda w: pl.BlockSpec((T, w), lambda b, pt: (b, 0))
    seq = lambda n: pl.BlockSpec((1, NSA_KV_DIM, n), lambda b, pt: (b, 0, 0))
    pool_args = cmp_pools + sel_pools
    pool_specs = [pl.BlockSpec(memory_space=pl.ANY)] * len(pool_args)
    tables = (bc, bs, bw, ov, ex, pk[0], pv[0], pk[1], pv[1], pk[2], pv[2])
    grid_spec = pltpu.PrefetchScalarGridSpec(
        num_scalar_prefetch=1,
        grid=(B,),
        in_specs=[row(NSA_Q_DIM), row(NSA_KV_DIM), row(NSA_KV_DIM), row(NSA_KV_DIM), row(NSA_KV_DIM), row(LANES),
                  seq(W), seq(W)] + [full(a) for a in tables] + pool_specs,
        out_specs=[row(NSA_Q_DIM), seq(w_keep), seq(w_keep)],
        scratch_shapes=[pltpu.VMEM((NSA_HEADS * T, past + ATT_TILE), F32),
                        pltpu.VMEM((4, ATT_TILE, NSA_KV_DIM), F32),
                        pltpu.VMEM((2, past, NSA_KV_DIM), F32),
                        pltpu.VMEM((2, len(pool_args), n_pages, NSA_KV_DIM, PAGE_SIZE), F32),
                        pltpu.SemaphoreType.DMA((2,))],
    )
    o, wk_next, wv_next = pl.pallas_call(
        functools.partial(_nsa_sample_body, n_pages=n_pages, n_sel=n_sel, n_top=n_top, past=past),
        grid_spec=grid_spec,
        out_shape=[jax.ShapeDtypeStruct((B * T, NSA_Q_DIM), F32),
                   jax.ShapeDtypeStruct((B, NSA_KV_DIM, w_keep), F32),
                   jax.ShapeDtypeStruct((B, NSA_KV_DIM, w_keep), F32)],
        compiler_params=pltpu.CompilerParams(dimension_semantics=("arbitrary",), vmem_limit_bytes=VMEM_LIMIT),
        name="nsa_sample",
    )(page_table, q, skn, svn, wkn, wvn, small, wk3, wv3, *tables, *pool_args)
    row_major = lambda a: a.reshape(B, NSA_KV_HEADS, HEAD_DIM, w_keep).transpose(0, 3, 1, 2)
    return o, row_major(wk_next), row_major(wv_next)


HIST_ROWS = 8
DN_CHUNKS_PER_STEP = 4
DN_SEQS_PER_STEP = 8
DN_INV_BASE = 8


def _split_bf16(x):
    hi = x.astype(BF16)
    lo = (x - hi.astype(F32)).astype(BF16)
    return hi, lo


def _bmm(a, b):
    return jnp.einsum('hij,hjk->hik', a, b, preferred_element_type=F32)


def _bmm_nt(a, b):
    return jnp.einsum('hid,hjd->hij', a, b, preferred_element_type=F32)


def _cumsum_rows(tri, x):
    hi, lo = _split_bf16(x)
    lo2 = (x - hi.astype(F32) - lo.astype(F32)).astype(BF16)
    d = lambda y: jnp.dot(tri, y, preferred_element_type=F32)
    return d(hi) + (d(lo) + d(lo2))


def _softplus(x):
    return jnp.maximum(x, 0.0) + jnp.log1p(jnp.exp(-jnp.abs(x)))


def _silu(x):
    return x * jax.nn.sigmoid(x)


def _deltanet_body(x_ref, sm_ref, dg_ref, hist_ref, s0_ref, cw_ref, apar_ref, dt_ref, ng_ref,
                   o_ref, s_out_ref, hist_out_ref, xbuf_ref, s_ref, *, c, n_seq):
    j = pl.program_id(1)
    rows = x_ref.shape[0]
    seq_rows = rows // n_seq
    n_grp = seq_rows // c
    hd = DN_HEAD_DIM
    n_hist = CONV_W - 1

    n_pair = DN_HEADS // 2
    zero_blk = jnp.zeros((hd, hd), F32)

    @pl.when(j == 0)
    def _():
        xbuf_ref[:, HIST_ROWS - n_hist:HIST_ROWS, :] = hist_ref[...]
        for sq in range(n_seq):
            for p in range(n_pair):
                s_ref[sq * n_pair + p] = jnp.concatenate(
                    [jnp.concatenate([s0_ref[sq, 2 * p], zero_blk], axis=1),
                     jnp.concatenate([zero_blk, s0_ref[sq, 2 * p + 1]], axis=1)], axis=0)

    xbuf_ref[:, HIST_ROWS:HIST_ROWS + seq_rows, :] = x_ref[...].reshape(n_seq, seq_rows, x_ref.shape[1])
    y = None
    for t in range(CONV_W):
        lo = HIST_ROWS - n_hist + t
        term = xbuf_ref[:, lo:lo + seq_rows, :] * cw_ref[t:t + 1, :]
        y = term if y is None else y + term
    y = _silu(y).reshape(rows, x_ref.shape[1])
    tail = xbuf_ref[:, HIST_ROWS + seq_rows - n_hist:HIST_ROWS + seq_rows, :]
    xbuf_ref[:, HIST_ROWS - n_hist:HIST_ROWS, :] = tail

    sm = sm_ref[...]
    beta_all = jax.nn.sigmoid(sm)
    g_all = apar_ref[...] * _softplus(sm + dt_ref[...])
    row = lax.broadcasted_iota(jnp.int32, (rows, rows), 0)
    col = lax.broadcasted_iota(jnp.int32, (rows, rows), 1)
    tri = jnp.where((row >= col) & (row // c == col // c), 1.0, 0.0).astype(BF16)
    gcum_all = _cumsum_rows(tri, g_all)
    gcum_t = gcum_all.T

    wd, wj = 2 * hd, 2 * c
    first_d = lax.broadcasted_iota(jnp.int32, (1, wd), 1) < hd
    first_j = lax.broadcasted_iota(jnp.int32, (1, wj), 1) < c
    rj = lax.broadcasted_iota(jnp.int32, (c, wj), 0)
    cj = lax.broadcasted_iota(jnp.int32, (c, wj), 1)
    cj = jnp.where(cj < c, cj, cj - c)
    incl, strict = rj >= cj, rj > cj
    eye = jnp.where(rj == cj, 1.0, 0.0)
    same_head = (lax.broadcasted_iota(jnp.int32, (wd, wd), 0) < hd) == (lax.broadcasted_iota(jnp.int32, (wd, wd), 1) < hd)
    ones_bd = jnp.where(same_head, 1.0, 0.0).astype(BF16)
    first_rows = lax.broadcasted_iota(jnp.int32, (wd, 1), 0) < hd

    items = [(g, p) for g in range(n_seq * n_grp) for p in range(n_pair)]

    def block_diag(x, first):
        zero = jnp.zeros_like(x)
        return jnp.concatenate([jnp.where(first, x, zero), jnp.where(first, zero, x)], axis=1)

    def head_sums(x):
        n = x.shape[0]
        hi, lo = _split_bf16(x.reshape(n * c, wd))
        tot = jnp.dot(hi, ones_bd, preferred_element_type=F32) + jnp.dot(lo, ones_bd, preferred_element_type=F32)
        return tot.reshape(n, c, wd)

    def mm_hi(a_parts, b_parts):
        return _bmm(a_parts[0], b_parts[0]) + (_bmm(a_parts[0], b_parts[1]) + _bmm(a_parts[1], b_parts[0]))

    def per_head(cols, base, first):
        return jnp.stack([jnp.where(first, cols[g * c:(g + 1) * c, base + 2 * p:base + 2 * p + 1],
                                    cols[g * c:(g + 1) * c, base + 2 * p + 1:base + 2 * p + 2])
                          for g, p in items], axis=0)

    lanes = lambda x, off: jnp.stack([x[g * c:(g + 1) * c, off + p * wd:off + (p + 1) * wd] for g, p in items], axis=0)
    q = lanes(y, 0)
    k = lanes(y, DN_DIM)
    v = lanes(y, 2 * DN_DIM)
    q = q * lax.rsqrt(head_sums(q * q) + EPS) * hd ** -0.5
    k = k * lax.rsqrt(head_sums(k * k) + EPS)
    beta = per_head(beta_all, SM_BETA, first_d)
    gc_d = per_head(gcum_all, SM_DECAY, first_d)
    gc_j = gc_d if wj == wd else per_head(gcum_all, SM_DECAY, first_j)
    gr_j = jnp.stack([jnp.concatenate([gcum_t[SM_DECAY + 2 * p:SM_DECAY + 2 * p + 1, g * c:(g + 1) * c],
                                       gcum_t[SM_DECAY + 2 * p + 1:SM_DECAY + 2 * p + 2, g * c:(g + 1) * c]], axis=1)
                      for g, p in items], axis=0)
    gl_d = gc_d[:, c - 1:c, :]
    decay = jnp.where(incl, jnp.exp(jnp.where(incl, gc_j - gr_j, 0.0)), 0.0)
    eg = jnp.exp(gc_d)
    kb = k * beta
    k_bd = block_diag(k.astype(BF16), first_d)
    a = jnp.where(strict, _bmm_nt(kb.astype(BF16), k_bd) * decay, 0.0)
    split_bd = lambda parts, first: tuple(block_diag(x, first) for x in parts)
    base = min(c, DN_INV_BASE)
    same_block = lambda b: (rj // b) == (cj // b)
    diag = jnp.where(same_block(base), a, 0.0)
    t_inv = eye - diag
    pow_parts = _split_bf16(diag)
    pow_bd = split_bd(pow_parts, first_j)
    for _ in range(int(math.log2(base)) - 1):
        pow_parts = _split_bf16(mm_hi(pow_parts, pow_bd))
        pow_bd = split_bd(pow_parts, first_j)
        t_inv = t_inv + mm_hi(_split_bf16(t_inv), pow_bd)
    blk = base
    while blk < c:
        lower = jnp.where(same_block(2 * blk) & jnp.logical_not(same_block(blk)), a, 0.0)
        t_parts = _split_bf16(t_inv)
        lt = mm_hi(_split_bf16(lower), split_bd(t_parts, first_j))
        t_inv = t_inv - mm_hi(t_parts, split_bd(_split_bf16(lt), first_j))
        blk *= 2
    vb_parts = split_bd(_split_bf16(v * beta), first_d)
    kbg_parts = split_bd(_split_bf16(kb * eg), first_d)
    rhs = tuple(jnp.concatenate([x, z], axis=2) for x, z in zip(vb_parts, kbg_parts))
    uw = mm_hi(_split_bf16(t_inv), rhs)
    u, w = uw[:, :, :wd], uw[:, :, wd:]
    qk = jnp.where(incl, _bmm_nt(q.astype(BF16), k_bd) * decay, 0.0).astype(BF16)
    wq = jnp.concatenate([w, q * eg], axis=1).astype(BF16)
    kg = (k * jnp.exp(gl_d - gc_d)).astype(BF16)
    gl_rows = jnp.where(first_rows, jnp.exp(gl_d[:, :, 0:1]), jnp.exp(gl_d[:, :, hd:hd + 1]))

    s = s_ref[...]
    outs = {}
    for kc in range(n_grp):
        pick = lambda x: jnp.concatenate([x[(sq * n_grp + kc) * n_pair:(sq * n_grp + kc + 1) * n_pair]
                                          for sq in range(n_seq)], axis=0)
        ws = _bmm(pick(wq), s.astype(BF16))
        vn = (pick(u) - ws[:, :c]).astype(BF16)
        o_k = ws[:, c:] + _bmm(pick(qk), block_diag(vn, first_d))
        kv_outer = jnp.einsum('pck,pcv->pkv', pick(kg), vn, preferred_element_type=F32)
        s = s * pick(gl_rows) + jnp.where(same_head, kv_outer, 0.0)
        for sq in range(n_seq):
            outs[sq * n_grp + kc] = o_k[sq * n_pair:(sq + 1) * n_pair]
    s_ref[...] = s
    o = jnp.concatenate([outs[g] for g in range(n_seq * n_grp)], axis=0)
    ng = jnp.concatenate([ng_ref[...], ng_ref[...]], axis=1)
    o = o * lax.rsqrt(head_sums(o * o) * (1.0 / hd) + EPS) * ng
    o = jnp.concatenate([jnp.concatenate([o[g * n_pair + p] for p in range(n_pair)], axis=1)
                         for g in range(n_seq * n_grp)], axis=0)
    o_ref[...] = (o * _silu(dg_ref[...])).astype(o_ref.dtype)

    @pl.when(j == pl.num_programs(1) - 1)
    def _():
        for sq in range(n_seq):
            for p in range(n_pair):
                s_pair = s_ref[sq * n_pair + p]
                s_out_ref[sq, 2 * p] = s_pair[0:hd, 0:hd]
                s_out_ref[sq, 2 * p + 1] = s_pair[hd:2 * hd, hd:2 * hd]
        hist_out_ref[...] = tail


def deltanet(qkv_raw, small, dg, conv_hist, s0, lp, B, T):
    c = DN_CHUNK if T >= DN_CHUNK else T
    assert T % c == 0 and c % 8 == 0 and c >= CONV_W - 1 and (c & (c - 1)) == 0
    n_grp = math.gcd(T // c, DN_CHUNKS_PER_STEP)
    n_seq = math.gcd(B, DN_SEQS_PER_STEP) if T == c else 1
    seq_rows = n_grp * c
    step = n_seq * seq_rows
    n = T // seq_rows
    put = lambda vals, base: jnp.zeros((LANES,), F32).at[base + jnp.arange(DN_HEADS)].set(vals).reshape(1, LANES)
    apar = put(-jnp.exp(lp['dn_a_log'].astype(F32)), SM_DECAY)
    dtb = put(lp['dn_dt_bias'].astype(F32), SM_DECAY)
    assert n_seq == 1 or n == 1
    rows = lambda w: pl.BlockSpec((step, w), lambda b, j: (b * n + j, 0))
    per_b = lambda a: pl.BlockSpec((n_seq,) + a.shape[1:], lambda b, j: (b,) + (0,) * (a.ndim - 1))
    full = lambda a: pl.BlockSpec(a.shape, lambda b, j: (0,) * a.ndim)
    ng = lp['dn_norm_g'].reshape(1, DN_HEAD_DIM).astype(F32)
    return pl.pallas_call(
        functools.partial(_deltanet_body, c=c, n_seq=n_seq),
        grid=(B // n_seq, n),
        in_specs=[rows(DN_CONV_DIM), rows(LANES), rows(DN_DIM), per_b(conv_hist), per_b(s0),
                  full(lp['conv_w']), full(apar), full(dtb), full(ng)],
        out_specs=[rows(DN_DIM), per_b(s0), per_b(conv_hist)],
        out_shape=[jax.ShapeDtypeStruct((B * T, DN_DIM), BF16 if step % 16 == 0 else F32),
                   jax.ShapeDtypeStruct(s0.shape, F32),
                   jax.ShapeDtypeStruct(conv_hist.shape, F32)],
        scratch_shapes=[pltpu.VMEM((n_seq, HIST_ROWS + seq_rows, DN_CONV_DIM), F32),
                        pltpu.VMEM((n_seq * (DN_HEADS // 2), 2 * DN_HEAD_DIM, 2 * DN_HEAD_DIM), F32)],
        compiler_params=pltpu.CompilerParams(dimension_semantics=("parallel", "arbitrary"),
                                             vmem_limit_bytes=VMEM_LIMIT),
        name="deltanet",
    )(qkv_raw, small, dg, conv_hist, s0, lp['conv_w'], apar, dtb, ng)


MOE_EXPERTS_PER_STEP = 4
RT_GROUP = 0
RT_EXPERT = N_GROUPS


def pack_router(wg, bg, we, be):
    w = jnp.concatenate([wg, we], axis=1)
    b = jnp.concatenate([bg, be], axis=0)
    pad = LANES - w.shape[1]
    return jnp.pad(w, ((0, 0), (0, pad))).astype(BF16), jnp.pad(b, (0, pad)).reshape(1, LANES).astype(F32)


def _route(r):
    lane = lax.broadcasted_iota(jnp.int32, r.shape, 1).astype(F32)
    big = float(LANES)
    is_grp = lane < N_GROUPS
    lg = jnp.where(is_grp, r, -jnp.inf)
    mg = jnp.max(lg, axis=-1, keepdims=True)
    eg = jnp.where(is_grp, jnp.exp(lg - mg), 0.0)
    grp = jnp.min(jnp.where(lg == mg, lane, big), axis=-1, keepdims=True)
    wg = 1.0 / jnp.sum(eg, axis=-1, keepdims=True)
    first = RT_EXPERT + grp * EXPERTS_PER_GROUP
    in_grp = (lane >= first) & (lane < first + EXPERTS_PER_GROUP)
    le = jnp.where(in_grp, r, -jnp.inf)
    me = jnp.max(le, axis=-1, keepdims=True)
    ee = jnp.where(in_grp, jnp.exp(le - me), 0.0)
    pe = jnp.where(in_grp, ee / jnp.sum(ee, axis=-1, keepdims=True), -1.0)
    v1 = jnp.max(pe, axis=-1, keepdims=True)
    i1 = jnp.min(jnp.where(pe == v1, lane, big), axis=-1, keepdims=True)
    pe2 = jnp.where(lane == i1, -1.0, pe)
    v2 = jnp.max(pe2, axis=-1, keepdims=True)
    i2 = jnp.min(jnp.where(pe2 == v2, lane, big), axis=-1, keepdims=True)
    tot = v1 + v2
    return jnp.where(lane == i1, wg * (v1 / tot), jnp.where(lane == i2, wg * (v2 / tot), 0.0))


def _finish_body(x_ref, on_ref, od_ref, g1_ref, sc_ref, sh_ref, g2_ref, won_ref, wod_ref, ng_ref, wr_ref, br_ref,
                 wgu_ref, wd_ref, fg_ref, o_ref, x1_ref, h_ref, gate_ref, y_ref):
    e = pl.program_id(1)
    tm, d = x_ref.shape
    nb = g1_ref.shape[0]
    per_batch = lambda a: a.reshape(nb, tm // nb, d)

    @pl.when(e == 0)
    def _():
        mix = jnp.dot(on_ref[...].astype(BF16), won_ref[...], preferred_element_type=F32) \
            + jnp.dot(od_ref[...].astype(BF16), wod_ref[...], preferred_element_type=F32)
        x1 = per_batch(x_ref[...]) + g1_ref[...] * per_batch(mix)
        x1 = x1.reshape(tm, d)
        x1_ref[...] = x1
        xn = x1 * lax.rsqrt(jnp.mean(x1 * x1, axis=-1, keepdims=True) + EPS) * ng_ref[...]
        h = (per_batch(xn) * (1 + sc_ref[...]) + sh_ref[...]).reshape(tm, d).astype(BF16)
        h_ref[...] = h
        gate_ref[...] = _route(jnp.dot(h, wr_ref[...], preferred_element_type=F32) + br_ref[...])
        y_ref[...] = jnp.zeros(y_ref.shape, F32)

    h = h_ref[...]
    lane = lax.broadcasted_iota(jnp.int32, gate_ref.shape, 1)
    y = y_ref[...]
    for k in range(wgu_ref.shape[0]):
        gu = jnp.dot(h, wgu_ref[k], preferred_element_type=F32)
        he = _silu(gu[:, :D_EXPERT]) * gu[:, D_EXPERT:]
        out = jnp.dot(he.astype(BF16), wd_ref[k], preferred_element_type=F32)
        expert_lane = RT_EXPERT + e * wgu_ref.shape[0] + k
        g_col = jnp.sum(jnp.where(lane == expert_lane, gate_ref[...], 0.0), axis=-1, keepdims=True)
        y = y + g_col * out
    y_ref[...] = y

    @pl.when(e == pl.num_programs(1) - 1)
    def _():
        x2 = per_batch(x1_ref[...]) + g2_ref[...] * per_batch(y_ref[...])
        x2 = x2.reshape(tm, d)
        o_ref[...] = x2 * lax.rsqrt(jnp.mean(x2 * x2, axis=-1, keepdims=True) + EPS) * fg_ref[...]


def finish_layer(x, o_nsa, o_dn, g1, sc2, sh2, g2, packed, norm_ffn_g, final_g, *, tm=1024):
    B, T, D = x.shape
    n = B * T
    tm = min(tm, n)
    rows = max(tm // T, 1)
    per = max(T // tm, 1)
    row = lambda w: pl.BlockSpec((tm, w), lambda i, e: (i, 0))
    mod = pl.BlockSpec((rows, 1, D), lambda i, e: (i // per, 0, 0))
    full = lambda a: pl.BlockSpec(a.shape, lambda i, e: (0,) * a.ndim)
    expert = lambda a: pl.BlockSpec((MOE_EXPERTS_PER_STEP,) + a.shape[1:], lambda i, e: (e, 0, 0))
    vec = lambda v: v.reshape(1, D).astype(F32)
    args = (x.reshape(n, D), o_nsa, o_dn, g1, sc2, sh2, g2, packed['w_out_nsa'], packed['w_out_dn'],
            vec(norm_ffn_g), packed['router_w'], packed['router_b'], packed['w_gate_up'], packed['w_down'],
            vec(final_g))
    specs = [row(D), row(o_nsa.shape[1]), row(o_dn.shape[1]), mod, mod, mod, mod] + \
            [full(a) for a in args[7:12]] + [expert(args[12]), expert(args[13]), full(args[14])]
    return pl.pallas_call(
        _finish_body,
        grid=(n // tm, N_EXPERTS // MOE_EXPERTS_PER_STEP),
        in_specs=specs,
        out_specs=row(D),
        out_shape=jax.ShapeDtypeStruct((n, D), F32),
        scratch_shapes=[pltpu.VMEM((tm, D), F32), pltpu.VMEM((tm, D), BF16), pltpu.VMEM((tm, LANES), F32),
                        pltpu.VMEM((tm, D), F32)],
        compiler_params=pltpu.CompilerParams(dimension_semantics=("parallel", "arbitrary"),
                                             vmem_limit_bytes=VMEM_LIMIT),
        name="finish_layer",
    )(*args).reshape(B, T, D)


def permute_w_out_nsa(w_out):
    w = w_out[:NSA_Q_DIM].reshape(NSA_KV_HEADS, NSA_GROUP, HEAD_DIM, -1)
    return w.transpose(1, 0, 2, 3).reshape(NSA_Q_DIM, -1).astype(BF16)


def layer_prompt(x, mod, lp, rel_bias, packed):
    B, T, _ = x.shape
    sh1, sc1, g1, sh2, sc2, g2 = mod
    (q, ck, cv, sk, sv, wk, wv, skb, svb, wkb, wvb, qkv_raw, dg, small, ck_rows, cv_rows) = in_projection(
        x, lp['norm_mix_g'], sc1, sh1, packed['w_in'], tm=512, feature_major=True)
    kc, vc = compress_tokens(ck_rows, cv_rows, packed['cmp_k'], packed['cmp_v'], B)
    o_nsa = nsa_prompt(q, kc, vc, skb, svb, wkb, wvb, small, rel_bias, B, T)
    kv = lambda r: r.reshape(B, NSA_KV_HEADS, HEAD_DIM, -1).transpose(0, 3, 1, 2)
    w_keep = min(WINDOW, T)
    ck, cv, sk, sv, wk, wv = kv(ck), kv(cv), kv(sk), kv(sv), kv(wk[:, :, T - w_keep:]), kv(wv[:, :, T - w_keep:])
    conv_hist = jnp.zeros((B, CONV_W - 1, DN_CONV_DIM), x.dtype)
    s0 = jnp.zeros((B, DN_HEADS, DN_HEAD_DIM, DN_HEAD_DIM), x.dtype)
    o_dn, s_new, conv_new = deltanet(qkv_raw, small, dg, conv_hist, s0, lp, B, T)
    y = finish_layer(x, o_nsa, o_dn, g1, sc2, sh2, g2, packed, lp['norm_ffn_g'], packed['final_norm_g'])
    return y, (ck, cv, sk, sv, wk, wv, s_new, conv_new)


def layer_sample(x, mod, cmp_k_pool, cmp_v_pool, sel_k_pool, sel_v_pool, win_k, win_v, s0, conv_hist,
                 page_table, lp, rel_bias, packed):
    B, T, _ = x.shape
    sh1, sc1, g1, sh2, sc2, g2 = mod
    (q, ck, cv, sk, sv, wk, wv, _, _, _, _, qkv_raw, dg, small) = in_projection(
        x, lp['norm_mix_g'], sc1, sh1, packed['w_in'])
    o_nsa, wk_next, wv_next = nsa_sample(q, sk, sv, wk, wv, small, (cmp_k_pool, cmp_v_pool, sel_k_pool, sel_v_pool),
                                         win_k, win_v, page_table, rel_bias, packed['cmp_k'], packed['cmp_v'], B, T)
    kv = lambda r: r.reshape(B, -1, NSA_KV_HEADS, HEAD_DIM)
    o_dn, s_new, conv_new = deltanet(qkv_raw, small, dg, conv_hist, s0, lp, B, T)
    y = finish_layer(x, o_nsa, o_dn, g1, sc2, sh2, g2, packed, lp['norm_ffn_g'], packed['final_norm_g'])
    return y, (kv(ck), kv(cv), kv(sk), kv(sv), kv(wk_next), kv(wv_next), s_new, conv_new)


def kernel(x_prompt, x_sample, cache_cmp_k, cache_cmp_v, cache_sel_k, cache_sel_v, cache_win_k, cache_win_v,
           state_delta, state_conv, page_table, c_prompt, c_sample, rel_bias, w_ada, b_ada, norm_mix_g,
           norm_ffn_g, w_in, cmp_pos_k, cmp_w1_k, cmp_w2_k, cmp_pos_v, cmp_w1_v, cmp_w2_v, conv_w, dn_a_log,
           dn_dt_bias, dn_norm_g, w_out, router_group_w, router_group_b, router_expert_w, router_expert_b,
           expert_w_gate, expert_w_up, expert_w_down, final_norm_g):
    xp, xs = x_prompt, x_sample
    new_p, new_s = [], []
    for l in range(DEPTH):
        lp = dict(norm_mix_g=norm_mix_g[l], norm_ffn_g=norm_ffn_g[l], conv_w=conv_w[l],
                  dn_a_log=dn_a_log[l], dn_dt_bias=dn_dt_bias[l], dn_norm_g=dn_norm_g[l])
        packed = dict(w_in=pack_w_in(w_in[l]),
                      cmp_k=pack_compress(cmp_pos_k[l], cmp_w1_k[l], cmp_w2_k[l]),
                      cmp_v=pack_compress(cmp_pos_v[l], cmp_w1_v[l], cmp_w2_v[l]),
                      w_out_nsa=permute_w_out_nsa(w_out[l]),
                      w_out_dn=w_out[l][NSA_Q_DIM:].astype(BF16),
                      w_gate_up=jnp.concatenate([expert_w_gate[l], expert_w_up[l]], axis=-1).astype(BF16),
                      w_down=expert_w_down[l].astype(BF16),
                      final_norm_g=final_norm_g)
        packed['router_w'], packed['router_b'] = pack_router(router_group_w[l], router_group_b[l],
                                                             router_expert_w[l], router_expert_b[l])
        n_p = c_prompt.shape[0]
        mod = modulation(jnp.concatenate([c_prompt, c_sample], axis=0), w_ada[l], b_ada[l])
        xp, st_p = layer_prompt(xp, [m[:n_p] for m in mod], lp, rel_bias, packed)
        xs, st_s = layer_sample(xs, [m[n_p:] for m in mod], cache_cmp_k[l], cache_cmp_v[l], cache_sel_k[l],
                                cache_sel_v[l],
                                cache_win_k[l], cache_win_v[l], state_delta[l], state_conv[l], page_table,
                                lp, rel_bias, packed)
        new_p.append(st_p)
        new_s.append(st_s)
    assert DEPTH == 1
    y_prompt, y_sample = xp, xs

    def stk(states, i):
        return jnp.stack([s[i] for s in states])

    return (y_prompt, y_sample,
            stk(new_p, 0), stk(new_p, 1), stk(new_p, 2), stk(new_p, 3),
            stk(new_p, 4), stk(new_p, 5), stk(new_p, 6), stk(new_p, 7),
            stk(new_s, 0), stk(new_s, 1), stk(new_s, 2), stk(new_s, 3),
            stk(new_s, 4), stk(new_s, 5), stk(new_s, 6), stk(new_s, 7))
```
